```python
import math
import jax, jax.numpy as jnp
from jax import lax
import numpy as np

D_MODEL = 1024
BATCH = 8
SEQ = 2048
DEPTH = 4
DEC_BATCH = 128
DEC_SEQ = 1
PAST_LEN = 16384
PAGE_SIZE = 128

R_HEADS = 8
R_HD = 64
R_W = R_HEADS * R_HD
R_LORA_W = 64
R_LORA_A = 64
R_SHIFT_W = 3 * R_W + R_LORA_W + R_LORA_A
R_GN_EPS = 64e-5
G_HEADS = 4
G_DK = 64
G_DV = 128
G_QK = G_HEADS * G_DK
G_W = G_HEADS * G_DV
G_LORA = 16
G_GATE_TEMP = 16.0
G_CHUNK = 64
M_HEADS = 4
M_DK = 128
M_DV = 128
M_QK = M_HEADS * M_DK
M_W = M_HEADS * M_DV
M_CHUNK = 64
N_BRANCH = 3
EPS = 1e-6
IN_SEGMENTS = (R_SHIFT_W, R_W, G_QK, G_QK, G_W, G_LORA, G_W,
               M_QK, M_QK, M_W, M_HEADS, M_HEADS, M_W, M_W, N_BRANCH * D_MODEL)
N_IN = R_SHIFT_W + R_W + 2 * G_QK + 2 * G_W + G_LORA + 2 * M_QK + 3 * M_W + 2 * M_HEADS + N_BRANCH * D_MODEL

kernel_name = 'hybrid_rwkv7_gla_mlstm_decode_step'


def _rms_norm(x, g):
    xf = x.astype(jnp.float32)
    y = xf * lax.rsqrt(jnp.mean(xf * xf, -1, keepdims=True) + EPS)
    return (y * g).astype(x.dtype)


def _head_norm(y, g, eps, center):
    B, T = y.shape[:2]
    yf = y.astype(jnp.float32)
    if center:
        yf = yf - jnp.mean(yf, -1, keepdims=True)
    yf = yf * lax.rsqrt(jnp.mean(yf * yf, -1, keepdims=True) + eps)
    return (yf.reshape(B, T, -1) * g).astype(y.dtype)


def _split_cols(p, sizes):
    out = []
    start = 0
    for size in sizes:
        out.append(p[..., start:start + size])
        start += size
    return out


def _to_chunks(t, L):
    B, T = t.shape[:2]
    t = t.reshape((B, T // L, L) + t.shape[2:])
    return jnp.moveaxis(jnp.moveaxis(t, 1, 0), 2, 3)


def _from_chunks(o):
    NC, B, H, L, d = o.shape
    return jnp.transpose(o, (1, 0, 3, 2, 4)).reshape(B, NC * L, H, d)


def _rwkv7(r, k, v, xw, xa, z, s0, w_lora_b, w0, a_lora_b, a0, k_k, k_a, r_k, gn):
    B, T, _ = r.shape
    heads = lambda t: t.reshape(B, T, R_HEADS, R_HD)
    log_w = -jax.nn.softplus(-(w0 + jnp.tanh(xw) @ w_lora_b)) - 0.5
    decay = jnp.exp(-jnp.exp(log_w))
    a = jax.nn.sigmoid(a0 + xa @ a_lora_b)
    kk = heads(k * k_k)
    kk = kk / jnp.maximum(jnp.sqrt(jnp.sum(kk * kk, -1, keepdims=True)), 1e-12)
    k = k * (1 + (a - 1) * k_a)
    rh, kh, vh, wh, ah = heads(r), heads(k), heads(v), heads(decay), heads(a)

    def step(s, inp):
        r_t, w_t, k_t, v_t, kk_t, a_t = inp
        sa = jnp.einsum('bhvk,bhk->bhv', s, -kk_t)
        s = (s * w_t[:, :, None, :] + sa[..., None] * (kk_t * a_t)[:, :, None, :]
             + v_t[..., None] * k_t[:, :, None, :]).astype(s.dtype)
        return s, jnp.einsum('bhvk,bhk->bhv', s, r_t)

    seq = tuple(jnp.moveaxis(t, 1, 0) for t in (rh, wh, kh, vh, kk, ah))
    s_T, ys = lax.scan(step, s0, seq)
    y = _head_norm(jnp.moveaxis(ys, 0, 1), gn, R_GN_EPS, True)
    bonus = jnp.sum(rh * kh * r_k.reshape(R_HEADS, R_HD), -1, keepdims=True) * vh
    y = y + bonus.reshape(B, T, R_W)
    return y * jax.nn.silu(z), s_T


def _gla(q, k, v, xa, z, s0, a_lora_b, a_bias, gn):
    B, T, _ = q.shape
    L = math.gcd(T, G_CHUNK)
    q = q.reshape(B, T, G_HEADS, G_DK) * G_DK ** -0.5
    k = k.reshape(B, T, G_HEADS, G_DK)
    v = v.reshape(B, T, G_HEADS, G_DV)
    log_a = (jax.nn.log_sigmoid(xa @ a_lora_b + a_bias) / G_GATE_TEMP).reshape(B, T, G_HEADS, G_DK)
    causal = jnp.tril(jnp.ones((L, L), dtype=bool))

    def step(s, inp):
        qc, kc, vc, gc = inp
        cum = jnp.cumsum(gc, axis=2)
        diff = cum[:, :, :, None, :] - cum[:, :, None, :, :]
        decay = jnp.exp(jnp.where(causal[:, :, None], diff, -jnp.inf))
        att = jnp.einsum('bhtk,bhsk,bhtsk->bhts', qc, kc, decay)
        o = (jnp.einsum('bhts,bhsv->bhtv', att, vc)
             + jnp.einsum('bhtk,bhkv->bhtv', qc * jnp.exp(cum), s))
        last = cum[:, :, -1:]
        s = (s * jnp.exp(last[:, :, 0])[..., None]
             + jnp.einsum('bhsk,bhsv->bhkv', kc * jnp.exp(last - cum), vc)).astype(s.dtype)
        return s, o

    s_T, o = lax.scan(step, s0, tuple(_to_chunks(t, L) for t in (q, k, v, log_a)))
    y = _head_norm(_from_chunks(o), gn, EPS, False)
    return y * jax.nn.silu(z), s_T


def _mlstm(q, k, v, i_pre, f_pre, o_pre, z, c0, n0, m0, if_bias, gn):
    B, T, _ = q.shape
    L = math.gcd(T, M_CHUNK)
    q = q.reshape(B, T, M_HEADS, M_DK) * M_DK ** -0.5
    k = k.reshape(B, T, M_HEADS, M_DK)
    v = v.reshape(B, T, M_HEADS, M_DV)
    log_i = i_pre + if_bias[:M_HEADS]
    log_f = jax.nn.log_sigmoid(f_pre + if_bias[M_HEADS:])
    causal = jnp.tril(jnp.ones((L, L), dtype=bool))

    def step(carry, inp):
        c, n, m = carry
        qc, kc, vc, li, lf = inp
        b = jnp.cumsum(lf, axis=-1)
        d_log = jnp.where(causal, b[..., :, None] - b[..., None, :] + li[..., None, :], -jnp.inf)
        m_inter = b + m[..., None]
        m_t = jnp.maximum(m_inter, jnp.max(d_log, axis=-1))
        w_inter = jnp.exp(m_inter - m_t)
        qk = jnp.einsum('bhtk,bhsk->bhts', qc, kc) * jnp.exp(d_log - m_t[..., None])
        num = (jnp.einsum('bhts,bhsv->bhtv', qk, vc)
               + w_inter[..., None] * jnp.einsum('bhtk,bhkv->bhtv', qc, c))
        den = jnp.sum(qk, -1) + w_inter * jnp.einsum('bhtk,bhk->bht', qc, n)
        h = num / jnp.maximum(jnp.abs(den), jnp.exp(-m_t))[..., None]
        m_new = m_t[..., -1]
        w_state = jnp.exp(b[..., -1:] - b + li - m_new[..., None])
        carry_scale = jnp.exp(b[..., -1] + m - m_new)
        c = (carry_scale[..., None, None] * c
             + jnp.einsum('bhs,bhsk,bhsv->bhkv', w_state, kc, vc)).astype(c.dtype)
        n = (carry_scale[..., None] * n + jnp.einsum('bhs,bhsk->bhk', w_state, kc)).astype(n.dtype)
        return (c, n, m_new.astype(m.dtype)), h

    (c, n, m), h = lax.scan(step, (c0, n0, m0),
                            tuple(_to_chunks(t, L) for t in (q, k, v, log_i, log_f)))
    y = _head_norm(_from_chunks(h), gn, EPS, True)
    return y * jax.nn.sigmoid(o_pre) * jax.nn.silu(z), c, n, m


def _layer(x, states, lp):
    shift0, s_r0, s_g0, c0, n0, m0 = states
    (g_pre, g_post, w_in, r_mu, r_wb, r_w0, r_ab, r_a0, r_kk, r_ka, r_rk, r_gn,
     g_ab, g_abias, g_gn, m_ifb, m_gn, w_br_r, w_br_g, w_br_m, w_out) = lp
    B, T, _ = x.shape
    h = _rms_norm(x, g_pre)
    proj = h @ w_in
    (r_in, r_z, g_q, g_k, g_v, g_a, g_z, m_q, m_k, m_v, m_i, m_f, m_o, m_z,
     gate) = _split_cols(proj, IN_SEGMENTS)
    prev = jnp.concatenate([shift0[:, None, :].astype(r_in.dtype), r_in[:, :-1]], axis=1)
    r_mix = r_in + r_mu * (prev - r_in)
    new_shift = r_in[:, -1]
    r_r, r_k, r_v, r_xw, r_xa = _split_cols(r_mix, (R_W, R_W, R_W, R_LORA_W, R_LORA_A))
    y_r, s_r = _rwkv7(r_r, r_k, r_v, r_xw, r_xa, r_z, s_r0, r_wb, r_w0, r_ab, r_a0,
                      r_kk, r_ka, r_rk, r_gn)
    y_g, s_g = _gla(g_q, g_k, g_v, g_a, g_z, s_g0, g_ab, g_abias, g_gn)
    y_m, c, n, m = _mlstm(m_q, m_k, m_v, m_i, m_f, m_o, m_z, c0, n0, m0, m_ifb, m_gn)
    gates = jax.nn.sigmoid(gate).reshape(B, T, N_BRANCH, D_MODEL)
    merged = (gates[:, :, 0] * (y_r @ w_br_r) + gates[:, :, 1] * (y_g @ w_br_g)
              + gates[:, :, 2] * (y_m @ w_br_m))
    x = x + _rms_norm(merged @ w_out, g_post)
    return x, (new_shift, s_r, s_g, c, n, m)


def setup_inputs(seed: int = 0) -> dict:
    key = jax.random.key(seed)
    ks = jax.random.split(key, 32)
    nrm = lambda k, shape, s: jax.random.normal(k, shape, jnp.float32) * s
    gain = lambda k, shape: 1.0 + nrm(k, shape, 0.02)
    if_bias = jnp.concatenate([nrm(ks[27], (DEPTH, M_HEADS), 0.1),
                               3.0 + nrm(ks[28], (DEPTH, M_HEADS), 0.5)], axis=-1)
    return {
        'x_prompt': nrm(ks[0], (BATCH, SEQ, D_MODEL), 1.0),
        'x_sample': nrm(ks[1], (DEC_BATCH, DEC_SEQ, D_MODEL), 1.0),
        'state_rwkv_shift': nrm(ks[2], (DEPTH, DEC_BATCH, R_SHIFT_W), 1.0),
        'state_rwkv': nrm(ks[3], (DEPTH, DEC_BATCH, R_HEADS, R_HD, R_HD), 0.3),
        'state_gla': nrm(ks[4], (DEPTH, DEC_BATCH, G_HEADS, G_DK, G_DV), 0.3),
        'state_mlstm_C': nrm(ks[5], (DEPTH, DEC_BATCH, M_HEADS, M_DK, M_DV), 0.3),
        'state_mlstm_n': nrm(ks[6], (DEPTH, DEC_BATCH, M_HEADS, M_DK), 0.3),
        'state_mlstm_m': nrm(ks[7], (DEPTH, DEC_BATCH, M_HEADS), 1.0),
        'norm_pre': gain(ks[8], (DEPTH, D_MODEL)),
        'norm_post': gain(ks[9], (DEPTH, D_MODEL)),
        'w_in': nrm(ks[10], (DEPTH, D_MODEL, N_IN), D_MODEL ** -0.5),
        'r_mu_shift': jax.random.uniform(ks[11], (DEPTH, R_SHIFT_W), jnp.float32),
        'r_w_lora_b': nrm(ks[12], (DEPTH, R_LORA_W, R_W), 0.1 * R_LORA_W ** -0.5),
        'r_w0': nrm(ks[13], (DEPTH, R_W), 0.5),
        'r_a_lora_b': nrm(ks[14], (DEPTH, R_LORA_A, R_W), 0.1 * R_LORA_A ** -0.5),
        'r_a0': nrm(ks[15], (DEPTH, R_W), 0.1),
        'r_k_k': 0.85 + nrm(ks[16], (DEPTH, R_W), 0.05),
        'r_k_a': 1.0 + nrm(ks[17], (DEPTH, R_W), 0.05),
        'r_r_k': nrm(ks[18], (DEPTH, R_W), 0.1),
        'r_gn': gain(ks[19], (DEPTH, R_W)),
        'g_a_lora_b': nrm(ks[20], (DEPTH, G_LORA, G_QK), G_LORA ** -0.5),
        'g_a_bias': nrm(ks[21], (DEPTH, G_QK), 0.5),
        'g_gn': gain(ks[22], (DEPTH, G_W)),
        'm_if_bias': if_bias,
        'm_gn': gain(ks[23], (DEPTH, M_W)),
        'w_br_rwkv': nrm(ks[24], (DEPTH, R_W, D_MODEL), R_W ** -0.5),
        'w_br_gla': nrm(ks[25], (DEPTH, G_W, D_MODEL), G_W ** -0.5),
        'w_br_mlstm': nrm(ks[26], (DEPTH, M_W, D_MODEL), M_W ** -0.5),
        'w_out': nrm(ks[29], (DEPTH, D_MODEL, D_MODEL), D_MODEL ** -0.5),
    }


def reference(x_prompt, x_sample, state_rwkv_shift, state_rwkv, state_gla, state_mlstm_C,
              state_mlstm_n, state_mlstm_m, norm_pre, norm_post, w_in, r_mu_shift, r_w_lora_b,
              r_w0, r_a_lora_b, r_a0, r_k_k, r_k_a, r_r_k, r_gn, g_a_lora_b, g_a_bias, g_gn,
              m_if_bias, m_gn, w_br_rwkv, w_br_gla, w_br_mlstm, w_out):
    dt = x_prompt.dtype
    B = x_prompt.shape[0]
    zero_states = (jnp.zeros((B, R_SHIFT_W), dt),
                   jnp.zeros((B, R_HEADS, R_HD, R_HD), dt),
                   jnp.zeros((B, G_HEADS, G_DK, G_DV), dt),
                   jnp.zeros((B, M_HEADS, M_DK, M_DV), dt),
                   jnp.zeros((B, M_HEADS, M_DK), dt),
                   jnp.zeros((B, M_HEADS), dt))
    xp, xs = x_prompt, x_sample
    new_p, new_s = [], []
    for l in range(DEPTH):
        lp = (norm_pre[l], norm_post[l], w_in[l], r_mu_shift[l], r_w_lora_b[l], r_w0[l],
              r_a_lora_b[l], r_a0[l], r_k_k[l], r_k_a[l], r_r_k[l], r_gn[l], g_a_lora_b[l],
              g_a_bias[l], g_gn[l], m_if_bias[l], m_gn[l], w_br_rwkv[l], w_br_gla[l],
              w_br_mlstm[l], w_out[l])
        xp, st_p = _layer(xp, zero_states, lp)
        new_p.append(st_p)
        past = (state_rwkv_shift[l], state_rwkv[l], state_gla[l], state_mlstm_C[l],
                state_mlstm_n[l], state_mlstm_m[l])
        xs, st_s = _layer(xs, past, lp)
        new_s.append(st_s)
    stk = lambda sts, i: jnp.stack([s[i] for s in sts])
    return (xp, xs,
            stk(new_p, 0), stk(new_p, 1), stk(new_p, 2), stk(new_p, 3), stk(new_p, 4), stk(new_p, 5),
            stk(new_s, 0), stk(new_s, 1), stk(new_s, 2), stk(new_s, 3), stk(new_s, 4), stk(new_s, 5))
```

```python
import functools
import math

import jax
import jax.numpy as jnp
from jax import lax
from jax.experimental import pallas as pl
from jax.experimental.pallas import tpu as pltpu

F32 = jnp.float32
BF16 = jnp.bfloat16

D_MODEL = 1024
R_HEADS, R_HD = 8, 64
R_W = R_HEADS * R_HD
R_LORA = 64
R_SHIFT_W = 3 * R_W + 2 * R_LORA
R_GN_EPS = 64e-5
G_HEADS, G_DK, G_DV = 4, 64, 128
G_QK = G_HEADS * G_DK
G_W = G_HEADS * G_DV
G_LORA = 16
G_GATE_TEMP = 16.0
M_HEADS, M_DK, M_DV = 4, 128, 128
M_QK = M_HEADS * M_DK
M_W = M_HEADS * M_DV
EPS = 1e-6

LANES = 128
VMEM_LIMIT = 48 * 1024 * 1024

C_GATE = 0
C_RZ = 3072
C_GV = 3584
C_GZ = 4096
C_MQ = 4608
C_MK = 5120
C_MV = 5632
C_MO = 6144
C_MZ = 6656
C_GQK = 7168
C_RIN = 7680
C_RLORA = C_RIN + 3 * R_W
C_SMALL = C_RLORA + LANES
SM_GA, SM_MI, SM_MF = 0, 16, 20
N_USED = C_SMALL + LANES
PROJ_TN = 512
N_PAD = -(-N_USED // PROJ_TN) * PROJ_TN

CHUNK = 64


def _cparams(sem):
    return pltpu.CompilerParams(dimension_semantics=sem, vmem_limit_bytes=VMEM_LIMIT)


def _mm(a, b):
    return jnp.dot(a.astype(BF16), b.astype(BF16), preferred_element_type=F32)


def _mm_nt(a, b):
    return lax.dot_general(a.astype(BF16), b.astype(BF16), (((1,), (1,)), ((), ())),
                           preferred_element_type=F32)


def _mm_tn(a, b):
    return lax.dot_general(a.astype(BF16), b.astype(BF16), (((0,), (0,)), ((), ())),
                           preferred_element_type=F32)


def _mm_exact_lhs(m01, x):
    hi = x.astype(BF16)
    r1 = x - hi.astype(F32)
    mid = r1.astype(BF16)
    lo = (r1 - mid.astype(F32)).astype(BF16)
    m = m01.astype(BF16)
    d = lambda p: jnp.dot(m, p, preferred_element_type=F32)
    return d(hi) + d(mid) + d(lo)


def _sigmoid(x):
    return 1.0 / (1.0 + jnp.exp(-x))


def _silu(x):
    return x * _sigmoid(x)


def _softplus(x):
    return jnp.maximum(x, 0.0) + jnp.log(1.0 + jnp.exp(-jnp.abs(x)))


def _log_sigmoid(x):
    return -_softplus(-x)


def _iota2(shape, dim):
    return lax.broadcasted_iota(jnp.int32, shape, dim)


def _lane_sum(x):
    return jnp.sum(x, axis=-1, keepdims=True)


def _inproj_kernel(x_ref, g_ref, w_ref, o_ref, h_ref):
    @pl.when(pl.program_id(1) == 0)
    def _():
        x = x_ref[...]
        y = x * lax.rsqrt(jnp.mean(x * x, axis=-1, keepdims=True) + EPS)
        h_ref[...] = (y * g_ref[...]).astype(BF16)

    o_ref[...] = jnp.dot(h_ref[...], w_ref[...], preferred_element_type=F32)


def _inproj(x2d, g_pre, w_pad, tm):
    m = x2d.shape[0]
    return pl.pallas_call(
        _inproj_kernel,
        grid=(m // tm, N_PAD // PROJ_TN),
        in_specs=[pl.BlockSpec((tm, D_MODEL), lambda i, j: (i, 0)),
                  pl.BlockSpec((1, D_MODEL), lambda i, j: (0, 0)),
                  pl.BlockSpec((D_MODEL, PROJ_TN), lambda i, j: (0, j))],
        out_specs=pl.BlockSpec((tm, PROJ_TN), lambda i, j: (i, j)),
        out_shape=jax.ShapeDtypeStruct((m, N_PAD), F32),
        scratch_shapes=[pltpu.VMEM((tm, D_MODEL), BF16)],
        compiler_params=_cparams(("parallel", "arbitrary")),
        name="inproj",
    )(x2d, g_pre, w_pad)


def _merge_kernel(gate_ref, yr_ref, yg_ref, ym_ref, x_ref, wr_ref, wg_ref, wm_ref, wo_ref, gp_ref, o_ref):
    d = D_MODEL
    merged = (_sigmoid(gate_ref[:, 0:d]) * _mm(yr_ref[...], wr_ref[...])
              + _sigmoid(gate_ref[:, d:2 * d]) * _mm(yg_ref[...], wg_ref[...])
              + _sigmoid(gate_ref[:, 2 * d:3 * d]) * _mm(ym_ref[...], wm_ref[...]))
    o = _mm(merged, wo_ref[...])
    y = o * lax.rsqrt(jnp.mean(o * o, axis=-1, keepdims=True) + EPS)
    o_ref[...] = x_ref[...] + y * gp_ref[...]


def _merge(proj, y_r, y_g, y_m, x2d, w_r, w_g, w_m, w_o, g_post, tm):
    m = x2d.shape[0]
    row = lambda w: pl.BlockSpec((tm, w), lambda i: (i, 0))
    full = lambda a: pl.BlockSpec(a.shape, lambda i: (0, 0))
    return pl.pallas_call(
        _merge_kernel,
        grid=(m // tm,),
        in_specs=[row(3 * D_MODEL), row(R_W), row(G_W), row(M_W), row(D_MODEL),
                  full(w_r), full(w_g), full(w_m), full(w_o), full(g_post)],
        out_specs=row(D_MODEL),
        out_shape=jax.ShapeDtypeStruct((m, D_MODEL), F32),
        compiler_params=_cparams(("parallel",)),
        name="merge",
    )(proj, y_r, y_g, y_m, x2d, w_r, w_g, w_m, w_o, g_post)


def _rwkv_front(r_in, k_in, v_in, l_in, prev_r, prev_k, prev_v, prev_l, p):
    (mu_r, mu_k, mu_v, mu_l, wb, w0, ab, a0, k_k, k_a) = p
    r = r_in + mu_r * (prev_r - r_in)
    k = k_in + mu_k * (prev_k - k_in)
    v = v_in + mu_v * (prev_v - v_in)
    lo = l_in + mu_l * (prev_l - l_in)
    log_w = -_softplus(-(w0 + _mm(jnp.tanh(lo), wb))) - 0.5
    lw = -jnp.exp(log_w)
    a = _sigmoid(a0 + _mm(lo, ab))
    kk = k * k_k
    k2 = k * (1.0 + (a - 1.0) * k_a)
    return r, k2, v, lw, a, kk


def _rwkv_out(y_heads, r, k2, v, z, r_k, gn):
    outs = []
    for h in range(R_HEADS):
        sl = slice(h * R_HD, (h + 1) * R_HD)
        y = y_heads[h]
        y = y - jnp.mean(y, axis=-1, keepdims=True)
        y = y * lax.rsqrt(jnp.mean(y * y, axis=-1, keepdims=True) + R_GN_EPS)
        bonus = _lane_sum(r[:, sl] * k2[:, sl] * r_k[:, sl]) * v[:, sl]
        outs.append(y * gn[:, sl] + bonus)
    return jnp.concatenate(outs, axis=-1) * _silu(z)


def _inv_unit_lower(a, blk_mask, eye):
    d = jnp.where(blk_mask, a, 0.0)
    d2 = _mm(d, d)
    d4 = _mm(d2, d2)
    d8 = _mm(d4, d4)
    x = eye - d + d2 - _mm(d, d2)
    x = x + _mm(x, d4)
    td = x + _mm(x, d8)
    m = _mm(td, a - d)
    m2 = _mm(m, m)
    q = eye - m + m2 - _mm(m, m2)
    return _mm(q, td)


def _rwkv_prompt_kernel(z_ref, r_ref, k_ref, v_ref, l_ref,
                        mu_r, mu_k, mu_v, mu_l, wb, w0, ab, a0, k_k, k_a, r_k, gn,
                        y_ref, s_ref, carry_ref, *, n_chunks):
    L = CHUNK

    @pl.when(pl.program_id(1) == 0)
    def _():
        s_ref[...] = jnp.zeros_like(s_ref)
        carry_ref[...] = jnp.zeros_like(carry_ref)

    params = tuple(x[...] for x in (mu_r, mu_k, mu_v, mu_l, wb, w0, ab, a0, k_k, k_a))
    row = _iota2((L, L), 0)
    col = _iota2((L, L), 1)
    tril_incl = (col <= row)
    tril_f = tril_incl.astype(F32)
    eye = (col == row).astype(F32)
    blk_mask = (row // 16) == (col // 16)
    r2 = _iota2((2 * L, 2 * L), 0)
    c2 = _iota2((2 * L, 2 * L), 1)
    tr, tc = r2 % L, c2 % L
    g_mask = tc < tr + (r2 >= L).astype(jnp.int32)
    first_row = _iota2((L, 1), 0) == 0

    def shifted(x, carry):
        return jnp.where(first_row, carry, pltpu.roll(x, 1, 0))

    def chunk(ci, _):
        rows = pl.ds(pl.multiple_of(ci * L, L), L)
        r_in, k_in, v_in, l_in = r_ref[rows, :], k_ref[rows, :], v_ref[rows, :], l_ref[rows, :]
        prev_r = shifted(r_in, carry_ref[0:1, 0:R_W])
        prev_k = shifted(k_in, carry_ref[0:1, R_W:2 * R_W])
        prev_v = shifted(v_in, carry_ref[0:1, 2 * R_W:3 * R_W])
        prev_l = shifted(l_in, carry_ref[0:1, 3 * R_W:R_SHIFT_W])
        carry_ref[0:1, 0:R_W] = r_in[L - 1:L, :]
        carry_ref[0:1, R_W:2 * R_W] = k_in[L - 1:L, :]
        carry_ref[0:1, 2 * R_W:3 * R_W] = v_in[L - 1:L, :]
        carry_ref[0:1, 3 * R_W:R_SHIFT_W] = l_in[L - 1:L, :]
        r, k2, v, lw, a, kk = _rwkv_front(r_in, k_in, v_in, l_in, prev_r, prev_k, prev_v, prev_l, params)

        cum = _mm_exact_lhs(tril_f, lw)
        p_inc = jnp.exp(cum)
        p_exc = jnp.exp(cum - lw)
        p_inv = jnp.exp(-cum)
        p_last = p_inc[L - 1:L, :]
        r_t = r * p_inc
        k_h = k2 * p_inv
        k_hl = k_h * p_last

        y_heads = []
        for h in range(R_HEADS):
            sl = slice(h * R_HD, (h + 1) * R_HD)
            kk_h = kk[:, sl]
            inv_n = 1.0 / jnp.maximum(jnp.sqrt(_lane_sum(kk_h * kk_h)), 1e-12)
            kap = kk_h * inv_n
            beta = kap * a[:, sl]
            kap_t = kap * p_exc[:, sl]
            beta_h = beta * p_inv[:, sl]
            lhs2 = jnp.concatenate([kap_t, r_t[:, sl]], axis=0)
            rhs2 = jnp.concatenate([beta_h, k_h[:, sl]], axis=0)
            g = jnp.where(g_mask, _mm_nt(lhs2, rhs2), 0.0)
            s0 = s_ref[0, h]
            w0s = _mm_nt(lhs2, s0)
            v_h = v[:, sl]
            rhs = w0s[0:L] + _mm(g[0:L, L:2 * L], v_h)
            t_inv = _inv_unit_lower(g[0:L, 0:L], blk_mask, eye)
            u = -_mm(t_inv, rhs)
            uv = jnp.concatenate([u, v_h], axis=0)
            y_heads.append(w0s[L:2 * L] + _mm(g[L:2 * L, :], uv))
            rhs2_l = jnp.concatenate([beta_h * p_last[:, sl], k_hl[:, sl]], axis=0)
            s_ref[0, h] = s0 * p_last[:, sl] + _mm_tn(uv, rhs2_l)

        y_ref[rows, :] = _rwkv_out(y_heads, r, k2, v, z_ref[rows, :], r_k[...], gn[...])
        return 0

    lax.fori_loop(0, n_chunks, chunk, 0)


def _rwkv_params(mu, wb, w0, ab, a0, k_k, k_a, r_k, gn):
    row = lambda x: x.reshape(1, -1)
    zeros = jnp.zeros((R_LORA, R_W), F32)
    wb_p = jnp.concatenate([wb, zeros], axis=0).astype(BF16)
    ab_p = jnp.concatenate([zeros, ab], axis=0).astype(BF16)
    return (row(mu[0:R_W]), row(mu[R_W:2 * R_W]), row(mu[2 * R_W:3 * R_W]), row(mu[3 * R_W:]),
            wb_p, row(w0), ab_p, row(a0), row(k_k), row(k_a), row(r_k), row(gn))


def _rwkv_prompt(proj, params, batch, seq, tb):
    nt = seq // tb
    cb = lambda c, w: pl.BlockSpec((tb, w), lambda b, t, c=c: (b * nt + t, c // w))
    full = lambda a: pl.BlockSpec(a.shape, lambda b, t: (0,) * a.ndim)
    kern = functools.partial(_rwkv_prompt_kernel, n_chunks=tb // CHUNK)
    return pl.pallas_call(
        kern,
        grid=(batch, nt),
        in_specs=[cb(C_RZ, R_W), cb(C_RIN, R_W), cb(C_RIN + R_W, R_W), cb(C_RIN + 2 * R_W, R_W),
                  cb(C_RLORA, LANES)] + [full(a) for a in params],
        out_specs=[pl.BlockSpec((tb, R_W), lambda b, t: (b * nt + t, 0)),
                   pl.BlockSpec((1, R_HEADS, R_HD, R_HD), lambda b, t: (b, 0, 0, 0))],
        out_shape=[jax.ShapeDtypeStruct((batch * seq, R_W), F32),
                   jax.ShapeDtypeStruct((batch, R_HEADS, R_HD, R_HD), F32)],
        scratch_shapes=[pltpu.VMEM((8, R_SHIFT_W), F32)],
        compiler_params=_cparams(("parallel", "arbitrary")),
        name="rwkv_prompt",
    )(proj, proj, proj, proj, proj, *params)


def _rwkv_step_kernel(z_ref, r_ref, k_ref, v_ref, l_ref, pr_ref, pk_ref, pv_ref, pl_ref, s_in,
                      mu_r, mu_k, mu_v, mu_l, wb, w0, ab, a0, k_k, k_a, r_k, gn,
                      y_ref, s_out, vec_ref, yrow_ref, *, bb):
    params = tuple(x[...] for x in (mu_r, mu_k, mu_v, mu_l, wb, w0, ab, a0, k_k, k_a))
    r, k2, v, lw, a, kk = _rwkv_front(r_ref[...], k_ref[...], v_ref[...], l_ref[...],
                                      pr_ref[...], pk_ref[...], pv_ref[...], pl_ref[...], params)
    w = jnp.exp(lw)
    kaps, betas = [], []
    for h in range(R_HEADS):
        sl = slice(h * R_HD, (h + 1) * R_HD)
        kk_h = kk[:, sl]
        kap = kk_h / jnp.maximum(jnp.sqrt(_lane_sum(kk_h * kk_h)), 1e-12)
        kaps.append(kap)
        betas.append(kap * a[:, sl])
    vec_ref[0] = jnp.concatenate(kaps, axis=-1)
    vec_ref[1] = jnp.concatenate(betas, axis=-1)
    vec_ref[2] = w
    vec_ref[3] = k2
    vec_ref[4] = v
    vec_ref[5] = r
    eye = (_iota2((R_HD, R_HD), 0) == _iota2((R_HD, R_HD), 1)).astype(F32)

    def per_row(b, _):
        rowv = lambda i: vec_ref[i, pl.ds(b, 1), :]
        kap_b, beta_b, w_b, k_b, v_b, r_b = (rowv(i) for i in range(6))
        y_parts = []
        for h in range(R_HEADS):
            sl = slice(h * R_HD, (h + 1) * R_HD)
            s = s_in[b, h]
            v_col = _lane_sum(eye * v_b[:, sl])
            sa = -_lane_sum(s * kap_b[:, sl])
            s_new = s * w_b[:, sl] + sa * beta_b[:, sl] + v_col * k_b[:, sl]
            s_out[b, h] = s_new
            y_col = _lane_sum(s_new * r_b[:, sl])
            y_parts.append(jnp.sum(eye * y_col, axis=0, keepdims=True))
        yrow_ref[pl.ds(b, 1), :] = jnp.concatenate(y_parts, axis=-1)
        return 0

    lax.fori_loop(0, bb, per_row, 0)
    y = yrow_ref[...]
    y_heads = [y[:, h * R_HD:(h + 1) * R_HD] for h in range(R_HEADS)]
    y_ref[...] = _rwkv_out(y_heads, r, k2, v, z_ref[...], r_k[...], gn[...])


def _rwkv_step(proj, shift0, s0, params, bb):
    nb = proj.shape[0]
    cb = lambda c, w: pl.BlockSpec((bb, w), lambda i, c=c: (i, c // w))
    full = lambda a: pl.BlockSpec(a.shape, lambda i: (0,) * a.ndim)
    st = pl.BlockSpec((bb, R_HEADS, R_HD, R_HD), lambda i: (i, 0, 0, 0))
    kern = functools.partial(_rwkv_step_kernel, bb=bb)
    return pl.pallas_call(
        kern,
        grid=(nb // bb,),
        in_specs=[cb(C_RZ, R_W), cb(C_RIN, R_W), cb(C_RIN + R_W, R_W), cb(C_RIN + 2 * R_W, R_W),
                  cb(C_RLORA, LANES),
                  cb(0, R_W), cb(R_W, R_W), cb(2 * R_W, R_W), cb(3 * R_W, LANES), st]
                 + [full(a) for a in params],
        out_specs=[pl.BlockSpec((bb, R_W), lambda i: (i, 0)), st],
        out_shape=[jax.ShapeDtypeStruct((nb, R_W), F32),
                   jax.ShapeDtypeStruct(s0.shape, F32)],
        scratch_shapes=[pltpu.VMEM((6, bb, R_W), F32), pltpu.VMEM((bb, R_W), F32)],
        compiler_params=_cparams(("parallel",)),
        name="rwkv_step",
    )(proj, proj, proj, proj, proj, shift0, shift0, shift0, shift0, s0, *params)


def _gla_log_gate(small, ab_p, a_bias):
    return _log_sigmoid(_mm(small, ab_p) + a_bias) / G_GATE_TEMP


def _gla_out(o_heads, z, gn):
    outs = []
    for h in range(G_HEADS):
        o = o_heads[h]
        outs.append(o * lax.rsqrt(jnp.mean(o * o, axis=-1, keepdims=True) + EPS))
    return jnp.concatenate(outs, axis=-1) * gn * _silu(z)


def _gla_prompt_kernel(qk_ref, v_ref, z_ref, sm_ref, ab, a_bias, gn, y_ref, s_ref, st_ref, *, n_chunks):
    L = CHUNK
    last = pl.num_programs(1) - 1

    @pl.when(pl.program_id(1) == 0)
    def _():
        st_ref[...] = jnp.zeros_like(st_ref)

    row = _iota2((L, L), 0)
    col = _iota2((L, L), 1)
    tril_incl = col <= row
    tril_f = tril_incl.astype(F32)
    scale = G_DK ** -0.5

    def chunk(ci, _):
        rows = pl.ds(pl.multiple_of(ci * L, L), L)
        qk = qk_ref[rows, :]
        q, k = qk[:, 0:G_QK], qk[:, G_QK:2 * G_QK]
        v = v_ref[rows, :]
        la = _gla_log_gate(sm_ref[rows, :], ab[...], a_bias[...])
        cum = _mm_exact_lhs(tril_f, la)
        ref_row = cum[L // 2:L // 2 + 1, :]
        last_row = cum[L - 1:L, :]
        q0 = q * scale * jnp.exp(cum)
        qe = q * scale * jnp.exp(cum - ref_row)
        ke = k * jnp.exp(ref_row - cum)
        kl = k * jnp.exp(last_row - cum)
        e_last = jnp.exp(last_row)
        o_heads = []
        for h in range(G_HEADS):
            sk = slice(h * G_DK, (h + 1) * G_DK)
            sv = slice(h * G_DV, (h + 1) * G_DV)
            att = jnp.where(tril_incl, _mm_nt(qe[:, sk], ke[:, sk]), 0.0)
            st = st_ref[h]
            v_h = v[:, sv]
            o_heads.append(_mm(att, v_h) + _mm_nt(q0[:, sk], st))
            st_ref[h] = st * e_last[:, sk] + _mm_tn(v_h, kl[:, sk])
        y_ref[rows, :] = _gla_out(o_heads, z_ref[rows, :], gn[...])
        return 0

    lax.fori_loop(0, n_chunks, chunk, 0)

    @pl.when(pl.program_id(1) == last)
    def _():
        for h in range(G_HEADS):
            s_ref[0, h] = st_ref[h].T


def _gla_params(ab, a_bias, gn):
    ab_p = jnp.zeros((LANES, G_QK), F32).at[SM_GA:SM_GA + G_LORA].set(ab).astype(BF16)
    return ab_p, a_bias.reshape(1, -1), gn.reshape(1, -1)


def _gla_prompt(proj, params, batch, seq, tb):
    nt = seq // tb
    cb = lambda c, w: pl.BlockSpec((tb, w), lambda b, t, c=c: (b * nt + t, c // w))
    full = lambda a: pl.BlockSpec(a.shape, lambda b, t: (0,) * a.ndim)
    kern = functools.partial(_gla_prompt_kernel, n_chunks=tb // CHUNK)
    return pl.pallas_call(
        kern,
        grid=(batch, nt),
        in_specs=[cb(C_GQK, 2 * G_QK), cb(C_GV, G_W), cb(C_GZ, G_W), cb(C_SMALL, LANES)]
                 + [full(a) for a in params],
        out_specs=[pl.BlockSpec((tb, G_W), lambda b, t: (b * nt + t, 0)),
                   pl.BlockSpec((1, G_HEADS, G_DK, G_DV), lambda b, t: (b, 0, 0, 0))],
        out_shape=[jax.ShapeDtypeStruct((batch * seq, G_W), F32),
                   jax.ShapeDtypeStruct((batch, G_HEADS, G_DK, G_DV), F32)],
        scratch_shapes=[pltpu.VMEM((G_HEADS, G_DV, G_DK), F32)],
        compiler_params=_cparams(("parallel", "arbitrary")),
        name="gla_prompt",
    )(proj, proj, proj, proj, *params)


def _gla_step_kernel(qk_ref, v_ref, z_ref, sm_ref, s_in, ab, a_bias, gn, y_ref, s_out,
                     vec_ref, vrow_ref, orow_ref, *, bb):
    qk = qk_ref[...]
    q = qk[:, 0:G_QK] * (G_DK ** -0.5)
    k = qk[:, G_QK:2 * G_QK]
    g = _gla_log_gate(sm_ref[...], ab[...], a_bias[...])
    vec_ref[0] = q
    vec_ref[1] = k
    vec_ref[2] = jnp.exp(g)
    vrow_ref[...] = v_ref[...]
    eye = (_iota2((G_DK, G_DK), 0) == _iota2((G_DK, G_DK), 1)).astype(F32)

    def per_row(b, _):
        q_b, k_b, e_b = (vec_ref[i, pl.ds(b, 1), :] for i in range(3))
        v_b = vrow_ref[pl.ds(b, 1), :]
        o_parts = []
        for h in range(G_HEADS):
            sk = slice(h * G_DK, (h + 1) * G_DK)
            sv = slice(h * G_DV, (h + 1) * G_DV)
            s = s_in[b, h]
            q_col = _lane_sum(eye * q_b[:, sk])
            k_col = _lane_sum(eye * k_b[:, sk])
            e_col = _lane_sum(eye * e_b[:, sk])
            v_h = v_b[:, sv]
            qk_dot = _lane_sum(q_b[:, sk] * k_b[:, sk])
            o_parts.append(qk_dot * v_h + jnp.sum((q_col * e_col) * s, axis=0, keepdims=True))
            s_out[b, h] = s * e_col + k_col * v_h
        orow_ref[pl.ds(b, 1), :] = jnp.concatenate(o_parts, axis=-1)
        return 0

    lax.fori_loop(0, bb, per_row, 0)
    o = orow_ref[...]
    o_heads = [o[:, h * G_DV:(h + 1) * G_DV] for h in range(G_HEADS)]
    y_ref[...] = _gla_out(o_heads, z_ref[...], gn[...])


def _gla_step(proj, s0, params, bb):
    nb = proj.shape[0]
    cb = lambda c, w: pl.BlockSpec((bb, w), lambda i, c=c: (i, c // w))
    full = lambda a: pl.BlockSpec(a.shape, lambda i: (0,) * a.ndim)
    st = pl.BlockSpec((bb, G_HEADS, G_DK, G_DV), lambda i: (i, 0, 0, 0))
    kern = functools.partial(_gla_step_kernel, bb=bb)
    return pl.pallas_call(
        kern,
        grid=(nb // bb,),
        in_specs=[cb(C_GQK, 2 * G_QK), cb(C_GV, G_W), cb(C_GZ, G_W), cb(C_SMALL, LANES), st]
                 + [full(a) for a in params],
        out_specs=[pl.BlockSpec((bb, G_W), lambda i: (i, 0)), st],
        out_shape=[jax.ShapeDtypeStruct((nb, G_W), F32), jax.ShapeDtypeStruct(s0.shape, F32)],
        scratch_shapes=[pltpu.VMEM((3, bb, G_QK), F32), pltpu.VMEM((bb, G_W), F32),
                        pltpu.VMEM((bb, G_W), F32)],
        compiler_params=_cparams(("parallel",)),
        name="gla_step",
    )(proj, proj, proj, proj, s0, *params)


def _mlstm_out(h_heads, o_pre, z, gn):
    outs = []
    for h in range(M_HEADS):
        y = h_heads[h]
        y = y - jnp.mean(y, axis=-1, keepdims=True)
        outs.append(y * lax.rsqrt(jnp.mean(y * y, axis=-1, keepdims=True) + EPS))
    return jnp.concatenate(outs, axis=-1) * gn * _sigmoid(o_pre) * _silu(z)


def _mlstm_prompt_kernel(q_ref, k_ref, v_ref, o_ref, z_ref, sm_ref, ifb, gn,
                         y_ref, c_ref, n_ref, m_ref, *, n_chunks):
    L = CHUNK

    @pl.when(pl.program_id(1) == 0)
    def _():
        c_ref[...] = jnp.zeros_like(c_ref)
        n_ref[...] = jnp.zeros_like(n_ref)
        m_ref[...] = jnp.zeros_like(m_ref)

    row = _iota2((L, L), 0)
    col = _iota2((L, L), 1)
    tril_incl = col <= row
    tril_f = tril_incl.astype(F32)
    scale = M_DK ** -0.5
    neg_inf = -jnp.inf

    def chunk(ci, _):
        rows = pl.ds(pl.multiple_of(ci * L, L), L)
        pre = sm_ref[rows, :] + ifb[...]
        lf = _log_sigmoid(pre)
        b_all = _mm_exact_lhs(tril_f, lf)
        lib_all = pre - pltpu.roll(b_all, LANES - (SM_MF - SM_MI), 1)
        lib_t = lib_all.T
        bcum = b_all[:, SM_MF:SM_MF + M_HEADS]
        lib = lib_all[:, SM_MI:SM_MI + M_HEADS]
        b_last = bcum[L - 1:L, :]
        q, k, v = q_ref[rows, :], k_ref[rows, :], v_ref[rows, :]
        h_heads = []
        for h in range(M_HEADS):
            sl = slice(h * M_DK, (h + 1) * M_DK)
            b_col = bcum[:, h:h + 1]
            m_prev = m_ref[0, :, h:h + 1]
            d_log = jnp.where(tril_incl, b_col + lib_t[SM_MI + h:SM_MI + h + 1, :], neg_inf)
            m_inter = b_col + m_prev
            m_t = jnp.maximum(m_inter, jnp.max(d_log, axis=-1, keepdims=True))
            w_inter = jnp.exp(m_inter - m_t)
            qs = q[:, sl] * scale
            k_h, v_h = k[:, sl], v[:, sl]
            qk = _mm_nt(qs, k_h) * jnp.exp(d_log - m_t)
            c = c_ref[0, h]
            n = n_ref[0, h:h + 1, :]
            num = _mm(qk, v_h) + w_inter * _mm(qs, c)
            den = _lane_sum(qk) + w_inter * _lane_sum(qs * n)
            h_heads.append(num / jnp.maximum(jnp.abs(den), jnp.exp(-m_t)))
            m_new = m_t[L - 1:L, :]
            w_state = jnp.exp(b_last[:, h:h + 1] + lib[:, h:h + 1] - m_new)
            carry = jnp.exp(b_last[:, h:h + 1] + m_prev - m_new)
            wk = w_state * k_h
            c_ref[0, h] = carry * c + _mm_tn(wk, v_h)
            n_ref[0, h:h + 1, :] = carry * n + jnp.sum(wk, axis=0, keepdims=True)
            m_ref[0, :, h:h + 1] = m_new
        y_ref[rows, :] = _mlstm_out(h_heads, o_ref[rows, :], z_ref[rows, :], gn[...])
        return 0

    lax.fori_loop(0, n_chunks, chunk, 0)


def _mlstm_params(ifb, gn):
    ifb_p = jnp.zeros((1, LANES), F32).at[0, SM_MI:SM_MI + 2 * M_HEADS].set(ifb)
    return ifb_p, gn.reshape(1, -1)


def _mlstm_prompt(proj, params, batch, seq, tb):
    nt = seq // tb
    cb = lambda c, w: pl.BlockSpec((tb, w), lambda b, t, c=c: (b * nt + t, c // w))
    full = lambda a: pl.BlockSpec(a.shape, lambda b, t: (0,) * a.ndim)
    kern = functools.partial(_mlstm_prompt_kernel, n_chunks=tb // CHUNK)
    return pl.pallas_call(
        kern,
        grid=(batch, nt),
        in_specs=[cb(C_MQ, M_QK), cb(C_MK, M_QK), cb(C_MV, M_W), cb(C_MO, M_W), cb(C_MZ, M_W),
                  cb(C_SMALL, LANES)] + [full(a) for a in params],
        out_specs=[pl.BlockSpec((tb, M_W), lambda b, t: (b * nt + t, 0)),
                   pl.BlockSpec((1, M_HEADS, M_DK, M_DV), lambda b, t: (b, 0, 0, 0)),
                   pl.BlockSpec((1, M_HEADS, M_DK), lambda b, t: (b, 0, 0)),
                   pl.BlockSpec((1, 1, M_HEADS), lambda b, t: (b, 0, 0))],
        out_shape=[jax.ShapeDtypeStruct((batch * seq, M_W), F32),
                   jax.ShapeDtypeStruct((batch, M_HEADS, M_DK, M_DV), F32),
                   jax.ShapeDtypeStruct((batch, M_HEADS, M_DK), F32),
                   jax.ShapeDtypeStruct((batch, 1, M_HEADS), F32)],
        compiler_params=_cparams(("parallel", "arbitrary")),
        name="mlstm_prompt",
    )(proj, proj, proj, proj, proj, proj, *params)


def _mlstm_step_kernel(q_ref, k_ref, v_ref, o_ref, z_ref, sm_ref, c_in, n_in, m_in, ifb, gn,
                       y_ref, c_out, n_out, m_out, vec_ref, sc_ref, hrow_ref, *, bb):
    pre = sm_ref[...] + ifb[...]
    li = pre[:, SM_MI:SM_MI + M_HEADS]
    lf = _log_sigmoid(pre[:, SM_MF:SM_MF + M_HEADS])
    m0 = m_in[...]
    m_inter = lf + m0
    m_t = jnp.maximum(m_inter, li)
    m_out[...] = m_t
    pad = lambda x: jnp.concatenate([x, jnp.zeros((bb, LANES - M_HEADS), F32)], axis=-1)
    sc_ref[0] = pad(jnp.exp(m_inter - m_t))
    sc_ref[1] = pad(jnp.exp(li - m_t))
    sc_ref[2] = pad(jnp.exp(-m_t))
    vec_ref[0] = q_ref[...] * (M_DK ** -0.5)
    vec_ref[1] = k_ref[...]
    vec_ref[2] = v_ref[...]
    eye = (_iota2((M_DK, M_DK), 0) == _iota2((M_DK, M_DK), 1)).astype(F32)

    def per_row(b, _):
        q_b, k_b, v_b = (vec_ref[i, pl.ds(b, 1), :] for i in range(3))
        wi_b, ei_b, en_b = (sc_ref[i, pl.ds(b, 1), :] for i in range(3))
        h_parts = []
        for h in range(M_HEADS):
            sl = slice(h * M_DK, (h + 1) * M_DK)
            q_h, k_h, v_h = q_b[:, sl], k_b[:, sl], v_b[:, sl]
            wi, ei, en = wi_b[:, h:h + 1], ei_b[:, h:h + 1], en_b[:, h:h + 1]
            c = c_in[b, h]
            n = n_in[b, h:h + 1, :]
            q_col = _lane_sum(eye * q_h)
            k_col = _lane_sum(eye * k_h)
            qk = _lane_sum(q_h * k_h) * ei
            num = qk * v_h + wi * jnp.sum(q_col * c, axis=0, keepdims=True)
            den = qk + wi * _lane_sum(q_h * n)
            h_parts.append(num / jnp.maximum(jnp.abs(den), en))
            c_out[b, h] = wi * c + (ei * k_col) * v_h
            n_out[b, h:h + 1, :] = wi * n + ei * k_h
        hrow_ref[pl.ds(b, 1), :] = jnp.concatenate(h_parts, axis=-1)
        return 0

    lax.fori_loop(0, bb, per_row, 0)
    hh = hrow_ref[...]
    h_heads = [hh[:, h * M_DV:(h + 1) * M_DV] for h in range(M_HEADS)]
    y_ref[...] = _mlstm_out(h_heads, o_ref[...], z_ref[...], gn[...])


def _mlstm_step(proj, c0, n0, m0, params, bb):
    nb = proj.shape[0]
    cb = lambda c, w: pl.BlockSpec((bb, w), lambda i, c=c: (i, c // w))
    full = lambda a: pl.BlockSpec(a.shape, lambda i: (0,) * a.ndim)
    cs = pl.BlockSpec((bb, M_HEADS, M_DK, M_DV), lambda i: (i, 0, 0, 0))
    ns = pl.BlockSpec((bb, M_HEADS, M_DK), lambda i: (i, 0, 0))
    ms = pl.BlockSpec((bb, M_HEADS), lambda i: (i, 0))
    kern = functools.partial(_mlstm_step_kernel, bb=bb)
    return pl.pallas_call(
        kern,
        grid=(nb // bb,),
        in_specs=[cb(C_MQ, M_QK), cb(C_MK, M_QK), cb(C_MV, M_W), cb(C_MO, M_W), cb(C_MZ, M_W),
                  cb(C_SMALL, LANES), cs, ns, ms] + [full(a) for a in params],
        out_specs=[pl.BlockSpec((bb, M_W), lambda i: (i, 0)), cs, ns, ms],
        out_shape=[jax.ShapeDtypeStruct((nb, M_W), F32), jax.ShapeDtypeStruct(c0.shape, F32),
                   jax.ShapeDtypeStruct(n0.shape, F32), jax.ShapeDtypeStruct(m0.shape, F32)],
        scratch_shapes=[pltpu.VMEM((3, bb, M_QK), F32), pltpu.VMEM((3, bb, LANES), F32),
                        pltpu.VMEM((bb, M_W), F32)],
        compiler_params=_cparams(("parallel",)),
        name="mlstm_step",
    )(proj, proj, proj, proj, proj, proj, c0, n0, m0, *params)


def _pad_w_in(w_in):
    sizes = (R_SHIFT_W, R_W, G_QK, G_QK, G_W, G_LORA, G_W, M_QK, M_QK, M_W, M_HEADS, M_HEADS, M_W, M_W,
             3 * D_MODEL)
    offs = [0]
    for s in sizes:
        offs.append(offs[-1] + s)
    seg = lambda i: w_in[..., offs[i]:offs[i + 1]]
    (r_in, r_z, g_q, g_k, g_v, g_a, g_z, m_q, m_k, m_v, m_i, m_f, m_o, m_z, gate) = (seg(i) for i in range(15))
    zeros = lambda n: jnp.zeros(w_in.shape[:-1] + (n,), w_in.dtype)
    small = jnp.concatenate([g_a, m_i, m_f, zeros(LANES - G_LORA - 2 * M_HEADS)], axis=-1)
    w = jnp.concatenate([gate, r_z, g_v, g_z, m_q, m_k, m_v, m_o, m_z, g_q, g_k, r_in, small,
                         zeros(N_PAD - N_USED)], axis=-1)
    return w.astype(BF16)


def _pick(n, prefs):
    for p in prefs:
        if n % p == 0:
            return p
    return n


def kernel(x_prompt, x_sample, state_rwkv_shift, state_rwkv, state_gla, state_mlstm_C, state_mlstm_n,
           state_mlstm_m, norm_pre, norm_post, w_in, r_mu_shift, r_w_lora_b, r_w0, r_a_lora_b, r_a0,
           r_k_k, r_k_a, r_r_k, r_gn, g_a_lora_b, g_a_bias, g_gn, m_if_bias, m_gn, w_br_rwkv, w_br_gla,
           w_br_mlstm, w_out):
    depth = w_in.shape[0]
    bp, tp, _ = x_prompt.shape
    bs, ts, _ = x_sample.shape
    assert ts == 1 and tp % CHUNK == 0
    w_pad = _pad_w_in(w_in)
    wr, wg, wm, wo = (w.astype(BF16) for w in (w_br_rwkv, w_br_gla, w_br_mlstm, w_out))

    mp = bp * tp
    tm_in = _pick(mp, (2048, 1024, 512, 256, 128, 64))
    tm_mg = _pick(mp, (512, 256, 128, 64))
    tb = _pick(tp, (256, 128, 64))
    bb = _pick(bs, (8,))

    xp = x_prompt.reshape(mp, D_MODEL)
    xs = x_sample.reshape(bs, D_MODEL)
    new_p = [[] for _ in range(6)]
    new_s = [[] for _ in range(6)]
    for l in range(depth):
        g_pre = norm_pre[l].reshape(1, -1)
        g_post = norm_post[l].reshape(1, -1)
        rp = _rwkv_params(r_mu_shift[l], r_w_lora_b[l], r_w0[l], r_a_lora_b[l], r_a0[l], r_k_k[l],
                          r_k_a[l], r_r_k[l], r_gn[l])
        gp = _gla_params(g_a_lora_b[l], g_a_bias[l], g_gn[l])
        mparams = _mlstm_params(m_if_bias[l], m_gn[l])

        proj = _inproj(xp, g_pre, w_pad[l], tm_in)
        y_r, s_r = _rwkv_prompt(proj, rp, bp, tp, tb)
        y_g, s_g = _gla_prompt(proj, gp, bp, tp, tb)
        y_m, c, n, m = _mlstm_prompt(proj, mparams, bp, tp, tb)
        xp = _merge(proj, y_r, y_g, y_m, xp, wr[l], wg[l], wm[l], wo[l], g_post, tm_mg)
        shift = proj.reshape(bp, tp, N_PAD)[:, tp - 1, C_RIN:C_RIN + R_SHIFT_W]
        for lst, val in zip(new_p, (shift, s_r, s_g, c, n, m.reshape(bp, M_HEADS))):
            lst.append(val)

        proj = _inproj(xs, g_pre, w_pad[l], bs)
        y_r, s_r = _rwkv_step(proj, state_rwkv_shift[l], state_rwkv[l], rp, bb)
        y_g, s_g = _gla_step(proj, state_gla[l], gp, bb)
        y_m, c, n, m = _mlstm_step(proj, state_mlstm_C[l], state_mlstm_n[l], state_mlstm_m[l], mparams, bb)
        xs = _merge(proj, y_r, y_g, y_m, xs, wr[l], wg[l], wm[l], wo[l], g_post, bs)
        shift = proj[:, C_RIN:C_RIN + R_SHIFT_W]
        for lst, val in zip(new_s, (shift, s_r, s_g, c, n, m)):
            lst.append(val)

    stk = lambda lst: jnp.stack(lst)
    return (xp.reshape(bp, tp, D_MODEL), xs.reshape(bs, ts, D_MODEL),
            *(stk(v) for v in new_p), *(stk(v) for v in new_s))
```

```python
import functools
import math

import jax
import jax.numpy as jnp
from jax import lax
from jax.experimental import pallas as pl
from jax.experimental.pallas import tpu as pltpu

F32 = jnp.float32
BF16 = jnp.bfloat16

D_MODEL = 1024
R_HEADS, R_HD = 8, 64
R_W = R_HEADS * R_HD
R_LORA = 64
R_SHIFT_W = 3 * R_W + 2 * R_LORA
R_GN_EPS = 64e-5
G_HEADS, G_DK, G_DV = 4, 64, 128
G_QK = G_HEADS * G_DK
G_W = G_HEADS * G_DV
G_LORA = 16
G_GATE_TEMP = 16.0
M_HEADS, M_DK, M_DV = 4, 128, 128
M_QK = M_HEADS * M_DK
M_W = M_HEADS * M_DV
EPS = 1e-6

LANES = 128
VMEM_LIMIT = 48 * 1024 * 1024

C_GATE = 0
C_RZ = 3072
C_GV = 3584
C_GZ = 4096
C_MQ = 4608
C_MK = 5120
C_MV = 5632
C_MO = 6144
C_MZ = 6656
C_GQK = 7168
C_RIN = 7680
C_RLORA = C_RIN + 3 * R_W
C_SMALL = C_RLORA + LANES
SM_GA, SM_MI, SM_MF = 0, 16, 20
N_USED = C_SMALL + LANES
PROJ_TN = 512
N_PAD = -(-N_USED // PROJ_TN) * PROJ_TN

CHUNK = 64


def _cparams(sem):
    return pltpu.CompilerParams(dimension_semantics=sem, vmem_limit_bytes=VMEM_LIMIT)


def _mm(a, b):
    return jnp.dot(a.astype(BF16), b.astype(BF16), preferred_element_type=F32)


def _mm_nt(a, b):
    return lax.dot_general(a.astype(BF16), b.astype(BF16), (((1,), (1,)), ((), ())),
                           preferred_element_type=F32)


def _mm_tn(a, b):
    return lax.dot_general(a.astype(BF16), b.astype(BF16), (((0,), (0,)), ((), ())),
                           preferred_element_type=F32)


def _mm_exact_lhs(m01, x):
    hi = x.astype(BF16)
    r1 = x - hi.astype(F32)
    mid = r1.astype(BF16)
    lo = (r1 - mid.astype(F32)).astype(BF16)
    m = m01.astype(BF16)
    d = lambda p: jnp.dot(m, p, preferred_element_type=F32)
    return d(hi) + d(mid) + d(lo)


def _sigmoid(x):
    return 1.0 / (1.0 + jnp.exp(-x))


def _silu(x):
    return x * _sigmoid(x)


def _softplus(x):
    return jnp.maximum(x, 0.0) + jnp.log(1.0 + jnp.exp(-jnp.abs(x)))


def _log_sigmoid(x):
    return -_softplus(-x)


def _iota2(shape, dim):
    return lax.broadcasted_iota(jnp.int32, shape, dim)


def _lane_sum(x):
    return jnp.sum(x, axis=-1, keepdims=True)


def _inproj_kernel(x_ref, g_ref, w_ref, o_ref, h_ref):
    @pl.when(pl.program_id(1) == 0)
    def _():
        x = x_ref[...]
        y = x * lax.rsqrt(jnp.mean(x * x, axis=-1, keepdims=True) + EPS)
        h_ref[...] = (y * g_ref[...]).astype(BF16)

    o_ref[...] = jnp.dot(h_ref[...], w_ref[...], preferred_element_type=F32)


def _inproj(x2d, g_pre, w_pad, layer, tm):
    m = x2d.shape[0]
    return pl.pallas_call(
        _inproj_kernel,
        grid=(m // tm, N_PAD // PROJ_TN),
        in_specs=[pl.BlockSpec((tm, D_MODEL), lambda i, j: (i, 0)),
                  pl.BlockSpec((1, D_MODEL), lambda i, j: (0, 0)),
                  pl.BlockSpec((None, D_MODEL, PROJ_TN), lambda i, j: (layer, 0, j))],
        out_specs=pl.BlockSpec((tm, PROJ_TN), lambda i, j: (i, j)),
        out_shape=jax.ShapeDtypeStruct((m, N_PAD), F32),
        scratch_shapes=[pltpu.VMEM((tm, D_MODEL), BF16)],
        compiler_params=_cparams(("parallel", "arbitrary")),
        name="inproj",
    )(x2d, g_pre, w_pad)


def _merge_kernel(gate_ref, yr_ref, yg_ref, ym_ref, x_ref, wr_ref, wg_ref, wm_ref, wo_ref, gp_ref, o_ref):
    d = D_MODEL
    merged = (_sigmoid(gate_ref[:, 0:d]) * _mm(yr_ref[...], wr_ref[...])
              + _sigmoid(gate_ref[:, d:2 * d]) * _mm(yg_ref[...], wg_ref[...])
              + _sigmoid(gate_ref[:, 2 * d:3 * d]) * _mm(ym_ref[...], wm_ref[...]))
    o = _mm(merged, wo_ref[...])
    y = o * lax.rsqrt(jnp.mean(o * o, axis=-1, keepdims=True) + EPS)
    o_ref[...] = x_ref[...] + y * gp_ref[...]


def _merge(proj, y_r, y_g, y_m, x2d, w_r, w_g, w_m, w_o, g_post, layer, tm):
    m = x2d.shape[0]
    row = lambda w: pl.BlockSpec((tm, w), lambda i: (i, 0))
    full = lambda a: pl.BlockSpec(a.shape, lambda i: (0, 0))
    wl = lambda a: pl.BlockSpec((None,) + a.shape[1:], lambda i: (layer, 0, 0))
    return pl.pallas_call(
        _merge_kernel,
        grid=(m // tm,),
        in_specs=[row(3 * D_MODEL), row(R_W), row(G_W), row(M_W), row(D_MODEL),
                  wl(w_r), wl(w_g), wl(w_m), wl(w_o), full(g_post)],
        out_specs=row(D_MODEL),
        out_shape=jax.ShapeDtypeStruct((m, D_MODEL), F32),
        compiler_params=_cparams(("parallel",)),
        name="merge",
    )(proj, y_r, y_g, y_m, x2d, w_r, w_g, w_m, w_o, g_post)


def _rwkv_front(r_in, k_in, v_in, l_in, prev_r, prev_k, prev_v, prev_l, p):
    (mu_r, mu_k, mu_v, mu_l, wb, w0, ab, a0, k_k, k_a) = p
    r = r_in + mu_r * (prev_r - r_in)
    k = k_in + mu_k * (prev_k - k_in)
    v = v_in + mu_v * (prev_v - v_in)
    lo = l_in + mu_l * (prev_l - l_in)
    log_w = -_softplus(-(w0 + _mm(jnp.tanh(lo), wb))) - 0.5
    lw = -jnp.exp(log_w)
    a = _sigmoid(a0 + _mm(lo, ab))
    kk = k * k_k
    k2 = k * (1.0 + (a - 1.0) * k_a)
    return r, k2, v, lw, a, kk


def _rwkv_out(y_heads, r, k2, v, z, r_k, gn):
    outs = []
    for h in range(R_HEADS):
        sl = slice(h * R_HD, (h + 1) * R_HD)
        y = y_heads[h]
        y = y - jnp.mean(y, axis=-1, keepdims=True)
        y = y * lax.rsqrt(jnp.mean(y * y, axis=-1, keepdims=True) + R_GN_EPS)
        bonus = _lane_sum(r[:, sl] * k2[:, sl] * r_k[:, sl]) * v[:, sl]
        outs.append(y * gn[:, sl] + bonus)
    return jnp.concatenate(outs, axis=-1) * _silu(z)


def _rwkv_prompt_kernel(z_ref, r_ref, k_ref, v_ref, l_ref,
                        mu_r, mu_k, mu_v, mu_l, wb, w0, ab, a0, k_k, k_a, r_k, gn,
                        y_ref, s_ref, carry_ref, *, n_chunks):
    L = CHUNK

    @pl.when(pl.program_id(1) == 0)
    def _():
        s_ref[...] = jnp.zeros_like(s_ref)
        carry_ref[...] = jnp.zeros_like(carry_ref)

    params = tuple(x[...] for x in (mu_r, mu_k, mu_v, mu_l, wb, w0, ab, a0, k_k, k_a))
    row = _iota2((L, L), 0)
    col = _iota2((L, L), 1)
    tril_incl = (col <= row)
    tril_f = tril_incl.astype(F32)
    eye = (col == row).astype(F32)
    blk_mask = (row // 16) == (col // 16)
    r2 = _iota2((2 * L, 2 * L), 0)
    c2 = _iota2((2 * L, 2 * L), 1)
    tr, tc = r2 % L, c2 % L
    g_mask = tc < tr + (r2 >= L).astype(jnp.int32)
    first_row = _iota2((L, 1), 0) == 0

    def shifted(x, carry):
        return jnp.where(first_row, carry, pltpu.roll(x, 1, 0))

    def chunk(ci, _):
        rows = pl.ds(pl.multiple_of(ci * L, L), L)
        r_in, k_in, v_in, l_in = r_ref[rows, :], k_ref[rows, :], v_ref[rows, :], l_ref[rows, :]
        prev_r = shifted(r_in, carry_ref[0:1, 0:R_W])
        prev_k = shifted(k_in, carry_ref[0:1, R_W:2 * R_W])
        prev_v = shifted(v_in, carry_ref[0:1, 2 * R_W:3 * R_W])
        prev_l = shifted(l_in, carry_ref[0:1, 3 * R_W:R_SHIFT_W])
        carry_ref[0:1, 0:R_W] = r_in[L - 1:L, :]
        carry_ref[0:1, R_W:2 * R_W] = k_in[L - 1:L, :]
        carry_ref[0:1, 2 * R_W:3 * R_W] = v_in[L - 1:L, :]
        carry_ref[0:1, 3 * R_W:R_SHIFT_W] = l_in[L - 1:L, :]
        r, k2, v, lw, a, kk = _rwkv_front(r_in, k_in, v_in, l_in, prev_r, prev_k, prev_v, prev_l, params)

        cum = _mm_exact_lhs(tril_f, lw)
        p_inc = jnp.exp(cum)
        p_exc = jnp.exp(cum - lw)
        p_inv = jnp.exp(-cum)
        p_last = p_inc[L - 1:L, :]
        r_t = r * p_inc
        k_h = k2 * p_inv
        k_hl = k_h * p_last

        H = range(R_HEADS)
        sls = [slice(h * R_HD, (h + 1) * R_HD) for h in H]
        lhs2, rhs2, rhs2_l, v_h, s0 = [], [], [], [], []
        for sl in sls:
            kk_h = kk[:, sl]
            kap = kk_h * (1.0 / jnp.maximum(jnp.sqrt(_lane_sum(kk_h * kk_h)), 1e-12))
            beta_h = kap * a[:, sl] * p_inv[:, sl]
            lhs2.append(jnp.concatenate([kap * p_exc[:, sl], r_t[:, sl]], axis=0))
            rhs2.append(jnp.concatenate([beta_h, k_h[:, sl]], axis=0))
            rhs2_l.append(jnp.concatenate([beta_h * p_last[:, sl], k_hl[:, sl]], axis=0))
            v_h.append(v[:, sl])
        for h in H:
            s0.append(s_ref[0, h])
        g = [jnp.where(g_mask, _mm_nt(lhs2[h], rhs2[h]), 0.0) for h in H]
        w0s = [_mm_nt(lhs2[h], s0[h]) for h in H]
        a_m = [g[h][0:L, 0:L] for h in H]
        d = [jnp.where(blk_mask, a_m[h], 0.0) for h in H]
        rhs = [w0s[h][0:L] + _mm(g[h][0:L, L:2 * L], v_h[h]) for h in H]
        d2 = [_mm(d[h], d[h]) for h in H]
        d4 = [_mm(d2[h], d2[h]) for h in H]
        dd2 = [_mm(d[h], d2[h]) for h in H]
        d8 = [_mm(d4[h], d4[h]) for h in H]
        x = [eye - d[h] + d2[h] - dd2[h] for h in H]
        x = [x[h] + _mm(x[h], d4[h]) for h in H]
        td = [x[h] + _mm(x[h], d8[h]) for h in H]
        m = [_mm(td[h], a_m[h] - d[h]) for h in H]
        tdr = [_mm(td[h], rhs[h]) for h in H]
        m2 = [_mm(m[h], m[h]) for h in H]
        q = [eye - m[h] + m2[h] - _mm(m[h], m2[h]) for h in H]
        u = [-_mm(q[h], tdr[h]) for h in H]
        uv = [jnp.concatenate([u[h], v_h[h]], axis=0) for h in H]
        y_heads = [w0s[h][L:2 * L] + _mm(g[h][L:2 * L, :], uv[h]) for h in H]
        for h in H:
            s_ref[0, h] = s0[h] * p_last[:, sls[h]] + _mm_tn(uv[h], rhs2_l[h])

        y_ref[rows, :] = _rwkv_out(y_heads, r, k2, v, z_ref[rows, :], r_k[...], gn[...])
        return 0

    lax.fori_loop(0, n_chunks, chunk, 0)


def _rwkv_params(mu, wb, w0, ab, a0, k_k, k_a, r_k, gn):
    row = lambda x: x.reshape(1, -1)
    zeros = jnp.zeros((R_LORA, R_W), F32)
    wb_p = jnp.concatenate([wb, zeros], axis=0).astype(BF16)
    ab_p = jnp.concatenate([zeros, ab], axis=0).astype(BF16)
    return (row(mu[0:R_W]), row(mu[R_W:2 * R_W]), row(mu[2 * R_W:3 * R_W]), row(mu[3 * R_W:]),
            wb_p, row(w0), ab_p, row(a0), row(k_k), row(k_a), row(r_k), row(gn))


def _rwkv_prompt(proj, params, batch, seq, tb):
    nt = seq // tb
    cb = lambda c, w: pl.BlockSpec((tb, w), lambda b, t, c=c: (b * nt + t, c // w))
    full = lambda a: pl.BlockSpec(a.shape, lambda b, t: (0,) * a.ndim)
    kern = functools.partial(_rwkv_prompt_kernel, n_chunks=tb // CHUNK)
    return pl.pallas_call(
        kern,
        grid=(batch, nt),
        in_specs=[cb(C_RZ, R_W), cb(C_RIN, R_W), cb(C_RIN + R_W, R_W), cb(C_RIN + 2 * R_W, R_W),
                  cb(C_RLORA, LANES)] + [full(a) for a in params],
        out_specs=[pl.BlockSpec((tb, R_W), lambda b, t: (b * nt + t, 0)),
                   pl.BlockSpec((1, R_HEADS, R_HD, R_HD), lambda b, t: (b, 0, 0, 0))],
        out_shape=[jax.ShapeDtypeStruct((batch * seq, R_W), F32),
                   jax.ShapeDtypeStruct((batch, R_HEADS, R_HD, R_HD), F32)],
        scratch_shapes=[pltpu.VMEM((8, R_SHIFT_W), F32)],
        compiler_params=_cparams(("parallel", "arbitrary")),
        name="rwkv_prompt",
    )(proj, proj, proj, proj, proj, *params)


def _rwkv_step_kernel(z_ref, r_ref, k_ref, v_ref, l_ref, pr_ref, pk_ref, pv_ref, pl_ref, s_in,
                      mu_r, mu_k, mu_v, mu_l, wb, w0, ab, a0, k_k, k_a, r_k, gn,
                      y_ref, s_out, vec_ref, yrow_ref, *, bb):
    params = tuple(x[...] for x in (mu_r, mu_k, mu_v, mu_l, wb, w0, ab, a0, k_k, k_a))
    r, k2, v, lw, a, kk = _rwkv_front(r_ref[...], k_ref[...], v_ref[...], l_ref[...],
                                      pr_ref[...], pk_ref[...], pv_ref[...], pl_ref[...], params)
    w = jnp.exp(lw)
    kaps, betas = [], []
    for h in range(R_HEADS):
        sl = slice(h * R_HD, (h + 1) * R_HD)
        kk_h = kk[:, sl]
        kap = kk_h / jnp.maximum(jnp.sqrt(_lane_sum(kk_h * kk_h)), 1e-12)
        kaps.append(kap)
        betas.append(kap * a[:, sl])
    vec_ref[0] = jnp.concatenate(kaps, axis=-1)
    vec_ref[1] = jnp.concatenate(betas, axis=-1)
    vec_ref[2] = w
    vec_ref[3] = k2
    vec_ref[4] = v
    vec_ref[5] = r
    eye = (_iota2((R_HD, R_HD), 0) == _iota2((R_HD, R_HD), 1)).astype(F32)

    def per_row(b, _):
        rowv = lambda i: vec_ref[i, pl.ds(b, 1), :]
        kap_b, beta_b, w_b, k_b, v_b, r_b = (rowv(i) for i in range(6))
        y_parts = []
        for h in range(R_HEADS):
            sl = slice(h * R_HD, (h + 1) * R_HD)
            s = s_in[b, h]
            v_col = _lane_sum(eye * v_b[:, sl])
            sa = -_lane_sum(s * kap_b[:, sl])
            s_new = s * w_b[:, sl] + sa * beta_b[:, sl] + v_col * k_b[:, sl]
            s_out[b, h] = s_new
            y_col = _lane_sum(s_new * r_b[:, sl])
            y_parts.append(jnp.sum(eye * y_col, axis=0, keepdims=True))
        yrow_ref[pl.ds(b, 1), :] = jnp.concatenate(y_parts, axis=-1)
        return 0

    lax.fori_loop(0, bb, per_row, 0)
    y = yrow_ref[...]
    y_heads = [y[:, h * R_HD:(h + 1) * R_HD] for h in range(R_HEADS)]
    y_ref[...] = _rwkv_out(y_heads, r, k2, v, z_ref[...], r_k[...], gn[...])


def _rwkv_step(proj, shift0, s0, params, bb):
    nb = proj.shape[0]
    cb = lambda c, w: pl.BlockSpec((bb, w), lambda i, c=c: (i, c // w))
    full = lambda a: pl.BlockSpec(a.shape, lambda i: (0,) * a.ndim)
    st = pl.BlockSpec((bb, R_HEADS, R_HD, R_HD), lambda i: (i, 0, 0, 0))
    kern = functools.partial(_rwkv_step_kernel, bb=bb)
    return pl.pallas_call(
        kern,
        grid=(nb // bb,),
        in_specs=[cb(C_RZ, R_W), cb(C_RIN, R_W), cb(C_RIN + R_W, R_W), cb(C_RIN + 2 * R_W, R_W),
                  cb(C_RLORA, LANES),
                  cb(0, R_W), cb(R_W, R_W), cb(2 * R_W, R_W), cb(3 * R_W, LANES), st]
                 + [full(a) for a in params],
        out_specs=[pl.BlockSpec((bb, R_W), lambda i: (i, 0)), st],
        out_shape=[jax.ShapeDtypeStruct((nb, R_W), F32),
                   jax.ShapeDtypeStruct(s0.shape, F32)],
        scratch_shapes=[pltpu.VMEM((6, bb, R_W), F32), pltpu.VMEM((bb, R_W), F32)],
        compiler_params=_cparams(("parallel",)),
        name="rwkv_step",
    )(proj, proj, proj, proj, proj, shift0, shift0, shift0, shift0, s0, *params)


def _gla_log_gate(small, ab_p, a_bias):
    return _log_sigmoid(_mm(small, ab_p) + a_bias) / G_GATE_TEMP


def _gla_out(o_heads, z, gn):
    outs = []
    for h in range(G_HEADS):
        o = o_heads[h]
        outs.append(o * lax.rsqrt(jnp.mean(o * o, axis=-1, keepdims=True) + EPS))
    return jnp.concatenate(outs, axis=-1) * gn * _silu(z)


def _gla_prompt_kernel(qk_ref, v_ref, z_ref, sm_ref, ab, a_bias, gn, y_ref, s_ref, st_ref, *, n_chunks):
    L = CHUNK
    last = pl.num_programs(1) - 1

    @pl.when(pl.program_id(1) == 0)
    def _():
        st_ref[...] = jnp.zeros_like(st_ref)

    row = _iota2((L, L), 0)
    col = _iota2((L, L), 1)
    tril_incl = col <= row
    tril_f = tril_incl.astype(F32)
    scale = G_DK ** -0.5

    def chunk(ci, _):
        rows = pl.ds(pl.multiple_of(ci * L, L), L)
        qk = qk_ref[rows, :]
        q, k = qk[:, 0:G_QK], qk[:, G_QK:2 * G_QK]
        v = v_ref[rows, :]
        la = _gla_log_gate(sm_ref[rows, :], ab[...], a_bias[...])
        cum = _mm_exact_lhs(tril_f, la)
        ref_row = cum[L // 2:L // 2 + 1, :]
        last_row = cum[L - 1:L, :]
        q0 = q * scale * jnp.exp(cum)
        qe = q * scale * jnp.exp(cum - ref_row)
        ke = k * jnp.exp(ref_row - cum)
        kl = k * jnp.exp(last_row - cum)
        e_last = jnp.exp(last_row)
        H = range(G_HEADS)
        sks = [slice(h * G_DK, (h + 1) * G_DK) for h in H]
        v_h = [v[:, h * G_DV:(h + 1) * G_DV] for h in H]
        st = [st_ref[h] for h in H]
        att = [jnp.where(tril_incl, _mm_nt(qe[:, sks[h]], ke[:, sks[h]]), 0.0) for h in H]
        o_st = [_mm_nt(q0[:, sks[h]], st[h]) for h in H]
        upd = [_mm_tn(v_h[h], kl[:, sks[h]]) for h in H]
        o_heads = [_mm(att[h], v_h[h]) + o_st[h] for h in H]
        for h in H:
            st_ref[h] = st[h] * e_last[:, sks[h]] + upd[h]
        y_ref[rows, :] = _gla_out(o_heads, z_ref[rows, :], gn[...])
        return 0

    lax.fori_loop(0, n_chunks, chunk, 0)

    @pl.when(pl.program_id(1) == last)
    def _():
        for h in range(G_HEADS):
            s_ref[0, h] = st_ref[h].T


def _gla_params(ab, a_bias, gn):
    ab_p = jnp.zeros((LANES, G_QK), F32).at[SM_GA:SM_GA + G_LORA].set(ab).astype(BF16)
    return ab_p, a_bias.reshape(1, -1), gn.reshape(1, -1)


def _gla_prompt(proj, params, batch, seq, tb):
    nt = seq // tb
    cb = lambda c, w: pl.BlockSpec((tb, w), lambda b, t, c=c: (b * nt + t, c // w))
    full = lambda a: pl.BlockSpec(a.shape, lambda b, t: (0,) * a.ndim)
    kern = functools.partial(_gla_prompt_kernel, n_chunks=tb // CHUNK)
    return pl.pallas_call(
        kern,
        grid=(batch, nt),
        in_specs=[cb(C_GQK, 2 * G_QK), cb(C_GV, G_W), cb(C_GZ, G_W), cb(C_SMALL, LANES)]
                 + [full(a) for a in params],
        out_specs=[pl.BlockSpec((tb, G_W), lambda b, t: (b * nt + t, 0)),
                   pl.BlockSpec((1, G_HEADS, G_DK, G_DV), lambda b, t: (b, 0, 0, 0))],
        out_shape=[jax.ShapeDtypeStruct((batch * seq, G_W), F32),
                   jax.ShapeDtypeStruct((batch, G_HEADS, G_DK, G_DV), F32)],
        scratch_shapes=[pltpu.VMEM((G_HEADS, G_DV, G_DK), F32)],
        compiler_params=_cparams(("parallel", "arbitrary")),
        name="gla_prompt",
    )(proj, proj, proj, proj, *params)


def _gla_step_kernel(qk_ref, v_ref, z_ref, sm_ref, s_in, ab, a_bias, gn, y_ref, s_out,
                     vec_ref, vrow_ref, orow_ref, *, bb):
    qk = qk_ref[...]
    q = qk[:, 0:G_QK] * (G_DK ** -0.5)
    k = qk[:, G_QK:2 * G_QK]
    g = _gla_log_gate(sm_ref[...], ab[...], a_bias[...])
    vec_ref[0] = q
    vec_ref[1] = k
    vec_ref[2] = jnp.exp(g)
    vrow_ref[...] = v_ref[...]
    eye = (_iota2((G_DK, G_DK), 0) == _iota2((G_DK, G_DK), 1)).astype(F32)

    def per_row(b, _):
        q_b, k_b, e_b = (vec_ref[i, pl.ds(b, 1), :] for i in range(3))
        v_b = vrow_ref[pl.ds(b, 1), :]
        o_parts = []
        for h in range(G_HEADS):
            sk = slice(h * G_DK, (h + 1) * G_DK)
            sv = slice(h * G_DV, (h + 1) * G_DV)
            s = s_in[b, h]
            q_col = _lane_sum(eye * q_b[:, sk])
            k_col = _lane_sum(eye * k_b[:, sk])
            e_col = _lane_sum(eye * e_b[:, sk])
            v_h = v_b[:, sv]
            qk_dot = _lane_sum(q_b[:, sk] * k_b[:, sk])
            o_parts.append(qk_dot * v_h + jnp.sum((q_col * e_col) * s, axis=0, keepdims=True))
            s_out[b, h] = s * e_col + k_col * v_h
        orow_ref[pl.ds(b, 1), :] = jnp.concatenate(o_parts, axis=-1)
        return 0

    lax.fori_loop(0, bb, per_row, 0)
    o = orow_ref[...]
    o_heads = [o[:, h * G_DV:(h + 1) * G_DV] for h in range(G_HEADS)]
    y_ref[...] = _gla_out(o_heads, z_ref[...], gn[...])


def _gla_step(proj, s0, params, bb):
    nb = proj.shape[0]
    cb = lambda c, w: pl.BlockSpec((bb, w), lambda i, c=c: (i, c // w))
    full = lambda a: pl.BlockSpec(a.shape, lambda i: (0,) * a.ndim)
    st = pl.BlockSpec((bb, G_HEADS, G_DK, G_DV), lambda i: (i, 0, 0, 0))
    kern = functools.partial(_gla_step_kernel, bb=bb)
    return pl.pallas_call(
        kern,
        grid=(nb // bb,),
        in_specs=[cb(C_GQK, 2 * G_QK), cb(C_GV, G_W), cb(C_GZ, G_W), cb(C_SMALL, LANES), st]
                 + [full(a) for a in params],
        out_specs=[pl.BlockSpec((bb, G_W), lambda i: (i, 0)), st],
        out_shape=[jax.ShapeDtypeStruct((nb, G_W), F32), jax.ShapeDtypeStruct(s0.shape, F32)],
        scratch_shapes=[pltpu.VMEM((3, bb, G_QK), F32), pltpu.VMEM((bb, G_W), F32),
                        pltpu.VMEM((bb, G_W), F32)],
        compiler_params=_cparams(("parallel",)),
        name="gla_step",
    )(proj, proj, proj, proj, s0, *params)


def _mlstm_out(h_heads, o_pre, z, gn):
    outs = []
    for h in range(M_HEADS):
        y = h_heads[h]
        y = y - jnp.mean(y, axis=-1, keepdims=True)
        outs.append(y * lax.rsqrt(jnp.mean(y * y, axis=-1, keepdims=True) + EPS))
    return jnp.concatenate(outs, axis=-1) * gn * _sigmoid(o_pre) * _silu(z)


def _mlstm_prompt_kernel(q_ref, k_ref, v_ref, o_ref, z_ref, sm_ref, ifb, gn,
                         y_ref, c_ref, n_ref, m_ref, *, n_chunks):
    L = CHUNK

    @pl.when(pl.program_id(1) == 0)
    def _():
        c_ref[...] = jnp.zeros_like(c_ref)
        n_ref[...] = jnp.zeros_like(n_ref)
        m_ref[...] = jnp.zeros_like(m_ref)

    row = _iota2((L, L), 0)
    col = _iota2((L, L), 1)
    tril_incl = col <= row
    tril_f = tril_incl.astype(F32)
    scale = M_DK ** -0.5
    neg_inf = -jnp.inf

    def chunk(ci, _):
        rows = pl.ds(pl.multiple_of(ci * L, L), L)
        pre = sm_ref[rows, :] + ifb[...]
        lf = _log_sigmoid(pre)
        b_all = _mm_exact_lhs(tril_f, lf)
        lib_all = pre - pltpu.roll(b_all, LANES - (SM_MF - SM_MI), 1)
        lib_t = lib_all.T
        bcum = b_all[:, SM_MF:SM_MF + M_HEADS]
        lib = lib_all[:, SM_MI:SM_MI + M_HEADS]
        b_last = bcum[L - 1:L, :]
        q, k, v = q_ref[rows, :], k_ref[rows, :], v_ref[rows, :]
        H = range(M_HEADS)
        sls = [slice(h * M_DK, (h + 1) * M_DK) for h in H]
        qs = [q[:, sl] * scale for sl in sls]
        k_h = [k[:, sl] for sl in sls]
        v_h = [v[:, sl] for sl in sls]
        c = [c_ref[0, h] for h in H]
        n = [n_ref[0, h:h + 1, :] for h in H]
        qk_raw = [_mm_nt(qs[h], k_h[h]) for h in H]
        qc = [_mm(qs[h], c[h]) for h in H]
        d_exp, w_inter, m_t, wk, carry = [], [], [], [], []
        for h in H:
            b_col = bcum[:, h:h + 1]
            m_prev = m_ref[0, :, h:h + 1]
            d_log = jnp.where(tril_incl, b_col + lib_t[SM_MI + h:SM_MI + h + 1, :], neg_inf)
            m_inter = b_col + m_prev
            mt = jnp.maximum(m_inter, jnp.max(d_log, axis=-1, keepdims=True))
            m_t.append(mt)
            w_inter.append(jnp.exp(m_inter - mt))
            d_exp.append(jnp.exp(d_log - mt))
            m_new = mt[L - 1:L, :]
            w_state = jnp.exp(b_last[:, h:h + 1] + lib[:, h:h + 1] - m_new)
            carry.append(jnp.exp(b_last[:, h:h + 1] + m_prev - m_new))
            wk.append(w_state * k_h[h])
            m_ref[0, :, h:h + 1] = m_new
        upd = [_mm_tn(wk[h], v_h[h]) for h in H]
        qk = [qk_raw[h] * d_exp[h] for h in H]
        qkv = [_mm(qk[h], v_h[h]) for h in H]
        h_heads = []
        for h in H:
            num = qkv[h] + w_inter[h] * qc[h]
            den = _lane_sum(qk[h]) + w_inter[h] * _lane_sum(qs[h] * n[h])
            h_heads.append(num / jnp.maximum(jnp.abs(den), jnp.exp(-m_t[h])))
            c_ref[0, h] = carry[h] * c[h] + upd[h]
            n_ref[0, h:h + 1, :] = carry[h] * n[h] + jnp.sum(wk[h], axis=0, keepdims=True)
        y_ref[rows, :] = _mlstm_out(h_heads, o_ref[rows, :], z_ref[rows, :], gn[...])
        return 0

    lax.fori_loop(0, n_chunks, chunk, 0)


def _mlstm_params(ifb, gn):
    ifb_p = jnp.zeros((1, LANES), F32).at[0, SM_MI:SM_MI + 2 * M_HEADS].set(ifb)
    return ifb_p, gn.reshape(1, -1)


def _mlstm_prompt(proj, params, batch, seq, tb):
    nt = seq // tb
    cb = lambda c, w: pl.BlockSpec((tb, w), lambda b, t, c=c: (b * nt + t, c // w))
    full = lambda a: pl.BlockSpec(a.shape, lambda b, t: (0,) * a.ndim)
    kern = functools.partial(_mlstm_prompt_kernel, n_chunks=tb // CHUNK)
    return pl.pallas_call(
        kern,
        grid=(batch, nt),
        in_specs=[cb(C_MQ, M_QK), cb(C_MK, M_QK), cb(C_MV, M_W), cb(C_MO, M_W), cb(C_MZ, M_W),
                  cb(C_SMALL, LANES)] + [full(a) for a in params],
        out_specs=[pl.BlockSpec((tb, M_W), lambda b, t: (b * nt + t, 0)),
                   pl.BlockSpec((1, M_HEADS, M_DK, M_DV), lambda b, t: (b, 0, 0, 0)),
                   pl.BlockSpec((1, M_HEADS, M_DK), lambda b, t: (b, 0, 0)),
                   pl.BlockSpec((1, 1, M_HEADS), lambda b, t: (b, 0, 0))],
        out_shape=[jax.ShapeDtypeStruct((batch * seq, M_W), F32),
                   jax.ShapeDtypeStruct((batch, M_HEADS, M_DK, M_DV), F32),
                   jax.ShapeDtypeStruct((batch, M_HEADS, M_DK), F32),
                   jax.ShapeDtypeStruct((batch, 1, M_HEADS), F32)],
        compiler_params=_cparams(("parallel", "arbitrary")),
        name="mlstm_prompt",
    )(proj, proj, proj, proj, proj, proj, *params)


def _mlstm_step_kernel(q_ref, k_ref, v_ref, o_ref, z_ref, sm_ref, c_in, n_in, m_in, ifb, gn,
                       y_ref, c_out, n_out, m_out, vec_ref, sc_ref, hrow_ref, *, bb):
    pre = sm_ref[...] + ifb[...]
    li = pre[:, SM_MI:SM_MI + M_HEADS]
    lf = _log_sigmoid(pre[:, SM_MF:SM_MF + M_HEADS])
    m0 = m_in[...]
    m_inter = lf + m0
    m_t = jnp.maximum(m_inter, li)
    m_out[...] = m_t
    pad = lambda x: jnp.concatenate([x, jnp.zeros((bb, LANES - M_HEADS), F32)], axis=-1)
    sc_ref[0] = pad(jnp.exp(m_inter - m_t))
    sc_ref[1] = pad(jnp.exp(li - m_t))
    sc_ref[2] = pad(jnp.exp(-m_t))
    vec_ref[0] = q_ref[...] * (M_DK ** -0.5)
    vec_ref[1] = k_ref[...]
    vec_ref[2] = v_ref[...]
    eye = (_iota2((M_DK, M_DK), 0) == _iota2((M_DK, M_DK), 1)).astype(F32)

    def per_row(b, _):
        q_b, k_b, v_b = (vec_ref[i, pl.ds(b, 1), :] for i in range(3))
        wi_b, ei_b, en_b = (sc_ref[i, pl.ds(b, 1), :] for i in range(3))
        h_parts = []
        for h in range(M_HEADS):
            sl = slice(h * M_DK, (h + 1) * M_DK)
            q_h, k_h, v_h = q_b[:, sl], k_b[:, sl], v_b[:, sl]
            wi, ei, en = wi_b[:, h:h + 1], ei_b[:, h:h + 1], en_b[:, h:h + 1]
            c = c_in[b, h]
            n = n_in[b, h:h + 1, :]
            q_col = _lane_sum(eye * q_h)
            k_col = _lane_sum(eye * k_h)
            qk = _lane_sum(q_h * k_h) * ei
            num = qk * v_h + wi * jnp.sum(q_col * c, axis=0, keepdims=True)
            den = qk + wi * _lane_sum(q_h * n)
            h_parts.append(num / jnp.maximum(jnp.abs(den), en))
            c_out[b, h] = wi * c + (ei * k_col) * v_h
            n_out[b, h:h + 1, :] = wi * n + ei * k_h
        hrow_ref[pl.ds(b, 1), :] = jnp.concatenate(h_parts, axis=-1)
        return 0

    lax.fori_loop(0, bb, per_row, 0)
    hh = hrow_ref[...]
    h_heads = [hh[:, h * M_DV:(h + 1) * M_DV] for h in range(M_HEADS)]
    y_ref[...] = _mlstm_out(h_heads, o_ref[...], z_ref[...], gn[...])


def _mlstm_step(proj, c0, n0, m0, params, bb):
    nb = proj.shape[0]
    cb = lambda c, w: pl.BlockSpec((bb, w), lambda i, c=c: (i, c // w))
    full = lambda a: pl.BlockSpec(a.shape, lambda i: (0,) * a.ndim)
    cs = pl.BlockSpec((bb, M_HEADS, M_DK, M_DV), lambda i: (i, 0, 0, 0))
    ns = pl.BlockSpec((bb, M_HEADS, M_DK), lambda i: (i, 0, 0))
    ms = pl.BlockSpec((bb, M_HEADS), lambda i: (i, 0))
    kern = functools.partial(_mlstm_step_kernel, bb=bb)
    return pl.pallas_call(
        kern,
        grid=(nb // bb,),
        in_specs=[cb(C_MQ, M_QK), cb(C_MK, M_QK), cb(C_MV, M_W), cb(C_MO, M_W), cb(C_MZ, M_W),
                  cb(C_SMALL, LANES), cs, ns, ms] + [full(a) for a in params],
        out_specs=[pl.BlockSpec((bb, M_W), lambda i: (i, 0)), cs, ns, ms],
        out_shape=[jax.ShapeDtypeStruct((nb, M_W), F32), jax.ShapeDtypeStruct(c0.shape, F32),
                   jax.ShapeDtypeStruct(n0.shape, F32), jax.ShapeDtypeStruct(m0.shape, F32)],
        scratch_shapes=[pltpu.VMEM((3, bb, M_QK), F32), pltpu.VMEM((3, bb, LANES), F32),
                        pltpu.VMEM((bb, M_W), F32)],
        compiler_params=_cparams(("parallel",)),
        name="mlstm_step",
    )(proj, proj, proj, proj, proj, proj, c0, n0, m0, *params)


def _pad_w_in(w_in):
    sizes = (R_SHIFT_W, R_W, G_QK, G_QK, G_W, G_LORA, G_W, M_QK, M_QK, M_W, M_HEADS, M_HEADS, M_W, M_W,
             3 * D_MODEL)
    offs = [0]
    for s in sizes:
        offs.append(offs[-1] + s)
    seg = lambda i: w_in[..., offs[i]:offs[i + 1]]
    (r_in, r_z, g_q, g_k, g_v, g_a, g_z, m_q, m_k, m_v, m_i, m_f, m_o, m_z, gate) = (seg(i) for i in range(15))
    zeros = lambda n: jnp.zeros(w_in.shape[:-1] + (n,), w_in.dtype)
    small = jnp.concatenate([g_a, m_i, m_f, zeros(LANES - G_LORA - 2 * M_HEADS)], axis=-1)
    w = jnp.concatenate([gate, r_z, g_v, g_z, m_q, m_k, m_v, m_o, m_z, g_q, g_k, r_in, small,
                         zeros(N_PAD - N_USED)], axis=-1)
    return w.astype(BF16)


def _pick(n, prefs):
    for p in prefs:
        if n % p == 0:
            return p
    return n


def kernel(x_prompt, x_sample, state_rwkv_shift, state_rwkv, state_gla, state_mlstm_C, state_mlstm_n,
           state_mlstm_m, norm_pre, norm_post, w_in, r_mu_shift, r_w_lora_b, r_w0, r_a_lora_b, r_a0,
           r_k_k, r_k_a, r_r_k, r_gn, g_a_lora_b, g_a_bias, g_gn, m_if_bias, m_gn, w_br_rwkv, w_br_gla,
           w_br_mlstm, w_out):
    depth = w_in.shape[0]
    bp, tp, _ = x_prompt.shape
    bs, ts, _ = x_sample.shape
    assert ts == 1 and tp % CHUNK == 0
    w_pad = _pad_w_in(w_in)
    wr, wg, wm, wo = (w.astype(BF16) for w in (w_br_rwkv, w_br_gla, w_br_mlstm, w_out))

    mp = bp * tp
    tm_in = _pick(mp, (2048, 1024, 512, 256, 128, 64))
    tm_mg = _pick(mp, (512, 256, 128, 64))
    tb = _pick(tp, (256, 128, 64))
    bb = _pick(bs, (8,))

    xp = x_prompt.reshape(mp, D_MODEL)
    xs = x_sample.reshape(bs, D_MODEL)
    new_p = [[] for _ in range(6)]
    new_s = [[] for _ in range(6)]
    for l in range(depth):
        g_pre = norm_pre[l].reshape(1, -1)
        g_post = norm_post[l].reshape(1, -1)
        rp = _rwkv_params(r_mu_shift[l], r_w_lora_b[l], r_w0[l], r_a_lora_b[l], r_a0[l], r_k_k[l],
                          r_k_a[l], r_r_k[l], r_gn[l])
        gp = _gla_params(g_a_lora_b[l], g_a_bias[l], g_gn[l])
        mparams = _mlstm_params(m_if_bias[l], m_gn[l])

        proj = _inproj(xp, g_pre, w_pad, l, tm_in)
        y_r, s_r = _rwkv_prompt(proj, rp, bp, tp, tb)
        y_g, s_g = _gla_prompt(proj, gp, bp, tp, tb)
        y_m, c, n, m = _mlstm_prompt(proj, mparams, bp, tp, tb)
        xp = _merge(proj, y_r, y_g, y_m, xp, wr, wg, wm, wo, g_post, l, tm_mg)
        shift = proj.reshape(bp, tp, N_PAD)[:, tp - 1, C_RIN:C_RIN + R_SHIFT_W]
        for lst, val in zip(new_p, (shift, s_r, s_g, c, n, m.reshape(bp, M_HEADS))):
            lst.append(val)

        proj = _inproj(xs, g_pre, w_pad, l, bs)
        y_r, s_r = _rwkv_step(proj, state_rwkv_shift[l], state_rwkv[l], rp, bb)
        y_g, s_g = _gla_step(proj, state_gla[l], gp, bb)
        y_m, c, n, m = _mlstm_step(proj, state_mlstm_C[l], state_mlstm_n[l], state_mlstm_m[l], mparams, bb)
        xs = _merge(proj, y_r, y_g, y_m, xs, wr, wg, wm, wo, g_post, l, bs)
        shift = proj[:, C_RIN:C_RIN + R_SHIFT_W]
        for lst, val in zip(new_s, (shift, s_r, s_g, c, n, m)):
            lst.append(val)

    stk = lambda lst: jnp.stack(lst)
    return (xp.reshape(bp, tp, D_MODEL), xs.reshape(bs, ts, D_MODEL),
            *(stk(v) for v in new_p), *(stk(v) for v in new_s))
```

```python
import functools
import math

import jax
import jax.numpy as jnp
from jax import lax
from jax.experimental import pallas as pl
from jax.experimental.pallas import tpu as pltpu

F32 = jnp.float32
BF16 = jnp.bfloat16

D_MODEL = 1024
R_HEADS, R_HD = 8, 64
R_W = R_HEADS * R_HD
R_LORA = 64
R_SHIFT_W = 3 * R_W + 2 * R_LORA
R_GN_EPS = 64e-5
G_HEADS, G_DK, G_DV = 4, 64, 128
G_QK = G_HEADS * G_DK
G_W = G_HEADS * G_DV
G_LORA = 16
G_GATE_TEMP = 16.0
M_HEADS, M_DK, M_DV = 4, 128, 128
M_QK = M_HEADS * M_DK
M_W = M_HEADS * M_DV
EPS = 1e-6

LANES = 128
VMEM_LIMIT = 48 * 1024 * 1024

C_GATE = 0
C_RZ = 3072
C_GV = 3584
C_GZ = 4096
C_MQ = 4608
C_MK = 5120
C_MV = 5632
C_MO = 6144
C_MZ = 6656
C_GQK = 7168
C_RIN = 7680
C_RLORA = C_RIN + 3 * R_W
C_SMALL = C_RLORA + LANES
SM_GA, SM_MI, SM_MF = 0, 16, 20
N_USED = C_SMALL + LANES
PROJ_TN = 512
N_PAD = -(-N_USED // PROJ_TN) * PROJ_TN

CHUNK = 64


def _cparams(sem):
    return pltpu.CompilerParams(dimension_semantics=sem, vmem_limit_bytes=VMEM_LIMIT)


def _mm(a, b):
    return jnp.dot(a.astype(BF16), b.astype(BF16), preferred_element_type=F32)


def _mm_nt(a, b):
    return lax.dot_general(a.astype(BF16), b.astype(BF16), (((1,), (1,)), ((), ())),
                           preferred_element_type=F32)


def _mm_tn(a, b):
    return lax.dot_general(a.astype(BF16), b.astype(BF16), (((0,), (0,)), ((), ())),
                           preferred_element_type=F32)


def _mm_exact_lhs(m01, x):
    hi = x.astype(BF16)
    r1 = x - hi.astype(F32)
    mid = r1.astype(BF16)
    lo = (r1 - mid.astype(F32)).astype(BF16)
    m = m01.astype(BF16)
    d = lambda p: jnp.dot(m, p, preferred_element_type=F32)
    return d(hi) + d(mid) + d(lo)


def _sigmoid(x):
    return 1.0 / (1.0 + jnp.exp(-x))


def _silu(x):
    return x * _sigmoid(x)


def _softplus(x):
    return jnp.maximum(x, 0.0) + jnp.log(1.0 + jnp.exp(-jnp.abs(x)))


def _log_sigmoid(x):
    return -_softplus(-x)


def _iota2(shape, dim):
    return lax.broadcasted_iota(jnp.int32, shape, dim)


def _lane_sum(x):
    return jnp.sum(x, axis=-1, keepdims=True)


def _block_tril(n, blk):
    r, c = _iota2((n, n), 0), _iota2((n, n), 1)
    return ((r - c).astype(jnp.uint32) <= (r % blk).astype(jnp.uint32)).astype(F32)


def _inproj_kernel(x_ref, g_ref, w_ref, o_ref, h_ref):
    @pl.when(pl.program_id(1) == 0)
    def _():
        x = x_ref[...]
        y = x * lax.rsqrt(jnp.mean(x * x, axis=-1, keepdims=True) + EPS)
        h_ref[...] = (y * g_ref[...]).astype(BF16)

    o_ref[...] = jnp.dot(h_ref[...], w_ref[...], preferred_element_type=F32)


def _inproj(x2d, g_pre, w_pad, layer, tm):
    m = x2d.shape[0]
    return pl.pallas_call(
        _inproj_kernel,
        grid=(m // tm, N_PAD // PROJ_TN),
        in_specs=[pl.BlockSpec((tm, D_MODEL), lambda i, j: (i, 0)),
                  pl.BlockSpec((1, D_MODEL), lambda i, j: (0, 0)),
                  pl.BlockSpec((None, D_MODEL, PROJ_TN), lambda i, j: (layer, 0, j))],
        out_specs=pl.BlockSpec((tm, PROJ_TN), lambda i, j: (i, j)),
        out_shape=jax.ShapeDtypeStruct((m, N_PAD), F32),
        scratch_shapes=[pltpu.VMEM((tm, D_MODEL), BF16)],
        compiler_params=_cparams(("parallel", "arbitrary")),
        name="inproj",
    )(x2d, g_pre, w_pad)


def _merge_kernel(gate_ref, yr_ref, yg_ref, ym_ref, x_ref, wr_ref, wg_ref, wm_ref, wo_ref, gp_ref, o_ref):
    d = D_MODEL
    merged = (_sigmoid(gate_ref[:, 0:d]) * _mm(yr_ref[...], wr_ref[...])
              + _sigmoid(gate_ref[:, d:2 * d]) * _mm(yg_ref[...], wg_ref[...])
              + _sigmoid(gate_ref[:, 2 * d:3 * d]) * _mm(ym_ref[...], wm_ref[...]))
    o = _mm(merged, wo_ref[...])
    y = o * lax.rsqrt(jnp.mean(o * o, axis=-1, keepdims=True) + EPS)
    o_ref[...] = x_ref[...] + y * gp_ref[...]


def _merge(proj, y_r, y_g, y_m, x2d, w_r, w_g, w_m, w_o, g_post, layer, tm):
    m = x2d.shape[0]
    row = lambda w: pl.BlockSpec((tm, w), lambda i: (i, 0))
    full = lambda a: pl.BlockSpec(a.shape, lambda i: (0, 0))
    wl = lambda a: pl.BlockSpec((None,) + a.shape[1:], lambda i: (layer, 0, 0))
    return pl.pallas_call(
        _merge_kernel,
        grid=(m // tm,),
        in_specs=[row(3 * D_MODEL), row(R_W), row(G_W), row(M_W), row(D_MODEL),
                  wl(w_r), wl(w_g), wl(w_m), wl(w_o), full(g_post)],
        out_specs=row(D_MODEL),
        out_shape=jax.ShapeDtypeStruct((m, D_MODEL), F32),
        compiler_params=_cparams(("parallel",)),
        name="merge",
    )(proj, y_r, y_g, y_m, x2d, w_r, w_g, w_m, w_o, g_post)


def _rwkv_front(r_in, k_in, v_in, l_in, prev_r, prev_k, prev_v, prev_l, p):
    (mu_r, mu_k, mu_v, mu_l, wb, w0, ab, a0, k_k, k_a) = p
    r = r_in + mu_r * (prev_r - r_in)
    k = k_in + mu_k * (prev_k - k_in)
    v = v_in + mu_v * (prev_v - v_in)
    lo = l_in + mu_l * (prev_l - l_in)
    log_w = -_softplus(-(w0 + _mm(jnp.tanh(lo), wb))) - 0.5
    lw = -jnp.exp(log_w)
    a = _sigmoid(a0 + _mm(lo, ab))
    kk = k * k_k
    k2 = k * (1.0 + (a - 1.0) * k_a)
    return r, k2, v, lw, a, kk


def _rwkv_out(y_heads, r, k2, v, z, r_k, gn):
    outs = []
    for h in range(R_HEADS):
        sl = slice(h * R_HD, (h + 1) * R_HD)
        y = y_heads[h]
        y = y - jnp.mean(y, axis=-1, keepdims=True)
        y = y * lax.rsqrt(jnp.mean(y * y, axis=-1, keepdims=True) + R_GN_EPS)
        bonus = _lane_sum(r[:, sl] * k2[:, sl] * r_k[:, sl]) * v[:, sl]
        outs.append(y * gn[:, sl] + bonus)
    return jnp.concatenate(outs, axis=-1) * _silu(z)


def _rwkv_prompt_kernel(z_ref, r_ref, k_ref, v_ref, l_ref,
                        mu_r, mu_k, mu_v, mu_l, wb, w0, ab, a0, k_k, k_a, r_k, gn,
                        y_ref, s_ref, carry_ref, *, n_chunks):
    L = CHUNK

    @pl.when(pl.program_id(1) == 0)
    def _():
        s_ref[...] = jnp.zeros_like(s_ref)
        carry_ref[...] = jnp.zeros_like(carry_ref)

    tb = n_chunks * L
    params = tuple(x[...] for x in (mu_r, mu_k, mu_v, mu_l, wb, w0, ab, a0, k_k, k_a))
    row = _iota2((L, L), 0)
    col = _iota2((L, L), 1)
    eye = (col == row).astype(F32)
    blk_mask = (row // 16) == (col // 16)
    r2 = _iota2((2 * L, 2 * L), 0)
    c2 = _iota2((2 * L, 2 * L), 1)
    tr, tc = r2 % L, c2 % L
    g_mask = tc < tr + (r2 >= L).astype(jnp.int32)
    first_row = _iota2((tb, 1), 0) == 0

    def shifted(x, carry):
        return jnp.where(first_row, carry, pltpu.roll(x, 1, 0))

    r_in, k_in, v_in, l_in = r_ref[...], k_ref[...], v_ref[...], l_ref[...]
    prev_r = shifted(r_in, carry_ref[0:1, 0:R_W])
    prev_k = shifted(k_in, carry_ref[0:1, R_W:2 * R_W])
    prev_v = shifted(v_in, carry_ref[0:1, 2 * R_W:3 * R_W])
    prev_l = shifted(l_in, carry_ref[0:1, 3 * R_W:R_SHIFT_W])
    carry_ref[0:1, 0:R_W] = r_in[tb - 1:tb, :]
    carry_ref[0:1, R_W:2 * R_W] = k_in[tb - 1:tb, :]
    carry_ref[0:1, 2 * R_W:3 * R_W] = v_in[tb - 1:tb, :]
    carry_ref[0:1, 3 * R_W:R_SHIFT_W] = l_in[tb - 1:tb, :]
    r, k2, v, lw, a, kk = _rwkv_front(r_in, k_in, v_in, l_in, prev_r, prev_k, prev_v, prev_l, params)

    cum = _mm_exact_lhs(_block_tril(tb, L), lw)
    p_inc = jnp.exp(cum)
    p_exc = jnp.exp(cum - lw)
    p_inv = jnp.exp(-cum)
    r_t = r * p_inc
    k_h = k2 * p_inv

    CH = [(c, h) for c in range(n_chunks) for h in range(R_HEADS)]
    N = range(len(CH))
    lhs2, rhs2, rhs2_l, v_h, p_last = [], [], [], [], []
    for h in range(R_HEADS):
        sl = slice(h * R_HD, (h + 1) * R_HD)
        kk_h = kk[:, sl]
        kap = kk_h * (1.0 / jnp.maximum(jnp.sqrt(_lane_sum(kk_h * kk_h)), 1e-12))
        beta_h = kap * a[:, sl] * p_inv[:, sl]
        kap_t = kap * p_exc[:, sl]
        lhs2.append([jnp.concatenate([kap_t[c * L:(c + 1) * L], r_t[c * L:(c + 1) * L, sl]], axis=0)
                     for c in range(n_chunks)])
        rhs2.append([jnp.concatenate([beta_h[c * L:(c + 1) * L], k_h[c * L:(c + 1) * L, sl]], axis=0)
                     for c in range(n_chunks)])
        p_last.append([p_inc[(c + 1) * L - 1:(c + 1) * L, sl] for c in range(n_chunks)])
        v_h.append([v[c * L:(c + 1) * L, sl] for c in range(n_chunks)])
    lhs2 = [lhs2[h][c] for c, h in CH]
    rhs2 = [rhs2[h][c] for c, h in CH]
    p_last = [p_last[h][c] for c, h in CH]
    v_h = [v_h[h][c] for c, h in CH]
    rhs2_l = [rhs2[i] * p_last[i] for i in N]
    g = [jnp.where(g_mask, _mm_nt(lhs2[i], rhs2[i]), 0.0) for i in N]
    a_m = [g[i][0:L, 0:L] for i in N]
    d = [jnp.where(blk_mask, a_m[i], 0.0) for i in N]
    akv = [_mm(g[i][0:L, L:2 * L], v_h[i]) for i in N]
    d2 = [_mm(d[i], d[i]) for i in N]
    d4 = [_mm(d2[i], d2[i]) for i in N]
    dd2 = [_mm(d[i], d2[i]) for i in N]
    d8 = [_mm(d4[i], d4[i]) for i in N]
    x = [eye - d[i] + d2[i] - dd2[i] for i in N]
    x = [x[i] + _mm(x[i], d4[i]) for i in N]
    td = [x[i] + _mm(x[i], d8[i]) for i in N]
    m = [_mm(td[i], a_m[i] - d[i]) for i in N]
    m2 = [_mm(m[i], m[i]) for i in N]
    q = [eye - m[i] + m2[i] - _mm(m[i], m2[i]) for i in N]
    t_inv = [_mm(q[i], td[i]) for i in N]
    kbar = [_mm(t_inv[i], lhs2[i][0:L]) for i in N]
    u0 = [-_mm(t_inv[i], akv[i]) for i in N]
    uv0 = [jnp.concatenate([u0[i], v_h[i]], axis=0) for i in N]
    n_c = [_mm_tn(kbar[i], rhs2_l[i][0:L]) for i in N]
    c_c = [_mm_tn(uv0[i], rhs2_l[i]) for i in N]
    qt = [lhs2[i][L:2 * L] - _mm(g[i][L:2 * L, 0:L], kbar[i]) for i in N]
    y0 = [_mm(g[i][L:2 * L, :], uv0[i]) for i in N]
    s = [s_ref[0, h] for h in range(R_HEADS)]
    y_ch = []
    for c in range(n_chunks):
        ids = [c * R_HEADS + h for h in range(R_HEADS)]
        y_ch.append([_mm_nt(qt[i], s[h]) + y0[i] for h, i in enumerate(ids)])
        s = [s[h] * p_last[i] - _mm(s[h], n_c[i]) + c_c[i] for h, i in enumerate(ids)]
    for h in range(R_HEADS):
        s_ref[0, h] = s[h]
    y_heads = [jnp.concatenate([y_ch[c][h] for c in range(n_chunks)], axis=0) for h in range(R_HEADS)]
    y_ref[...] = _rwkv_out(y_heads, r, k2, v, z_ref[...], r_k[...], gn[...])


def _rwkv_params(mu, wb, w0, ab, a0, k_k, k_a, r_k, gn):
    row = lambda x: x.reshape(1, -1)
    zeros = jnp.zeros((R_LORA, R_W), F32)
    wb_p = jnp.concatenate([wb, zeros], axis=0).astype(BF16)
    ab_p = jnp.concatenate([zeros, ab], axis=0).astype(BF16)
    return (row(mu[0:R_W]), row(mu[R_W:2 * R_W]), row(mu[2 * R_W:3 * R_W]), row(mu[3 * R_W:]),
            wb_p, row(w0), ab_p, row(a0), row(k_k), row(k_a), row(r_k), row(gn))


def _rwkv_prompt(proj, params, batch, seq, tb):
    nt = seq // tb
    cb = lambda c, w: pl.BlockSpec((tb, w), lambda b, t, c=c: (b * nt + t, c // w))
    full = lambda a: pl.BlockSpec(a.shape, lambda b, t: (0,) * a.ndim)
    kern = functools.partial(_rwkv_prompt_kernel, n_chunks=tb // CHUNK)
    return pl.pallas_call(
        kern,
        grid=(batch, nt),
        in_specs=[cb(C_RZ, R_W), cb(C_RIN, R_W), cb(C_RIN + R_W, R_W), cb(C_RIN + 2 * R_W, R_W),
                  cb(C_RLORA, LANES)] + [full(a) for a in params],
        out_specs=[pl.BlockSpec((tb, R_W), lambda b, t: (b * nt + t, 0)),
                   pl.BlockSpec((1, R_HEADS, R_HD, R_HD), lambda b, t: (b, 0, 0, 0))],
        out_shape=[jax.ShapeDtypeStruct((batch * seq, R_W), F32),
                   jax.ShapeDtypeStruct((batch, R_HEADS, R_HD, R_HD), F32)],
        scratch_shapes=[pltpu.VMEM((8, R_SHIFT_W), F32)],
        compiler_params=_cparams(("parallel", "arbitrary")),
        name="rwkv_prompt",
    )(proj, proj, proj, proj, proj, *params)


def _rwkv_step_kernel(z_ref, r_ref, k_ref, v_ref, l_ref, pr_ref, pk_ref, pv_ref, pl_ref, s_in,
                      mu_r, mu_k, mu_v, mu_l, wb, w0, ab, a0, k_k, k_a, r_k, gn,
                      y_ref, s_out, vec_ref, yrow_ref, *, bb):
    params = tuple(x[...] for x in (mu_r, mu_k, mu_v, mu_l, wb, w0, ab, a0, k_k, k_a))
    r, k2, v, lw, a, kk = _rwkv_front(r_ref[...], k_ref[...], v_ref[...], l_ref[...],
                                      pr_ref[...], pk_ref[...], pv_ref[...], pl_ref[...], params)
    w = jnp.exp(lw)
    kaps, betas = [], []
    for h in range(R_HEADS):
        sl = slice(h * R_HD, (h + 1) * R_HD)
        kk_h = kk[:, sl]
        kap = kk_h / jnp.maximum(jnp.sqrt(_lane_sum(kk_h * kk_h)), 1e-12)
        kaps.append(kap)
        betas.append(kap * a[:, sl])
    vec_ref[0] = jnp.concatenate(kaps, axis=-1)
    vec_ref[1] = jnp.concatenate(betas, axis=-1)
    vec_ref[2] = w
    vec_ref[3] = k2
    vec_ref[4] = v
    vec_ref[5] = r
    eye = (_iota2((R_HD, R_HD), 0) == _iota2((R_HD, R_HD), 1)).astype(F32)

    def per_row(b, _):
        rowv = lambda i: vec_ref[i, pl.ds(b, 1), :]
        kap_b, beta_b, w_b, k_b, v_b, r_b = (rowv(i) for i in range(6))
        y_parts = []
        for h in range(R_HEADS):
            sl = slice(h * R_HD, (h + 1) * R_HD)
            s = s_in[b, h]
            v_col = _lane_sum(eye * v_b[:, sl])
            sa = -_lane_sum(s * kap_b[:, sl])
            s_new = s * w_b[:, sl] + sa * beta_b[:, sl] + v_col * k_b[:, sl]
            s_out[b, h] = s_new
            y_col = _lane_sum(s_new * r_b[:, sl])
            y_parts.append(jnp.sum(eye * y_col, axis=0, keepdims=True))
        yrow_ref[pl.ds(b, 1), :] = jnp.concatenate(y_parts, axis=-1)
        return 0

    lax.fori_loop(0, bb, per_row, 0)
    y = yrow_ref[...]
    y_heads = [y[:, h * R_HD:(h + 1) * R_HD] for h in range(R_HEADS)]
    y_ref[...] = _rwkv_out(y_heads, r, k2, v, z_ref[...], r_k[...], gn[...])


def _rwkv_step(proj, shift0, s0, params, bb):
    nb = proj.shape[0]
    cb = lambda c, w: pl.BlockSpec((bb, w), lambda i, c=c: (i, c // w))
    full = lambda a: pl.BlockSpec(a.shape, lambda i: (0,) * a.ndim)
    st = pl.BlockSpec((bb, R_HEADS, R_HD, R_HD), lambda i: (i, 0, 0, 0))
    kern = functools.partial(_rwkv_step_kernel, bb=bb)
    return pl.pallas_call(
        kern,
        grid=(nb // bb,),
        in_specs=[cb(C_RZ, R_W), cb(C_RIN, R_W), cb(C_RIN + R_W, R_W), cb(C_RIN + 2 * R_W, R_W),
                  cb(C_RLORA, LANES),
                  cb(0, R_W), cb(R_W, R_W), cb(2 * R_W, R_W), cb(3 * R_W, LANES), st]
                 + [full(a) for a in params],
        out_specs=[pl.BlockSpec((bb, R_W), lambda i: (i, 0)), st],
        out_shape=[jax.ShapeDtypeStruct((nb, R_W), F32),
                   jax.ShapeDtypeStruct(s0.shape, F32)],
        scratch_shapes=[pltpu.VMEM((6, bb, R_W), F32), pltpu.VMEM((bb, R_W), F32)],
        compiler_params=_cparams(("parallel",)),
        name="rwkv_step",
    )(proj, proj, proj, proj, proj, shift0, shift0, shift0, shift0, s0, *params)


def _gla_log_gate(small, ab_p, a_bias):
    return _log_sigmoid(_mm(small, ab_p) + a_bias) / G_GATE_TEMP


def _gla_out(o_heads, z, gn):
    outs = []
    for h in range(G_HEADS):
        o = o_heads[h]
        outs.append(o * lax.rsqrt(jnp.mean(o * o, axis=-1, keepdims=True) + EPS))
    return jnp.concatenate(outs, axis=-1) * gn * _silu(z)


def _gla_prompt_kernel(qk_ref, v_ref, z_ref, sm_ref, ab, a_bias, gn, y_ref, s_ref, st_ref, *, n_chunks):
    L = CHUNK
    last = pl.num_programs(1) - 1

    @pl.when(pl.program_id(1) == 0)
    def _():
        st_ref[...] = jnp.zeros_like(st_ref)

    tril_incl = _iota2((L, L), 1) <= _iota2((L, L), 0)
    scale = G_DK ** -0.5
    C, H = range(n_chunks), range(G_HEADS)
    rws = [slice(c * L, (c + 1) * L) for c in C]
    sks = [slice(h * G_DK, (h + 1) * G_DK) for h in H]

    qk = qk_ref[...]
    q, k = qk[:, 0:G_QK] * scale, qk[:, G_QK:2 * G_QK]
    v = v_ref[...]
    la = _gla_log_gate(sm_ref[...], ab[...], a_bias[...])
    cum = _mm_exact_lhs(_block_tril(n_chunks * L, L), la)
    q0, qe, ke, kl, e_last = [], [], [], [], []
    for c in C:
        cum_c = cum[rws[c]]
        ref_row = cum_c[L // 2:L // 2 + 1, :]
        last_row = cum_c[L - 1:L, :]
        q0.append(q[rws[c]] * jnp.exp(cum_c))
        qe.append(q[rws[c]] * jnp.exp(cum_c - ref_row))
        ke.append(k[rws[c]] * jnp.exp(ref_row - cum_c))
        kl.append(k[rws[c]] * jnp.exp(last_row - cum_c))
        e_last.append(jnp.exp(last_row))
    v_h = [[v[rws[c], h * G_DV:(h + 1) * G_DV] for h in H] for c in C]
    att = [[jnp.where(tril_incl, _mm_nt(qe[c][:, sks[h]], ke[c][:, sks[h]]), 0.0) for h in H] for c in C]
    upd = [[_mm_tn(v_h[c][h], kl[c][:, sks[h]]) for h in H] for c in C]
    av = [[_mm(att[c][h], v_h[c][h]) for h in H] for c in C]
    st = [st_ref[h] for h in H]
    o_ch = []
    for c in C:
        o_ch.append([av[c][h] + _mm_nt(q0[c][:, sks[h]], st[h]) for h in H])
        st = [st[h] * e_last[c][:, sks[h]] + upd[c][h] for h in H]
    for h in H:
        st_ref[h] = st[h]
    o_heads = [jnp.concatenate([o_ch[c][h] for c in C], axis=0) for h in H]
    y_ref[...] = _gla_out(o_heads, z_ref[...], gn[...])

    @pl.when(pl.program_id(1) == last)
    def _():
        for h in range(G_HEADS):
            s_ref[0, h] = st_ref[h].T


def _gla_params(ab, a_bias, gn):
    ab_p = jnp.zeros((LANES, G_QK), F32).at[SM_GA:SM_GA + G_LORA].set(ab).astype(BF16)
    return ab_p, a_bias.reshape(1, -1), gn.reshape(1, -1)


def _gla_prompt(proj, params, batch, seq, tb):
    nt = seq // tb
    cb = lambda c, w: pl.BlockSpec((tb, w), lambda b, t, c=c: (b * nt + t, c // w))
    full = lambda a: pl.BlockSpec(a.shape, lambda b, t: (0,) * a.ndim)
    kern = functools.partial(_gla_prompt_kernel, n_chunks=tb // CHUNK)
    return pl.pallas_call(
        kern,
        grid=(batch, nt),
        in_specs=[cb(C_GQK, 2 * G_QK), cb(C_GV, G_W), cb(C_GZ, G_W), cb(C_SMALL, LANES)]
                 + [full(a) for a in params],
        out_specs=[pl.BlockSpec((tb, G_W), lambda b, t: (b * nt + t, 0)),
                   pl.BlockSpec((1, G_HEADS, G_DK, G_DV), lambda b, t: (b, 0, 0, 0))],
        out_shape=[jax.ShapeDtypeStruct((batch * seq, G_W), F32),
                   jax.ShapeDtypeStruct((batch, G_HEADS, G_DK, G_DV), F32)],
        scratch_shapes=[pltpu.VMEM((G_HEADS, G_DV, G_DK), F32)],
        compiler_params=_cparams(("parallel", "arbitrary")),
        name="gla_prompt",
    )(proj, proj, proj, proj, *params)


def _gla_step_kernel(qk_ref, v_ref, z_ref, sm_ref, s_in, ab, a_bias, gn, y_ref, s_out,
                     vec_ref, vrow_ref, orow_ref, *, bb):
    qk = qk_ref[...]
    q = qk[:, 0:G_QK] * (G_DK ** -0.5)
    k = qk[:, G_QK:2 * G_QK]
    g = _gla_log_gate(sm_ref[...], ab[...], a_bias[...])
    vec_ref[0] = q
    vec_ref[1] = k
    vec_ref[2] = jnp.exp(g)
    vrow_ref[...] = v_ref[...]
    eye = (_iota2((G_DK, G_DK), 0) == _iota2((G_DK, G_DK), 1)).astype(F32)

    def per_row(b, _):
        q_b, k_b, e_b = (vec_ref[i, pl.ds(b, 1), :] for i in range(3))
        v_b = vrow_ref[pl.ds(b, 1), :]
        o_parts = []
        for h in range(G_HEADS):
            sk = slice(h * G_DK, (h + 1) * G_DK)
            sv = slice(h * G_DV, (h + 1) * G_DV)
            s = s_in[b, h]
            q_col = _lane_sum(eye * q_b[:, sk])
            k_col = _lane_sum(eye * k_b[:, sk])
            e_col = _lane_sum(eye * e_b[:, sk])
            v_h = v_b[:, sv]
            qk_dot = _lane_sum(q_b[:, sk] * k_b[:, sk])
            o_parts.append(qk_dot * v_h + jnp.sum((q_col * e_col) * s, axis=0, keepdims=True))
            s_out[b, h] = s * e_col + k_col * v_h
        orow_ref[pl.ds(b, 1), :] = jnp.concatenate(o_parts, axis=-1)
        return 0

    lax.fori_loop(0, bb, per_row, 0)
    o = orow_ref[...]
    o_heads = [o[:, h * G_DV:(h + 1) * G_DV] for h in range(G_HEADS)]
    y_ref[...] = _gla_out(o_heads, z_ref[...], gn[...])


def _gla_step(proj, s0, params, bb):
    nb = proj.shape[0]
    cb = lambda c, w: pl.BlockSpec((bb, w), lambda i, c=c: (i, c // w))
    full = lambda a: pl.BlockSpec(a.shape, lambda i: (0,) * a.ndim)
    st = pl.BlockSpec((bb, G_HEADS, G_DK, G_DV), lambda i: (i, 0, 0, 0))
    kern = functools.partial(_gla_step_kernel, bb=bb)
    return pl.pallas_call(
        kern,
        grid=(nb // bb,),
        in_specs=[cb(C_GQK, 2 * G_QK), cb(C_GV, G_W), cb(C_GZ, G_W), cb(C_SMALL, LANES), st]
                 + [full(a) for a in params],
        out_specs=[pl.BlockSpec((bb, G_W), lambda i: (i, 0)), st],
        out_shape=[jax.ShapeDtypeStruct((nb, G_W), F32), jax.ShapeDtypeStruct(s0.shape, F32)],
        scratch_shapes=[pltpu.VMEM((3, bb, G_QK), F32), pltpu.VMEM((bb, G_W), F32),
                        pltpu.VMEM((bb, G_W), F32)],
        compiler_params=_cparams(("parallel",)),
        name="gla_step",
    )(proj, proj, proj, proj, s0, *params)


def _mlstm_out(h_heads, o_pre, z, gn):
    outs = []
    for h in range(M_HEADS):
        y = h_heads[h]
        y = y - jnp.mean(y, axis=-1, keepdims=True)
        outs.append(y * lax.rsqrt(jnp.mean(y * y, axis=-1, keepdims=True) + EPS))
    return jnp.concatenate(outs, axis=-1) * gn * _sigmoid(o_pre) * _silu(z)


def _mlstm_prompt_kernel(q_ref, k_ref, v_ref, o_ref, z_ref, sm_ref, ifb, gn,
                         y_ref, c_ref, n_ref, m_ref, *, n_chunks):
    L = CHUNK

    @pl.when(pl.program_id(1) == 0)
    def _():
        c_ref[...] = jnp.zeros_like(c_ref)
        n_ref[...] = jnp.zeros_like(n_ref)
        m_ref[...] = jnp.zeros_like(m_ref)

    tril_incl = _iota2((L, L), 1) <= _iota2((L, L), 0)
    scale = M_DK ** -0.5
    neg_inf = -jnp.inf
    C, H = range(n_chunks), range(M_HEADS)
    rws = [slice(c * L, (c + 1) * L) for c in C]
    sls = [slice(h * M_DK, (h + 1) * M_DK) for h in H]

    pre = sm_ref[...] + ifb[...]
    lf = _log_sigmoid(pre)
    b_all = _mm_exact_lhs(_block_tril(n_chunks * L, L), lf)
    lib_all = pre - pltpu.roll(b_all, LANES - (SM_MF - SM_MI), 1)
    q, k, v = q_ref[...] * scale, k_ref[...], v_ref[...]
    qs = [[q[rws[c], sls[h]] for h in H] for c in C]
    k_h = [[k[rws[c], sls[h]] for h in H] for c in C]
    v_h = [[v[rws[c], sls[h]] for h in H] for c in C]
    qk_raw = [[_mm_nt(qs[c][h], k_h[c][h]) for h in H] for c in C]
    d_log, d_max, b_col, b_last, lib = [], [], [], [], []
    for c in C:
        lib_t = lib_all[rws[c]].T
        b_col.append([b_all[rws[c], SM_MF + h:SM_MF + h + 1] for h in H])
        lib.append([lib_all[rws[c], SM_MI + h:SM_MI + h + 1] for h in H])
        b_last.append([b_col[c][h][L - 1:L, :] for h in H])
        d_log.append([jnp.where(tril_incl, b_col[c][h] + lib_t[SM_MI + h:SM_MI + h + 1, :], neg_inf)
                      for h in H])
        d_max.append([jnp.max(d_log[c][h], axis=-1, keepdims=True) for h in H])
    m_prev = [m_ref[0, :, h:h + 1] for h in H]
    m_t, w_inter, carry, wk, qk = [], [], [], [], []
    for c in C:
        m_inter = [b_col[c][h] + m_prev[h] for h in H]
        m_t.append([jnp.maximum(m_inter[h], d_max[c][h]) for h in H])
        w_inter.append([jnp.exp(m_inter[h] - m_t[c][h]) for h in H])
        m_new = [m_t[c][h][L - 1:L, :] for h in H]
        carry.append([jnp.exp(b_last[c][h] + m_prev[h] - m_new[h]) for h in H])
        wk.append([jnp.exp(b_last[c][h] + lib[c][h] - m_new[h]) * k_h[c][h] for h in H])
        qk.append([qk_raw[c][h] * jnp.exp(d_log[c][h] - m_t[c][h]) for h in H])
        m_prev = m_new
    for h in H:
        m_ref[0, :, h:h + 1] = m_prev[h]
    upd = [[_mm_tn(wk[c][h], v_h[c][h]) for h in H] for c in C]
    qkv = [[_mm(qk[c][h], v_h[c][h]) for h in H] for c in C]
    cs = [c_ref[0, h] for h in H]
    ns = [n_ref[0, h:h + 1, :] for h in H]
    h_ch = []
    for c in C:
        qc = [_mm(qs[c][h], cs[h]) for h in H]
        hh = []
        for h in H:
            num = qkv[c][h] + w_inter[c][h] * qc[h]
            den = _lane_sum(qk[c][h]) + w_inter[c][h] * _lane_sum(qs[c][h] * ns[h])
            hh.append(num / jnp.maximum(jnp.abs(den), jnp.exp(-m_t[c][h])))
        h_ch.append(hh)
        cs = [carry[c][h] * cs[h] + upd[c][h] for h in H]
        ns = [carry[c][h] * ns[h] + jnp.sum(wk[c][h], axis=0, keepdims=True) for h in H]
    for h in H:
        c_ref[0, h] = cs[h]
        n_ref[0, h:h + 1, :] = ns[h]
    h_heads = [jnp.concatenate([h_ch[c][h] for c in C], axis=0) for h in H]
    y_ref[...] = _mlstm_out(h_heads, o_ref[...], z_ref[...], gn[...])


def _mlstm_params(ifb, gn):
    ifb_p = jnp.zeros((1, LANES), F32).at[0, SM_MI:SM_MI + 2 * M_HEADS].set(ifb)
    return ifb_p, gn.reshape(1, -1)


def _mlstm_prompt(proj, params, batch, seq, tb):
    nt = seq // tb
    cb = lambda c, w: pl.BlockSpec((tb, w), lambda b, t, c=c: (b * nt + t, c // w))
    full = lambda a: pl.BlockSpec(a.shape, lambda b, t: (0,) * a.ndim)
    kern = functools.partial(_mlstm_prompt_kernel, n_chunks=tb // CHUNK)
    return pl.pallas_call(
        kern,
        grid=(batch, nt),
        in_specs=[cb(C_MQ, M_QK), cb(C_MK, M_QK), cb(C_MV, M_W), cb(C_MO, M_W), cb(C_MZ, M_W),
                  cb(C_SMALL, LANES)] + [full(a) for a in params],
        out_specs=[pl.BlockSpec((tb, M_W), lambda b, t: (b * nt + t, 0)),
                   pl.BlockSpec((1, M_HEADS, M_DK, M_DV), lambda b, t: (b, 0, 0, 0)),
                   pl.BlockSpec((1, M_HEADS, M_DK), lambda b, t: (b, 0, 0)),
                   pl.BlockSpec((1, 1, M_HEADS), lambda b, t: (b, 0, 0))],
        out_shape=[jax.ShapeDtypeStruct((batch * seq, M_W), F32),
                   jax.ShapeDtypeStruct((batch, M_HEADS, M_DK, M_DV), F32),
                   jax.ShapeDtypeStruct((batch, M_HEADS, M_DK), F32),
                   jax.ShapeDtypeStruct((batch, 1, M_HEADS), F32)],
        compiler_params=_cparams(("parallel", "arbitrary")),
        name="mlstm_prompt",
    )(proj, proj, proj, proj, proj, proj, *params)


def _mlstm_step_kernel(q_ref, k_ref, v_ref, o_ref, z_ref, sm_ref, c_in, n_in, m_in, ifb, gn,
                       y_ref, c_out, n_out, m_out, vec_ref, sc_ref, hrow_ref, *, bb):
    pre = sm_ref[...] + ifb[...]
    li = pre[:, SM_MI:SM_MI + M_HEADS]
    lf = _log_sigmoid(pre[:, SM_MF:SM_MF + M_HEADS])
    m0 = m_in[...]
    m_inter = lf + m0
    m_t = jnp.maximum(m_inter, li)
    m_out[...] = m_t
    pad = lambda x: jnp.concatenate([x, jnp.zeros((bb, LANES - M_HEADS), F32)], axis=-1)
    sc_ref[0] = pad(jnp.exp(m_inter - m_t))
    sc_ref[1] = pad(jnp.exp(li - m_t))
    sc_ref[2] = pad(jnp.exp(-m_t))
    vec_ref[0] = q_ref[...] * (M_DK ** -0.5)
    vec_ref[1] = k_ref[...]
    vec_ref[2] = v_ref[...]
    eye = (_iota2((M_DK, M_DK), 0) == _iota2((M_DK, M_DK), 1)).astype(F32)

    def per_row(b, _):
        q_b, k_b, v_b = (vec_ref[i, pl.ds(b, 1), :] for i in range(3))
        wi_b, ei_b, en_b = (sc_ref[i, pl.ds(b, 1), :] for i in range(3))
        h_parts = []
        for h in range(M_HEADS):
            sl = slice(h * M_DK, (h + 1) * M_DK)
            q_h, k_h, v_h = q_b[:, sl], k_b[:, sl], v_b[:, sl]
            wi, ei, en = wi_b[:, h:h + 1], ei_b[:, h:h + 1], en_b[:, h:h + 1]
            c = c_in[b, h]
            n = n_in[b, h:h + 1, :]
            q_col = _lane_sum(eye * q_h)
            k_col = _lane_sum(eye * k_h)
            qk = _lane_sum(q_h * k_h) * ei
            num = qk * v_h + wi * jnp.sum(q_col * c, axis=0, keepdims=True)
            den = qk + wi * _lane_sum(q_h * n)
            h_parts.append(num / jnp.maximum(jnp.abs(den), en))
            c_out[b, h] = wi * c + (ei * k_col) * v_h
            n_out[b, h:h + 1, :] = wi * n + ei * k_h
        hrow_ref[pl.ds(b, 1), :] = jnp.concatenate(h_parts, axis=-1)
        return 0

    lax.fori_loop(0, bb, per_row, 0)
    hh = hrow_ref[...]
    h_heads = [hh[:, h * M_DV:(h + 1) * M_DV] for h in range(M_HEADS)]
    y_ref[...] = _mlstm_out(h_heads, o_ref[...], z_ref[...], gn[...])


def _mlstm_step(proj, c0, n0, m0, params, bb):
    nb = proj.shape[0]
    cb = lambda c, w: pl.BlockSpec((bb, w), lambda i, c=c: (i, c // w))
    full = lambda a: pl.BlockSpec(a.shape, lambda i: (0,) * a.ndim)
    cs = pl.BlockSpec((bb, M_HEADS, M_DK, M_DV), lambda i: (i, 0, 0, 0))
    ns = pl.BlockSpec((bb, M_HEADS, M_DK), lambda i: (i, 0, 0))
    ms = pl.BlockSpec((bb, M_HEADS), lambda i: (i, 0))
    kern = functools.partial(_mlstm_step_kernel, bb=bb)
    return pl.pallas_call(
        kern,
        grid=(nb // bb,),
        in_specs=[cb(C_MQ, M_QK), cb(C_MK, M_QK), cb(C_MV, M_W), cb(C_MO, M_W), cb(C_MZ, M_W),
                  cb(C_SMALL, LANES), cs, ns, ms] + [full(a) for a in params],
        out_specs=[pl.BlockSpec((bb, M_W), lambda i: (i, 0)), cs, ns, ms],
        out_shape=[jax.ShapeDtypeStruct((nb, M_W), F32), jax.ShapeDtypeStruct(c0.shape, F32),
                   jax.ShapeDtypeStruct(n0.shape, F32), jax.ShapeDtypeStruct(m0.shape, F32)],
        scratch_shapes=[pltpu.VMEM((3, bb, M_QK), F32), pltpu.VMEM((3, bb, LANES), F32),
                        pltpu.VMEM((bb, M_W), F32)],
        compiler_params=_cparams(("parallel",)),
        name="mlstm_step",
    )(proj, proj, proj, proj, proj, proj, c0, n0, m0, *params)


def _pad_w_in(w_in):
    sizes = (R_SHIFT_W, R_W, G_QK, G_QK, G_W, G_LORA, G_W, M_QK, M_QK, M_W, M_HEADS, M_HEADS, M_W, M_W,
             3 * D_MODEL)
    offs = [0]
    for s in sizes:
        offs.append(offs[-1] + s)
    seg = lambda i: w_in[..., offs[i]:offs[i + 1]]
    (r_in, r_z, g_q, g_k, g_v, g_a, g_z, m_q, m_k, m_v, m_i, m_f, m_o, m_z, gate) = (seg(i) for i in range(15))
    zeros = lambda n: jnp.zeros(w_in.shape[:-1] + (n,), w_in.dtype)
    small = jnp.concatenate([g_a, m_i, m_f, zeros(LANES - G_LORA - 2 * M_HEADS)], axis=-1)
    w = jnp.concatenate([gate, r_z, g_v, g_z, m_q, m_k, m_v, m_o, m_z, g_q, g_k, r_in, small,
                         zeros(N_PAD - N_USED)], axis=-1)
    return w.astype(BF16)


def _pick(n, prefs):
    for p in prefs:
        if n % p == 0:
            return p
    return n


def kernel(x_prompt, x_sample, state_rwkv_shift, state_rwkv, state_gla, state_mlstm_C, state_mlstm_n,
           state_mlstm_m, norm_pre, norm_post, w_in, r_mu_shift, r_w_lora_b, r_w0, r_a_lora_b, r_a0,
           r_k_k, r_k_a, r_r_k, r_gn, g_a_lora_b, g_a_bias, g_gn, m_if_bias, m_gn, w_br_rwkv, w_br_gla,
           w_br_mlstm, w_out):
    depth = w_in.shape[0]
    bp, tp, _ = x_prompt.shape
    bs, ts, _ = x_sample.shape
    assert ts == 1 and tp % CHUNK == 0
    w_pad = _pad_w_in(w_in)
    wr, wg, wm, wo = (w.astype(BF16) for w in (w_br_rwkv, w_br_gla, w_br_mlstm, w_out))

    mp = bp * tp
    tm_in = _pick(mp, (2048, 1024, 512, 256, 128, 64))
    tm_mg = _pick(mp, (512, 256, 128, 64))
    tb = _pick(tp, (256, 128, 64))
    bb = _pick(bs, (8,))

    xp = x_prompt.reshape(mp, D_MODEL)
    xs = x_sample.reshape(bs, D_MODEL)
    new_p = [[] for _ in range(6)]
    new_s = [[] for _ in range(6)]
    for l in range(depth):
        g_pre = norm_pre[l].reshape(1, -1)
        g_post = norm_post[l].reshape(1, -1)
        rp = _rwkv_params(r_mu_shift[l], r_w_lora_b[l], r_w0[l], r_a_lora_b[l], r_a0[l], r_k_k[l],
                          r_k_a[l], r_r_k[l], r_gn[l])
        gp = _gla_params(g_a_lora_b[l], g_a_bias[l], g_gn[l])
        mparams = _mlstm_params(m_if_bias[l], m_gn[l])

        proj = _inproj(xp, g_pre, w_pad, l, tm_in)
        y_r, s_r = _rwkv_prompt(proj, rp, bp, tp, tb)
        y_g, s_g = _gla_prompt(proj, gp, bp, tp, tb)
        y_m, c, n, m = _mlstm_prompt(proj, mparams, bp, tp, tb)
        xp = _merge(proj, y_r, y_g, y_m, xp, wr, wg, wm, wo, g_post, l, tm_mg)
        shift = proj.reshape(bp, tp, N_PAD)[:, tp - 1, C_RIN:C_RIN + R_SHIFT_W]
        for lst, val in zip(new_p, (shift, s_r, s_g, c, n, m.reshape(bp, M_HEADS))):
            lst.append(val)

        proj = _inproj(xs, g_pre, w_pad, l, bs)
        y_r, s_r = _rwkv_step(proj, state_rwkv_shift[l], state_rwkv[l], rp, bb)
        y_g, s_g = _gla_step(proj, state_gla[l], gp, bb)
        y_m, c, n, m = _mlstm_step(proj, state_mlstm_C[l], state_mlstm_n[l], state_mlstm_m[l], mparams, bb)
        xs = _merge(proj, y_r, y_g, y_m, xs, wr, wg, wm, wo, g_post, l, bs)
        shift = proj[:, C_RIN:C_RIN + R_SHIFT_W]
        for lst, val in zip(new_s, (shift, s_r, s_g, c, n, m)):
            lst.append(val)

    stk = lambda lst: jnp.stack(lst)
    return (xp.reshape(bp, tp, D_MODEL), xs.reshape(bs, ts, D_MODEL),
            *(stk(v) for v in new_p), *(stk(v) for v in new_s))
```

```python
import functools
import math

import jax
import jax.numpy as jnp
from jax import lax
from jax.experimental import pallas as pl
from jax.experimental.pallas import tpu as pltpu

F32 = jnp.float32
BF16 = jnp.bfloat16

D_MODEL = 1024
R_HEADS, R_HD = 8, 64
R_W = R_HEADS * R_HD
R_LORA = 64
R_SHIFT_W = 3 * R_W + 2 * R_LORA
R_GN_EPS = 64e-5
G_HEADS, G_DK, G_DV = 4, 64, 128
G_QK = G_HEADS * G_DK
G_W = G_HEADS * G_DV
G_LORA = 16
G_GATE_TEMP = 16.0
M_HEADS, M_DK, M_DV = 4, 128, 128
M_QK = M_HEADS * M_DK
M_W = M_HEADS * M_DV
EPS = 1e-6

LANES = 128
VMEM_LIMIT = 48 * 1024 * 1024

C_GATE = 0
C_RZ = 3072
C_GV = 3584
C_GZ = 4096
C_MQ = 4608
C_MK = 5120
C_MV = 5632
C_MO = 6144
C_MZ = 6656
C_GQK = 7168
C_RIN = 7680
C_RLORA = C_RIN + 3 * R_W
C_SMALL = C_RLORA + LANES
SM_GA, SM_MI, SM_MF = 0, 16, 20
N_USED = C_SMALL + LANES
PROJ_TN = 512
N_PAD = -(-N_USED // PROJ_TN) * PROJ_TN

CHUNK = 64


def _cparams(sem):
    return pltpu.CompilerParams(dimension_semantics=sem, vmem_limit_bytes=VMEM_LIMIT)


def _mm(a, b):
    return jnp.dot(a.astype(BF16), b.astype(BF16), preferred_element_type=F32)


def _mm_nt(a, b):
    return lax.dot_general(a.astype(BF16), b.astype(BF16), (((1,), (1,)), ((), ())),
                           preferred_element_type=F32)


def _mm_tn(a, b):
    return lax.dot_general(a.astype(BF16), b.astype(BF16), (((0,), (0,)), ((), ())),
                           preferred_element_type=F32)


def _mm_exact_lhs(m01, x):
    hi = x.astype(BF16)
    r1 = x - hi.astype(F32)
    mid = r1.astype(BF16)
    lo = (r1 - mid.astype(F32)).astype(BF16)
    m = m01.astype(BF16)
    d = lambda p: jnp.dot(m, p, preferred_element_type=F32)
    return d(hi) + d(mid) + d(lo)


def _sigmoid(x):
    return 1.0 / (1.0 + jnp.exp(-x))


def _silu(x):
    return x * _sigmoid(x)


def _softplus(x):
    return jnp.maximum(x, 0.0) + jnp.log(1.0 + jnp.exp(-jnp.abs(x)))


def _log_sigmoid(x):
    return -_softplus(-x)


def _iota2(shape, dim):
    return lax.broadcasted_iota(jnp.int32, shape, dim)


def _lane_sum(x):
    return jnp.sum(x, axis=-1, keepdims=True)


def _rows8(x):
    return jnp.broadcast_to(x, (8, x.shape[1]))


def _outer_masks():
    rid = _iota2((8, 1), 0)
    on = lambda *rows: sum((rid == r).astype(F32) for r in rows)
    return (on(0, 2), on(1), on(0, 1), on(2)), (on(3, 5), on(4), on(3, 4), on(5))


def _outer_operands(masks, *pairs):
    a_op, b_op = 0.0, 0.0
    for (a, b), (m_ahi, m_alo, m_bhi, m_blo) in zip(pairs, masks):
        a_hi = a.astype(BF16).astype(F32)
        b_hi = b.astype(BF16).astype(F32)
        a_op = a_op + m_ahi * a_hi + m_alo * (a - a_hi)
        b_op = b_op + m_bhi * b_hi + m_blo * (b - b_hi)
    return a_op, b_op


def _block_tril(n, blk):
    r, c = _iota2((n, n), 0), _iota2((n, n), 1)
    return ((r - c).astype(jnp.uint32) <= (r % blk).astype(jnp.uint32)).astype(F32)


def _inproj_kernel(x_ref, g_ref, w_ref, o_ref, h_ref):
    @pl.when(pl.program_id(1) == 0)
    def _():
        x = x_ref[...]
        y = x * lax.rsqrt(jnp.mean(x * x, axis=-1, keepdims=True) + EPS)
        h_ref[...] = (y * g_ref[...]).astype(BF16)

    o_ref[...] = jnp.dot(h_ref[...], w_ref[...], preferred_element_type=F32)


def _inproj(x2d, g_pre, w_pad, layer, tm):
    m = x2d.shape[0]
    return pl.pallas_call(
        _inproj_kernel,
        grid=(m // tm, N_PAD // PROJ_TN),
        in_specs=[pl.BlockSpec((tm, D_MODEL), lambda i, j: (i, 0)),
                  pl.BlockSpec((1, D_MODEL), lambda i, j: (0, 0)),
                  pl.BlockSpec((None, D_MODEL, PROJ_TN), lambda i, j: (layer, 0, j))],
        out_specs=pl.BlockSpec((tm, PROJ_TN), lambda i, j: (i, j)),
        out_shape=jax.ShapeDtypeStruct((m, N_PAD), F32),
        scratch_shapes=[pltpu.VMEM((tm, D_MODEL), BF16)],
        compiler_params=_cparams(("parallel", "arbitrary")),
        name="inproj",
    )(x2d, g_pre, w_pad)


def _merge_kernel(gate_ref, yr_ref, yg_ref, ym_ref, x_ref, wr_ref, wg_ref, wm_ref, wo_ref, gp_ref, o_ref):
    d = D_MODEL
    merged = (_sigmoid(gate_ref[:, 0:d]) * _mm(yr_ref[...], wr_ref[...])
              + _sigmoid(gate_ref[:, d:2 * d]) * _mm(yg_ref[...], wg_ref[...])
              + _sigmoid(gate_ref[:, 2 * d:3 * d]) * _mm(ym_ref[...], wm_ref[...]))
    o = _mm(merged, wo_ref[...])
    y = o * lax.rsqrt(jnp.mean(o * o, axis=-1, keepdims=True) + EPS)
    o_ref[...] = x_ref[...] + y * gp_ref[...]


def _merge(proj, y_r, y_g, y_m, x2d, w_r, w_g, w_m, w_o, g_post, layer, tm):
    m = x2d.shape[0]
    row = lambda w: pl.BlockSpec((tm, w), lambda i: (i, 0))
    full = lambda a: pl.BlockSpec(a.shape, lambda i: (0, 0))
    wl = lambda a: pl.BlockSpec((None,) + a.shape[1:], lambda i: (layer, 0, 0))
    return pl.pallas_call(
        _merge_kernel,
        grid=(m // tm,),
        in_specs=[row(3 * D_MODEL), row(R_W), row(G_W), row(M_W), row(D_MODEL),
                  wl(w_r), wl(w_g), wl(w_m), wl(w_o), full(g_post)],
        out_specs=row(D_MODEL),
        out_shape=jax.ShapeDtypeStruct((m, D_MODEL), F32),
        compiler_params=_cparams(("parallel",)),
        name="merge",
    )(proj, y_r, y_g, y_m, x2d, w_r, w_g, w_m, w_o, g_post)


def _rwkv_front(r_in, k_in, v_in, l_in, prev_r, prev_k, prev_v, prev_l, p):
    (mu_r, mu_k, mu_v, mu_l, wb, w0, ab, a0, k_k, k_a) = p
    r = r_in + mu_r * (prev_r - r_in)
    k = k_in + mu_k * (prev_k - k_in)
    v = v_in + mu_v * (prev_v - v_in)
    lo = l_in + mu_l * (prev_l - l_in)
    log_w = -_softplus(-(w0 + _mm(jnp.tanh(lo), wb))) - 0.5
    lw = -jnp.exp(log_w)
    a = _sigmoid(a0 + _mm(lo, ab))
    kk = k * k_k
    k2 = k * (1.0 + (a - 1.0) * k_a)
    return r, k2, v, lw, a, kk


def _rwkv_out(y_heads, r, k2, v, z, r_k, gn):
    outs = []
    for h in range(R_HEADS):
        sl = slice(h * R_HD, (h + 1) * R_HD)
        y = y_heads[h]
        y = y - jnp.mean(y, axis=-1, keepdims=True)
        y = y * lax.rsqrt(jnp.mean(y * y, axis=-1, keepdims=True) + R_GN_EPS)
        bonus = _lane_sum(r[:, sl] * k2[:, sl] * r_k[:, sl]) * v[:, sl]
        outs.append(y * gn[:, sl] + bonus)
    return jnp.concatenate(outs, axis=-1) * _silu(z)


def _rwkv_prompt_kernel(z_ref, r_ref, k_ref, v_ref, l_ref,
                        mu_r, mu_k, mu_v, mu_l, wb, w0, ab, a0, k_k, k_a, r_k, gn,
                        y_ref, s_ref, carry_ref, *, n_chunks):
    L = CHUNK

    @pl.when(pl.program_id(1) == 0)
    def _():
        s_ref[...] = jnp.zeros_like(s_ref)
        carry_ref[...] = jnp.zeros_like(carry_ref)

    tb = n_chunks * L
    params = tuple(x[...] for x in (mu_r, mu_k, mu_v, mu_l, wb, w0, ab, a0, k_k, k_a))
    row = _iota2((L, L), 0)
    col = _iota2((L, L), 1)
    eye = (col == row).astype(F32)
    blk_mask = (row // 16) == (col // 16)
    r2 = _iota2((2 * L, 2 * L), 0)
    c2 = _iota2((2 * L, 2 * L), 1)
    tr, tc = r2 % L, c2 % L
    g_mask = tc < tr + (r2 >= L).astype(jnp.int32)
    first_row = _iota2((tb, 1), 0) == 0

    def shifted(x, carry):
        return jnp.where(first_row, carry, pltpu.roll(x, 1, 0))

    r_in, k_in, v_in, l_in = r_ref[...], k_ref[...], v_ref[...], l_ref[...]
    prev_r = shifted(r_in, carry_ref[0:1, 0:R_W])
    prev_k = shifted(k_in, carry_ref[0:1, R_W:2 * R_W])
    prev_v = shifted(v_in, carry_ref[0:1, 2 * R_W:3 * R_W])
    prev_l = shifted(l_in, carry_ref[0:1, 3 * R_W:R_SHIFT_W])
    carry_ref[0:1, 0:R_W] = r_in[tb - 1:tb, :]
    carry_ref[0:1, R_W:2 * R_W] = k_in[tb - 1:tb, :]
    carry_ref[0:1, 2 * R_W:3 * R_W] = v_in[tb - 1:tb, :]
    carry_ref[0:1, 3 * R_W:R_SHIFT_W] = l_in[tb - 1:tb, :]
    r, k2, v, lw, a, kk = _rwkv_front(r_in, k_in, v_in, l_in, prev_r, prev_k, prev_v, prev_l, params)

    cum = _mm_exact_lhs(_block_tril(tb, L), lw)
    p_inc = jnp.exp(cum)
    p_exc = jnp.exp(cum - lw)
    p_inv = jnp.exp(-cum)
    r_t = r * p_inc
    k_h = k2 * p_inv

    CH = [(c, h) for c in range(n_chunks) for h in range(R_HEADS)]
    N = range(len(CH))
    lhs2, rhs2, rhs2_l, v_h, p_last = [], [], [], [], []
    for h in range(R_HEADS):
        sl = slice(h * R_HD, (h + 1) * R_HD)
        kk_h = kk[:, sl]
        kap = kk_h * (1.0 / jnp.maximum(jnp.sqrt(_lane_sum(kk_h * kk_h)), 1e-12))
        beta_h = kap * a[:, sl] * p_inv[:, sl]
        kap_t = kap * p_exc[:, sl]
        lhs2.append([jnp.concatenate([kap_t[c * L:(c + 1) * L], r_t[c * L:(c + 1) * L, sl]], axis=0)
                     for c in range(n_chunks)])
        rhs2.append([jnp.concatenate([beta_h[c * L:(c + 1) * L], k_h[c * L:(c + 1) * L, sl]], axis=0)
                     for c in range(n_chunks)])
        p_last.append([p_inc[(c + 1) * L - 1:(c + 1) * L, sl] for c in range(n_chunks)])
        v_h.append([v[c * L:(c + 1) * L, sl] for c in range(n_chunks)])
    lhs2 = [lhs2[h][c] for c, h in CH]
    rhs2 = [rhs2[h][c] for c, h in CH]
    p_last = [p_last[h][c] for c, h in CH]
    v_h = [v_h[h][c] for c, h in CH]
    rhs2_l = [rhs2[i] * p_last[i] for i in N]
    g = [jnp.where(g_mask, _mm_nt(lhs2[i], rhs2[i]), 0.0) for i in N]
    a_m = [g[i][0:L, 0:L] for i in N]
    d = [jnp.where(blk_mask, a_m[i], 0.0) for i in N]
    akv = [_mm(g[i][0:L, L:2 * L], v_h[i]) for i in N]
    d2 = [_mm(d[i], d[i]) for i in N]
    d4 = [_mm(d2[i], d2[i]) for i in N]
    dd2 = [_mm(d[i], d2[i]) for i in N]
    d8 = [_mm(d4[i], d4[i]) for i in N]
    x = [eye - d[i] + d2[i] - dd2[i] for i in N]
    x = [x[i] + _mm(x[i], d4[i]) for i in N]
    td = [x[i] + _mm(x[i], d8[i]) for i in N]
    m = [_mm(td[i], a_m[i] - d[i]) for i in N]
    m2 = [_mm(m[i], m[i]) for i in N]
    q = [eye - m[i] + m2[i] - _mm(m[i], m2[i]) for i in N]
    t_inv = [_mm(q[i], td[i]) for i in N]
    kbar = [_mm(t_inv[i], lhs2[i][0:L]) for i in N]
    u0 = [-_mm(t_inv[i], akv[i]) for i in N]
    uv0 = [jnp.concatenate([u0[i], v_h[i]], axis=0) for i in N]
    n_c = [_mm_tn(kbar[i], rhs2_l[i][0:L]) for i in N]
    c_c = [_mm_tn(uv0[i], rhs2_l[i]) for i in N]
    qt = [lhs2[i][L:2 * L] - _mm(g[i][L:2 * L, 0:L], kbar[i]) for i in N]
    y0 = [_mm(g[i][L:2 * L, :], uv0[i]) for i in N]
    s = [s_ref[0, h] for h in range(R_HEADS)]
    y_ch = []
    for c in range(n_chunks):
        ids = [c * R_HEADS + h for h in range(R_HEADS)]
        y_ch.append([_mm_nt(qt[i], s[h]) + y0[i] for h, i in enumerate(ids)])
        s = [s[h] * p_last[i] - _mm(s[h], n_c[i]) + c_c[i] for h, i in enumerate(ids)]
    for h in range(R_HEADS):
        s_ref[0, h] = s[h]
    y_heads = [jnp.concatenate([y_ch[c][h] for c in range(n_chunks)], axis=0) for h in range(R_HEADS)]
    y_ref[...] = _rwkv_out(y_heads, r, k2, v, z_ref[...], r_k[...], gn[...])


def _rwkv_params(mu, wb, w0, ab, a0, k_k, k_a, r_k, gn):
    row = lambda x: x.reshape(1, -1)
    zeros = jnp.zeros((R_LORA, R_W), F32)
    wb_p = jnp.concatenate([wb, zeros], axis=0).astype(BF16)
    ab_p = jnp.concatenate([zeros, ab], axis=0).astype(BF16)
    return (row(mu[0:R_W]), row(mu[R_W:2 * R_W]), row(mu[2 * R_W:3 * R_W]), row(mu[3 * R_W:]),
            wb_p, row(w0), ab_p, row(a0), row(k_k), row(k_a), row(r_k), row(gn))


def _rwkv_prompt(proj, params, batch, seq, tb):
    nt = seq // tb
    cb = lambda c, w: pl.BlockSpec((tb, w), lambda b, t, c=c: (b * nt + t, c // w))
    full = lambda a: pl.BlockSpec(a.shape, lambda b, t: (0,) * a.ndim)
    kern = functools.partial(_rwkv_prompt_kernel, n_chunks=tb // CHUNK)
    return pl.pallas_call(
        kern,
        grid=(batch, nt),
        in_specs=[cb(C_RZ, R_W), cb(C_RIN, R_W), cb(C_RIN + R_W, R_W), cb(C_RIN + 2 * R_W, R_W),
                  cb(C_RLORA, LANES)] + [full(a) for a in params],
        out_specs=[pl.BlockSpec((tb, R_W), lambda b, t: (b * nt + t, 0)),
                   pl.BlockSpec((1, R_HEADS, R_HD, R_HD), lambda b, t: (b, 0, 0, 0))],
        out_shape=[jax.ShapeDtypeStruct((batch * seq, R_W), F32),
                   jax.ShapeDtypeStruct((batch, R_HEADS, R_HD, R_HD), F32)],
        scratch_shapes=[pltpu.VMEM((8, R_SHIFT_W), F32)],
        compiler_params=_cparams(("parallel", "arbitrary")),
        name="rwkv_prompt",
    )(proj, proj, proj, proj, proj, *params)


STEP_ROWS = 4
GLA_STEP_ROWS = 2


def _rwkv_step_kernel(z_ref, r_ref, k_ref, v_ref, l_ref, pr_ref, pk_ref, pv_ref, pl_ref, s_in, acc_ref,
                      mu_r, mu_k, mu_v, mu_l, wb, w0, ab, a0, k_k, k_a, r_k, gn,
                      y_ref, s_out, vec_ref, yrow_ref, *, bb):
    del acc_ref
    params = tuple(x[...] for x in (mu_r, mu_k, mu_v, mu_l, wb, w0, ab, a0, k_k, k_a))
    r, k2, v, lw, a, kk = _rwkv_front(r_ref[...], k_ref[...], v_ref[...], l_ref[...],
                                      pr_ref[...], pk_ref[...], pv_ref[...], pl_ref[...], params)
    w = jnp.exp(lw)
    kaps, betas = [], []
    for h in range(R_HEADS):
        sl = slice(h * R_HD, (h + 1) * R_HD)
        kk_h = kk[:, sl]
        kap = kk_h / jnp.maximum(jnp.sqrt(_lane_sum(kk_h * kk_h)), 1e-12)
        kaps.append(kap)
        betas.append(kap * a[:, sl])
    vec_ref[0] = jnp.concatenate(kaps, axis=-1)
    vec_ref[1] = jnp.concatenate(betas, axis=-1)
    vec_ref[2] = w
    vec_ref[3] = k2
    vec_ref[4] = v
    vec_ref[5] = r
    masks = _outer_masks()

    def per_rows(j, _):
        bs = [j * STEP_ROWS + i for i in range(STEP_ROWS)]
        vecs = [[vec_ref[i, pl.ds(b, 1), :] for i in range(6)] for b in bs]
        items = [(bi, h) for bi in range(STEP_ROWS) for h in range(R_HEADS)]
        hs = lambda x, h: x[:, h * R_HD:(h + 1) * R_HD]
        kap, beta, w_, k_, v_, r_ = ([hs(vecs[bi][i], h) for bi, h in items] for i in range(6))
        n = range(len(items))
        s = [s_in[bs[bi], h] for bi, h in items]
        sa = [-_mm_nt(_rows8(kap[i]), s[i])[0:1] for i in n]
        ops = [_outer_operands(masks, (sa[i], beta[i]), (v_[i], k_[i])) for i in n]
        s_new = [s[i] * w_[i] + _mm_tn(*ops[i]) for i in n]
        for i, (bi, h) in enumerate(items):
            s_out[bs[bi], h] = s_new[i]
        y_row = [_mm_nt(_rows8(r_[i]), s_new[i])[0:1] for i in n]
        for bi in range(STEP_ROWS):
            yrow_ref[pl.ds(bs[bi], 1), :] = jnp.concatenate(
                [y_row[bi * R_HEADS + h] for h in range(R_HEADS)], axis=-1)
        return 0

    lax.fori_loop(0, bb // STEP_ROWS, per_rows, 0)
    y = yrow_ref[...]
    y_heads = [y[:, h * R_HD:(h + 1) * R_HD] for h in range(R_HEADS)]
    y_ref[...] = _rwkv_out(y_heads, r, k2, v, z_ref[...], r_k[...], gn[...])


def _layer_block(shape_tail, bb, layer):
    zeros = (0,) * len(shape_tail)
    return pl.BlockSpec((None, bb) + tuple(shape_tail), lambda i: (layer, i) + zeros)


def _rwkv_step(proj, shift_all, s_all, acc, params, layer, bb):
    nb = proj.shape[0]
    cb = lambda c, w: pl.BlockSpec((bb, w), lambda i, c=c: (i, c // w))
    sh = lambda c, w: pl.BlockSpec((None, bb, w), lambda i, c=c: (layer, i, c // w))
    full = lambda a: pl.BlockSpec(a.shape, lambda i: (0,) * a.ndim)
    st = _layer_block(s_all.shape[2:], bb, layer)
    kern = functools.partial(_rwkv_step_kernel, bb=bb)
    return pl.pallas_call(
        kern,
        grid=(nb // bb,),
        in_specs=[cb(C_RZ, R_W), cb(C_RIN, R_W), cb(C_RIN + R_W, R_W), cb(C_RIN + 2 * R_W, R_W),
                  cb(C_RLORA, LANES),
                  sh(0, R_W), sh(R_W, R_W), sh(2 * R_W, R_W), sh(3 * R_W, LANES), st,
                  pl.BlockSpec(memory_space=pl.ANY)]
                 + [full(a) for a in params],
        out_specs=[pl.BlockSpec((bb, R_W), lambda i: (i, 0)), st],
        out_shape=[jax.ShapeDtypeStruct((nb, R_W), F32),
                   jax.ShapeDtypeStruct(acc.shape, F32)],
        input_output_aliases={10: 1},
        scratch_shapes=[pltpu.VMEM((6, bb, R_W), F32), pltpu.VMEM((bb, R_W), F32)],
        compiler_params=_cparams(("parallel",)),
        name="rwkv_step",
    )(proj, proj, proj, proj, proj, shift_all, shift_all, shift_all, shift_all, s_all, acc, *params)


def _gla_log_gate(small, ab_p, a_bias):
    return _log_sigmoid(_mm(small, ab_p) + a_bias) / G_GATE_TEMP


def _gla_out(o_heads, z, gn):
    outs = []
    for h in range(G_HEADS):
        o = o_heads[h]
        outs.append(o * lax.rsqrt(jnp.mean(o * o, axis=-1, keepdims=True) + EPS))
    return jnp.concatenate(outs, axis=-1) * gn * _silu(z)


def _gla_prompt_kernel(qk_ref, v_ref, z_ref, sm_ref, ab, a_bias, gn, y_ref, s_ref, st_ref, *, n_chunks):
    L = CHUNK
    last = pl.num_programs(1) - 1

    @pl.when(pl.program_id(1) == 0)
    def _():
        st_ref[...] = jnp.zeros_like(st_ref)

    tril_incl = _iota2((L, L), 1) <= _iota2((L, L), 0)
    scale = G_DK ** -0.5
    C, H = range(n_chunks), range(G_HEADS)
    rws = [slice(c * L, (c + 1) * L) for c in C]
    sks = [slice(h * G_DK, (h + 1) * G_DK) for h in H]

    qk = qk_ref[...]
    q, k = qk[:, 0:G_QK] * scale, qk[:, G_QK:2 * G_QK]
    v = v_ref[...]
    la = _gla_log_gate(sm_ref[...], ab[...], a_bias[...])
    cum = _mm_exact_lhs(_block_tril(n_chunks * L, L), la)
    q0, qe, ke, kl, e_last = [], [], [], [], []
    for c in C:
        cum_c = cum[rws[c]]
        ref_row = cum_c[L // 2:L // 2 + 1, :]
        last_row = cum_c[L - 1:L, :]
        q0.append(q[rws[c]] * jnp.exp(cum_c))
        qe.append(q[rws[c]] * jnp.exp(cum_c - ref_row))
        ke.append(k[rws[c]] * jnp.exp(ref_row - cum_c))
        kl.append(k[rws[c]] * jnp.exp(last_row - cum_c))
        e_last.append(jnp.exp(last_row))
    v_h = [[v[rws[c], h * G_DV:(h + 1) * G_DV] for h in H] for c in C]
    att = [[jnp.where(tril_incl, _mm_nt(qe[c][:, sks[h]], ke[c][:, sks[h]]), 0.0) for h in H] for c in C]
    upd = [[_mm_tn(v_h[c][h], kl[c][:, sks[h]]) for h in H] for c in C]
    av = [[_mm(att[c][h], v_h[c][h]) for h in H] for c in C]
    st = [st_ref[h] for h in H]
    o_ch = []
    for c in C:
        o_ch.append([av[c][h] + _mm_nt(q0[c][:, sks[h]], st[h]) for h in H])
        st = [st[h] * e_last[c][:, sks[h]] + upd[c][h] for h in H]
    for h in H:
        st_ref[h] = st[h]
    o_heads = [jnp.concatenate([o_ch[c][h] for c in C], axis=0) for h in H]
    y_ref[...] = _gla_out(o_heads, z_ref[...], gn[...])

    @pl.when(pl.program_id(1) == last)
    def _():
        for h in range(G_HEADS):
            s_ref[0, h] = st_ref[h].T


def _gla_params(ab, a_bias, gn):
    ab_p = jnp.zeros((LANES, G_QK), F32).at[SM_GA:SM_GA + G_LORA].set(ab).astype(BF16)
    return ab_p, a_bias.reshape(1, -1), gn.reshape(1, -1)


def _gla_prompt(proj, params, batch, seq, tb):
    nt = seq // tb
    cb = lambda c, w: pl.BlockSpec((tb, w), lambda b, t, c=c: (b * nt + t, c // w))
    full = lambda a: pl.BlockSpec(a.shape, lambda b, t: (0,) * a.ndim)
    kern = functools.partial(_gla_prompt_kernel, n_chunks=tb // CHUNK)
    return pl.pallas_call(
        kern,
        grid=(batch, nt),
        in_specs=[cb(C_GQK, 2 * G_QK), cb(C_GV, G_W), cb(C_GZ, G_W), cb(C_SMALL, LANES)]
                 + [full(a) for a in params],
        out_specs=[pl.BlockSpec((tb, G_W), lambda b, t: (b * nt + t, 0)),
                   pl.BlockSpec((1, G_HEADS, G_DK, G_DV), lambda b, t: (b, 0, 0, 0))],
        out_shape=[jax.ShapeDtypeStruct((batch * seq, G_W), F32),
                   jax.ShapeDtypeStruct((batch, G_HEADS, G_DK, G_DV), F32)],
        scratch_shapes=[pltpu.VMEM((G_HEADS, G_DV, G_DK), F32)],
        compiler_params=_cparams(("parallel", "arbitrary")),
        name="gla_prompt",
    )(proj, proj, proj, proj, *params)


def _gla_step_kernel(qk_ref, v_ref, z_ref, sm_ref, s_in, acc_ref, ab, a_bias, gn, y_ref, s_out,
                     vec_ref, vrow_ref, orow_ref, *, bb):
    del acc_ref
    qk = qk_ref[...]
    q = qk[:, 0:G_QK] * (G_DK ** -0.5)
    k = qk[:, G_QK:2 * G_QK]
    g = _gla_log_gate(sm_ref[...], ab[...], a_bias[...])
    vec_ref[0] = q
    vec_ref[1] = k
    vec_ref[2] = jnp.exp(g)
    vrow_ref[...] = v_ref[...]
    eye = (_iota2((G_DK, G_DK), 0) == _iota2((G_DK, G_DK), 1)).astype(F32)
    masks = _outer_masks()

    def per_rows(j, _):
        bs = [j * GLA_STEP_ROWS + i for i in range(GLA_STEP_ROWS)]
        vecs = [[vec_ref[i, pl.ds(b, 1), :] for i in range(3)] for b in bs]
        v_b = [vrow_ref[pl.ds(b, 1), :] for b in bs]
        items = [(bi, h) for bi in range(GLA_STEP_ROWS) for h in range(G_HEADS)]
        n = range(len(items))
        hk = lambda x, h: x[:, h * G_DK:(h + 1) * G_DK]
        q_, k_, e_ = ([hk(vecs[bi][i], h) for bi, h in items] for i in range(3))
        v_h = [v_b[bi][:, h * G_DV:(h + 1) * G_DV] for bi, h in items]
        s = [s_in[bs[bi], h] for bi, h in items]
        e_col = [_lane_sum(eye * e_[i]) for i in n]
        qk_dot = [_lane_sum(q_[i] * k_[i]) for i in n]
        o_s = [_mm(_rows8(q_[i] * e_[i]), s[i])[0:1] for i in n]
        kv = [_mm_tn(*_outer_operands(masks, (k_[i], v_h[i]))) for i in n]
        o = [qk_dot[i] * v_h[i] + o_s[i] for i in n]
        for i, (bi, h) in enumerate(items):
            s_out[bs[bi], h] = s[i] * e_col[i] + kv[i]
        for bi in range(GLA_STEP_ROWS):
            orow_ref[pl.ds(bs[bi], 1), :] = jnp.concatenate(
                [o[bi * G_HEADS + h] for h in range(G_HEADS)], axis=-1)
        return 0

    lax.fori_loop(0, bb // GLA_STEP_ROWS, per_rows, 0)
    o = orow_ref[...]
    o_heads = [o[:, h * G_DV:(h + 1) * G_DV] for h in range(G_HEADS)]
    y_ref[...] = _gla_out(o_heads, z_ref[...], gn[...])


def _gla_step(proj, s_all, acc, params, layer, bb):
    nb = proj.shape[0]
    cb = lambda c, w: pl.BlockSpec((bb, w), lambda i, c=c: (i, c // w))
    full = lambda a: pl.BlockSpec(a.shape, lambda i: (0,) * a.ndim)
    st = _layer_block(s_all.shape[2:], bb, layer)
    kern = functools.partial(_gla_step_kernel, bb=bb)
    return pl.pallas_call(
        kern,
        grid=(nb // bb,),
        in_specs=[cb(C_GQK, 2 * G_QK), cb(C_GV, G_W), cb(C_GZ, G_W), cb(C_SMALL, LANES), st,
                  pl.BlockSpec(memory_space=pl.ANY)]
                 + [full(a) for a in params],
        out_specs=[pl.BlockSpec((bb, G_W), lambda i: (i, 0)), st],
        out_shape=[jax.ShapeDtypeStruct((nb, G_W), F32), jax.ShapeDtypeStruct(acc.shape, F32)],
        input_output_aliases={5: 1},
        scratch_shapes=[pltpu.VMEM((3, bb, G_QK), F32), pltpu.VMEM((bb, G_W), F32),
                        pltpu.VMEM((bb, G_W), F32)],
        compiler_params=_cparams(("parallel",)),
        name="gla_step",
    )(proj, proj, proj, proj, s_all, acc, *params)


def _mlstm_out(h_heads, o_pre, z, gn):
    outs = []
    for h in range(M_HEADS):
        y = h_heads[h]
        y = y - jnp.mean(y, axis=-1, keepdims=True)
        outs.append(y * lax.rsqrt(jnp.mean(y * y, axis=-1, keepdims=True) + EPS))
    return jnp.concatenate(outs, axis=-1) * gn * _sigmoid(o_pre) * _silu(z)


def _mlstm_prompt_kernel(q_ref, k_ref, v_ref, o_ref, z_ref, sm_ref, ifb, gn,
                         y_ref, c_ref, n_ref, m_ref, *, n_chunks):
    L = CHUNK

    @pl.when(pl.program_id(1) == 0)
    def _():
        c_ref[...] = jnp.zeros_like(c_ref)
        n_ref[...] = jnp.zeros_like(n_ref)
        m_ref[...] = jnp.zeros_like(m_ref)

    tril_incl = _iota2((L, L), 1) <= _iota2((L, L), 0)
    scale = M_DK ** -0.5
    neg_inf = -jnp.inf
    C, H = range(n_chunks), range(M_HEADS)
    rws = [slice(c * L, (c + 1) * L) for c in C]
    sls = [slice(h * M_DK, (h + 1) * M_DK) for h in H]

    pre = sm_ref[...] + ifb[...]
    lf = _log_sigmoid(pre)
    b_all = _mm_exact_lhs(_block_tril(n_chunks * L, L), lf)
    lib_all = pre - pltpu.roll(b_all, LANES - (SM_MF - SM_MI), 1)
    q, k, v = q_ref[...] * scale, k_ref[...], v_ref[...]
    qs = [[q[rws[c], sls[h]] for h in H] for c in C]
    k_h = [[k[rws[c], sls[h]] for h in H] for c in C]
    v_h = [[v[rws[c], sls[h]] for h in H] for c in C]
    qk_raw = [[_mm_nt(qs[c][h], k_h[c][h]) for h in H] for c in C]
    d_log, d_max, b_col, b_last, lib = [], [], [], [], []
    for c in C:
        lib_t = lib_all[rws[c]].T
        b_col.append([b_all[rws[c], SM_MF + h:SM_MF + h + 1] for h in H])
        lib.append([lib_all[rws[c], SM_MI + h:SM_MI + h + 1] for h in H])
        b_last.append([b_col[c][h][L - 1:L, :] for h in H])
        d_log.append([jnp.where(tril_incl, b_col[c][h] + lib_t[SM_MI + h:SM_MI + h + 1, :], neg_inf)
                      for h in H])
        d_max.append([jnp.max(d_log[c][h], axis=-1, keepdims=True) for h in H])
    m_prev = [m_ref[0, :, h:h + 1] for h in H]
    m_t, w_inter, carry, wk, qk = [], [], [], [], []
    for c in C:
        m_inter = [b_col[c][h] + m_prev[h] for h in H]
        m_t.append([jnp.maximum(m_inter[h], d_max[c][h]) for h in H])
        w_inter.append([jnp.exp(m_inter[h] - m_t[c][h]) for h in H])
        m_new = [m_t[c][h][L - 1:L, :] for h in H]
        carry.append([jnp.exp(b_last[c][h] + m_prev[h] - m_new[h]) for h in H])
        wk.append([jnp.exp(b_last[c][h] + lib[c][h] - m_new[h]) * k_h[c][h] for h in H])
        qk.append([qk_raw[c][h] * jnp.exp(d_log[c][h] - m_t[c][h]) for h in H])
        m_prev = m_new
    for h in H:
        m_ref[0, :, h:h + 1] = m_prev[h]
    upd = [[_mm_tn(wk[c][h], v_h[c][h]) for h in H] for c in C]
    qkv = [[_mm(qk[c][h], v_h[c][h]) for h in H] for c in C]
    cs = [c_ref[0, h] for h in H]
    ns = [n_ref[0, h:h + 1, :] for h in H]
    h_ch = []
    for c in C:
        qc = [_mm(qs[c][h], cs[h]) for h in H]
        hh = []
        for h in H:
            num = qkv[c][h] + w_inter[c][h] * qc[h]
            den = _lane_sum(qk[c][h]) + w_inter[c][h] * _lane_sum(qs[c][h] * ns[h])
            hh.append(num / jnp.maximum(jnp.abs(den), jnp.exp(-m_t[c][h])))
        h_ch.append(hh)
        cs = [carry[c][h] * cs[h] + upd[c][h] for h in H]
        ns = [carry[c][h] * ns[h] + jnp.sum(wk[c][h], axis=0, keepdims=True) for h in H]
    for h in H:
        c_ref[0, h] = cs[h]
        n_ref[0, h:h + 1, :] = ns[h]
    h_heads = [jnp.concatenate([h_ch[c][h] for c in C], axis=0) for h in H]
    y_ref[...] = _mlstm_out(h_heads, o_ref[...], z_ref[...], gn[...])


def _mlstm_params(ifb, gn):
    ifb_p = jnp.zeros((1, LANES), F32).at[0, SM_MI:SM_MI + 2 * M_HEADS].set(ifb)
    return ifb_p, gn.reshape(1, -1)


def _mlstm_prompt(proj, params, batch, seq, tb):
    nt = seq // tb
    cb = lambda c, w: pl.BlockSpec((tb, w), lambda b, t, c=c: (b * nt + t, c // w))
    full = lambda a: pl.BlockSpec(a.shape, lambda b, t: (0,) * a.ndim)
    kern = functools.partial(_mlstm_prompt_kernel, n_chunks=tb // CHUNK)
    return pl.pallas_call(
        kern,
        grid=(batch, nt),
        in_specs=[cb(C_MQ, M_QK), cb(C_MK, M_QK), cb(C_MV, M_W), cb(C_MO, M_W), cb(C_MZ, M_W),
                  cb(C_SMALL, LANES)] + [full(a) for a in params],
        out_specs=[pl.BlockSpec((tb, M_W), lambda b, t: (b * nt + t, 0)),
                   pl.BlockSpec((1, M_HEADS, M_DK, M_DV), lambda b, t: (b, 0, 0, 0)),
                   pl.BlockSpec((1, M_HEADS, M_DK), lambda b, t: (b, 0, 0)),
                   pl.BlockSpec((1, 1, M_HEADS), lambda b, t: (b, 0, 0))],
        out_shape=[jax.ShapeDtypeStruct((batch * seq, M_W), F32),
                   jax.ShapeDtypeStruct((batch, M_HEADS, M_DK, M_DV), F32),
                   jax.ShapeDtypeStruct((batch, M_HEADS, M_DK), F32),
                   jax.ShapeDtypeStruct((batch, 1, M_HEADS), F32)],
        compiler_params=_cparams(("parallel", "arbitrary")),
        name="mlstm_prompt",
    )(proj, proj, proj, proj, proj, proj, *params)


def _mlstm_step_kernel(q_ref, k_ref, v_ref, o_ref, z_ref, sm_ref, c_in, n_in, m_in, acc_ref, ifb, gn,
                       y_ref, c_out, n_out, m_out, vec_ref, sc_ref, hrow_ref, *, bb):
    del acc_ref
    pre = sm_ref[...] + ifb[...]
    li = pre[:, SM_MI:SM_MI + M_HEADS]
    lf = _log_sigmoid(pre[:, SM_MF:SM_MF + M_HEADS])
    m0 = m_in[...]
    m_inter = lf + m0
    m_t = jnp.maximum(m_inter, li)
    m_out[...] = m_t
    pad = lambda x: jnp.concatenate([x, jnp.zeros((bb, LANES - M_HEADS), F32)], axis=-1)
    sc_ref[0] = pad(jnp.exp(m_inter - m_t))
    sc_ref[1] = pad(jnp.exp(li - m_t))
    sc_ref[2] = pad(jnp.exp(-m_t))
    vec_ref[0] = q_ref[...] * (M_DK ** -0.5)
    vec_ref[1] = k_ref[...]
    vec_ref[2] = v_ref[...]
    masks = _outer_masks()

    def per_rows(j, _):
        bs = [j * STEP_ROWS + i for i in range(STEP_ROWS)]
        vecs = [[vec_ref[i, pl.ds(b, 1), :] for i in range(3)] for b in bs]
        scs = [[sc_ref[i, pl.ds(b, 1), :] for i in range(3)] for b in bs]
        items = [(bi, h) for bi in range(STEP_ROWS) for h in range(M_HEADS)]
        n_it = range(len(items))
        hs = lambda x, h: x[:, h * M_DK:(h + 1) * M_DK]
        q_h, k_h, v_h = ([hs(vecs[bi][i], h) for bi, h in items] for i in range(3))
        wi, ei, en = ([scs[bi][i][:, h:h + 1] for bi, h in items] for i in range(3))
        c = [c_in[bs[bi], h] for bi, h in items]
        n = [n_in[bs[bi], h:h + 1, :] for bi, h in items]
        qk = [_lane_sum(q_h[i] * k_h[i]) * ei[i] for i in n_it]
        qn = [_lane_sum(q_h[i] * n[i]) for i in n_it]
        qc = [_mm(_rows8(q_h[i]), c[i])[0:1] for i in n_it]
        kv = [_mm_tn(*_outer_operands(masks, (ei[i] * k_h[i], v_h[i]))) for i in n_it]
        hh = []
        for i, (bi, h) in enumerate(items):
            num = qk[i] * v_h[i] + wi[i] * qc[i]
            den = qk[i] + wi[i] * qn[i]
            hh.append(num / jnp.maximum(jnp.abs(den), en[i]))
            c_out[bs[bi], h] = wi[i] * c[i] + kv[i]
            n_out[bs[bi], h:h + 1, :] = wi[i] * n[i] + ei[i] * k_h[i]
        for bi in range(STEP_ROWS):
            hrow_ref[pl.ds(bs[bi], 1), :] = jnp.concatenate(
                [hh[bi * M_HEADS + h] for h in range(M_HEADS)], axis=-1)
        return 0

    lax.fori_loop(0, bb // STEP_ROWS, per_rows, 0)
    hh = hrow_ref[...]
    h_heads = [hh[:, h * M_DV:(h + 1) * M_DV] for h in range(M_HEADS)]
    y_ref[...] = _mlstm_out(h_heads, o_ref[...], z_ref[...], gn[...])


def _mlstm_step(proj, c_all, n_all, m_all, acc, params, layer, bb):
    nb = proj.shape[0]
    cb = lambda c, w: pl.BlockSpec((bb, w), lambda i, c=c: (i, c // w))
    full = lambda a: pl.BlockSpec(a.shape, lambda i: (0,) * a.ndim)
    cs = _layer_block(c_all.shape[2:], bb, layer)
    ns_in = _layer_block(n_all.shape[2:], bb, layer)
    ms_in = _layer_block(m_all.shape[2:], bb, layer)
    ns = pl.BlockSpec((bb, M_HEADS, M_DK), lambda i: (i, 0, 0))
    ms = pl.BlockSpec((bb, M_HEADS), lambda i: (i, 0))
    kern = functools.partial(_mlstm_step_kernel, bb=bb)
    return pl.pallas_call(
        kern,
        grid=(nb // bb,),
        in_specs=[cb(C_MQ, M_QK), cb(C_MK, M_QK), cb(C_MV, M_W), cb(C_MO, M_W), cb(C_MZ, M_W),
                  cb(C_SMALL, LANES), cs, ns_in, ms_in, pl.BlockSpec(memory_space=pl.ANY)]
                 + [full(a) for a in params],
        out_specs=[pl.BlockSpec((bb, M_W), lambda i: (i, 0)), cs, ns, ms],
        out_shape=[jax.ShapeDtypeStruct((nb, M_W), F32), jax.ShapeDtypeStruct(acc.shape, F32),
                   jax.ShapeDtypeStruct(n_all.shape[1:], F32), jax.ShapeDtypeStruct(m_all.shape[1:], F32)],
        input_output_aliases={9: 1},
        scratch_shapes=[pltpu.VMEM((3, bb, M_QK), F32), pltpu.VMEM((3, bb, LANES), F32),
                        pltpu.VMEM((bb, M_W), F32)],
        compiler_params=_cparams(("parallel",)),
        name="mlstm_step",
    )(proj, proj, proj, proj, proj, proj, c_all, n_all, m_all, acc, *params)


def _pad_w_in(w_in):
    sizes = (R_SHIFT_W, R_W, G_QK, G_QK, G_W, G_LORA, G_W, M_QK, M_QK, M_W, M_HEADS, M_HEADS, M_W, M_W,
             3 * D_MODEL)
    offs = [0]
    for s in sizes:
        offs.append(offs[-1] + s)
    seg = lambda i: w_in[..., offs[i]:offs[i + 1]]
    (r_in, r_z, g_q, g_k, g_v, g_a, g_z, m_q, m_k, m_v, m_i, m_f, m_o, m_z, gate) = (seg(i) for i in range(15))
    zeros = lambda n: jnp.zeros(w_in.shape[:-1] + (n,), w_in.dtype)
    small = jnp.concatenate([g_a, m_i, m_f, zeros(LANES - G_LORA - 2 * M_HEADS)], axis=-1)
    w = jnp.concatenate([gate, r_z, g_v, g_z, m_q, m_k, m_v, m_o, m_z, g_q, g_k, r_in, small,
                         zeros(N_PAD - N_USED)], axis=-1)
    return w.astype(BF16)


def _pick(n, prefs):
    for p in prefs:
        if n % p == 0:
            return p
    return n


def kernel(x_prompt, x_sample, state_rwkv_shift, state_rwkv, state_gla, state_mlstm_C, state_mlstm_n,
           state_mlstm_m, norm_pre, norm_post, w_in, r_mu_shift, r_w_lora_b, r_w0, r_a_lora_b, r_a0,
           r_k_k, r_k_a, r_r_k, r_gn, g_a_lora_b, g_a_bias, g_gn, m_if_bias, m_gn, w_br_rwkv, w_br_gla,
           w_br_mlstm, w_out):
    depth = w_in.shape[0]
    bp, tp, _ = x_prompt.shape
    bs, ts, _ = x_sample.shape
    assert ts == 1 and tp % CHUNK == 0
    w_pad = _pad_w_in(w_in)
    wr, wg, wm, wo = (w.astype(BF16) for w in (w_br_rwkv, w_br_gla, w_br_mlstm, w_out))

    mp = bp * tp
    tm_in = _pick(mp, (2048, 1024, 512, 256, 128, 64))
    tm_mg = _pick(mp, (512, 256, 128, 64))
    tb = _pick(tp, (256, 128, 64))
    bb = _pick(bs, (16, 8))

    xp = x_prompt.reshape(mp, D_MODEL)
    xs = x_sample.reshape(bs, D_MODEL)
    new_p = [[] for _ in range(6)]
    new_s = [[] for _ in range(3)]
    acc_r, acc_g, acc_c = (jnp.zeros(s.shape, F32) for s in (state_rwkv, state_gla, state_mlstm_C))
    for l in range(depth):
        g_pre = norm_pre[l].reshape(1, -1)
        g_post = norm_post[l].reshape(1, -1)
        rp = _rwkv_params(r_mu_shift[l], r_w_lora_b[l], r_w0[l], r_a_lora_b[l], r_a0[l], r_k_k[l],
                          r_k_a[l], r_r_k[l], r_gn[l])
        gp = _gla_params(g_a_lora_b[l], g_a_bias[l], g_gn[l])
        mparams = _mlstm_params(m_if_bias[l], m_gn[l])

        proj = _inproj(xp, g_pre, w_pad, l, tm_in)
        y_r, s_r = _rwkv_prompt(proj, rp, bp, tp, tb)
        y_g, s_g = _gla_prompt(proj, gp, bp, tp, tb)
        y_m, c, n, m = _mlstm_prompt(proj, mparams, bp, tp, tb)
        xp = _merge(proj, y_r, y_g, y_m, xp, wr, wg, wm, wo, g_post, l, tm_mg)
        shift = proj.reshape(bp, tp, N_PAD)[:, tp - 1, C_RIN:C_RIN + R_SHIFT_W]
        for lst, val in zip(new_p, (shift, s_r, s_g, c, n, m.reshape(bp, M_HEADS))):
            lst.append(val)

        proj = _inproj(xs, g_pre, w_pad, l, bs)
        y_r, acc_r = _rwkv_step(proj, state_rwkv_shift, state_rwkv, acc_r, rp, l, bb)
        y_g, acc_g = _gla_step(proj, state_gla, acc_g, gp, l, bb)
        y_m, acc_c, n, m = _mlstm_step(proj, state_mlstm_C, state_mlstm_n, state_mlstm_m, acc_c,
                                       mparams, l, bb)
        xs = _merge(proj, y_r, y_g, y_m, xs, wr, wg, wm, wo, g_post, l, bs)
        shift = proj[:, C_RIN:C_RIN + R_SHIFT_W]
        for lst, val in zip(new_s, (shift, n, m)):
            lst.append(val)

    stk = lambda lst: jnp.stack(lst)
    return (xp.reshape(bp, tp, D_MODEL), xs.reshape(bs, ts, D_MODEL),
            *(stk(v) for v in new_p),
            stk(new_s[0]), acc_r, acc_g, acc_c, stk(new_s[1]), stk(new_s[2]))
```

```python
import functools
import math

import jax
import jax.numpy as jnp
from jax import lax
from jax.experimental import pallas as pl
from jax.experimental.pallas import tpu as pltpu

F32 = jnp.float32
BF16 = jnp.bfloat16

D_MODEL = 1024
R_HEADS, R_HD = 8, 64
R_W = R_HEADS * R_HD
R_LORA = 64
R_SHIFT_W = 3 * R_W + 2 * R_LORA
R_GN_EPS = 64e-5
G_HEADS, G_DK, G_DV = 4, 64, 128
G_QK = G_HEADS * G_DK
G_W = G_HEADS * G_DV
G_LORA = 16
G_GATE_TEMP = 16.0
M_HEADS, M_DK, M_DV = 4, 128, 128
M_QK = M_HEADS * M_DK
M_W = M_HEADS * M_DV
EPS = 1e-6

LANES = 128
VMEM_LIMIT = 48 * 1024 * 1024

C_GATE = 0
C_RZ = 3072
C_GV = 3584
C_GZ = 4096
C_MQ = 4608
C_MK = 5120
C_MV = 5632
C_MO = 6144
C_MZ = 6656
C_GQK = 7168
C_RIN = 7680
C_RLORA = C_RIN + 3 * R_W
C_SMALL = C_RLORA + LANES
SM_GA, SM_MI, SM_MF = 0, 16, 20
N_USED = C_SMALL + LANES
PROJ_TN = 512
N_PAD = -(-N_USED // PROJ_TN) * PROJ_TN
N_MAIN_BLOCKS = C_RLORA // PROJ_TN
SIDE_W = 2 * LANES
S_RLORA, S_SMALL = 0, LANES

CHUNK = 64


def _cparams(sem):
    return pltpu.CompilerParams(dimension_semantics=sem, vmem_limit_bytes=VMEM_LIMIT)


def _mm(a, b):
    return jnp.dot(a.astype(BF16), b.astype(BF16), preferred_element_type=F32)


def _mm_nt(a, b):
    return lax.dot_general(a.astype(BF16), b.astype(BF16), (((1,), (1,)), ((), ())),
                           preferred_element_type=F32)


def _mm_tn(a, b):
    return lax.dot_general(a.astype(BF16), b.astype(BF16), (((0,), (0,)), ((), ())),
                           preferred_element_type=F32)


def _mm_exact_lhs(m01, x):
    hi = x.astype(BF16)
    r1 = x - hi.astype(F32)
    mid = r1.astype(BF16)
    lo = (r1 - mid.astype(F32)).astype(BF16)
    m = m01.astype(BF16)
    d = lambda p: jnp.dot(m, p, preferred_element_type=F32)
    return d(hi) + d(mid) + d(lo)


def _sigmoid(x):
    return 1.0 / (1.0 + jnp.exp(-x))


def _silu(x):
    return x * _sigmoid(x)


def _softplus(x):
    return jnp.maximum(x, 0.0) + jnp.log(1.0 + jnp.exp(-jnp.abs(x)))


def _log_sigmoid(x):
    return -_softplus(-x)


def _iota2(shape, dim):
    return lax.broadcasted_iota(jnp.int32, shape, dim)


def _lane_sum(x):
    return jnp.sum(x, axis=-1, keepdims=True)


def _rows8(x):
    return jnp.broadcast_to(x, (8, x.shape[1]))


def _outer_masks():
    rid = _iota2((8, 1), 0)
    on = lambda *rows: sum((rid == r).astype(F32) for r in rows)
    return (on(0, 2), on(1), on(0, 1), on(2)), (on(3, 5), on(4), on(3, 4), on(5))


def _outer_operands(masks, *pairs):
    a_op, b_op = 0.0, 0.0
    for (a, b), (m_ahi, m_alo, m_bhi, m_blo) in zip(pairs, masks):
        a_hi = a.astype(BF16).astype(F32)
        b_hi = b.astype(BF16).astype(F32)
        a_op = a_op + m_ahi * a_hi + m_alo * (a - a_hi)
        b_op = b_op + m_bhi * b_hi + m_blo * (b - b_hi)
    return a_op, b_op


def _block_tril(n, blk):
    r, c = _iota2((n, n), 0), _iota2((n, n), 1)
    return ((r - c).astype(jnp.uint32) <= (r % blk).astype(jnp.uint32)).astype(F32)


def _inproj_kernel(x_ref, g_ref, w_ref, o_ref, side_ref, h_ref):
    j = pl.program_id(1)

    @pl.when(j == 0)
    def _():
        x = x_ref[...]
        y = x * lax.rsqrt(jnp.mean(x * x, axis=-1, keepdims=True) + EPS)
        h_ref[...] = (y * g_ref[...]).astype(BF16)

    acc = jnp.dot(h_ref[...], w_ref[...], preferred_element_type=F32)

    @pl.when(j < N_MAIN_BLOCKS)
    def _():
        o_ref[...] = acc.astype(BF16)

    @pl.when(j == N_MAIN_BLOCKS)
    def _():
        side_ref[...] = acc[:, 0:SIDE_W]


def _inproj(x2d, g_pre, w_pad, layer, tm):
    m = x2d.shape[0]
    return pl.pallas_call(
        _inproj_kernel,
        grid=(m // tm, N_MAIN_BLOCKS + 1),
        in_specs=[pl.BlockSpec((tm, D_MODEL), lambda i, j: (i, 0)),
                  pl.BlockSpec((1, D_MODEL), lambda i, j: (0, 0)),
                  pl.BlockSpec((None, D_MODEL, PROJ_TN), lambda i, j: (layer, 0, j))],
        out_specs=[pl.BlockSpec((tm, PROJ_TN), lambda i, j: (i, jnp.minimum(j, N_MAIN_BLOCKS - 1))),
                   pl.BlockSpec((tm, SIDE_W), lambda i, j: (i, 0))],
        out_shape=[jax.ShapeDtypeStruct((m, C_RLORA), BF16),
                   jax.ShapeDtypeStruct((m, SIDE_W), F32)],
        scratch_shapes=[pltpu.VMEM((tm, D_MODEL), BF16)],
        compiler_params=_cparams(("parallel", "arbitrary")),
        name="inproj",
    )(x2d, g_pre, w_pad)


def _merge_kernel(gate_ref, yr_ref, yg_ref, ym_ref, x_ref, wr_ref, wg_ref, wm_ref, wo_ref, gp_ref, o_ref):
    d = D_MODEL
    gate = lambda b: _sigmoid(gate_ref[:, b * d:(b + 1) * d].astype(F32))
    merged = (gate(0) * _mm(yr_ref[...], wr_ref[...]) + gate(1) * _mm(yg_ref[...], wg_ref[...])
              + gate(2) * _mm(ym_ref[...], wm_ref[...]))
    o = _mm(merged, wo_ref[...])
    y = o * lax.rsqrt(jnp.mean(o * o, axis=-1, keepdims=True) + EPS)
    o_ref[...] = x_ref[...] + y * gp_ref[...]


def _merge(proj, y_r, y_g, y_m, x2d, w_r, w_g, w_m, w_o, g_post, layer, tm):
    m = x2d.shape[0]
    row = lambda w: pl.BlockSpec((tm, w), lambda i: (i, 0))
    full = lambda a: pl.BlockSpec(a.shape, lambda i: (0, 0))
    wl = lambda a: pl.BlockSpec((None,) + a.shape[1:], lambda i: (layer, 0, 0))
    return pl.pallas_call(
        _merge_kernel,
        grid=(m // tm,),
        in_specs=[row(3 * D_MODEL), row(R_W), row(G_W), row(M_W), row(D_MODEL),
                  wl(w_r), wl(w_g), wl(w_m), wl(w_o), full(g_post)],
        out_specs=row(D_MODEL),
        out_shape=jax.ShapeDtypeStruct((m, D_MODEL), F32),
        compiler_params=_cparams(("parallel",)),
        name="merge",
    )(proj, y_r, y_g, y_m, x2d, w_r, w_g, w_m, w_o, g_post)


def _rwkv_front(r_in, k_in, v_in, l_in, prev_r, prev_k, prev_v, prev_l, p):
    (mu_r, mu_k, mu_v, mu_l, wb, w0, ab, a0, k_k, k_a) = p
    r = r_in + mu_r * (prev_r - r_in)
    k = k_in + mu_k * (prev_k - k_in)
    v = v_in + mu_v * (prev_v - v_in)
    lo = l_in + mu_l * (prev_l - l_in)
    log_w = -_softplus(-(w0 + _mm(jnp.tanh(lo), wb))) - 0.5
    lw = -jnp.exp(log_w)
    a = _sigmoid(a0 + _mm(lo, ab))
    kk = k * k_k
    k2 = k * (1.0 + (a - 1.0) * k_a)
    return r, k2, v, lw, a, kk


def _rwkv_out(y_heads, r, k2, v, z, r_k, gn):
    outs = []
    for h in range(R_HEADS):
        sl = slice(h * R_HD, (h + 1) * R_HD)
        y = y_heads[h]
        y = y - jnp.mean(y, axis=-1, keepdims=True)
        y = y * lax.rsqrt(jnp.mean(y * y, axis=-1, keepdims=True) + R_GN_EPS)
        bonus = _lane_sum(r[:, sl] * k2[:, sl] * r_k[:, sl]) * v[:, sl]
        outs.append(y * gn[:, sl] + bonus)
    return jnp.concatenate(outs, axis=-1) * _silu(z)


def _rwkv_prompt_kernel(z_ref, r_ref, k_ref, v_ref, l_ref,
                        mu_r, mu_k, mu_v, mu_l, wb, w0, ab, a0, k_k, k_a, r_k, gn,
                        y_ref, s_ref, carry_ref, *, n_chunks):
    L = CHUNK

    @pl.when(pl.program_id(1) == 0)
    def _():
        s_ref[...] = jnp.zeros_like(s_ref)
        carry_ref[...] = jnp.zeros_like(carry_ref)

    tb = n_chunks * L
    params = tuple(x[...] for x in (mu_r, mu_k, mu_v, mu_l, wb, w0, ab, a0, k_k, k_a))
    row = _iota2((L, L), 0)
    col = _iota2((L, L), 1)
    eye = (col == row).astype(F32)
    blk_mask = (row // 16) == (col // 16)
    r2 = _iota2((2 * L, 2 * L), 0)
    c2 = _iota2((2 * L, 2 * L), 1)
    tr, tc = r2 % L, c2 % L
    g_mask = tc < tr + (r2 >= L).astype(jnp.int32)
    first_row = _iota2((tb, 1), 0) == 0

    def shifted(x, carry):
        return jnp.where(first_row, carry, pltpu.roll(x, 1, 0))

    r_in, k_in, v_in = (x[...].astype(F32) for x in (r_ref, k_ref, v_ref))
    l_in = l_ref[...]
    prev_r = shifted(r_in, carry_ref[0:1, 0:R_W])
    prev_k = shifted(k_in, carry_ref[0:1, R_W:2 * R_W])
    prev_v = shifted(v_in, carry_ref[0:1, 2 * R_W:3 * R_W])
    prev_l = shifted(l_in, carry_ref[0:1, 3 * R_W:R_SHIFT_W])
    carry_ref[0:1, 0:R_W] = r_in[tb - 1:tb, :]
    carry_ref[0:1, R_W:2 * R_W] = k_in[tb - 1:tb, :]
    carry_ref[0:1, 2 * R_W:3 * R_W] = v_in[tb - 1:tb, :]
    carry_ref[0:1, 3 * R_W:R_SHIFT_W] = l_in[tb - 1:tb, :]
    r, k2, v, lw, a, kk = _rwkv_front(r_in, k_in, v_in, l_in, prev_r, prev_k, prev_v, prev_l, params)

    cum = _mm_exact_lhs(_block_tril(tb, L), lw)
    p_inc = jnp.exp(cum)
    p_exc = jnp.exp(cum - lw)
    p_inv = jnp.exp(-cum)
    r_t = r * p_inc
    k_h = k2 * p_inv

    CH = [(c, h) for c in range(n_chunks) for h in range(R_HEADS)]
    N = range(len(CH))
    lhs2, rhs2, rhs2_l, v_h, p_last = [], [], [], [], []
    for h in range(R_HEADS):
        sl = slice(h * R_HD, (h + 1) * R_HD)
        kk_h = kk[:, sl]
        kap = kk_h * (1.0 / jnp.maximum(jnp.sqrt(_lane_sum(kk_h * kk_h)), 1e-12))
        beta_h = kap * a[:, sl] * p_inv[:, sl]
        kap_t = kap * p_exc[:, sl]
        lhs2.append([jnp.concatenate([kap_t[c * L:(c + 1) * L], r_t[c * L:(c + 1) * L, sl]], axis=0)
                     for c in range(n_chunks)])
        rhs2.append([jnp.concatenate([beta_h[c * L:(c + 1) * L], k_h[c * L:(c + 1) * L, sl]], axis=0)
                     for c in range(n_chunks)])
        p_last.append([p_inc[(c + 1) * L - 1:(c + 1) * L, sl] for c in range(n_chunks)])
        v_h.append([v[c * L:(c + 1) * L, sl] for c in range(n_chunks)])
    lhs2 = [lhs2[h][c] for c, h in CH]
    rhs2 = [rhs2[h][c] for c, h in CH]
    p_last = [p_last[h][c] for c, h in CH]
    v_h = [v_h[h][c] for c, h in CH]
    rhs2_l = [rhs2[i] * p_last[i] for i in N]
    g = [jnp.where(g_mask, _mm_nt(lhs2[i], rhs2[i]), 0.0) for i in N]
    a_m = [g[i][0:L, 0:L] for i in N]
    d = [jnp.where(blk_mask, a_m[i], 0.0) for i in N]
    akv = [_mm(g[i][0:L, L:2 * L], v_h[i]) for i in N]
    d2 = [_mm(d[i], d[i]) for i in N]
    d4 = [_mm(d2[i], d2[i]) for i in N]
    dd2 = [_mm(d[i], d2[i]) for i in N]
    d8 = [_mm(d4[i], d4[i]) for i in N]
    x = [eye - d[i] + d2[i] - dd2[i] for i in N]
    x = [x[i] + _mm(x[i], d4[i]) for i in N]
    td = [x[i] + _mm(x[i], d8[i]) for i in N]
    m = [_mm(td[i], a_m[i] - d[i]) for i in N]
    m2 = [_mm(m[i], m[i]) for i in N]
    q = [eye - m[i] + m2[i] - _mm(m[i], m2[i]) for i in N]
    t_inv = [_mm(q[i], td[i]) for i in N]
    kbar = [_mm(t_inv[i], lhs2[i][0:L]) for i in N]
    u0 = [-_mm(t_inv[i], akv[i]) for i in N]
    uv0 = [jnp.concatenate([u0[i], v_h[i]], axis=0) for i in N]
    n_c = [_mm_tn(kbar[i], rhs2_l[i][0:L]) for i in N]
    c_c = [_mm_tn(uv0[i], rhs2_l[i]) for i in N]
    qt = [lhs2[i][L:2 * L] - _mm(g[i][L:2 * L, 0:L], kbar[i]) for i in N]
    y0 = [_mm(g[i][L:2 * L, :], uv0[i]) for i in N]
    s = [s_ref[0, h] for h in range(R_HEADS)]
    y_ch = []
    for c in range(n_chunks):
        ids = [c * R_HEADS + h for h in range(R_HEADS)]
        y_ch.append([_mm_nt(qt[i], s[h]) + y0[i] for h, i in enumerate(ids)])
        s = [s[h] * p_last[i] - _mm(s[h], n_c[i]) + c_c[i] for h, i in enumerate(ids)]
    for h in range(R_HEADS):
        s_ref[0, h] = s[h]
    y_heads = [jnp.concatenate([y_ch[c][h] for c in range(n_chunks)], axis=0) for h in range(R_HEADS)]
    y_ref[...] = _rwkv_out(y_heads, r, k2, v, z_ref[...].astype(F32), r_k[...], gn[...])


def _rwkv_params(mu, wb, w0, ab, a0, k_k, k_a, r_k, gn):
    row = lambda x: x.reshape(1, -1)
    zeros = jnp.zeros((R_LORA, R_W), F32)
    wb_p = jnp.concatenate([wb, zeros], axis=0).astype(BF16)
    ab_p = jnp.concatenate([zeros, ab], axis=0).astype(BF16)
    return (row(mu[0:R_W]), row(mu[R_W:2 * R_W]), row(mu[2 * R_W:3 * R_W]), row(mu[3 * R_W:]),
            wb_p, row(w0), ab_p, row(a0), row(k_k), row(k_a), row(r_k), row(gn))


def _rwkv_prompt(proj, side, params, batch, seq, tb):
    nt = seq // tb
    cb = lambda c, w: pl.BlockSpec((tb, w), lambda b, t, c=c: (b * nt + t, c // w))
    full = lambda a: pl.BlockSpec(a.shape, lambda b, t: (0,) * a.ndim)
    kern = functools.partial(_rwkv_prompt_kernel, n_chunks=tb // CHUNK)
    return pl.pallas_call(
        kern,
        grid=(batch, nt),
        in_specs=[cb(C_RZ, R_W), cb(C_RIN, R_W), cb(C_RIN + R_W, R_W), cb(C_RIN + 2 * R_W, R_W),
                  cb(S_RLORA, LANES)] + [full(a) for a in params],
        out_specs=[pl.BlockSpec((tb, R_W), lambda b, t: (b * nt + t, 0)),
                   pl.BlockSpec((1, R_HEADS, R_HD, R_HD), lambda b, t: (b, 0, 0, 0))],
        out_shape=[jax.ShapeDtypeStruct((batch * seq, R_W), F32),
                   jax.ShapeDtypeStruct((batch, R_HEADS, R_HD, R_HD), F32)],
        scratch_shapes=[pltpu.VMEM((8, R_SHIFT_W), F32)],
        compiler_params=_cparams(("parallel", "arbitrary")),
        name="rwkv_prompt",
    )(proj, proj, proj, proj, side, *params)


STEP_ROWS = 4
GLA_STEP_ROWS = 2


def _rwkv_step_kernel(z_ref, r_ref, k_ref, v_ref, l_ref, pr_ref, pk_ref, pv_ref, pl_ref, s_in, acc_ref,
                      mu_r, mu_k, mu_v, mu_l, wb, w0, ab, a0, k_k, k_a, r_k, gn,
                      y_ref, s_out, vec_ref, yrow_ref, *, bb):
    del acc_ref
    params = tuple(x[...] for x in (mu_r, mu_k, mu_v, mu_l, wb, w0, ab, a0, k_k, k_a))
    r, k2, v, lw, a, kk = _rwkv_front(r_ref[...].astype(F32), k_ref[...].astype(F32), v_ref[...].astype(F32),
                                      l_ref[...], pr_ref[...], pk_ref[...], pv_ref[...], pl_ref[...], params)
    w = jnp.exp(lw)
    kaps, betas = [], []
    for h in range(R_HEADS):
        sl = slice(h * R_HD, (h + 1) * R_HD)
        kk_h = kk[:, sl]
        kap = kk_h / jnp.maximum(jnp.sqrt(_lane_sum(kk_h * kk_h)), 1e-12)
        kaps.append(kap)
        betas.append(kap * a[:, sl])
    vec_ref[0] = jnp.concatenate(kaps, axis=-1)
    vec_ref[1] = jnp.concatenate(betas, axis=-1)
    vec_ref[2] = w
    vec_ref[3] = k2
    vec_ref[4] = v
    vec_ref[5] = r
    masks = _outer_masks()

    def per_rows(j, _):
        bs = [j * STEP_ROWS + i for i in range(STEP_ROWS)]
        vecs = [[vec_ref[i, pl.ds(b, 1), :] for i in range(6)] for b in bs]
        items = [(bi, h) for bi in range(STEP_ROWS) for h in range(R_HEADS)]
        hs = lambda x, h: x[:, h * R_HD:(h + 1) * R_HD]
        kap, beta, w_, k_, v_, r_ = ([hs(vecs[bi][i], h) for bi, h in items] for i in range(6))
        n = range(len(items))
        s = [s_in[bs[bi], h] for bi, h in items]
        sa = [-_mm_nt(_rows8(kap[i]), s[i])[0:1] for i in n]
        ops = [_outer_operands(masks, (sa[i], beta[i]), (v_[i], k_[i])) for i in n]
        s_new = [s[i] * w_[i] + _mm_tn(*ops[i]) for i in n]
        for i, (bi, h) in enumerate(items):
            s_out[bs[bi], h] = s_new[i]
        y_row = [_mm_nt(_rows8(r_[i]), s_new[i])[0:1] for i in n]
        for bi in range(STEP_ROWS):
            yrow_ref[pl.ds(bs[bi], 1), :] = jnp.concatenate(
                [y_row[bi * R_HEADS + h] for h in range(R_HEADS)], axis=-1)
        return 0

    lax.fori_loop(0, bb // STEP_ROWS, per_rows, 0)
    y = yrow_ref[...]
    y_heads = [y[:, h * R_HD:(h + 1) * R_HD] for h in range(R_HEADS)]
    y_ref[...] = _rwkv_out(y_heads, r, k2, v, z_ref[...].astype(F32), r_k[...], gn[...])


def _layer_block(shape_tail, bb, layer):
    zeros = (0,) * len(shape_tail)
    return pl.BlockSpec((None, bb) + tuple(shape_tail), lambda i: (layer, i) + zeros)


def _rwkv_step(proj, side, shift_all, s_all, acc, params, layer, bb):
    nb = proj.shape[0]
    cb = lambda c, w: pl.BlockSpec((bb, w), lambda i, c=c: (i, c // w))
    sh = lambda c, w: pl.BlockSpec((None, bb, w), lambda i, c=c: (layer, i, c // w))
    full = lambda a: pl.BlockSpec(a.shape, lambda i: (0,) * a.ndim)
    st = _layer_block(s_all.shape[2:], bb, layer)
    kern = functools.partial(_rwkv_step_kernel, bb=bb)
    return pl.pallas_call(
        kern,
        grid=(nb // bb,),
        in_specs=[cb(C_RZ, R_W), cb(C_RIN, R_W), cb(C_RIN + R_W, R_W), cb(C_RIN + 2 * R_W, R_W),
                  cb(S_RLORA, LANES),
                  sh(0, R_W), sh(R_W, R_W), sh(2 * R_W, R_W), sh(3 * R_W, LANES), st,
                  pl.BlockSpec(memory_space=pl.ANY)]
                 + [full(a) for a in params],
        out_specs=[pl.BlockSpec((bb, R_W), lambda i: (i, 0)), st],
        out_shape=[jax.ShapeDtypeStruct((nb, R_W), F32),
                   jax.ShapeDtypeStruct(acc.shape, F32)],
        input_output_aliases={10: 1},
        scratch_shapes=[pltpu.VMEM((6, bb, R_W), F32), pltpu.VMEM((bb, R_W), F32)],
        compiler_params=_cparams(("parallel",)),
        name="rwkv_step",
    )(proj, proj, proj, proj, side, shift_all, shift_all, shift_all, shift_all, s_all, acc, *params)


def _gla_log_gate(small, ab_p, a_bias):
    return _log_sigmoid(_mm(small, ab_p) + a_bias) / G_GATE_TEMP


def _gla_out(o_heads, z, gn):
    outs = []
    for h in range(G_HEADS):
        o = o_heads[h]
        outs.append(o * lax.rsqrt(jnp.mean(o * o, axis=-1, keepdims=True) + EPS))
    return jnp.concatenate(outs, axis=-1) * gn * _silu(z)


def _gla_prompt_kernel(qk_ref, v_ref, z_ref, sm_ref, ab, a_bias, gn, y_ref, s_ref, st_ref, *, n_chunks):
    L = CHUNK
    last = pl.num_programs(1) - 1

    @pl.when(pl.program_id(1) == 0)
    def _():
        st_ref[...] = jnp.zeros_like(st_ref)

    tril_incl = _iota2((L, L), 1) <= _iota2((L, L), 0)
    scale = G_DK ** -0.5
    C, H = range(n_chunks), range(G_HEADS)
    rws = [slice(c * L, (c + 1) * L) for c in C]
    sks = [slice(h * G_DK, (h + 1) * G_DK) for h in H]

    qk = qk_ref[...].astype(F32)
    q, k = qk[:, 0:G_QK] * scale, qk[:, G_QK:2 * G_QK]
    v = v_ref[...].astype(F32)
    la = _gla_log_gate(sm_ref[...], ab[...], a_bias[...])
    cum = _mm_exact_lhs(_block_tril(n_chunks * L, L), la)
    q0, qe, ke, kl, e_last = [], [], [], [], []
    for c in C:
        cum_c = cum[rws[c]]
        ref_row = cum_c[L // 2:L // 2 + 1, :]
        last_row = cum_c[L - 1:L, :]
        q0.append(q[rws[c]] * jnp.exp(cum_c))
        qe.append(q[rws[c]] * jnp.exp(cum_c - ref_row))
        ke.append(k[rws[c]] * jnp.exp(ref_row - cum_c))
        kl.append(k[rws[c]] * jnp.exp(last_row - cum_c))
        e_last.append(jnp.exp(last_row))
    v_h = [[v[rws[c], h * G_DV:(h + 1) * G_DV] for h in H] for c in C]
    att = [[jnp.where(tril_incl, _mm_nt(qe[c][:, sks[h]], ke[c][:, sks[h]]), 0.0) for h in H] for c in C]
    upd = [[_mm_tn(v_h[c][h], kl[c][:, sks[h]]) for h in H] for c in C]
    av = [[_mm(att[c][h], v_h[c][h]) for h in H] for c in C]
    st = [st_ref[h] for h in H]
    o_ch = []
    for c in C:
        o_ch.append([av[c][h] + _mm_nt(q0[c][:, sks[h]], st[h]) for h in H])
        st = [st[h] * e_last[c][:, sks[h]] + upd[c][h] for h in H]
    for h in H:
        st_ref[h] = st[h]
    o_heads = [jnp.concatenate([o_ch[c][h] for c in C], axis=0) for h in H]
    y_ref[...] = _gla_out(o_heads, z_ref[...].astype(F32), gn[...])

    @pl.when(pl.program_id(1) == last)
    def _():
        for h in range(G_HEADS):
            s_ref[0, h] = st_ref[h].T


def _gla_params(ab, a_bias, gn):
    ab_p = jnp.zeros((LANES, G_QK), F32).at[SM_GA:SM_GA + G_LORA].set(ab).astype(BF16)
    return ab_p, a_bias.reshape(1, -1), gn.reshape(1, -1)


def _gla_prompt(proj, side, params, batch, seq, tb):
    nt = seq // tb
    cb = lambda c, w: pl.BlockSpec((tb, w), lambda b, t, c=c: (b * nt + t, c // w))
    full = lambda a: pl.BlockSpec(a.shape, lambda b, t: (0,) * a.ndim)
    kern = functools.partial(_gla_prompt_kernel, n_chunks=tb // CHUNK)
    return pl.pallas_call(
        kern,
        grid=(batch, nt),
        in_specs=[cb(C_GQK, 2 * G_QK), cb(C_GV, G_W), cb(C_GZ, G_W), cb(S_SMALL, LANES)]
                 + [full(a) for a in params],
        out_specs=[pl.BlockSpec((tb, G_W), lambda b, t: (b * nt + t, 0)),
                   pl.BlockSpec((1, G_HEADS, G_DK, G_DV), lambda b, t: (b, 0, 0, 0))],
        out_shape=[jax.ShapeDtypeStruct((batch * seq, G_W), F32),
                   jax.ShapeDtypeStruct((batch, G_HEADS, G_DK, G_DV), F32)],
        scratch_shapes=[pltpu.VMEM((G_HEADS, G_DV, G_DK), F32)],
        compiler_params=_cparams(("parallel", "arbitrary")),
        name="gla_prompt",
    )(proj, proj, proj, side, *params)


def _gla_step_kernel(qk_ref, v_ref, z_ref, sm_ref, s_in, acc_ref, ab, a_bias, gn, y_ref, s_out,
                     vec_ref, vrow_ref, orow_ref, *, bb):
    del acc_ref
    qk = qk_ref[...].astype(F32)
    q = qk[:, 0:G_QK] * (G_DK ** -0.5)
    k = qk[:, G_QK:2 * G_QK]
    g = _gla_log_gate(sm_ref[...], ab[...], a_bias[...])
    vec_ref[0] = q
    vec_ref[1] = k
    vec_ref[2] = jnp.exp(g)
    vrow_ref[...] = v_ref[...].astype(F32)
    eye = (_iota2((G_DK, G_DK), 0) == _iota2((G_DK, G_DK), 1)).astype(F32)
    masks = _outer_masks()

    def per_rows(j, _):
        bs = [j * GLA_STEP_ROWS + i for i in range(GLA_STEP_ROWS)]
        vecs = [[vec_ref[i, pl.ds(b, 1), :] for i in range(3)] for b in bs]
        v_b = [vrow_ref[pl.ds(b, 1), :] for b in bs]
        items = [(bi, h) for bi in range(GLA_STEP_ROWS) for h in range(G_HEADS)]
        n = range(len(items))
        hk = lambda x, h: x[:, h * G_DK:(h + 1) * G_DK]
        q_, k_, e_ = ([hk(vecs[bi][i], h) for bi, h in items] for i in range(3))
        v_h = [v_b[bi][:, h * G_DV:(h + 1) * G_DV] for bi, h in items]
        s = [s_in[bs[bi], h] for bi, h in items]
        e_col = [_lane_sum(eye * e_[i]) for i in n]
        qk_dot = [_lane_sum(q_[i] * k_[i]) for i in n]
        o_s = [_mm(_rows8(q_[i] * e_[i]), s[i])[0:1] for i in n]
        kv = [_mm_tn(*_outer_operands(masks, (k_[i], v_h[i]))) for i in n]
        o = [qk_dot[i] * v_h[i] + o_s[i] for i in n]
        for i, (bi, h) in enumerate(items):
            s_out[bs[bi], h] = s[i] * e_col[i] + kv[i]
        for bi in range(GLA_STEP_ROWS):
            orow_ref[pl.ds(bs[bi], 1), :] = jnp.concatenate(
                [o[bi * G_HEADS + h] for h in range(G_HEADS)], axis=-1)
        return 0

    lax.fori_loop(0, bb // GLA_STEP_ROWS, per_rows, 0)
    o = orow_ref[...]
    o_heads = [o[:, h * G_DV:(h + 1) * G_DV] for h in range(G_HEADS)]
    y_ref[...] = _gla_out(o_heads, z_ref[...].astype(F32), gn[...])


def _gla_step(proj, side, s_all, acc, params, layer, bb):
    nb = proj.shape[0]
    cb = lambda c, w: pl.BlockSpec((bb, w), lambda i, c=c: (i, c // w))
    full = lambda a: pl.BlockSpec(a.shape, lambda i: (0,) * a.ndim)
    st = _layer_block(s_all.shape[2:], bb, layer)
    kern = functools.partial(_gla_step_kernel, bb=bb)
    return pl.pallas_call(
        kern,
        grid=(nb // bb,),
        in_specs=[cb(C_GQK, 2 * G_QK), cb(C_GV, G_W), cb(C_GZ, G_W), cb(S_SMALL, LANES), st,
                  pl.BlockSpec(memory_space=pl.ANY)]
                 + [full(a) for a in params],
        out_specs=[pl.BlockSpec((bb, G_W), lambda i: (i, 0)), st],
        out_shape=[jax.ShapeDtypeStruct((nb, G_W), F32), jax.ShapeDtypeStruct(acc.shape, F32)],
        input_output_aliases={5: 1},
        scratch_shapes=[pltpu.VMEM((3, bb, G_QK), F32), pltpu.VMEM((bb, G_W), F32),
                        pltpu.VMEM((bb, G_W), F32)],
        compiler_params=_cparams(("parallel",)),
        name="gla_step",
    )(proj, proj, proj, side, s_all, acc, *params)


def _mlstm_out(h_heads, o_pre, z, gn):
    outs = []
    for h in range(M_HEADS):
        y = h_heads[h]
        y = y - jnp.mean(y, axis=-1, keepdims=True)
        outs.append(y * lax.rsqrt(jnp.mean(y * y, axis=-1, keepdims=True) + EPS))
    return jnp.concatenate(outs, axis=-1) * gn * _sigmoid(o_pre) * _silu(z)


def _mlstm_prompt_kernel(q_ref, k_ref, v_ref, o_ref, z_ref, sm_ref, ifb, gn,
                         y_ref, c_ref, n_ref, m_ref, *, n_chunks):
    L = CHUNK

    @pl.when(pl.program_id(1) == 0)
    def _():
        c_ref[...] = jnp.zeros_like(c_ref)
        n_ref[...] = jnp.zeros_like(n_ref)
        m_ref[...] = jnp.zeros_like(m_ref)

    tril_incl = _iota2((L, L), 1) <= _iota2((L, L), 0)
    scale = M_DK ** -0.5
    neg_inf = -jnp.inf
    C, H = range(n_chunks), range(M_HEADS)
    rws = [slice(c * L, (c + 1) * L) for c in C]
    sls = [slice(h * M_DK, (h + 1) * M_DK) for h in H]

    pre = sm_ref[...] + ifb[...]
    lf = _log_sigmoid(pre)
    b_all = _mm_exact_lhs(_block_tril(n_chunks * L, L), lf)
    lib_all = pre - pltpu.roll(b_all, LANES - (SM_MF - SM_MI), 1)
    q, k, v = q_ref[...].astype(F32) * scale, k_ref[...].astype(F32), v_ref[...].astype(F32)
    qs = [[q[rws[c], sls[h]] for h in H] for c in C]
    k_h = [[k[rws[c], sls[h]] for h in H] for c in C]
    v_h = [[v[rws[c], sls[h]] for h in H] for c in C]
    qk_raw = [[_mm_nt(qs[c][h], k_h[c][h]) for h in H] for c in C]
    d_log, d_max, b_col, b_last, lib = [], [], [], [], []
    for c in C:
        lib_t = lib_all[rws[c]].T
        b_col.append([b_all[rws[c], SM_MF + h:SM_MF + h + 1] for h in H])
        lib.append([lib_all[rws[c], SM_MI + h:SM_MI + h + 1] for h in H])
        b_last.append([b_col[c][h][L - 1:L, :] for h in H])
        d_log.append([jnp.where(tril_incl, b_col[c][h] + lib_t[SM_MI + h:SM_MI + h + 1, :], neg_inf)
                      for h in H])
        d_max.append([jnp.max(d_log[c][h], axis=-1, keepdims=True) for h in H])
    m_prev = [m_ref[0, :, h:h + 1] for h in H]
    m_t, w_inter, carry, wk, qk = [], [], [], [], []
    for c in C:
        m_inter = [b_col[c][h] + m_prev[h] for h in H]
        m_t.append([jnp.maximum(m_inter[h], d_max[c][h]) for h in H])
        w_inter.append([jnp.exp(m_inter[h] - m_t[c][h]) for h in H])
        m_new = [m_t[c][h][L - 1:L, :] for h in H]
        carry.append([jnp.exp(b_last[c][h] + m_prev[h] - m_new[h]) for h in H])
        wk.append([jnp.exp(b_last[c][h] + lib[c][h] - m_new[h]) * k_h[c][h] for h in H])
        qk.append([qk_raw[c][h] * jnp.exp(d_log[c][h] - m_t[c][h]) for h in H])
        m_prev = m_new
    for h in H:
        m_ref[0, :, h:h + 1] = m_prev[h]
    upd = [[_mm_tn(wk[c][h], v_h[c][h]) for h in H] for c in C]
    qkv = [[_mm(qk[c][h], v_h[c][h]) for h in H] for c in C]
    cs = [c_ref[0, h] for h in H]
    ns = [n_ref[0, h:h + 1, :] for h in H]
    h_ch = []
    for c in C:
        qc = [_mm(qs[c][h], cs[h]) for h in H]
        hh = []
        for h in H:
            num = qkv[c][h] + w_inter[c][h] * qc[h]
            den = _lane_sum(qk[c][h]) + w_inter[c][h] * _lane_sum(qs[c][h] * ns[h])
            hh.append(num / jnp.maximum(jnp.abs(den), jnp.exp(-m_t[c][h])))
        h_ch.append(hh)
        cs = [carry[c][h] * cs[h] + upd[c][h] for h in H]
        ns = [carry[c][h] * ns[h] + jnp.sum(wk[c][h], axis=0, keepdims=True) for h in H]
    for h in H:
        c_ref[0, h] = cs[h]
        n_ref[0, h:h + 1, :] = ns[h]
    h_heads = [jnp.concatenate([h_ch[c][h] for c in C], axis=0) for h in H]
    y_ref[...] = _mlstm_out(h_heads, o_ref[...].astype(F32), z_ref[...].astype(F32), gn[...])


def _mlstm_params(ifb, gn):
    ifb_p = jnp.zeros((1, LANES), F32).at[0, SM_MI:SM_MI + 2 * M_HEADS].set(ifb)
    return ifb_p, gn.reshape(1, -1)


def _mlstm_prompt(proj, side, params, batch, seq, tb):
    nt = seq // tb
    cb = lambda c, w: pl.BlockSpec((tb, w), lambda b, t, c=c: (b * nt + t, c // w))
    full = lambda a: pl.BlockSpec(a.shape, lambda b, t: (0,) * a.ndim)
    kern = functools.partial(_mlstm_prompt_kernel, n_chunks=tb // CHUNK)
    return pl.pallas_call(
        kern,
        grid=(batch, nt),
        in_specs=[cb(C_MQ, M_QK), cb(C_MK, M_QK), cb(C_MV, M_W), cb(C_MO, M_W), cb(C_MZ, M_W),
                  cb(S_SMALL, LANES)] + [full(a) for a in params],
        out_specs=[pl.BlockSpec((tb, M_W), lambda b, t: (b * nt + t, 0)),
                   pl.BlockSpec((1, M_HEADS, M_DK, M_DV), lambda b, t: (b, 0, 0, 0)),
                   pl.BlockSpec((1, M_HEADS, M_DK), lambda b, t: (b, 0, 0)),
                   pl.BlockSpec((1, 1, M_HEADS), lambda b, t: (b, 0, 0))],
        out_shape=[jax.ShapeDtypeStruct((batch * seq, M_W), F32),
                   jax.ShapeDtypeStruct((batch, M_HEADS, M_DK, M_DV), F32),
                   jax.ShapeDtypeStruct((batch, M_HEADS, M_DK), F32),
                   jax.ShapeDtypeStruct((batch, 1, M_HEADS), F32)],
        compiler_params=_cparams(("parallel", "arbitrary")),
        name="mlstm_prompt",
    )(proj, proj, proj, proj, proj, side, *params)


def _mlstm_step_kernel(q_ref, k_ref, v_ref, o_ref, z_ref, sm_ref, c_in, n_in, m_in, acc_ref, ifb, gn,
                       y_ref, c_out, n_out, m_out, vec_ref, sc_ref, hrow_ref, *, bb):
    del acc_ref
    pre = sm_ref[...] + ifb[...]
    li = pre[:, SM_MI:SM_MI + M_HEADS]
    lf = _log_sigmoid(pre[:, SM_MF:SM_MF + M_HEADS])
    m0 = m_in[...]
    m_inter = lf + m0
    m_t = jnp.maximum(m_inter, li)
    m_out[...] = m_t
    pad = lambda x: jnp.concatenate([x, jnp.zeros((bb, LANES - M_HEADS), F32)], axis=-1)
    sc_ref[0] = pad(jnp.exp(m_inter - m_t))
    sc_ref[1] = pad(jnp.exp(li - m_t))
    sc_ref[2] = pad(jnp.exp(-m_t))
    vec_ref[0] = q_ref[...].astype(F32) * (M_DK ** -0.5)
    vec_ref[1] = k_ref[...].astype(F32)
    vec_ref[2] = v_ref[...].astype(F32)
    masks = _outer_masks()

    def per_rows(j, _):
        bs = [j * STEP_ROWS + i for i in range(STEP_ROWS)]
        vecs = [[vec_ref[i, pl.ds(b, 1), :] for i in range(3)] for b in bs]
        scs = [[sc_ref[i, pl.ds(b, 1), :] for i in range(3)] for b in bs]
        items = [(bi, h) for bi in range(STEP_ROWS) for h in range(M_HEADS)]
        n_it = range(len(items))
        hs = lambda x, h: x[:, h * M_DK:(h + 1) * M_DK]
        q_h, k_h, v_h = ([hs(vecs[bi][i], h) for bi, h in items] for i in range(3))
        wi, ei, en = ([scs[bi][i][:, h:h + 1] for bi, h in items] for i in range(3))
        c = [c_in[bs[bi], h] for bi, h in items]
        n = [n_in[bs[bi], h:h + 1, :] for bi, h in items]
        qk = [_lane_sum(q_h[i] * k_h[i]) * ei[i] for i in n_it]
        qn = [_lane_sum(q_h[i] * n[i]) for i in n_it]
        qc = [_mm(_rows8(q_h[i]), c[i])[0:1] for i in n_it]
        kv = [_mm_tn(*_outer_operands(masks, (ei[i] * k_h[i], v_h[i]))) for i in n_it]
        hh = []
        for i, (bi, h) in enumerate(items):
            num = qk[i] * v_h[i] + wi[i] * qc[i]
            den = qk[i] + wi[i] * qn[i]
            hh.append(num / jnp.maximum(jnp.abs(den), en[i]))
            c_out[bs[bi], h] = wi[i] * c[i] + kv[i]
            n_out[bs[bi], h:h + 1, :] = wi[i] * n[i] + ei[i] * k_h[i]
        for bi in range(STEP_ROWS):
            hrow_ref[pl.ds(bs[bi], 1), :] = jnp.concatenate(
                [hh[bi * M_HEADS + h] for h in range(M_HEADS)], axis=-1)
        return 0

    lax.fori_loop(0, bb // STEP_ROWS, per_rows, 0)
    hh = hrow_ref[...]
    h_heads = [hh[:, h * M_DV:(h + 1) * M_DV] for h in range(M_HEADS)]
    y_ref[...] = _mlstm_out(h_heads, o_ref[...].astype(F32), z_ref[...].astype(F32), gn[...])


def _mlstm_step(proj, side, c_all, n_all, m_all, acc, params, layer, bb):
    nb = proj.shape[0]
    cb = lambda c, w: pl.BlockSpec((bb, w), lambda i, c=c: (i, c // w))
    full = lambda a: pl.BlockSpec(a.shape, lambda i: (0,) * a.ndim)
    cs = _layer_block(c_all.shape[2:], bb, layer)
    ns_in = _layer_block(n_all.shape[2:], bb, layer)
    ms_in = _layer_block(m_all.shape[2:], bb, layer)
    ns = pl.BlockSpec((bb, M_HEADS, M_DK), lambda i: (i, 0, 0))
    ms = pl.BlockSpec((bb, M_HEADS), lambda i: (i, 0))
    kern = functools.partial(_mlstm_step_kernel, bb=bb)
    return pl.pallas_call(
        kern,
        grid=(nb // bb,),
        in_specs=[cb(C_MQ, M_QK), cb(C_MK, M_QK), cb(C_MV, M_W), cb(C_MO, M_W), cb(C_MZ, M_W),
                  cb(S_SMALL, LANES), cs, ns_in, ms_in, pl.BlockSpec(memory_space=pl.ANY)]
                 + [full(a) for a in params],
        out_specs=[pl.BlockSpec((bb, M_W), lambda i: (i, 0)), cs, ns, ms],
        out_shape=[jax.ShapeDtypeStruct((nb, M_W), F32), jax.ShapeDtypeStruct(acc.shape, F32),
                   jax.ShapeDtypeStruct(n_all.shape[1:], F32), jax.ShapeDtypeStruct(m_all.shape[1:], F32)],
        input_output_aliases={9: 1},
        scratch_shapes=[pltpu.VMEM((3, bb, M_QK), F32), pltpu.VMEM((3, bb, LANES), F32),
                        pltpu.VMEM((bb, M_W), F32)],
        compiler_params=_cparams(("parallel",)),
        name="mlstm_step",
    )(proj, proj, proj, proj, proj, side, c_all, n_all, m_all, acc, *params)


def _pad_w_in(w_in):
    sizes = (R_SHIFT_W, R_W, G_QK, G_QK, G_W, G_LORA, G_W, M_QK, M_QK, M_W, M_HEADS, M_HEADS, M_W, M_W,
             3 * D_MODEL)
    offs = [0]
    for s in sizes:
        offs.append(offs[-1] + s)
    seg = lambda i: w_in[..., offs[i]:offs[i + 1]]
    (r_in, r_z, g_q, g_k, g_v, g_a, g_z, m_q, m_k, m_v, m_i, m_f, m_o, m_z, gate) = (seg(i) for i in range(15))
    zeros = lambda n: jnp.zeros(w_in.shape[:-1] + (n,), w_in.dtype)
    small = jnp.concatenate([g_a, m_i, m_f, zeros(LANES - G_LORA - 2 * M_HEADS)], axis=-1)
    w = jnp.concatenate([gate, r_z, g_v, g_z, m_q, m_k, m_v, m_o, m_z, g_q, g_k, r_in, small,
                         zeros(N_PAD - N_USED)], axis=-1)
    return w.astype(BF16)


def _shift_rows(proj_rows, side_rows):
    return jnp.concatenate([proj_rows[:, C_RIN:C_RIN + 3 * R_W].astype(F32),
                            side_rows[:, S_RLORA:S_RLORA + 2 * R_LORA]], axis=-1)


def _pick(n, prefs):
    for p in prefs:
        if n % p == 0:
            return p
    return n


def kernel(x_prompt, x_sample, state_rwkv_shift, state_rwkv, state_gla, state_mlstm_C, state_mlstm_n,
           state_mlstm_m, norm_pre, norm_post, w_in, r_mu_shift, r_w_lora_b, r_w0, r_a_lora_b, r_a0,
           r_k_k, r_k_a, r_r_k, r_gn, g_a_lora_b, g_a_bias, g_gn, m_if_bias, m_gn, w_br_rwkv, w_br_gla,
           w_br_mlstm, w_out):
    depth = w_in.shape[0]
    bp, tp, _ = x_prompt.shape
    bs, ts, _ = x_sample.shape
    assert ts == 1 and tp % CHUNK == 0
    w_pad = _pad_w_in(w_in)
    wr, wg, wm, wo = (w.astype(BF16) for w in (w_br_rwkv, w_br_gla, w_br_mlstm, w_out))

    mp = bp * tp
    tm_in = _pick(mp, (2048, 1024, 512, 256, 128, 64))
    tm_mg = _pick(mp, (512, 256, 128, 64))
    tb = _pick(tp, (256, 128, 64))
    bb = _pick(bs, (16, 8))

    xp = x_prompt.reshape(mp, D_MODEL)
    xs = x_sample.reshape(bs, D_MODEL)
    new_p = [[] for _ in range(6)]
    new_s = [[] for _ in range(3)]
    acc_r, acc_g, acc_c = (jnp.zeros(s.shape, F32) for s in (state_rwkv, state_gla, state_mlstm_C))
    for l in range(depth):
        g_pre = norm_pre[l].reshape(1, -1)
        g_post = norm_post[l].reshape(1, -1)
        rp = _rwkv_params(r_mu_shift[l], r_w_lora_b[l], r_w0[l], r_a_lora_b[l], r_a0[l], r_k_k[l],
                          r_k_a[l], r_r_k[l], r_gn[l])
        gp = _gla_params(g_a_lora_b[l], g_a_bias[l], g_gn[l])
        mparams = _mlstm_params(m_if_bias[l], m_gn[l])

        proj, side = _inproj(xp, g_pre, w_pad, l, tm_in)
        y_r, s_r = _rwkv_prompt(proj, side, rp, bp, tp, tb)
        y_g, s_g = _gla_prompt(proj, side, gp, bp, tp, tb)
        y_m, c, n, m = _mlstm_prompt(proj, side, mparams, bp, tp, tb)
        xp = _merge(proj, y_r, y_g, y_m, xp, wr, wg, wm, wo, g_post, l, tm_mg)
        shift = _shift_rows(proj.reshape(bp, tp, C_RLORA)[:, tp - 1], side.reshape(bp, tp, SIDE_W)[:, tp - 1])
        for lst, val in zip(new_p, (shift, s_r, s_g, c, n, m.reshape(bp, M_HEADS))):
            lst.append(val)

        proj, side = _inproj(xs, g_pre, w_pad, l, bs)
        y_r, acc_r = _rwkv_step(proj, side, state_rwkv_shift, state_rwkv, acc_r, rp, l, bb)
        y_g, acc_g = _gla_step(proj, side, state_gla, acc_g, gp, l, bb)
        y_m, acc_c, n, m = _mlstm_step(proj, side, state_mlstm_C, state_mlstm_n, state_mlstm_m, acc_c,
                                       mparams, l, bb)
        xs = _merge(proj, y_r, y_g, y_m, xs, wr, wg, wm, wo, g_post, l, bs)
        shift = _shift_rows(proj, side)
        for lst, val in zip(new_s, (shift, n, m)):
            lst.append(val)

    stk = lambda lst: jnp.stack(lst)
    return (xp.reshape(bp, tp, D_MODEL), xs.reshape(bs, ts, D_MODEL),
            *(stk(v) for v in new_p),
            stk(new_s[0]), acc_r, acc_g, acc_c, stk(new_s[1]), stk(new_s[2]))
```

```python
import functools
import math

import jax
import jax.numpy as jnp
from jax import lax
from jax.experimental import pallas as pl
from jax.experimental.pallas import tpu as pltpu

F32 = jnp.float32
BF16 = jnp.bfloat16

D_MODEL = 1024
R_HEADS, R_HD = 8, 64
R_W = R_HEADS * R_HD
R_LORA = 64
R_SHIFT_W = 3 * R_W + 2 * R_LORA
R_GN_EPS = 64e-5
G_HEADS, G_DK, G_DV = 4, 64, 128
G_QK = G_HEADS * G_DK
G_W = G_HEADS * G_DV
G_LORA = 16
G_GATE_TEMP = 16.0
M_HEADS, M_DK, M_DV = 4, 128, 128
M_QK = M_HEADS * M_DK
M_W = M_HEADS * M_DV
EPS = 1e-6

LANES = 128
VMEM_LIMIT = 48 * 1024 * 1024

C_GATE = 0
C_RZ = 3072
C_GV = 3584
C_GZ = 4096
C_MQ = 4608
C_MK = 5120
C_MV = 5632
C_MO = 6144
C_MZ = 6656
C_GQK = 7168
C_RIN = 7680
C_RLORA = C_RIN + 3 * R_W
C_SMALL = C_RLORA + LANES
SM_GA, SM_MI, SM_MF = 0, 16, 20
N_USED = C_SMALL + LANES
PROJ_TN = 512
N_PAD = -(-N_USED // PROJ_TN) * PROJ_TN
N_MAIN_BLOCKS = C_RLORA // PROJ_TN
SIDE_W = 2 * LANES
S_RLORA, S_SMALL = 0, LANES

CHUNK = 64


def _cparams(sem):
    return pltpu.CompilerParams(dimension_semantics=sem, vmem_limit_bytes=VMEM_LIMIT)


def _mm(a, b):
    return jnp.dot(a.astype(BF16), b.astype(BF16), preferred_element_type=F32)


def _mm_nt(a, b):
    return lax.dot_general(a.astype(BF16), b.astype(BF16), (((1,), (1,)), ((), ())),
                           preferred_element_type=F32)


def _mm_tn(a, b):
    return lax.dot_general(a.astype(BF16), b.astype(BF16), (((0,), (0,)), ((), ())),
                           preferred_element_type=F32)


def _mm_exact_lhs(m01, x):
    hi = x.astype(BF16)
    r1 = x - hi.astype(F32)
    mid = r1.astype(BF16)
    lo = (r1 - mid.astype(F32)).astype(BF16)
    m = m01.astype(BF16)
    d = lambda p: jnp.dot(m, p, preferred_element_type=F32)
    return d(hi) + d(mid) + d(lo)


def _mm_exact_rhs(x, m01):
    hi = x.astype(BF16)
    r1 = x - hi.astype(F32)
    mid = r1.astype(BF16)
    lo = (r1 - mid.astype(F32)).astype(BF16)
    m = m01.astype(BF16)
    d = lambda p: jnp.dot(p, m, preferred_element_type=F32)
    return d(hi) + d(mid) + d(lo)


def _sigmoid(x):
    return 1.0 / (1.0 + jnp.exp(-x))


def _silu(x):
    return x * _sigmoid(x)


def _softplus(x):
    return jnp.maximum(x, 0.0) + jnp.log(1.0 + jnp.exp(-jnp.abs(x)))


def _log_sigmoid(x):
    return -_softplus(-x)


def _iota2(shape, dim):
    return lax.broadcasted_iota(jnp.int32, shape, dim)


def _lane_sum(x):
    return jnp.sum(x, axis=-1, keepdims=True)


def _rows8(x):
    return jnp.broadcast_to(x, (8, x.shape[1]))


def _outer_masks():
    rid = _iota2((8, 1), 0)
    on = lambda *rows: sum((rid == r).astype(F32) for r in rows)
    return (on(0, 2), on(1), on(0, 1), on(2)), (on(3, 5), on(4), on(3, 4), on(5))


def _outer_operands(masks, *pairs):
    a_op, b_op = 0.0, 0.0
    for (a, b), (m_ahi, m_alo, m_bhi, m_blo) in zip(pairs, masks):
        a_hi = a.astype(BF16).astype(F32)
        b_hi = b.astype(BF16).astype(F32)
        a_op = a_op + m_ahi * a_hi + m_alo * (a - a_hi)
        b_op = b_op + m_bhi * b_hi + m_blo * (b - b_hi)
    return a_op, b_op


def _block_tril(n, blk):
    r, c = _iota2((n, n), 0), _iota2((n, n), 1)
    return ((r - c).astype(jnp.uint32) <= (r % blk).astype(jnp.uint32)).astype(F32)


def _inproj_kernel(x_ref, g_ref, w_ref, o_ref, side_ref, h_ref):
    j = pl.program_id(1)

    @pl.when(j == 0)
    def _():
        x = x_ref[...]
        y = x * lax.rsqrt(jnp.mean(x * x, axis=-1, keepdims=True) + EPS)
        h_ref[...] = (y * g_ref[...]).astype(BF16)

    acc = jnp.dot(h_ref[...], w_ref[...], preferred_element_type=F32)

    @pl.when(j < N_MAIN_BLOCKS)
    def _():
        o_ref[...] = acc.astype(BF16)

    @pl.when(j == N_MAIN_BLOCKS)
    def _():
        side_ref[...] = acc[:, 0:SIDE_W]


def _inproj(x2d, g_pre, w_pad, layer, tm):
    m = x2d.shape[0]
    return pl.pallas_call(
        _inproj_kernel,
        grid=(m // tm, N_MAIN_BLOCKS + 1),
        in_specs=[pl.BlockSpec((tm, D_MODEL), lambda i, j: (i, 0)),
                  pl.BlockSpec((1, D_MODEL), lambda i, j: (0, 0)),
                  pl.BlockSpec((None, D_MODEL, PROJ_TN), lambda i, j: (layer, 0, j))],
        out_specs=[pl.BlockSpec((tm, PROJ_TN), lambda i, j: (i, jnp.minimum(j, N_MAIN_BLOCKS - 1))),
                   pl.BlockSpec((tm, SIDE_W), lambda i, j: (i, 0))],
        out_shape=[jax.ShapeDtypeStruct((m, C_RLORA), BF16),
                   jax.ShapeDtypeStruct((m, SIDE_W), F32)],
        scratch_shapes=[pltpu.VMEM((tm, D_MODEL), BF16)],
        compiler_params=_cparams(("parallel", "arbitrary")),
        name="inproj",
    )(x2d, g_pre, w_pad)


def _merge_kernel(gate_ref, yr_ref, yg_ref, ym_ref, x_ref, wr_ref, wg_ref, wm_ref, wo_ref, gp_ref, o_ref):
    d = D_MODEL
    gate = lambda b: _sigmoid(gate_ref[:, b * d:(b + 1) * d].astype(F32))
    merged = (gate(0) * _mm(yr_ref[...], wr_ref[...]) + gate(1) * _mm(yg_ref[...], wg_ref[...])
              + gate(2) * _mm(ym_ref[...], wm_ref[...]))
    o = _mm(merged, wo_ref[...])
    y = o * lax.rsqrt(jnp.mean(o * o, axis=-1, keepdims=True) + EPS)
    o_ref[...] = x_ref[...] + y * gp_ref[...]


def _merge(proj, y_r, y_g, y_m, x2d, w_r, w_g, w_m, w_o, g_post, layer, tm):
    m = x2d.shape[0]
    row = lambda w: pl.BlockSpec((tm, w), lambda i: (i, 0))
    full = lambda a: pl.BlockSpec(a.shape, lambda i: (0, 0))
    wl = lambda a: pl.BlockSpec((None,) + a.shape[1:], lambda i: (layer, 0, 0))
    return pl.pallas_call(
        _merge_kernel,
        grid=(m // tm,),
        in_specs=[row(3 * D_MODEL), row(R_W), row(G_W), row(M_W), row(D_MODEL),
                  wl(w_r), wl(w_g), wl(w_m), wl(w_o), full(g_post)],
        out_specs=row(D_MODEL),
        out_shape=jax.ShapeDtypeStruct((m, D_MODEL), F32),
        compiler_params=_cparams(("parallel",)),
        name="merge",
    )(proj, y_r, y_g, y_m, x2d, w_r, w_g, w_m, w_o, g_post)


def _rwkv_front(r_in, k_in, v_in, l_in, prev_r, prev_k, prev_v, prev_l, p):
    (mu_r, mu_k, mu_v, mu_l, wb, w0, ab, a0, k_k, k_a) = p
    r = r_in + mu_r * (prev_r - r_in)
    k = k_in + mu_k * (prev_k - k_in)
    v = v_in + mu_v * (prev_v - v_in)
    lo = l_in + mu_l * (prev_l - l_in)
    log_w = -_softplus(-(w0 + _mm(jnp.tanh(lo), wb))) - 0.5
    lw = -jnp.exp(log_w)
    a = _sigmoid(a0 + _mm(lo, ab))
    kk = k * k_k
    k2 = k * (1.0 + (a - 1.0) * k_a)
    return r, k2, v, lw, a, kk


def _rwkv_out(y_heads, r, k2, v, z, r_k, gn):
    outs = []
    for h in range(R_HEADS):
        sl = slice(h * R_HD, (h + 1) * R_HD)
        y = y_heads[h]
        y = y - jnp.mean(y, axis=-1, keepdims=True)
        y = y * lax.rsqrt(jnp.mean(y * y, axis=-1, keepdims=True) + R_GN_EPS)
        bonus = _lane_sum(r[:, sl] * k2[:, sl] * r_k[:, sl]) * v[:, sl]
        outs.append(y * gn[:, sl] + bonus)
    return jnp.concatenate(outs, axis=-1) * _silu(z)


def _rwkv_prompt_kernel(z_ref, r_ref, k_ref, v_ref, l_ref,
                        mu_r, mu_k, mu_v, mu_l, wb, w0, ab, a0, k_k, k_a, r_k, gn,
                        y_ref, s_ref, carry_ref, *, n_chunks):
    L = CHUNK

    @pl.when(pl.program_id(1) == 0)
    def _():
        s_ref[...] = jnp.zeros_like(s_ref)
        carry_ref[...] = jnp.zeros_like(carry_ref)

    tb = n_chunks * L
    params = tuple(x[...] for x in (mu_r, mu_k, mu_v, mu_l, wb, w0, ab, a0, k_k, k_a))
    row = _iota2((L, L), 0)
    col = _iota2((L, L), 1)
    eye = (col == row).astype(F32)
    blk_mask = (row // 16) == (col // 16)
    r2 = _iota2((2 * L, 2 * L), 0)
    c2 = _iota2((2 * L, 2 * L), 1)
    tr, tc = r2 % L, c2 % L
    g_mask = tc < tr + (r2 >= L).astype(jnp.int32)
    first_row = _iota2((tb, 1), 0) == 0

    def shifted(x, carry):
        return jnp.where(first_row, carry, pltpu.roll(x, 1, 0))

    r_in, k_in, v_in = (x[...].astype(F32) for x in (r_ref, k_ref, v_ref))
    l_in = l_ref[...]
    prev_r = shifted(r_in, carry_ref[0:1, 0:R_W])
    prev_k = shifted(k_in, carry_ref[0:1, R_W:2 * R_W])
    prev_v = shifted(v_in, carry_ref[0:1, 2 * R_W:3 * R_W])
    prev_l = shifted(l_in, carry_ref[0:1, 3 * R_W:R_SHIFT_W])
    carry_ref[0:1, 0:R_W] = r_in[tb - 1:tb, :]
    carry_ref[0:1, R_W:2 * R_W] = k_in[tb - 1:tb, :]
    carry_ref[0:1, 2 * R_W:3 * R_W] = v_in[tb - 1:tb, :]
    carry_ref[0:1, 3 * R_W:R_SHIFT_W] = l_in[tb - 1:tb, :]
    r, k2, v, lw, a, kk = _rwkv_front(r_in, k_in, v_in, l_in, prev_r, prev_k, prev_v, prev_l, params)

    cum = _mm_exact_lhs(_block_tril(tb, L), lw)
    p_inc = jnp.exp(cum)
    p_exc = jnp.exp(cum - lw)
    p_inv = jnp.exp(-cum)
    r_t = r * p_inc
    k_h = k2 * p_inv

    CH = [(c, h) for c in range(n_chunks) for h in range(R_HEADS)]
    N = range(len(CH))
    lhs2, rhs2, rhs2_l, v_h, p_last = [], [], [], [], []
    for h in range(R_HEADS):
        sl = slice(h * R_HD, (h + 1) * R_HD)
        kk_h = kk[:, sl]
        kap = kk_h * (1.0 / jnp.maximum(jnp.sqrt(_lane_sum(kk_h * kk_h)), 1e-12))
        beta_h = kap * a[:, sl] * p_inv[:, sl]
        kap_t = kap * p_exc[:, sl]
        lhs2.append([jnp.concatenate([kap_t[c * L:(c + 1) * L], r_t[c * L:(c + 1) * L, sl]], axis=0)
                     for c in range(n_chunks)])
        rhs2.append([jnp.concatenate([beta_h[c * L:(c + 1) * L], k_h[c * L:(c + 1) * L, sl]], axis=0)
                     for c in range(n_chunks)])
        p_last.append([p_inc[(c + 1) * L - 1:(c + 1) * L, sl] for c in range(n_chunks)])
        v_h.append([v[c * L:(c + 1) * L, sl] for c in range(n_chunks)])
    lhs2 = [lhs2[h][c] for c, h in CH]
    rhs2 = [rhs2[h][c] for c, h in CH]
    p_last = [p_last[h][c] for c, h in CH]
    v_h = [v_h[h][c] for c, h in CH]
    rhs2_l = [rhs2[i] * p_last[i] for i in N]
    g = [jnp.where(g_mask, _mm_nt(lhs2[i], rhs2[i]), 0.0) for i in N]
    a_m = [g[i][0:L, 0:L] for i in N]
    d = [jnp.where(blk_mask, a_m[i], 0.0) for i in N]
    akv = [_mm(g[i][0:L, L:2 * L], v_h[i]) for i in N]
    d2 = [_mm(d[i], d[i]) for i in N]
    d4 = [_mm(d2[i], d2[i]) for i in N]
    dd2 = [_mm(d[i], d2[i]) for i in N]
    d8 = [_mm(d4[i], d4[i]) for i in N]
    x = [eye - d[i] + d2[i] - dd2[i] for i in N]
    x = [x[i] + _mm(x[i], d4[i]) for i in N]
    td = [x[i] + _mm(x[i], d8[i]) for i in N]
    m = [_mm(td[i], a_m[i] - d[i]) for i in N]
    m2 = [_mm(m[i], m[i]) for i in N]
    q = [eye - m[i] + m2[i] - _mm(m[i], m2[i]) for i in N]
    t_inv = [_mm(q[i], td[i]) for i in N]
    kbar = [_mm(t_inv[i], lhs2[i][0:L]) for i in N]
    u0 = [-_mm(t_inv[i], akv[i]) for i in N]
    uv0 = [jnp.concatenate([u0[i], v_h[i]], axis=0) for i in N]
    n_c = [_mm_tn(kbar[i], rhs2_l[i][0:L]) for i in N]
    c_c = [_mm_tn(uv0[i], rhs2_l[i]) for i in N]
    qt = [lhs2[i][L:2 * L] - _mm(g[i][L:2 * L, 0:L], kbar[i]) for i in N]
    y0 = [_mm(g[i][L:2 * L, :], uv0[i]) for i in N]
    s = [s_ref[0, h] for h in range(R_HEADS)]
    y_ch = []
    for c in range(n_chunks):
        ids = [c * R_HEADS + h for h in range(R_HEADS)]
        y_ch.append([_mm_nt(qt[i], s[h]) + y0[i] for h, i in enumerate(ids)])
        s = [s[h] * p_last[i] - _mm(s[h], n_c[i]) + c_c[i] for h, i in enumerate(ids)]
    for h in range(R_HEADS):
        s_ref[0, h] = s[h]
    y_heads = [jnp.concatenate([y_ch[c][h] for c in range(n_chunks)], axis=0) for h in range(R_HEADS)]
    y_ref[...] = _rwkv_out(y_heads, r, k2, v, z_ref[...].astype(F32), r_k[...], gn[...]).astype(y_ref.dtype)


def _rwkv_params(mu, wb, w0, ab, a0, k_k, k_a, r_k, gn):
    row = lambda x: x.reshape(1, -1)
    zeros = jnp.zeros((R_LORA, R_W), F32)
    wb_p = jnp.concatenate([wb, zeros], axis=0).astype(BF16)
    ab_p = jnp.concatenate([zeros, ab], axis=0).astype(BF16)
    return (row(mu[0:R_W]), row(mu[R_W:2 * R_W]), row(mu[2 * R_W:3 * R_W]), row(mu[3 * R_W:]),
            wb_p, row(w0), ab_p, row(a0), row(k_k), row(k_a), row(r_k), row(gn))


def _rwkv_prompt(proj, side, params, batch, seq, tb):
    nt = seq // tb
    cb = lambda c, w: pl.BlockSpec((tb, w), lambda b, t, c=c: (b * nt + t, c // w))
    full = lambda a: pl.BlockSpec(a.shape, lambda b, t: (0,) * a.ndim)
    kern = functools.partial(_rwkv_prompt_kernel, n_chunks=tb // CHUNK)
    return pl.pallas_call(
        kern,
        grid=(batch, nt),
        in_specs=[cb(C_RZ, R_W), cb(C_RIN, R_W), cb(C_RIN + R_W, R_W), cb(C_RIN + 2 * R_W, R_W),
                  cb(S_RLORA, LANES)] + [full(a) for a in params],
        out_specs=[pl.BlockSpec((tb, R_W), lambda b, t: (b * nt + t, 0)),
                   pl.BlockSpec((1, R_HEADS, R_HD, R_HD), lambda b, t: (b, 0, 0, 0))],
        out_shape=[jax.ShapeDtypeStruct((batch * seq, R_W), BF16),
                   jax.ShapeDtypeStruct((batch, R_HEADS, R_HD, R_HD), F32)],
        scratch_shapes=[pltpu.VMEM((8, R_SHIFT_W), F32)],
        compiler_params=_cparams(("parallel", "arbitrary")),
        name="rwkv_prompt",
    )(proj, proj, proj, proj, side, *params)


STEP_ROWS = 4
GLA_STEP_ROWS = 2


def _rwkv_step_kernel(z_ref, r_ref, k_ref, v_ref, l_ref, pr_ref, pk_ref, pv_ref, pl_ref, s_in, acc_ref,
                      mu_r, mu_k, mu_v, mu_l, wb, w0, ab, a0, k_k, k_a, r_k, gn,
                      y_ref, s_out, vec_ref, yrow_ref, *, bb):
    del acc_ref
    params = tuple(x[...] for x in (mu_r, mu_k, mu_v, mu_l, wb, w0, ab, a0, k_k, k_a))
    r, k2, v, lw, a, kk = _rwkv_front(r_ref[...].astype(F32), k_ref[...].astype(F32), v_ref[...].astype(F32),
                                      l_ref[...], pr_ref[...], pk_ref[...], pv_ref[...], pl_ref[...], params)
    w = jnp.exp(lw)
    kaps, betas = [], []
    for h in range(R_HEADS):
        sl = slice(h * R_HD, (h + 1) * R_HD)
        kk_h = kk[:, sl]
        kap = kk_h / jnp.maximum(jnp.sqrt(_lane_sum(kk_h * kk_h)), 1e-12)
        kaps.append(kap)
        betas.append(kap * a[:, sl])
    vec_ref[0] = jnp.concatenate(kaps, axis=-1)
    vec_ref[1] = jnp.concatenate(betas, axis=-1)
    vec_ref[2] = w
    vec_ref[3] = k2
    vec_ref[4] = v
    vec_ref[5] = r
    masks = _outer_masks()

    def per_rows(j, _):
        bs = [j * STEP_ROWS + i for i in range(STEP_ROWS)]
        vecs = [[vec_ref[i, pl.ds(b, 1), :] for i in range(6)] for b in bs]
        items = [(bi, h) for bi in range(STEP_ROWS) for h in range(R_HEADS)]
        hs = lambda x, h: x[:, h * R_HD:(h + 1) * R_HD]
        kap, beta, w_, k_, v_, r_ = ([hs(vecs[bi][i], h) for bi, h in items] for i in range(6))
        n = range(len(items))
        s = [s_in[bs[bi], h] for bi, h in items]
        sa = [-_mm_nt(_rows8(kap[i]), s[i])[0:1] for i in n]
        ops = [_outer_operands(masks, (sa[i], beta[i]), (v_[i], k_[i])) for i in n]
        s_new = [s[i] * w_[i] + _mm_tn(*ops[i]) for i in n]
        for i, (bi, h) in enumerate(items):
            s_out[bs[bi], h] = s_new[i]
        y_row = [_mm_nt(_rows8(r_[i]), s_new[i])[0:1] for i in n]
        for bi in range(STEP_ROWS):
            yrow_ref[pl.ds(bs[bi], 1), :] = jnp.concatenate(
                [y_row[bi * R_HEADS + h] for h in range(R_HEADS)], axis=-1)
        return 0

    lax.fori_loop(0, bb // STEP_ROWS, per_rows, 0)
    y = yrow_ref[...]
    y_heads = [y[:, h * R_HD:(h + 1) * R_HD] for h in range(R_HEADS)]
    y_ref[...] = _rwkv_out(y_heads, r, k2, v, z_ref[...].astype(F32), r_k[...], gn[...]).astype(y_ref.dtype)


def _layer_block(shape_tail, bb, layer):
    zeros = (0,) * len(shape_tail)
    return pl.BlockSpec((None, bb) + tuple(shape_tail), lambda i: (layer, i) + zeros)


def _rwkv_step(proj, side, shift_all, s_all, acc, params, layer, bb):
    nb = proj.shape[0]
    cb = lambda c, w: pl.BlockSpec((bb, w), lambda i, c=c: (i, c // w))
    sh = lambda c, w: pl.BlockSpec((None, bb, w), lambda i, c=c: (layer, i, c // w))
    full = lambda a: pl.BlockSpec(a.shape, lambda i: (0,) * a.ndim)
    st = _layer_block(s_all.shape[2:], bb, layer)
    kern = functools.partial(_rwkv_step_kernel, bb=bb)
    return pl.pallas_call(
        kern,
        grid=(nb // bb,),
        in_specs=[cb(C_RZ, R_W), cb(C_RIN, R_W), cb(C_RIN + R_W, R_W), cb(C_RIN + 2 * R_W, R_W),
                  cb(S_RLORA, LANES),
                  sh(0, R_W), sh(R_W, R_W), sh(2 * R_W, R_W), sh(3 * R_W, LANES), st,
                  pl.BlockSpec(memory_space=pl.ANY)]
                 + [full(a) for a in params],
        out_specs=[pl.BlockSpec((bb, R_W), lambda i: (i, 0)), st],
        out_shape=[jax.ShapeDtypeStruct((nb, R_W), BF16),
                   jax.ShapeDtypeStruct(acc.shape, F32)],
        input_output_aliases={10: 1},
        scratch_shapes=[pltpu.VMEM((6, bb, R_W), F32), pltpu.VMEM((bb, R_W), F32)],
        compiler_params=_cparams(("parallel",)),
        name="rwkv_step",
    )(proj, proj, proj, proj, side, shift_all, shift_all, shift_all, shift_all, s_all, acc, *params)


def _gla_log_gate(small, ab_p, a_bias):
    return _log_sigmoid(_mm(small, ab_p) + a_bias) / G_GATE_TEMP


def _gla_out(o_heads, z, gn):
    outs = []
    for h in range(G_HEADS):
        o = o_heads[h]
        outs.append(o * lax.rsqrt(jnp.mean(o * o, axis=-1, keepdims=True) + EPS))
    return jnp.concatenate(outs, axis=-1) * gn * _silu(z)


def _gla_prompt_kernel(qk_ref, v_ref, z_ref, sm_ref, ab, a_bias, gn, y_ref, s_ref, st_ref, *, n_chunks):
    L = CHUNK
    last = pl.num_programs(1) - 1

    @pl.when(pl.program_id(1) == 0)
    def _():
        st_ref[...] = jnp.zeros_like(st_ref)

    tril_incl = _iota2((L, L), 1) <= _iota2((L, L), 0)
    scale = G_DK ** -0.5
    C, H = range(n_chunks), range(G_HEADS)
    rws = [slice(c * L, (c + 1) * L) for c in C]
    sks = [slice(h * G_DK, (h + 1) * G_DK) for h in H]

    qk = qk_ref[...].astype(F32)
    q, k = qk[:, 0:G_QK] * scale, qk[:, G_QK:2 * G_QK]
    v = v_ref[...].astype(F32)
    la = _gla_log_gate(sm_ref[...], ab[...], a_bias[...])
    cum = _mm_exact_lhs(_block_tril(n_chunks * L, L), la)
    q0, qe, ke, kl, e_last = [], [], [], [], []
    for c in C:
        cum_c = cum[rws[c]]
        ref_row = cum_c[L // 2:L // 2 + 1, :]
        last_row = cum_c[L - 1:L, :]
        q0.append(q[rws[c]] * jnp.exp(cum_c))
        qe.append(q[rws[c]] * jnp.exp(cum_c - ref_row))
        ke.append(k[rws[c]] * jnp.exp(ref_row - cum_c))
        kl.append(k[rws[c]] * jnp.exp(last_row - cum_c))
        e_last.append(jnp.exp(last_row))
    v_h = [[v[rws[c], h * G_DV:(h + 1) * G_DV] for h in H] for c in C]
    att = [[jnp.where(tril_incl, _mm_nt(qe[c][:, sks[h]], ke[c][:, sks[h]]), 0.0) for h in H] for c in C]
    upd = [[_mm_tn(v_h[c][h], kl[c][:, sks[h]]) for h in H] for c in C]
    av = [[_mm(att[c][h], v_h[c][h]) for h in H] for c in C]
    st = [st_ref[h] for h in H]
    o_ch = []
    for c in C:
        o_ch.append([av[c][h] + _mm_nt(q0[c][:, sks[h]], st[h]) for h in H])
        st = [st[h] * e_last[c][:, sks[h]] + upd[c][h] for h in H]
    for h in H:
        st_ref[h] = st[h]
    o_heads = [jnp.concatenate([o_ch[c][h] for c in C], axis=0) for h in H]
    y_ref[...] = _gla_out(o_heads, z_ref[...].astype(F32), gn[...]).astype(y_ref.dtype)

    @pl.when(pl.program_id(1) == last)
    def _():
        for h in range(G_HEADS):
            s_ref[0, h] = st_ref[h].T


def _gla_params(ab, a_bias, gn):
    ab_p = jnp.zeros((LANES, G_QK), F32).at[SM_GA:SM_GA + G_LORA].set(ab).astype(BF16)
    return ab_p, a_bias.reshape(1, -1), gn.reshape(1, -1)


def _gla_prompt(proj, side, params, batch, seq, tb):
    nt = seq // tb
    cb = lambda c, w: pl.BlockSpec((tb, w), lambda b, t, c=c: (b * nt + t, c // w))
    full = lambda a: pl.BlockSpec(a.shape, lambda b, t: (0,) * a.ndim)
    kern = functools.partial(_gla_prompt_kernel, n_chunks=tb // CHUNK)
    return pl.pallas_call(
        kern,
        grid=(batch, nt),
        in_specs=[cb(C_GQK, 2 * G_QK), cb(C_GV, G_W), cb(C_GZ, G_W), cb(S_SMALL, LANES)]
                 + [full(a) for a in params],
        out_specs=[pl.BlockSpec((tb, G_W), lambda b, t: (b * nt + t, 0)),
                   pl.BlockSpec((1, G_HEADS, G_DK, G_DV), lambda b, t: (b, 0, 0, 0))],
        out_shape=[jax.ShapeDtypeStruct((batch * seq, G_W), BF16),
                   jax.ShapeDtypeStruct((batch, G_HEADS, G_DK, G_DV), F32)],
        scratch_shapes=[pltpu.VMEM((G_HEADS, G_DV, G_DK), F32)],
        compiler_params=_cparams(("parallel", "arbitrary")),
        name="gla_prompt",
    )(proj, proj, proj, side, *params)


def _gla_step_kernel(qk_ref, v_ref, z_ref, sm_ref, s_in, acc_ref, ab, a_bias, gn, y_ref, s_out,
                     vec_ref, vrow_ref, orow_ref, *, bb):
    del acc_ref
    qk = qk_ref[...].astype(F32)
    q = qk[:, 0:G_QK] * (G_DK ** -0.5)
    k = qk[:, G_QK:2 * G_QK]
    g = _gla_log_gate(sm_ref[...], ab[...], a_bias[...])
    vec_ref[0] = q
    vec_ref[1] = k
    vec_ref[2] = jnp.exp(g)
    vrow_ref[...] = v_ref[...].astype(F32)
    eye = (_iota2((G_DK, G_DK), 0) == _iota2((G_DK, G_DK), 1)).astype(F32)
    masks = _outer_masks()

    def per_rows(j, _):
        bs = [j * GLA_STEP_ROWS + i for i in range(GLA_STEP_ROWS)]
        vecs = [[vec_ref[i, pl.ds(b, 1), :] for i in range(3)] for b in bs]
        v_b = [vrow_ref[pl.ds(b, 1), :] for b in bs]
        items = [(bi, h) for bi in range(GLA_STEP_ROWS) for h in range(G_HEADS)]
        n = range(len(items))
        hk = lambda x, h: x[:, h * G_DK:(h + 1) * G_DK]
        q_, k_, e_ = ([hk(vecs[bi][i], h) for bi, h in items] for i in range(3))
        v_h = [v_b[bi][:, h * G_DV:(h + 1) * G_DV] for bi, h in items]
        s = [s_in[bs[bi], h] for bi, h in items]
        e_col = [_lane_sum(eye * e_[i]) for i in n]
        qk_dot = [_lane_sum(q_[i] * k_[i]) for i in n]
        o_s = [_mm(_rows8(q_[i] * e_[i]), s[i])[0:1] for i in n]
        kv = [_mm_tn(*_outer_operands(masks, (k_[i], v_h[i]))) for i in n]
        o = [qk_dot[i] * v_h[i] + o_s[i] for i in n]
        for i, (bi, h) in enumerate(items):
            s_out[bs[bi], h] = s[i] * e_col[i] + kv[i]
        for bi in range(GLA_STEP_ROWS):
            orow_ref[pl.ds(bs[bi], 1), :] = jnp.concatenate(
                [o[bi * G_HEADS + h] for h in range(G_HEADS)], axis=-1)
        return 0

    lax.fori_loop(0, bb // GLA_STEP_ROWS, per_rows, 0)
    o = orow_ref[...]
    o_heads = [o[:, h * G_DV:(h + 1) * G_DV] for h in range(G_HEADS)]
    y_ref[...] = _gla_out(o_heads, z_ref[...].astype(F32), gn[...]).astype(y_ref.dtype)


def _gla_step(proj, side, s_all, acc, params, layer, bb):
    nb = proj.shape[0]
    cb = lambda c, w: pl.BlockSpec((bb, w), lambda i, c=c: (i, c // w))
    full = lambda a: pl.BlockSpec(a.shape, lambda i: (0,) * a.ndim)
    st = _layer_block(s_all.shape[2:], bb, layer)
    kern = functools.partial(_gla_step_kernel, bb=bb)
    return pl.pallas_call(
        kern,
        grid=(nb // bb,),
        in_specs=[cb(C_GQK, 2 * G_QK), cb(C_GV, G_W), cb(C_GZ, G_W), cb(S_SMALL, LANES), st,
                  pl.BlockSpec(memory_space=pl.ANY)]
                 + [full(a) for a in params],
        out_specs=[pl.BlockSpec((bb, G_W), lambda i: (i, 0)), st],
        out_shape=[jax.ShapeDtypeStruct((nb, G_W), BF16), jax.ShapeDtypeStruct(acc.shape, F32)],
        input_output_aliases={5: 1},
        scratch_shapes=[pltpu.VMEM((3, bb, G_QK), F32), pltpu.VMEM((bb, G_W), F32),
                        pltpu.VMEM((bb, G_W), F32)],
        compiler_params=_cparams(("parallel",)),
        name="gla_step",
    )(proj, proj, proj, side, s_all, acc, *params)


def _mlstm_out(h_heads, o_pre, z, gn):
    outs = []
    for h in range(M_HEADS):
        y = h_heads[h]
        y = y - jnp.mean(y, axis=-1, keepdims=True)
        outs.append(y * lax.rsqrt(jnp.mean(y * y, axis=-1, keepdims=True) + EPS))
    return jnp.concatenate(outs, axis=-1) * gn * _sigmoid(o_pre) * _silu(z)


def _mlstm_prompt_kernel(q_ref, k_ref, v_ref, o_ref, z_ref, sm_ref, ifb, gn,
                         y_ref, c_ref, n_ref, m_ref, ct_ref, *, n_chunks):
    L = CHUNK

    @pl.when(pl.program_id(1) == 0)
    def _():
        ct_ref[...] = jnp.zeros_like(ct_ref)
        n_ref[...] = jnp.zeros_like(n_ref)
        m_ref[...] = jnp.zeros_like(m_ref)

    causal = _iota2((L, L), 0) <= _iota2((L, L), 1)
    eye = (_iota2((L, L), 0) == _iota2((L, L), 1)).astype(F32)
    scale = M_DK ** -0.5
    neg_inf = -jnp.inf
    C, H = range(n_chunks), range(M_HEADS)
    rws = [slice(c * L, (c + 1) * L) for c in C]
    sls = [slice(h * M_DK, (h + 1) * M_DK) for h in H]

    pre = sm_ref[...] + ifb[...]
    lf = _log_sigmoid(pre)
    b_all = _mm_exact_lhs(_block_tril(n_chunks * L, L), lf)
    lib_all = pre - pltpu.roll(b_all, LANES - (SM_MF - SM_MI), 1)
    sel = (_iota2((LANES, M_HEADS * LANES), 0)
           == SM_MI + _iota2((LANES, M_HEADS * LANES), 1) // LANES).astype(F32)
    lib_bc = _mm_exact_rhs(lib_all, sel)
    comb = jnp.where(_iota2((1, LANES), 1) < SM_MF, lib_all, b_all)
    q, k, v = q_ref[...].astype(F32) * scale, k_ref[...].astype(F32), v_ref[...].astype(F32)
    qs = [[q[rws[c], sls[h]] for h in H] for c in C]
    k_h = [[k[rws[c], sls[h]] for h in H] for c in C]
    v_h = [[v[rws[c], sls[h]] for h in H] for c in C]
    qk_raw = [[_mm_nt(k_h[c][h], qs[c][h]) for h in H] for c in C]
    d_log, d_max, b_row, lib_row = [], [], [], []
    for c in C:
        comb_t = comb[rws[c]].T
        b_row.append([comb_t[SM_MF + h:SM_MF + h + 1, :] for h in H])
        lib_row.append([comb_t[SM_MI + h:SM_MI + h + 1, :] for h in H])
        d_log.append([jnp.where(causal, lib_bc[rws[c], h * LANES:h * LANES + L] + b_row[c][h], neg_inf)
                      for h in H])
        d_max.append([jnp.max(d_log[c][h], axis=0, keepdims=True) for h in H])
    m_prev = [m_ref[0, :, h:h + 1] for h in H]
    m_t, w_inter, carry, w_state, qk = [], [], [], [], []
    for c in C:
        m_inter = [b_row[c][h] + m_prev[h] for h in H]
        m_t.append([jnp.maximum(m_inter[h], d_max[c][h]) for h in H])
        w_inter.append([jnp.exp(m_inter[h] - m_t[c][h]) for h in H])
        m_new = [m_t[c][h][:, L - 1:L] for h in H]
        b_last = [b_row[c][h][:, L - 1:L] for h in H]
        carry.append([jnp.exp(b_last[h] + m_prev[h] - m_new[h]) for h in H])
        w_state.append([jnp.exp(b_last[h] + lib_row[c][h] - m_new[h]) for h in H])
        qk.append([qk_raw[c][h] * jnp.exp(d_log[c][h] - m_t[c][h]) for h in H])
        m_prev = m_new
    for h in H:
        m_ref[0, :, h:h + 1] = m_prev[h]
    wk = [[_mm(eye * w_state[c][h], k_h[c][h]) for h in H] for c in C]
    n_upd = [[_mm(_rows8(w_state[c][h]), k_h[c][h])[0:1] for h in H] for c in C]
    upd = [[_mm_tn(v_h[c][h], wk[c][h]) for h in H] for c in C]
    qkv = [[_mm_tn(v_h[c][h], qk[c][h]) for h in H] for c in C]
    ct = [ct_ref[h] for h in H]
    ns = [n_ref[0, h:h + 1, :] for h in H]
    y_ch = []
    for c in C:
        qc = [_mm_nt(ct[h], qs[c][h]) for h in H]
        qn = [_mm_nt(_rows8(ns[h]), qs[c][h])[0:1] for h in H]
        yy = []
        for h in H:
            num = qkv[c][h] + w_inter[c][h] * qc[h]
            den = jnp.sum(qk[c][h], axis=0, keepdims=True) + w_inter[c][h] * qn[h]
            ht = num / jnp.maximum(jnp.abs(den), jnp.exp(-m_t[c][h]))
            ht = ht - jnp.mean(ht, axis=0, keepdims=True)
            yy.append((ht * lax.rsqrt(jnp.mean(ht * ht, axis=0, keepdims=True) + EPS)).T)
        y_ch.append(yy)
        ct = [carry[c][h] * ct[h] + upd[c][h] for h in H]
        ns = [carry[c][h] * ns[h] + n_upd[c][h] for h in H]
    for h in H:
        ct_ref[h] = ct[h]
        n_ref[0, h:h + 1, :] = ns[h]
    y = jnp.concatenate([jnp.concatenate([y_ch[c][h] for c in C], axis=0) for h in H], axis=-1)
    y_ref[...] = (y * gn[...] * _sigmoid(o_ref[...].astype(F32)) * _silu(z_ref[...].astype(F32))
                  ).astype(y_ref.dtype)

    @pl.when(pl.program_id(1) == pl.num_programs(1) - 1)
    def _():
        for h in H:
            c_ref[0, h] = ct_ref[h].T


def _mlstm_params(ifb, gn):
    ifb_p = jnp.zeros((1, LANES), F32).at[0, SM_MI:SM_MI + 2 * M_HEADS].set(ifb)
    return ifb_p, gn.reshape(1, -1)


def _mlstm_prompt(proj, side, params, batch, seq, tb):
    nt = seq // tb
    cb = lambda c, w: pl.BlockSpec((tb, w), lambda b, t, c=c: (b * nt + t, c // w))
    full = lambda a: pl.BlockSpec(a.shape, lambda b, t: (0,) * a.ndim)
    kern = functools.partial(_mlstm_prompt_kernel, n_chunks=tb // CHUNK)
    return pl.pallas_call(
        kern,
        grid=(batch, nt),
        in_specs=[cb(C_MQ, M_QK), cb(C_MK, M_QK), cb(C_MV, M_W), cb(C_MO, M_W), cb(C_MZ, M_W),
                  cb(S_SMALL, LANES)] + [full(a) for a in params],
        out_specs=[pl.BlockSpec((tb, M_W), lambda b, t: (b * nt + t, 0)),
                   pl.BlockSpec((1, M_HEADS, M_DK, M_DV), lambda b, t: (b, 0, 0, 0)),
                   pl.BlockSpec((1, M_HEADS, M_DK), lambda b, t: (b, 0, 0)),
                   pl.BlockSpec((1, 1, M_HEADS), lambda b, t: (b, 0, 0))],
        out_shape=[jax.ShapeDtypeStruct((batch * seq, M_W), BF16),
                   jax.ShapeDtypeStruct((batch, M_HEADS, M_DK, M_DV), F32),
                   jax.ShapeDtypeStruct((batch, M_HEADS, M_DK), F32),
                   jax.ShapeDtypeStruct((batch, 1, M_HEADS), F32)],
        scratch_shapes=[pltpu.VMEM((M_HEADS, M_DV, M_DK), F32)],
        compiler_params=_cparams(("parallel", "arbitrary")),
        name="mlstm_prompt",
    )(proj, proj, proj, proj, proj, side, *params)


def _mlstm_step_kernel(q_ref, k_ref, v_ref, o_ref, z_ref, sm_ref, c_in, n_in, m_in, acc_ref, ifb, gn,
                       y_ref, c_out, n_out, m_out, vec_ref, sc_ref, hrow_ref, *, bb):
    del acc_ref
    pre = sm_ref[...] + ifb[...]
    li = pre[:, SM_MI:SM_MI + M_HEADS]
    lf = _log_sigmoid(pre[:, SM_MF:SM_MF + M_HEADS])
    m0 = m_in[...]
    m_inter = lf + m0
    m_t = jnp.maximum(m_inter, li)
    m_out[...] = m_t
    pad = lambda x: jnp.concatenate([x, jnp.zeros((bb, LANES - M_HEADS), F32)], axis=-1)
    sc_ref[0] = pad(jnp.exp(m_inter - m_t))
    sc_ref[1] = pad(jnp.exp(li - m_t))
    sc_ref[2] = pad(jnp.exp(-m_t))
    vec_ref[0] = q_ref[...].astype(F32) * (M_DK ** -0.5)
    vec_ref[1] = k_ref[...].astype(F32)
    vec_ref[2] = v_ref[...].astype(F32)
    masks = _outer_masks()

    def per_rows(j, _):
        bs = [j * STEP_ROWS + i for i in range(STEP_ROWS)]
        vecs = [[vec_ref[i, pl.ds(b, 1), :] for i in range(3)] for b in bs]
        scs = [[sc_ref[i, pl.ds(b, 1), :] for i in range(3)] for b in bs]
        items = [(bi, h) for bi in range(STEP_ROWS) for h in range(M_HEADS)]
        n_it = range(len(items))
        hs = lambda x, h: x[:, h * M_DK:(h + 1) * M_DK]
        q_h, k_h, v_h = ([hs(vecs[bi][i], h) for bi, h in items] for i in range(3))
        wi, ei, en = ([scs[bi][i][:, h:h + 1] for bi, h in items] for i in range(3))
        c = [c_in[bs[bi], h] for bi, h in items]
        n = [n_in[bs[bi], h:h + 1, :] for bi, h in items]
        qk = [_lane_sum(q_h[i] * k_h[i]) * ei[i] for i in n_it]
        qn = [_lane_sum(q_h[i] * n[i]) for i in n_it]
        qc = [_mm(_rows8(q_h[i]), c[i])[0:1] for i in n_it]
        kv = [_mm_tn(*_outer_operands(masks, (ei[i] * k_h[i], v_h[i]))) for i in n_it]
        hh = []
        for i, (bi, h) in enumerate(items):
            num = qk[i] * v_h[i] + wi[i] * qc[i]
            den = qk[i] + wi[i] * qn[i]
            hh.append(num / jnp.maximum(jnp.abs(den), en[i]))
            c_out[bs[bi], h] = wi[i] * c[i] + kv[i]
            n_out[bs[bi], h:h + 1, :] = wi[i] * n[i] + ei[i] * k_h[i]
        for bi in range(STEP_ROWS):
            hrow_ref[pl.ds(bs[bi], 1), :] = jnp.concatenate(
                [hh[bi * M_HEADS + h] for h in range(M_HEADS)], axis=-1)
        return 0

    lax.fori_loop(0, bb // STEP_ROWS, per_rows, 0)
    hh = hrow_ref[...]
    h_heads = [hh[:, h * M_DV:(h + 1) * M_DV] for h in range(M_HEADS)]
    y_ref[...] = _mlstm_out(h_heads, o_ref[...].astype(F32), z_ref[...].astype(F32), gn[...]).astype(y_ref.dtype)


def _mlstm_step(proj, side, c_all, n_all, m_all, acc, params, layer, bb):
    nb = proj.shape[0]
    cb = lambda c, w: pl.BlockSpec((bb, w), lambda i, c=c: (i, c // w))
    full = lambda a: pl.BlockSpec(a.shape, lambda i: (0,) * a.ndim)
    cs = _layer_block(c_all.shape[2:], bb, layer)
    ns_in = _layer_block(n_all.shape[2:], bb, layer)
    ms_in = _layer_block(m_all.shape[2:], bb, layer)
    ns = pl.BlockSpec((bb, M_HEADS, M_DK), lambda i: (i, 0, 0))
    ms = pl.BlockSpec((bb, M_HEADS), lambda i: (i, 0))
    kern = functools.partial(_mlstm_step_kernel, bb=bb)
    return pl.pallas_call(
        kern,
        grid=(nb // bb,),
        in_specs=[cb(C_MQ, M_QK), cb(C_MK, M_QK), cb(C_MV, M_W), cb(C_MO, M_W), cb(C_MZ, M_W),
                  cb(S_SMALL, LANES), cs, ns_in, ms_in, pl.BlockSpec(memory_space=pl.ANY)]
                 + [full(a) for a in params],
        out_specs=[pl.BlockSpec((bb, M_W), lambda i: (i, 0)), cs, ns, ms],
        out_shape=[jax.ShapeDtypeStruct((nb, M_W), BF16), jax.ShapeDtypeStruct(acc.shape, F32),
                   jax.ShapeDtypeStruct(n_all.shape[1:], F32), jax.ShapeDtypeStruct(m_all.shape[1:], F32)],
        input_output_aliases={9: 1},
        scratch_shapes=[pltpu.VMEM((3, bb, M_QK), F32), pltpu.VMEM((3, bb, LANES), F32),
                        pltpu.VMEM((bb, M_W), F32)],
        compiler_params=_cparams(("parallel",)),
        name="mlstm_step",
    )(proj, proj, proj, proj, proj, side, c_all, n_all, m_all, acc, *params)


def _pad_w_in(w_in):
    sizes = (R_SHIFT_W, R_W, G_QK, G_QK, G_W, G_LORA, G_W, M_QK, M_QK, M_W, M_HEADS, M_HEADS, M_W, M_W,
             3 * D_MODEL)
    offs = [0]
    for s in sizes:
        offs.append(offs[-1] + s)
    seg = lambda i: w_in[..., offs[i]:offs[i + 1]]
    (r_in, r_z, g_q, g_k, g_v, g_a, g_z, m_q, m_k, m_v, m_i, m_f, m_o, m_z, gate) = (seg(i) for i in range(15))
    zeros = lambda n: jnp.zeros(w_in.shape[:-1] + (n,), w_in.dtype)
    small = jnp.concatenate([g_a, m_i, m_f, zeros(LANES - G_LORA - 2 * M_HEADS)], axis=-1)
    w = jnp.concatenate([gate, r_z, g_v, g_z, m_q, m_k, m_v, m_o, m_z, g_q, g_k, r_in, small,
                         zeros(N_PAD - N_USED)], axis=-1)
    return w.astype(BF16)


def _shift_rows(proj_rows, side_rows):
    return jnp.concatenate([proj_rows[:, C_RIN:C_RIN + 3 * R_W].astype(F32),
                            side_rows[:, S_RLORA:S_RLORA + 2 * R_LORA]], axis=-1)


def _pick(n, prefs):
    for p in prefs:
        if n % p == 0:
            return p
    return n


def kernel(x_prompt, x_sample, state_rwkv_shift, state_rwkv, state_gla, state_mlstm_C, state_mlstm_n,
           state_mlstm_m, norm_pre, norm_post, w_in, r_mu_shift, r_w_lora_b, r_w0, r_a_lora_b, r_a0,
           r_k_k, r_k_a, r_r_k, r_gn, g_a_lora_b, g_a_bias, g_gn, m_if_bias, m_gn, w_br_rwkv, w_br_gla,
           w_br_mlstm, w_out):
    depth = w_in.shape[0]
    bp, tp, _ = x_prompt.shape
    bs, ts, _ = x_sample.shape
    assert ts == 1 and tp % CHUNK == 0
    w_pad = _pad_w_in(w_in)
    wr, wg, wm, wo = (w.astype(BF16) for w in (w_br_rwkv, w_br_gla, w_br_mlstm, w_out))

    mp = bp * tp
    tm_in = _pick(mp, (2048, 1024, 512, 256, 128, 64))
    tm_mg = _pick(mp, (512, 256, 128, 64))
    tb = _pick(tp, (256, 128, 64))
    bb = _pick(bs, (16, 8))

    xp = x_prompt.reshape(mp, D_MODEL)
    xs = x_sample.reshape(bs, D_MODEL)
    new_p = [[] for _ in range(6)]
    new_s = [[] for _ in range(3)]
    acc_r, acc_g, acc_c = (jnp.zeros(s.shape, F32) for s in (state_rwkv, state_gla, state_mlstm_C))
    for l in range(depth):
        g_pre = norm_pre[l].reshape(1, -1)
        g_post = norm_post[l].reshape(1, -1)
        rp = _rwkv_params(r_mu_shift[l], r_w_lora_b[l], r_w0[l], r_a_lora_b[l], r_a0[l], r_k_k[l],
                          r_k_a[l], r_r_k[l], r_gn[l])
        gp = _gla_params(g_a_lora_b[l], g_a_bias[l], g_gn[l])
        mparams = _mlstm_params(m_if_bias[l], m_gn[l])

        proj, side = _inproj(xp, g_pre, w_pad, l, tm_in)
        y_r, s_r = _rwkv_prompt(proj, side, rp, bp, tp, tb)
        y_g, s_g = _gla_prompt(proj, side, gp, bp, tp, tb)
        y_m, c, n, m = _mlstm_prompt(proj, side, mparams, bp, tp, tb)
        xp = _merge(proj, y_r, y_g, y_m, xp, wr, wg, wm, wo, g_post, l, tm_mg)
        shift = _shift_rows(proj.reshape(bp, tp, C_RLORA)[:, tp - 1], side.reshape(bp, tp, SIDE_W)[:, tp - 1])
        for lst, val in zip(new_p, (shift, s_r, s_g, c, n, m.reshape(bp, M_HEADS))):
            lst.append(val)

        proj, side = _inproj(xs, g_pre, w_pad, l, bs)
        y_r, acc_r = _rwkv_step(proj, side, state_rwkv_shift, state_rwkv, acc_r, rp, l, bb)
        y_g, acc_g = _gla_step(proj, side, state_gla, acc_g, gp, l, bb)
        y_m, acc_c, n, m = _mlstm_step(proj, side, state_mlstm_C, state_mlstm_n, state_mlstm_m, acc_c,
                                       mparams, l, bb)
        xs = _merge(proj, y_r, y_g, y_m, xs, wr, wg, wm, wo, g_post, l, bs)
        shift = _shift_rows(proj, side)
        for lst, val in zip(new_s, (shift, n, m)):
            lst.append(val)

    stk = lambda lst: jnp.stack(lst)
    return (xp.reshape(bp, tp, D_MODEL), xs.reshape(bs, ts, D_MODEL),
            *(stk(v) for v in new_p),
            stk(new_s[0]), acc_r, acc_g, acc_c, stk(new_s[1]), stk(new_s[2]))
```

```python
import functools
import math

import jax
import jax.numpy as jnp
from jax import lax
from jax.experimental import pallas as pl
from jax.experimental.pallas import tpu as pltpu

F32 = jnp.float32
BF16 = jnp.bfloat16

D_MODEL = 1024
R_HEADS, R_HD = 8, 64
R_W = R_HEADS * R_HD
R_LORA = 64
R_SHIFT_W = 3 * R_W + 2 * R_LORA
R_GN_EPS = 64e-5
G_HEADS, G_DK, G_DV = 4, 64, 128
G_QK = G_HEADS * G_DK
G_W = G_HEADS * G_DV
G_LORA = 16
G_GATE_TEMP = 16.0
M_HEADS, M_DK, M_DV = 4, 128, 128
M_QK = M_HEADS * M_DK
M_W = M_HEADS * M_DV
EPS = 1e-6

LANES = 128
VMEM_LIMIT = 48 * 1024 * 1024

C_GATE = 0
C_RZ = 3072
C_GV = 3584
C_GZ = 4096
C_MQ = 4608
C_MK = 5120
C_MV = 5632
C_MO = 6144
C_MZ = 6656
C_GQK = 7168
C_RIN = 7680
C_RLORA = C_RIN + 3 * R_W
C_SMALL = C_RLORA + LANES
SM_GA, SM_MI, SM_MF = 0, 16, 20
N_USED = C_SMALL + LANES
PROJ_TN = 512
N_PAD = -(-N_USED // PROJ_TN) * PROJ_TN
N_MAIN_BLOCKS = C_RLORA // PROJ_TN
SIDE_W = 2 * LANES
S_RLORA, S_SMALL = 0, LANES

CHUNK = 64


def _cparams(sem):
    return pltpu.CompilerParams(dimension_semantics=sem, vmem_limit_bytes=VMEM_LIMIT)


def _mm(a, b):
    return jnp.dot(a.astype(BF16), b.astype(BF16), preferred_element_type=F32)


def _mm_nt(a, b):
    return lax.dot_general(a.astype(BF16), b.astype(BF16), (((1,), (1,)), ((), ())),
                           preferred_element_type=F32)


def _mm_tn(a, b):
    return lax.dot_general(a.astype(BF16), b.astype(BF16), (((0,), (0,)), ((), ())),
                           preferred_element_type=F32)


def _mm_exact_lhs(m01, x):
    hi = x.astype(BF16)
    r1 = x - hi.astype(F32)
    mid = r1.astype(BF16)
    lo = (r1 - mid.astype(F32)).astype(BF16)
    m = m01.astype(BF16)
    d = lambda p: jnp.dot(m, p, preferred_element_type=F32)
    return d(hi) + d(mid) + d(lo)


def _mm_exact_rhs(x, m01):
    hi = x.astype(BF16)
    r1 = x - hi.astype(F32)
    mid = r1.astype(BF16)
    lo = (r1 - mid.astype(F32)).astype(BF16)
    m = m01.astype(BF16)
    d = lambda p: jnp.dot(p, m, preferred_element_type=F32)
    return d(hi) + d(mid) + d(lo)


def _sigmoid(x):
    return 1.0 / (1.0 + jnp.exp(-x))


def _silu(x):
    return x * _sigmoid(x)


def _softplus(x):
    return jnp.maximum(x, 0.0) + jnp.log(1.0 + jnp.exp(-jnp.abs(x)))


def _log_sigmoid(x):
    return -_softplus(-x)


def _iota2(shape, dim):
    return lax.broadcasted_iota(jnp.int32, shape, dim)


def _lane_sum(x):
    return jnp.sum(x, axis=-1, keepdims=True)


def _rows8(x):
    return jnp.broadcast_to(x, (8, x.shape[1]))


def _outer_masks():
    rid = _iota2((8, 1), 0)
    on = lambda *rows: sum((rid == r).astype(F32) for r in rows)
    return (on(0, 2), on(1), on(0, 1), on(2)), (on(3, 5), on(4), on(3, 4), on(5))


def _outer_operands(masks, *pairs):
    a_op, b_op = 0.0, 0.0
    for (a, b), (m_ahi, m_alo, m_bhi, m_blo) in zip(pairs, masks):
        a_hi = a.astype(BF16).astype(F32)
        b_hi = b.astype(BF16).astype(F32)
        a_op = a_op + m_ahi * a_hi + m_alo * (a - a_hi)
        b_op = b_op + m_bhi * b_hi + m_blo * (b - b_hi)
    return a_op, b_op


def _block_tril(n, blk):
    r, c = _iota2((n, n), 0), _iota2((n, n), 1)
    return ((r - c).astype(jnp.uint32) <= (r % blk).astype(jnp.uint32)).astype(F32)


def _inproj_kernel(x_ref, g_ref, w_ref, o_ref, side_ref, h_ref):
    j = pl.program_id(1)

    @pl.when(j == 0)
    def _():
        x = x_ref[...]
        y = x * lax.rsqrt(jnp.mean(x * x, axis=-1, keepdims=True) + EPS)
        h_ref[...] = (y * g_ref[...]).astype(BF16)

    acc = jnp.dot(h_ref[...], w_ref[...], preferred_element_type=F32)

    @pl.when(j < N_MAIN_BLOCKS)
    def _():
        o_ref[...] = acc.astype(BF16)

    @pl.when(j == N_MAIN_BLOCKS)
    def _():
        side_ref[...] = acc[:, 0:SIDE_W]


def _inproj(x2d, g_pre, w_pad, layer, tm):
    m = x2d.shape[0]
    return pl.pallas_call(
        _inproj_kernel,
        grid=(m // tm, N_MAIN_BLOCKS + 1),
        in_specs=[pl.BlockSpec((tm, D_MODEL), lambda i, j: (i, 0)),
                  pl.BlockSpec((1, D_MODEL), lambda i, j: (0, 0)),
                  pl.BlockSpec((None, D_MODEL, PROJ_TN), lambda i, j: (layer, 0, j))],
        out_specs=[pl.BlockSpec((tm, PROJ_TN), lambda i, j: (i, jnp.minimum(j, N_MAIN_BLOCKS - 1))),
                   pl.BlockSpec((tm, SIDE_W), lambda i, j: (i, 0))],
        out_shape=[jax.ShapeDtypeStruct((m, C_RLORA), BF16),
                   jax.ShapeDtypeStruct((m, SIDE_W), F32)],
        scratch_shapes=[pltpu.VMEM((tm, D_MODEL), BF16)],
        compiler_params=_cparams(("parallel", "arbitrary")),
        name="inproj",
    )(x2d, g_pre, w_pad)


def _merge_kernel(gate_ref, yr_ref, yg_ref, ym_ref, x_ref, wr_ref, wg_ref, wm_ref, wo_ref, gp_ref, o_ref):
    d = D_MODEL
    gate = lambda b: _sigmoid(gate_ref[:, b * d:(b + 1) * d].astype(F32))
    merged = (gate(0) * _mm(yr_ref[...], wr_ref[...]) + gate(1) * _mm(yg_ref[...], wg_ref[...])
              + gate(2) * _mm(ym_ref[...], wm_ref[...]))
    o = _mm(merged, wo_ref[...])
    y = o * lax.rsqrt(jnp.mean(o * o, axis=-1, keepdims=True) + EPS)
    o_ref[...] = x_ref[...] + y * gp_ref[...]


def _merge(proj, y_r, y_g, y_m, x2d, w_r, w_g, w_m, w_o, g_post, layer, tm):
    m = x2d.shape[0]
    row = lambda w: pl.BlockSpec((tm, w), lambda i: (i, 0))
    full = lambda a: pl.BlockSpec(a.shape, lambda i: (0, 0))
    wl = lambda a: pl.BlockSpec((None,) + a.shape[1:], lambda i: (layer, 0, 0))
    return pl.pallas_call(
        _merge_kernel,
        grid=(m // tm,),
        in_specs=[row(3 * D_MODEL), row(R_W), row(G_W), row(M_W), row(D_MODEL),
                  wl(w_r), wl(w_g), wl(w_m), wl(w_o), full(g_post)],
        out_specs=row(D_MODEL),
        out_shape=jax.ShapeDtypeStruct((m, D_MODEL), F32),
        compiler_params=_cparams(("parallel",)),
        name="merge",
    )(proj, y_r, y_g, y_m, x2d, w_r, w_g, w_m, w_o, g_post)


def _rwkv_front(r_in, k_in, v_in, l_in, prev_r, prev_k, prev_v, prev_l, p):
    (mu_r, mu_k, mu_v, mu_l, wb, w0, ab, a0, k_k, k_a) = p
    r = r_in + mu_r * (prev_r - r_in)
    k = k_in + mu_k * (prev_k - k_in)
    v = v_in + mu_v * (prev_v - v_in)
    lo = l_in + mu_l * (prev_l - l_in)
    log_w = -_softplus(-(w0 + _mm(jnp.tanh(lo), wb))) - 0.5
    lw = -jnp.exp(log_w)
    a = _sigmoid(a0 + _mm(lo, ab))
    kk = k * k_k
    k2 = k * (1.0 + (a - 1.0) * k_a)
    return r, k2, v, lw, a, kk


def _rwkv_out(y_heads, r, k2, v, z, r_k, gn):
    outs = []
    for h in range(R_HEADS):
        sl = slice(h * R_HD, (h + 1) * R_HD)
        y = y_heads[h]
        y = y - jnp.mean(y, axis=-1, keepdims=True)
        y = y * lax.rsqrt(jnp.mean(y * y, axis=-1, keepdims=True) + R_GN_EPS)
        bonus = _lane_sum(r[:, sl] * k2[:, sl] * r_k[:, sl]) * v[:, sl]
        outs.append(y * gn[:, sl] + bonus)
    return jnp.concatenate(outs, axis=-1) * _silu(z)


def _rwkv_prompt_kernel(z_ref, r_ref, k_ref, v_ref, l_ref,
                        mu_r, mu_k, mu_v, mu_l, wb, w0, ab, a0, k_k, k_a, r_k, gn,
                        y_ref, s_ref, carry_ref, sp_ref, *, n_chunks):
    L = CHUNK

    @pl.when(pl.program_id(1) == 0)
    def _():
        sp_ref[...] = jnp.zeros_like(sp_ref)
        carry_ref[...] = jnp.zeros_like(carry_ref)

    tb = n_chunks * L
    params = tuple(x[...] for x in (mu_r, mu_k, mu_v, mu_l, wb, w0, ab, a0, k_k, k_a))
    assert L == R_HD
    m_a = (_iota2((1, LANES), 1) < R_HD).astype(F32)
    m_b = 1.0 - m_a
    m_a16, m_b16 = m_a.astype(BF16), m_b.astype(BF16)
    trow = _iota2((L, LANES), 0)
    tcol = _iota2((L, LANES), 1) % L
    eye = (tcol == trow).astype(F32)
    blk_mask = (trow // 16) == (tcol // 16)
    r2 = _iota2((2 * L, LANES), 0)
    g_mask = (_iota2((2 * L, LANES), 1) % L) < (r2 % L) + (r2 >= L).astype(jnp.int32)
    same_head = (_iota2((LANES, LANES), 0) // R_HD) == (_iota2((LANES, LANES), 1) // R_HD)
    first_row = _iota2((tb, 1), 0) == 0

    def seg_sum(x):
        return _lane_sum(x * m_a) * m_a + _lane_sum(x * m_b) * m_b

    def bd(y):
        yb = y.astype(BF16)
        return jnp.concatenate([yb * m_a16, yb * m_b16], axis=0)

    def shifted(x, carry):
        return jnp.where(first_row, carry, pltpu.roll(x, 1, 0))

    r_in, k_in, v_in = (x[...].astype(F32) for x in (r_ref, k_ref, v_ref))
    l_in = l_ref[...]
    prev_r = shifted(r_in, carry_ref[0:1, 0:R_W])
    prev_k = shifted(k_in, carry_ref[0:1, R_W:2 * R_W])
    prev_v = shifted(v_in, carry_ref[0:1, 2 * R_W:3 * R_W])
    prev_l = shifted(l_in, carry_ref[0:1, 3 * R_W:R_SHIFT_W])
    carry_ref[0:1, 0:R_W] = r_in[tb - 1:tb, :]
    carry_ref[0:1, R_W:2 * R_W] = k_in[tb - 1:tb, :]
    carry_ref[0:1, 2 * R_W:3 * R_W] = v_in[tb - 1:tb, :]
    carry_ref[0:1, 3 * R_W:R_SHIFT_W] = l_in[tb - 1:tb, :]
    r, k2, v, lw, a, kk = _rwkv_front(r_in, k_in, v_in, l_in, prev_r, prev_k, prev_v, prev_l, params)

    cum = _mm_exact_lhs(_block_tril(tb, L), lw)
    p_inc = jnp.exp(cum)
    p_exc = jnp.exp(cum - lw)
    p_inv = jnp.exp(-cum)
    r_t = r * p_inc
    k_h = k2 * p_inv

    n_pairs = R_HEADS // 2
    CP = [(c, p) for c in range(n_chunks) for p in range(n_pairs)]
    N = range(len(CP))
    kap_t, beta_h = [], []
    for p in range(n_pairs):
        sl = slice(p * LANES, (p + 1) * LANES)
        kk_p = kk[:, sl]
        kap = kk_p * (1.0 / jnp.maximum(jnp.sqrt(seg_sum(kk_p * kk_p)), 1e-12))
        beta_h.append(kap * a[:, sl] * p_inv[:, sl])
        kap_t.append(kap * p_exc[:, sl])
    cut = lambda x, c, p: x[c * L:(c + 1) * L, p * LANES:(p + 1) * LANES]
    kap_t = [kap_t[p][c * L:(c + 1) * L] for c, p in CP]
    beta_h = [beta_h[p][c * L:(c + 1) * L] for c, p in CP]
    rt_c = [cut(r_t, c, p) for c, p in CP]
    kh_c = [cut(k_h, c, p) for c, p in CP]
    v_c = [cut(v, c, p) for c, p in CP]
    p_last = [p_inc[(c + 1) * L - 1:(c + 1) * L, p * LANES:(p + 1) * LANES] for c, p in CP]
    bhl = [beta_h[i] * p_last[i] for i in N]
    khl = [kh_c[i] * p_last[i] for i in N]
    lhs2 = [jnp.concatenate([kap_t[i], rt_c[i]], axis=0) for i in N]
    bd_v = [bd(v_c[i]) for i in N]
    gb = [jnp.where(g_mask, _mm_nt(lhs2[i], bd(beta_h[i])), 0.0) for i in N]
    gk = [jnp.where(g_mask, _mm_nt(lhs2[i], bd(kh_c[i])), 0.0) for i in N]
    a_m = [gb[i][0:L] for i in N]
    d = [jnp.where(blk_mask, a_m[i], 0.0) for i in N]
    akv = [_mm(gk[i][0:L], bd_v[i]) for i in N]
    d2 = [_mm(d[i], bd(d[i])) for i in N]
    bd_d2 = [bd(d2[i]) for i in N]
    d4 = [_mm(d2[i], bd_d2[i]) for i in N]
    dd2 = [_mm(d[i], bd_d2[i]) for i in N]
    d8 = [_mm(d4[i], bd(d4[i])) for i in N]
    x = [eye - d[i] + d2[i] - dd2[i] for i in N]
    x = [x[i] + _mm(x[i], bd(d4[i])) for i in N]
    td = [x[i] + _mm(x[i], bd(d8[i])) for i in N]
    m = [_mm(td[i], bd(a_m[i] - d[i])) for i in N]
    m2 = [_mm(m[i], bd(m[i])) for i in N]
    q = [eye - m[i] + m2[i] - _mm(m[i], bd(m2[i])) for i in N]
    t_inv = [_mm(q[i], bd(td[i])) for i in N]
    kbar = [_mm(t_inv[i], bd(kap_t[i])) for i in N]
    u0 = [-_mm(t_inv[i], bd(akv[i])) for i in N]
    n_bd = [jnp.where(same_head, _mm_tn(kbar[i], bhl[i]), 0.0) for i in N]
    cz = [_mm_tn(jnp.concatenate([u0[i], v_c[i]], axis=0), jnp.concatenate([bhl[i], khl[i]], axis=0))
          for i in N]
    c_c = [cz[i][0:R_HD] * m_a + cz[i][R_HD:2 * R_HD] * m_b for i in N]
    qt = [rt_c[i] - _mm(gb[i][L:2 * L], bd(kbar[i])) for i in N]
    y0 = [_mm(gb[i][L:2 * L], bd(u0[i])) + _mm(gk[i][L:2 * L], bd_v[i]) for i in N]
    s = [sp_ref[p] for p in range(n_pairs)]
    y_ch = []
    for c in range(n_chunks):
        ids = [c * n_pairs + p for p in range(n_pairs)]
        y_ch.append([_mm_nt(qt[i], bd(s[p])) + y0[i] for p, i in enumerate(ids)])
        s = [s[p] * p_last[i] - _mm(s[p], n_bd[i]) + c_c[i] for p, i in enumerate(ids)]
    for p in range(n_pairs):
        sp_ref[p] = s[p]
    y_pairs = [jnp.concatenate([y_ch[c][p] for c in range(n_chunks)], axis=0) for p in range(n_pairs)]
    outs = []
    for p in range(n_pairs):
        sl = slice(p * LANES, (p + 1) * LANES)
        y = y_pairs[p]
        y = y - seg_sum(y) * (1.0 / R_HD)
        y = y * lax.rsqrt(seg_sum(y * y) * (1.0 / R_HD) + R_GN_EPS)
        bonus = seg_sum(r[:, sl] * k2[:, sl] * r_k[:, sl]) * v[:, sl]
        outs.append(y * gn[:, sl] + bonus)
    y_ref[...] = (jnp.concatenate(outs, axis=-1) * _silu(z_ref[...].astype(F32))).astype(y_ref.dtype)

    @pl.when(pl.program_id(1) == pl.num_programs(1) - 1)
    def _():
        for p in range(n_pairs):
            sp = sp_ref[p]
            s_ref[0, 2 * p] = sp[:, 0:R_HD]
            s_ref[0, 2 * p + 1] = sp[:, R_HD:2 * R_HD]


def _rwkv_params(mu, wb, w0, ab, a0, k_k, k_a, r_k, gn):
    row = lambda x: x.reshape(1, -1)
    zeros = jnp.zeros((R_LORA, R_W), F32)
    wb_p = jnp.concatenate([wb, zeros], axis=0).astype(BF16)
    ab_p = jnp.concatenate([zeros, ab], axis=0).astype(BF16)
    return (row(mu[0:R_W]), row(mu[R_W:2 * R_W]), row(mu[2 * R_W:3 * R_W]), row(mu[3 * R_W:]),
            wb_p, row(w0), ab_p, row(a0), row(k_k), row(k_a), row(r_k), row(gn))


def _rwkv_prompt(proj, side, params, batch, seq, tb):
    nt = seq // tb
    cb = lambda c, w: pl.BlockSpec((tb, w), lambda b, t, c=c: (b * nt + t, c // w))
    full = lambda a: pl.BlockSpec(a.shape, lambda b, t: (0,) * a.ndim)
    kern = functools.partial(_rwkv_prompt_kernel, n_chunks=tb // CHUNK)
    return pl.pallas_call(
        kern,
        grid=(batch, nt),
        in_specs=[cb(C_RZ, R_W), cb(C_RIN, R_W), cb(C_RIN + R_W, R_W), cb(C_RIN + 2 * R_W, R_W),
                  cb(S_RLORA, LANES)] + [full(a) for a in params],
        out_specs=[pl.BlockSpec((tb, R_W), lambda b, t: (b * nt + t, 0)),
                   pl.BlockSpec((1, R_HEADS, R_HD, R_HD), lambda b, t: (b, 0, 0, 0))],
        out_shape=[jax.ShapeDtypeStruct((batch * seq, R_W), BF16),
                   jax.ShapeDtypeStruct((batch, R_HEADS, R_HD, R_HD), F32)],
        scratch_shapes=[pltpu.VMEM((8, R_SHIFT_W), F32),
                        pltpu.VMEM((R_HEADS // 2, R_HD, LANES), F32)],
        compiler_params=_cparams(("parallel", "arbitrary")),
        name="rwkv_prompt",
    )(proj, proj, proj, proj, side, *params)


STEP_ROWS = 4
GLA_STEP_ROWS = 2


def _rwkv_step_kernel(z_ref, r_ref, k_ref, v_ref, l_ref, pr_ref, pk_ref, pv_ref, pl_ref, s_in, acc_ref,
                      mu_r, mu_k, mu_v, mu_l, wb, w0, ab, a0, k_k, k_a, r_k, gn,
                      y_ref, s_out, vec_ref, yrow_ref, *, bb):
    del acc_ref
    params = tuple(x[...] for x in (mu_r, mu_k, mu_v, mu_l, wb, w0, ab, a0, k_k, k_a))
    r, k2, v, lw, a, kk = _rwkv_front(r_ref[...].astype(F32), k_ref[...].astype(F32), v_ref[...].astype(F32),
                                      l_ref[...], pr_ref[...], pk_ref[...], pv_ref[...], pl_ref[...], params)
    w = jnp.exp(lw)
    kaps, betas = [], []
    for h in range(R_HEADS):
        sl = slice(h * R_HD, (h + 1) * R_HD)
        kk_h = kk[:, sl]
        kap = kk_h / jnp.maximum(jnp.sqrt(_lane_sum(kk_h * kk_h)), 1e-12)
        kaps.append(kap)
        betas.append(kap * a[:, sl])
    vec_ref[0] = jnp.concatenate(kaps, axis=-1)
    vec_ref[1] = jnp.concatenate(betas, axis=-1)
    vec_ref[2] = w
    vec_ref[3] = k2
    vec_ref[4] = v
    vec_ref[5] = r
    masks = _outer_masks()

    def per_rows(j, _):
        bs = [j * STEP_ROWS + i for i in range(STEP_ROWS)]
        vecs = [[vec_ref[i, pl.ds(b, 1), :] for i in range(6)] for b in bs]
        items = [(bi, h) for bi in range(STEP_ROWS) for h in range(R_HEADS)]
        hs = lambda x, h: x[:, h * R_HD:(h + 1) * R_HD]
        kap, beta, w_, k_, v_, r_ = ([hs(vecs[bi][i], h) for bi, h in items] for i in range(6))
        n = range(len(items))
        s = [s_in[bs[bi], h] for bi, h in items]
        sa = [-_mm_nt(_rows8(kap[i]), s[i])[0:1] for i in n]
        ops = [_outer_operands(masks, (sa[i], beta[i]), (v_[i], k_[i])) for i in n]
        s_new = [s[i] * w_[i] + _mm_tn(*ops[i]) for i in n]
        for i, (bi, h) in enumerate(items):
            s_out[bs[bi], h] = s_new[i]
        y_row = [_mm_nt(_rows8(r_[i]), s_new[i])[0:1] for i in n]
        for bi in range(STEP_ROWS):
            yrow_ref[pl.ds(bs[bi], 1), :] = jnp.concatenate(
                [y_row[bi * R_HEADS + h] for h in range(R_HEADS)], axis=-1)
        return 0

    lax.fori_loop(0, bb // STEP_ROWS, per_rows, 0)
    y = yrow_ref[...]
    y_heads = [y[:, h * R_HD:(h + 1) * R_HD] for h in range(R_HEADS)]
    y_ref[...] = _rwkv_out(y_heads, r, k2, v, z_ref[...].astype(F32), r_k[...], gn[...]).astype(y_ref.dtype)


def _layer_block(shape_tail, bb, layer):
    zeros = (0,) * len(shape_tail)
    return pl.BlockSpec((None, bb) + tuple(shape_tail), lambda i: (layer, i) + zeros)


def _rwkv_step(proj, side, shift_all, s_all, acc, params, layer, bb):
    nb = proj.shape[0]
    cb = lambda c, w: pl.BlockSpec((bb, w), lambda i, c=c: (i, c // w))
    sh = lambda c, w: pl.BlockSpec((None, bb, w), lambda i, c=c: (layer, i, c // w))
    full = lambda a: pl.BlockSpec(a.shape, lambda i: (0,) * a.ndim)
    st = _layer_block(s_all.shape[2:], bb, layer)
    kern = functools.partial(_rwkv_step_kernel, bb=bb)
    return pl.pallas_call(
        kern,
        grid=(nb // bb,),
        in_specs=[cb(C_RZ, R_W), cb(C_RIN, R_W), cb(C_RIN + R_W, R_W), cb(C_RIN + 2 * R_W, R_W),
                  cb(S_RLORA, LANES),
                  sh(0, R_W), sh(R_W, R_W), sh(2 * R_W, R_W), sh(3 * R_W, LANES), st,
                  pl.BlockSpec(memory_space=pl.ANY)]
                 + [full(a) for a in params],
        out_specs=[pl.BlockSpec((bb, R_W), lambda i: (i, 0)), st],
        out_shape=[jax.ShapeDtypeStruct((nb, R_W), BF16),
                   jax.ShapeDtypeStruct(acc.shape, F32)],
        input_output_aliases={10: 1},
        scratch_shapes=[pltpu.VMEM((6, bb, R_W), F32), pltpu.VMEM((bb, R_W), F32)],
        compiler_params=_cparams(("parallel",)),
        name="rwkv_step",
    )(proj, proj, proj, proj, side, shift_all, shift_all, shift_all, shift_all, s_all, acc, *params)


def _gla_log_gate(small, ab_p, a_bias):
    return _log_sigmoid(_mm(small, ab_p) + a_bias) / G_GATE_TEMP


def _gla_out(o_heads, z, gn):
    outs = []
    for h in range(G_HEADS):
        o = o_heads[h]
        outs.append(o * lax.rsqrt(jnp.mean(o * o, axis=-1, keepdims=True) + EPS))
    return jnp.concatenate(outs, axis=-1) * gn * _silu(z)


def _gla_prompt_kernel(qk_ref, v_ref, z_ref, sm_ref, ab, a_bias, gn, y_ref, s_ref, st_ref, *, n_chunks):
    L = CHUNK
    last = pl.num_programs(1) - 1

    @pl.when(pl.program_id(1) == 0)
    def _():
        st_ref[...] = jnp.zeros_like(st_ref)

    tril_incl = _iota2((L, L), 1) <= _iota2((L, L), 0)
    scale = G_DK ** -0.5
    C, H = range(n_chunks), range(G_HEADS)
    rws = [slice(c * L, (c + 1) * L) for c in C]
    sks = [slice(h * G_DK, (h + 1) * G_DK) for h in H]

    qk = qk_ref[...].astype(F32)
    q, k = qk[:, 0:G_QK] * scale, qk[:, G_QK:2 * G_QK]
    v = v_ref[...].astype(F32)
    la = _gla_log_gate(sm_ref[...], ab[...], a_bias[...])
    cum = _mm_exact_lhs(_block_tril(n_chunks * L, L), la)
    q0, qe, ke, kl, e_last = [], [], [], [], []
    for c in C:
        cum_c = cum[rws[c]]
        ref_row = cum_c[L // 2:L // 2 + 1, :]
        last_row = cum_c[L - 1:L, :]
        q0.append(q[rws[c]] * jnp.exp(cum_c))
        qe.append(q[rws[c]] * jnp.exp(cum_c - ref_row))
        ke.append(k[rws[c]] * jnp.exp(ref_row - cum_c))
        kl.append(k[rws[c]] * jnp.exp(last_row - cum_c))
        e_last.append(jnp.exp(last_row))
    v_h = [[v[rws[c], h * G_DV:(h + 1) * G_DV] for h in H] for c in C]
    att = [[jnp.where(tril_incl, _mm_nt(qe[c][:, sks[h]], ke[c][:, sks[h]]), 0.0) for h in H] for c in C]
    upd = [[_mm_tn(v_h[c][h], kl[c][:, sks[h]]) for h in H] for c in C]
    av = [[_mm(att[c][h], v_h[c][h]) for h in H] for c in C]
    st = [st_ref[h] for h in H]
    o_ch = []
    for c in C:
        o_ch.append([av[c][h] + _mm_nt(q0[c][:, sks[h]], st[h]) for h in H])
        st = [st[h] * e_last[c][:, sks[h]] + upd[c][h] for h in H]
    for h in H:
        st_ref[h] = st[h]
    o_heads = [jnp.concatenate([o_ch[c][h] for c in C], axis=0) for h in H]
    y_ref[...] = _gla_out(o_heads, z_ref[...].astype(F32), gn[...]).astype(y_ref.dtype)

    @pl.when(pl.program_id(1) == last)
    def _():
        for h in range(G_HEADS):
            s_ref[0, h] = st_ref[h].T


def _gla_params(ab, a_bias, gn):
    ab_p = jnp.zeros((LANES, G_QK), F32).at[SM_GA:SM_GA + G_LORA].set(ab).astype(BF16)
    return ab_p, a_bias.reshape(1, -1), gn.reshape(1, -1)


def _gla_prompt(proj, side, params, batch, seq, tb):
    nt = seq // tb
    cb = lambda c, w: pl.BlockSpec((tb, w), lambda b, t, c=c: (b * nt + t, c // w))
    full = lambda a: pl.BlockSpec(a.shape, lambda b, t: (0,) * a.ndim)
    kern = functools.partial(_gla_prompt_kernel, n_chunks=tb // CHUNK)
    return pl.pallas_call(
        kern,
        grid=(batch, nt),
        in_specs=[cb(C_GQK, 2 * G_QK), cb(C_GV, G_W), cb(C_GZ, G_W), cb(S_SMALL, LANES)]
                 + [full(a) for a in params],
        out_specs=[pl.BlockSpec((tb, G_W), lambda b, t: (b * nt + t, 0)),
                   pl.BlockSpec((1, G_HEADS, G_DK, G_DV), lambda b, t: (b, 0, 0, 0))],
        out_shape=[jax.ShapeDtypeStruct((batch * seq, G_W), BF16),
                   jax.ShapeDtypeStruct((batch, G_HEADS, G_DK, G_DV), F32)],
        scratch_shapes=[pltpu.VMEM((G_HEADS, G_DV, G_DK), F32)],
        compiler_params=_cparams(("parallel", "arbitrary")),
        name="gla_prompt",
    )(proj, proj, proj, side, *params)


def _gla_step_kernel(qk_ref, v_ref, z_ref, sm_ref, s_in, acc_ref, ab, a_bias, gn, y_ref, s_out,
                     vec_ref, vrow_ref, orow_ref, *, bb):
    del acc_ref
    qk = qk_ref[...].astype(F32)
    q = qk[:, 0:G_QK] * (G_DK ** -0.5)
    k = qk[:, G_QK:2 * G_QK]
    g = _gla_log_gate(sm_ref[...], ab[...], a_bias[...])
    vec_ref[0] = q
    vec_ref[1] = k
    vec_ref[2] = jnp.exp(g)
    vrow_ref[...] = v_ref[...].astype(F32)
    eye = (_iota2((G_DK, G_DK), 0) == _iota2((G_DK, G_DK), 1)).astype(F32)
    masks = _outer_masks()

    def per_rows(j, _):
        bs = [j * GLA_STEP_ROWS + i for i in range(GLA_STEP_ROWS)]
        vecs = [[vec_ref[i, pl.ds(b, 1), :] for i in range(3)] for b in bs]
        v_b = [vrow_ref[pl.ds(b, 1), :] for b in bs]
        items = [(bi, h) for bi in range(GLA_STEP_ROWS) for h in range(G_HEADS)]
        n = range(len(items))
        hk = lambda x, h: x[:, h * G_DK:(h + 1) * G_DK]
        q_, k_, e_ = ([hk(vecs[bi][i], h) for bi, h in items] for i in range(3))
        v_h = [v_b[bi][:, h * G_DV:(h + 1) * G_DV] for bi, h in items]
        s = [s_in[bs[bi], h] for bi, h in items]
        e_col = [_lane_sum(eye * e_[i]) for i in n]
        qk_dot = [_lane_sum(q_[i] * k_[i]) for i in n]
        o_s = [_mm(_rows8(q_[i] * e_[i]), s[i])[0:1] for i in n]
        kv = [_mm_tn(*_outer_operands(masks, (k_[i], v_h[i]))) for i in n]
        o = [qk_dot[i] * v_h[i] + o_s[i] for i in n]
        for i, (bi, h) in enumerate(items):
            s_out[bs[bi], h] = s[i] * e_col[i] + kv[i]
        for bi in range(GLA_STEP_ROWS):
            orow_ref[pl.ds(bs[bi], 1), :] = jnp.concatenate(
                [o[bi * G_HEADS + h] for h in range(G_HEADS)], axis=-1)
        return 0

    lax.fori_loop(0, bb // GLA_STEP_ROWS, per_rows, 0)
    o = orow_ref[...]
    o_heads = [o[:, h * G_DV:(h + 1) * G_DV] for h in range(G_HEADS)]
    y_ref[...] = _gla_out(o_heads, z_ref[...].astype(F32), gn[...]).astype(y_ref.dtype)


def _gla_step(proj, side, s_all, acc, params, layer, bb):
    nb = proj.shape[0]
    cb = lambda c, w: pl.BlockSpec((bb, w), lambda i, c=c: (i, c // w))
    full = lambda a: pl.BlockSpec(a.shape, lambda i: (0,) * a.ndim)
    st = _layer_block(s_all.shape[2:], bb, layer)
    kern = functools.partial(_gla_step_kernel, bb=bb)
    return pl.pallas_call(
        kern,
        grid=(nb // bb,),
        in_specs=[cb(C_GQK, 2 * G_QK), cb(C_GV, G_W), cb(C_GZ, G_W), cb(S_SMALL, LANES), st,
                  pl.BlockSpec(memory_space=pl.ANY)]
                 + [full(a) for a in params],
        out_specs=[pl.BlockSpec((bb, G_W), lambda i: (i, 0)), st],
        out_shape=[jax.ShapeDtypeStruct((nb, G_W), BF16), jax.ShapeDtypeStruct(acc.shape, F32)],
        input_output_aliases={5: 1},
        scratch_shapes=[pltpu.VMEM((3, bb, G_QK), F32), pltpu.VMEM((bb, G_W), F32),
                        pltpu.VMEM((bb, G_W), F32)],
        compiler_params=_cparams(("parallel",)),
        name="gla_step",
    )(proj, proj, proj, side, s_all, acc, *params)


def _mlstm_out(h_heads, o_pre, z, gn):
    outs = []
    for h in range(M_HEADS):
        y = h_heads[h]
        y = y - jnp.mean(y, axis=-1, keepdims=True)
        outs.append(y * lax.rsqrt(jnp.mean(y * y, axis=-1, keepdims=True) + EPS))
    return jnp.concatenate(outs, axis=-1) * gn * _sigmoid(o_pre) * _silu(z)


def _mlstm_prompt_kernel(q_ref, k_ref, v_ref, o_ref, z_ref, sm_ref, ifb, gn,
                         y_ref, c_ref, n_ref, m_ref, ct_ref, *, n_chunks):
    L = CHUNK

    @pl.when(pl.program_id(1) == 0)
    def _():
        ct_ref[...] = jnp.zeros_like(ct_ref)
        n_ref[...] = jnp.zeros_like(n_ref)
        m_ref[...] = jnp.zeros_like(m_ref)

    causal = _iota2((L, L), 0) <= _iota2((L, L), 1)
    eye = (_iota2((L, L), 0) == _iota2((L, L), 1)).astype(F32)
    scale = M_DK ** -0.5
    neg_inf = -jnp.inf
    C, H = range(n_chunks), range(M_HEADS)
    rws = [slice(c * L, (c + 1) * L) for c in C]
    sls = [slice(h * M_DK, (h + 1) * M_DK) for h in H]

    pre = sm_ref[...] + ifb[...]
    lf = _log_sigmoid(pre)
    b_all = _mm_exact_lhs(_block_tril(n_chunks * L, L), lf)
    lib_all = pre - pltpu.roll(b_all, LANES - (SM_MF - SM_MI), 1)
    sel = (_iota2((LANES, M_HEADS * LANES), 0)
           == SM_MI + _iota2((LANES, M_HEADS * LANES), 1) // LANES).astype(F32)
    lib_bc = _mm_exact_rhs(lib_all, sel)
    comb = jnp.where(_iota2((1, LANES), 1) < SM_MF, lib_all, b_all)
    q, k, v = q_ref[...].astype(F32) * scale, k_ref[...].astype(F32), v_ref[...].astype(F32)
    qs = [[q[rws[c], sls[h]] for h in H] for c in C]
    k_h = [[k[rws[c], sls[h]] for h in H] for c in C]
    v_h = [[v[rws[c], sls[h]] for h in H] for c in C]
    qk_raw = [[_mm_nt(k_h[c][h], qs[c][h]) for h in H] for c in C]
    d_log, d_max, b_row, lib_row = [], [], [], []
    for c in C:
        comb_t = comb[rws[c]].T
        b_row.append([comb_t[SM_MF + h:SM_MF + h + 1, :] for h in H])
        lib_row.append([comb_t[SM_MI + h:SM_MI + h + 1, :] for h in H])
        d_log.append([jnp.where(causal, lib_bc[rws[c], h * LANES:h * LANES + L] + b_row[c][h], neg_inf)
                      for h in H])
        d_max.append([jnp.max(d_log[c][h], axis=0, keepdims=True) for h in H])
    m_prev = [m_ref[0, :, h:h + 1] for h in H]
    m_t, w_inter, carry, w_state, qk = [], [], [], [], []
    for c in C:
        m_inter = [b_row[c][h] + m_prev[h] for h in H]
        m_t.append([jnp.maximum(m_inter[h], d_max[c][h]) for h in H])
        w_inter.append([jnp.exp(m_inter[h] - m_t[c][h]) for h in H])
        m_new = [m_t[c][h][:, L - 1:L] for h in H]
        b_last = [b_row[c][h][:, L - 1:L] for h in H]
        carry.append([jnp.exp(b_last[h] + m_prev[h] - m_new[h]) for h in H])
        w_state.append([jnp.exp(b_last[h] + lib_row[c][h] - m_new[h]) for h in H])
        qk.append([qk_raw[c][h] * jnp.exp(d_log[c][h] - m_t[c][h]) for h in H])
        m_prev = m_new
    for h in H:
        m_ref[0, :, h:h + 1] = m_prev[h]
    wk = [[_mm(eye * w_state[c][h], k_h[c][h]) for h in H] for c in C]
    n_upd = [[_mm(_rows8(w_state[c][h]), k_h[c][h])[0:1] for h in H] for c in C]
    upd = [[_mm_tn(v_h[c][h], wk[c][h]) for h in H] for c in C]
    qkv = [[_mm_tn(v_h[c][h], qk[c][h]) for h in H] for c in C]
    ct = [ct_ref[h] for h in H]
    ns = [n_ref[0, h:h + 1, :] for h in H]
    y_ch = []
    for c in C:
        qc = [_mm_nt(ct[h], qs[c][h]) for h in H]
        qn = [_mm_nt(_rows8(ns[h]), qs[c][h])[0:1] for h in H]
        yy = []
        for h in H:
            num = qkv[c][h] + w_inter[c][h] * qc[h]
            den = jnp.sum(qk[c][h], axis=0, keepdims=True) + w_inter[c][h] * qn[h]
            ht = num / jnp.maximum(jnp.abs(den), jnp.exp(-m_t[c][h]))
            ht = ht - jnp.mean(ht, axis=0, keepdims=True)
            yy.append((ht * lax.rsqrt(jnp.mean(ht * ht, axis=0, keepdims=True) + EPS)).T)
        y_ch.append(yy)
        ct = [carry[c][h] * ct[h] + upd[c][h] for h in H]
        ns = [carry[c][h] * ns[h] + n_upd[c][h] for h in H]
    for h in H:
        ct_ref[h] = ct[h]
        n_ref[0, h:h + 1, :] = ns[h]
    y = jnp.concatenate([jnp.concatenate([y_ch[c][h] for c in C], axis=0) for h in H], axis=-1)
    y_ref[...] = (y * gn[...] * _sigmoid(o_ref[...].astype(F32)) * _silu(z_ref[...].astype(F32))
                  ).astype(y_ref.dtype)

    @pl.when(pl.program_id(1) == pl.num_programs(1) - 1)
    def _():
        for h in H:
            c_ref[0, h] = ct_ref[h].T


def _mlstm_params(ifb, gn):
    ifb_p = jnp.zeros((1, LANES), F32).at[0, SM_MI:SM_MI + 2 * M_HEADS].set(ifb)
    return ifb_p, gn.reshape(1, -1)


def _mlstm_prompt(proj, side, params, batch, seq, tb):
    nt = seq // tb
    cb = lambda c, w: pl.BlockSpec((tb, w), lambda b, t, c=c: (b * nt + t, c // w))
    full = lambda a: pl.BlockSpec(a.shape, lambda b, t: (0,) * a.ndim)
    kern = functools.partial(_mlstm_prompt_kernel, n_chunks=tb // CHUNK)
    return pl.pallas_call(
        kern,
        grid=(batch, nt),
        in_specs=[cb(C_MQ, M_QK), cb(C_MK, M_QK), cb(C_MV, M_W), cb(C_MO, M_W), cb(C_MZ, M_W),
                  cb(S_SMALL, LANES)] + [full(a) for a in params],
        out_specs=[pl.BlockSpec((tb, M_W), lambda b, t: (b * nt + t, 0)),
                   pl.BlockSpec((1, M_HEADS, M_DK, M_DV), lambda b, t: (b, 0, 0, 0)),
                   pl.BlockSpec((1, M_HEADS, M_DK), lambda b, t: (b, 0, 0)),
                   pl.BlockSpec((1, 1, M_HEADS), lambda b, t: (b, 0, 0))],
        out_shape=[jax.ShapeDtypeStruct((batch * seq, M_W), BF16),
                   jax.ShapeDtypeStruct((batch, M_HEADS, M_DK, M_DV), F32),
                   jax.ShapeDtypeStruct((batch, M_HEADS, M_DK), F32),
                   jax.ShapeDtypeStruct((batch, 1, M_HEADS), F32)],
        scratch_shapes=[pltpu.VMEM((M_HEADS, M_DV, M_DK), F32)],
        compiler_params=_cparams(("parallel", "arbitrary")),
        name="mlstm_prompt",
    )(proj, proj, proj, proj, proj, side, *params)


def _mlstm_step_kernel(q_ref, k_ref, v_ref, o_ref, z_ref, sm_ref, c_in, n_in, m_in, acc_ref, ifb, gn,
                       y_ref, c_out, n_out, m_out, vec_ref, sc_ref, hrow_ref, *, bb):
    del acc_ref
    pre = sm_ref[...] + ifb[...]
    li = pre[:, SM_MI:SM_MI + M_HEADS]
    lf = _log_sigmoid(pre[:, SM_MF:SM_MF + M_HEADS])
    m0 = m_in[...]
    m_inter = lf + m0
    m_t = jnp.maximum(m_inter, li)
    m_out[...] = m_t
    pad = lambda x: jnp.concatenate([x, jnp.zeros((bb, LANES - M_HEADS), F32)], axis=-1)
    sc_ref[0] = pad(jnp.exp(m_inter - m_t))
    sc_ref[1] = pad(jnp.exp(li - m_t))
    sc_ref[2] = pad(jnp.exp(-m_t))
    vec_ref[0] = q_ref[...].astype(F32) * (M_DK ** -0.5)
    vec_ref[1] = k_ref[...].astype(F32)
    vec_ref[2] = v_ref[...].astype(F32)
    masks = _outer_masks()

    def per_rows(j, _):
        bs = [j * STEP_ROWS + i for i in range(STEP_ROWS)]
        vecs = [[vec_ref[i, pl.ds(b, 1), :] for i in range(3)] for b in bs]
        scs = [[sc_ref[i, pl.ds(b, 1), :] for i in range(3)] for b in bs]
        items = [(bi, h) for bi in range(STEP_ROWS) for h in range(M_HEADS)]
        n_it = range(len(items))
        hs = lambda x, h: x[:, h * M_DK:(h + 1) * M_DK]
        q_h, k_h, v_h = ([hs(vecs[bi][i], h) for bi, h in items] for i in range(3))
        wi, ei, en = ([scs[bi][i][:, h:h + 1] for bi, h in items] for i in range(3))
        c = [c_in[bs[bi], h] for bi, h in items]
        n = [n_in[bs[bi], h:h + 1, :] for bi, h in items]
        qk = [_lane_sum(q_h[i] * k_h[i]) * ei[i] for i in n_it]
        qn = [_lane_sum(q_h[i] * n[i]) for i in n_it]
        qc = [_mm(_rows8(q_h[i]), c[i])[0:1] for i in n_it]
        kv = [_mm_tn(*_outer_operands(masks, (ei[i] * k_h[i], v_h[i]))) for i in n_it]
        hh = []
        for i, (bi, h) in enumerate(items):
            num = qk[i] * v_h[i] + wi[i] * qc[i]
            den = qk[i] + wi[i] * qn[i]
            hh.append(num / jnp.maximum(jnp.abs(den), en[i]))
            c_out[bs[bi], h] = wi[i] * c[i] + kv[i]
            n_out[bs[bi], h:h + 1, :] = wi[i] * n[i] + ei[i] * k_h[i]
        for bi in range(STEP_ROWS):
            hrow_ref[pl.ds(bs[bi], 1), :] = jnp.concatenate(
                [hh[bi * M_HEADS + h] for h in range(M_HEADS)], axis=-1)
        return 0

    lax.fori_loop(0, bb // STEP_ROWS, per_rows, 0)
    hh = hrow_ref[...]
    h_heads = [hh[:, h * M_DV:(h + 1) * M_DV] for h in range(M_HEADS)]
    y_ref[...] = _mlstm_out(h_heads, o_ref[...].astype(F32), z_ref[...].astype(F32), gn[...]).astype(y_ref.dtype)


def _mlstm_step(proj, side, c_all, n_all, m_all, acc, params, layer, bb):
    nb = proj.shape[0]
    cb = lambda c, w: pl.BlockSpec((bb, w), lambda i, c=c: (i, c // w))
    full = lambda a: pl.BlockSpec(a.shape, lambda i: (0,) * a.ndim)
    cs = _layer_block(c_all.shape[2:], bb, layer)
    ns_in = _layer_block(n_all.shape[2:], bb, layer)
    ms_in = _layer_block(m_all.shape[2:], bb, layer)
    ns = pl.BlockSpec((bb, M_HEADS, M_DK), lambda i: (i, 0, 0))
    ms = pl.BlockSpec((bb, M_HEADS), lambda i: (i, 0))
    kern = functools.partial(_mlstm_step_kernel, bb=bb)
    return pl.pallas_call(
        kern,
        grid=(nb // bb,),
        in_specs=[cb(C_MQ, M_QK), cb(C_MK, M_QK), cb(C_MV, M_W), cb(C_MO, M_W), cb(C_MZ, M_W),
                  cb(S_SMALL, LANES), cs, ns_in, ms_in, pl.BlockSpec(memory_space=pl.ANY)]
                 + [full(a) for a in params],
        out_specs=[pl.BlockSpec((bb, M_W), lambda i: (i, 0)), cs, ns, ms],
        out_shape=[jax.ShapeDtypeStruct((nb, M_W), BF16), jax.ShapeDtypeStruct(acc.shape, F32),
                   jax.ShapeDtypeStruct(n_all.shape[1:], F32), jax.ShapeDtypeStruct(m_all.shape[1:], F32)],
        input_output_aliases={9: 1},
        scratch_shapes=[pltpu.VMEM((3, bb, M_QK), F32), pltpu.VMEM((3, bb, LANES), F32),
                        pltpu.VMEM((bb, M_W), F32)],
        compiler_params=_cparams(("parallel",)),
        name="mlstm_step",
    )(proj, proj, proj, proj, proj, side, c_all, n_all, m_all, acc, *params)


def _pad_w_in(w_in):
    sizes = (R_SHIFT_W, R_W, G_QK, G_QK, G_W, G_LORA, G_W, M_QK, M_QK, M_W, M_HEADS, M_HEADS, M_W, M_W,
             3 * D_MODEL)
    offs = [0]
    for s in sizes:
        offs.append(offs[-1] + s)
    seg = lambda i: w_in[..., offs[i]:offs[i + 1]]
    (r_in, r_z, g_q, g_k, g_v, g_a, g_z, m_q, m_k, m_v, m_i, m_f, m_o, m_z, gate) = (seg(i) for i in range(15))
    zeros = lambda n: jnp.zeros(w_in.shape[:-1] + (n,), w_in.dtype)
    small = jnp.concatenate([g_a, m_i, m_f, zeros(LANES - G_LORA - 2 * M_HEADS)], axis=-1)
    w = jnp.concatenate([gate, r_z, g_v, g_z, m_q, m_k, m_v, m_o, m_z, g_q, g_k, r_in, small,
                         zeros(N_PAD - N_USED)], axis=-1)
    return w.astype(BF16)


def _shift_rows(proj_rows, side_rows):
    return jnp.concatenate([proj_rows[:, C_RIN:C_RIN + 3 * R_W].astype(F32),
                            side_rows[:, S_RLORA:S_RLORA + 2 * R_LORA]], axis=-1)


def _pick(n, prefs):
    for p in prefs:
        if n % p == 0:
            return p
    return n


def kernel(x_prompt, x_sample, state_rwkv_shift, state_rwkv, state_gla, state_mlstm_C, state_mlstm_n,
           state_mlstm_m, norm_pre, norm_post, w_in, r_mu_shift, r_w_lora_b, r_w0, r_a_lora_b, r_a0,
           r_k_k, r_k_a, r_r_k, r_gn, g_a_lora_b, g_a_bias, g_gn, m_if_bias, m_gn, w_br_rwkv, w_br_gla,
           w_br_mlstm, w_out):
    depth = w_in.shape[0]
    bp, tp, _ = x_prompt.shape
    bs, ts, _ = x_sample.shape
    assert ts == 1 and tp % CHUNK == 0
    w_pad = _pad_w_in(w_in)
    wr, wg, wm, wo = (w.astype(BF16) for w in (w_br_rwkv, w_br_gla, w_br_mlstm, w_out))

    mp = bp * tp
    tm_in = _pick(mp, (2048, 1024, 512, 256, 128, 64))
    tm_mg = _pick(mp, (512, 256, 128, 64))
    tb = _pick(tp, (256, 128, 64))
    bb = _pick(bs, (16, 8))

    xp = x_prompt.reshape(mp, D_MODEL)
    xs = x_sample.reshape(bs, D_MODEL)
    new_p = [[] for _ in range(6)]
    new_s = [[] for _ in range(3)]
    acc_r, acc_g, acc_c = (jnp.zeros(s.shape, F32) for s in (state_rwkv, state_gla, state_mlstm_C))
    for l in range(depth):
        g_pre = norm_pre[l].reshape(1, -1)
        g_post = norm_post[l].reshape(1, -1)
        rp = _rwkv_params(r_mu_shift[l], r_w_lora_b[l], r_w0[l], r_a_lora_b[l], r_a0[l], r_k_k[l],
                          r_k_a[l], r_r_k[l], r_gn[l])
        gp = _gla_params(g_a_lora_b[l], g_a_bias[l], g_gn[l])
        mparams = _mlstm_params(m_if_bias[l], m_gn[l])

        proj, side = _inproj(xp, g_pre, w_pad, l, tm_in)
        y_r, s_r = _rwkv_prompt(proj, side, rp, bp, tp, tb)
        y_g, s_g = _gla_prompt(proj, side, gp, bp, tp, tb)
        y_m, c, n, m = _mlstm_prompt(proj, side, mparams, bp, tp, tb)
        xp = _merge(proj, y_r, y_g, y_m, xp, wr, wg, wm, wo, g_post, l, tm_mg)
        shift = _shift_rows(proj.reshape(bp, tp, C_RLORA)[:, tp - 1], side.reshape(bp, tp, SIDE_W)[:, tp - 1])
        for lst, val in zip(new_p, (shift, s_r, s_g, c, n, m.reshape(bp, M_HEADS))):
            lst.append(val)

        proj, side = _inproj(xs, g_pre, w_pad, l, bs)
        y_r, acc_r = _rwkv_step(proj, side, state_rwkv_shift, state_rwkv, acc_r, rp, l, bb)
        y_g, acc_g = _gla_step(proj, side, state_gla, acc_g, gp, l, bb)
        y_m, acc_c, n, m = _mlstm_step(proj, side, state_mlstm_C, state_mlstm_n, state_mlstm_m, acc_c,
                                       mparams, l, bb)
        xs = _merge(proj, y_r, y_g, y_m, xs, wr, wg, wm, wo, g_post, l, bs)
        shift = _shift_rows(proj, side)
        for lst, val in zip(new_s, (shift, n, m)):
            lst.append(val)

    stk = lambda lst: jnp.stack(lst)
    return (xp.reshape(bp, tp, D_MODEL), xs.reshape(bs, ts, D_MODEL),
            *(stk(v) for v in new_p),
            stk(new_s[0]), acc_r, acc_g, acc_c, stk(new_s[1]), stk(new_s[2]))
```

```python
import functools
import math

import jax
import jax.numpy as jnp
from jax import lax
from jax.experimental import pallas as pl
from jax.experimental.pallas import tpu as pltpu

F32 = jnp.float32
BF16 = jnp.bfloat16

D_MODEL = 1024
R_HEADS, R_HD = 8, 64
R_W = R_HEADS * R_HD
R_LORA = 64
R_SHIFT_W = 3 * R_W + 2 * R_LORA
R_GN_EPS = 64e-5
G_HEADS, G_DK, G_DV = 4, 64, 128
G_QK = G_HEADS * G_DK
G_W = G_HEADS * G_DV
G_LORA = 16
G_GATE_TEMP = 16.0
M_HEADS, M_DK, M_DV = 4, 128, 128
M_QK = M_HEADS * M_DK
M_W = M_HEADS * M_DV
EPS = 1e-6

LANES = 128
SUBLANES = 8
VMEM_LIMIT = 48 * 1024 * 1024

C_GATE = 0
C_RZ = 3072
C_GV = 3584
C_GZ = 4096
C_MQ = 4608
C_MK = 5120
C_MV = 5632
C_MO = 6144
C_MZ = 6656
C_GQK = 7168
C_RIN = 7680
C_RLORA = C_RIN + 3 * R_W
C_SMALL = C_RLORA + LANES
SM_GA, SM_MI, SM_MF = 0, 16, 20
N_USED = C_SMALL + LANES
PROJ_TN = 512
N_PAD = -(-N_USED // PROJ_TN) * PROJ_TN
N_MAIN_BLOCKS = C_RLORA // PROJ_TN
_W_SEGMENTS = (("r_in", R_SHIFT_W), ("r_z", R_W), ("g_q", G_QK), ("g_k", G_QK), ("g_v", G_W), ("g_a", G_LORA),
               ("g_z", G_W), ("m_q", M_QK), ("m_k", M_QK), ("m_v", M_W), ("m_i", M_HEADS), ("m_f", M_HEADS),
               ("m_o", M_W), ("m_z", M_W), ("gate", 3 * D_MODEL))
W_OFF = {}
_off = 0
for _name, _width in _W_SEGMENTS:
    W_OFF[_name] = _off
    _off += _width
N_IN = _off
W_MAIN_ROWS = tuple(W_OFF[name] + PROJ_TN * b
                    for name, nblk in (("gate", 6), ("r_z", 1), ("g_v", 1), ("g_z", 1), ("m_q", 1), ("m_k", 1),
                                       ("m_v", 1), ("m_o", 1), ("m_z", 1), ("g_q", 1), ("r_in", 3))
                    for b in range(nblk))
assert len(W_MAIN_ROWS) == N_MAIN_BLOCKS and all(r % SUBLANES == 0 for r in W_MAIN_ROWS)
SIDE_W = 2 * LANES
S_RLORA, S_SMALL = 0, LANES

CHUNK = 64


def _cparams(sem):
    return pltpu.CompilerParams(dimension_semantics=sem, vmem_limit_bytes=VMEM_LIMIT)


def _mm(a, b):
    return jnp.dot(a.astype(BF16), b.astype(BF16), preferred_element_type=F32)


def _mm_nt(a, b):
    return lax.dot_general(a.astype(BF16), b.astype(BF16), (((1,), (1,)), ((), ())),
                           preferred_element_type=F32)


def _mm_tn(a, b):
    return lax.dot_general(a.astype(BF16), b.astype(BF16), (((0,), (0,)), ((), ())),
                           preferred_element_type=F32)


def _mm_exact_lhs(m01, x):
    hi = x.astype(BF16)
    r1 = x - hi.astype(F32)
    mid = r1.astype(BF16)
    lo = (r1 - mid.astype(F32)).astype(BF16)
    m = m01.astype(BF16)
    d = lambda p: jnp.dot(m, p, preferred_element_type=F32)
    return d(hi) + d(mid) + d(lo)


def _mm_exact_rhs(x, m01):
    hi = x.astype(BF16)
    r1 = x - hi.astype(F32)
    mid = r1.astype(BF16)
    lo = (r1 - mid.astype(F32)).astype(BF16)
    m = m01.astype(BF16)
    d = lambda p: jnp.dot(p, m, preferred_element_type=F32)
    return d(hi) + d(mid) + d(lo)


def _sigmoid(x):
    return 1.0 / (1.0 + jnp.exp(-x))


def _silu(x):
    return x * _sigmoid(x)


def _softplus(x):
    return jnp.maximum(x, 0.0) + jnp.log(1.0 + jnp.exp(-jnp.abs(x)))


def _log_sigmoid(x):
    return -_softplus(-x)


def _iota2(shape, dim):
    return lax.broadcasted_iota(jnp.int32, shape, dim)


def _lane_sum(x):
    return jnp.sum(x, axis=-1, keepdims=True)


def _rows8(x):
    return jnp.broadcast_to(x, (8, x.shape[1]))


def _outer_masks():
    rid = _iota2((8, 1), 0)
    on = lambda *rows: sum((rid == r).astype(F32) for r in rows)
    return (on(0, 2), on(1), on(0, 1), on(2)), (on(3, 5), on(4), on(3, 4), on(5))


def _outer_operands(masks, *pairs):
    a_op, b_op = 0.0, 0.0
    for (a, b), (m_ahi, m_alo, m_bhi, m_blo) in zip(pairs, masks):
        a_hi = a.astype(BF16).astype(F32)
        b_hi = b.astype(BF16).astype(F32)
        a_op = a_op + m_ahi * a_hi + m_alo * (a - a_hi)
        b_op = b_op + m_bhi * b_hi + m_blo * (b - b_hi)
    return a_op, b_op


def _block_tril(n, blk):
    r, c = _iota2((n, n), 0), _iota2((n, n), 1)
    return ((r - c).astype(jnp.uint32) <= (r % blk).astype(jnp.uint32)).astype(F32)


def _inproj_kernel(rows_ref, x_ref, g_ref, w_ref, wl_ref, wa_ref, wif_ref, o_ref, side_ref, h_ref, ws_ref):
    del rows_ref

    @pl.when(pl.program_id(1) == 0)
    def _():
        x = x_ref[...]
        y = x * lax.rsqrt(jnp.mean(x * x, axis=-1, keepdims=True) + EPS)
        h_ref[...] = (y * g_ref[...]).astype(BF16)
        ws_ref[...] = jnp.zeros_like(ws_ref)
        ws_ref[S_RLORA:S_RLORA + 2 * R_LORA, :] = wl_ref[0]
        ws_ref[S_SMALL + SM_GA:S_SMALL + SM_GA + G_LORA, :] = wa_ref[0]
        ws_ref[S_SMALL + SM_MI:S_SMALL + SM_MI + 2 * M_HEADS, :] = wif_ref[0]
        side_ref[...] = _mm_nt(h_ref[...], ws_ref[...])

    o_ref[...] = _mm_nt(h_ref[...], w_ref[0]).astype(BF16)


def _inproj(x2d, g_pre, w_t, layer, tm):
    m = x2d.shape[0]
    w_blk = lambda n: (pl.Element(1), pl.Element(n), pl.Element(D_MODEL))
    w_rows = lambda n, start: pl.BlockSpec(w_blk(n), lambda i, j, rows, start=start: (layer, start, 0))
    grid_spec = pltpu.PrefetchScalarGridSpec(
        num_scalar_prefetch=1,
        grid=(m // tm, N_MAIN_BLOCKS),
        in_specs=[pl.BlockSpec((tm, D_MODEL), lambda i, j, rows: (i, 0)),
                  pl.BlockSpec((1, D_MODEL), lambda i, j, rows: (0, 0)),
                  pl.BlockSpec(w_blk(PROJ_TN), lambda i, j, rows: (layer, rows[j] * SUBLANES, 0)),
                  w_rows(2 * R_LORA, W_OFF["r_in"] + 3 * R_W),
                  w_rows(G_LORA, W_OFF["g_a"]),
                  w_rows(2 * M_HEADS, W_OFF["m_i"])],
        out_specs=[pl.BlockSpec((tm, PROJ_TN), lambda i, j, rows: (i, j)),
                   pl.BlockSpec((tm, SIDE_W), lambda i, j, rows: (i, 0))],
        scratch_shapes=[pltpu.VMEM((tm, D_MODEL), BF16), pltpu.VMEM((SIDE_W, D_MODEL), F32)])
    return pl.pallas_call(
        _inproj_kernel,
        grid_spec=grid_spec,
        out_shape=[jax.ShapeDtypeStruct((m, C_RLORA), BF16),
                   jax.ShapeDtypeStruct((m, SIDE_W), F32)],
        compiler_params=_cparams(("parallel", "arbitrary")),
        name="inproj",
    )(jnp.asarray([r // SUBLANES for r in W_MAIN_ROWS], jnp.int32), x2d, g_pre, w_t, w_t, w_t, w_t)


def _merge_kernel(gate_ref, yr_ref, yg_ref, ym_ref, x_ref, wr_ref, wg_ref, wm_ref, wo_ref, gp_ref, o_ref):
    d = D_MODEL
    gate = lambda b: _sigmoid(gate_ref[:, b * d:(b + 1) * d].astype(F32))
    merged = (gate(0) * _mm(yr_ref[...], wr_ref[...]) + gate(1) * _mm(yg_ref[...], wg_ref[...])
              + gate(2) * _mm(ym_ref[...], wm_ref[...]))
    o = _mm(merged, wo_ref[...])
    y = o * lax.rsqrt(jnp.mean(o * o, axis=-1, keepdims=True) + EPS)
    o_ref[...] = x_ref[...] + y * gp_ref[...]


def _merge(proj, y_r, y_g, y_m, x2d, w_r, w_g, w_m, w_o, g_post, layer, tm):
    m = x2d.shape[0]
    row = lambda w: pl.BlockSpec((tm, w), lambda i: (i, 0))
    full = lambda a: pl.BlockSpec(a.shape, lambda i: (0, 0))
    wl = lambda a: pl.BlockSpec((None,) + a.shape[1:], lambda i: (layer, 0, 0))
    return pl.pallas_call(
        _merge_kernel,
        grid=(m // tm,),
        in_specs=[row(3 * D_MODEL), row(R_W), row(G_W), row(M_W), row(D_MODEL),
                  wl(w_r), wl(w_g), wl(w_m), wl(w_o), full(g_post)],
        out_specs=row(D_MODEL),
        out_shape=jax.ShapeDtypeStruct((m, D_MODEL), F32),
        compiler_params=_cparams(("parallel",)),
        name="merge",
    )(proj, y_r, y_g, y_m, x2d, w_r, w_g, w_m, w_o, g_post)


def _rwkv_front(r_in, k_in, v_in, l_in, prev_r, prev_k, prev_v, prev_l, p):
    (mu_r, mu_k, mu_v, mu_l, wb, w0, ab, a0, k_k, k_a) = p
    r = r_in + mu_r * (prev_r - r_in)
    k = k_in + mu_k * (prev_k - k_in)
    v = v_in + mu_v * (prev_v - v_in)
    lo = l_in + mu_l * (prev_l - l_in)
    log_w = -_softplus(-(w0 + _mm(jnp.tanh(lo), wb))) - 0.5
    lw = -jnp.exp(log_w)
    a = _sigmoid(a0 + _mm(lo, ab))
    kk = k * k_k
    k2 = k * (1.0 + (a - 1.0) * k_a)
    return r, k2, v, lw, a, kk


def _rwkv_out(y_heads, r, k2, v, z, r_k, gn):
    outs = []
    for h in range(R_HEADS):
        sl = slice(h * R_HD, (h + 1) * R_HD)
        y = y_heads[h]
        y = y - jnp.mean(y, axis=-1, keepdims=True)
        y = y * lax.rsqrt(jnp.mean(y * y, axis=-1, keepdims=True) + R_GN_EPS)
        bonus = _lane_sum(r[:, sl] * k2[:, sl] * r_k[:, sl]) * v[:, sl]
        outs.append(y * gn[:, sl] + bonus)
    return jnp.concatenate(outs, axis=-1) * _silu(z)


def _rwkv_prompt_kernel(z_ref, r_ref, k_ref, v_ref, l_ref,
                        mu_r, mu_k, mu_v, mu_l, wb, w0, ab, a0, k_k, k_a, r_k, gn,
                        y_ref, s_ref, carry_ref, sp_ref, *, n_chunks):
    L = CHUNK

    @pl.when(pl.program_id(1) == 0)
    def _():
        sp_ref[...] = jnp.zeros_like(sp_ref)
        carry_ref[...] = jnp.zeros_like(carry_ref)

    tb = n_chunks * L
    params = tuple(x[...] for x in (mu_r, mu_k, mu_v, mu_l, wb, w0, ab, a0, k_k, k_a))
    assert L == R_HD
    m_a = (_iota2((1, LANES), 1) < R_HD).astype(F32)
    m_b = 1.0 - m_a
    m_a16, m_b16 = m_a.astype(BF16), m_b.astype(BF16)
    trow = _iota2((L, LANES), 0)
    tcol = _iota2((L, LANES), 1) % L
    eye = (tcol == trow).astype(F32)
    blk_mask = (trow // 16) == (tcol // 16)
    r2 = _iota2((2 * L, LANES), 0)
    g_mask = (_iota2((2 * L, LANES), 1) % L) < (r2 % L) + (r2 >= L).astype(jnp.int32)
    same_head = (_iota2((LANES, LANES), 0) // R_HD) == (_iota2((LANES, LANES), 1) // R_HD)
    first_row = _iota2((tb, 1), 0) == 0

    def seg_sum(x):
        return _lane_sum(x * m_a) * m_a + _lane_sum(x * m_b) * m_b

    def bd(y):
        yb = y.astype(BF16)
        return jnp.concatenate([yb * m_a16, yb * m_b16], axis=0)

    def shifted(x, carry):
        return jnp.where(first_row, carry, pltpu.roll(x, 1, 0))

    r_in, k_in, v_in = (x[...].astype(F32) for x in (r_ref, k_ref, v_ref))
    l_in = l_ref[...]
    prev_r = shifted(r_in, carry_ref[0:1, 0:R_W])
    prev_k = shifted(k_in, carry_ref[0:1, R_W:2 * R_W])
    prev_v = shifted(v_in, carry_ref[0:1, 2 * R_W:3 * R_W])
    prev_l = shifted(l_in, carry_ref[0:1, 3 * R_W:R_SHIFT_W])
    carry_ref[0:1, 0:R_W] = r_in[tb - 1:tb, :]
    carry_ref[0:1, R_W:2 * R_W] = k_in[tb - 1:tb, :]
    carry_ref[0:1, 2 * R_W:3 * R_W] = v_in[tb - 1:tb, :]
    carry_ref[0:1, 3 * R_W:R_SHIFT_W] = l_in[tb - 1:tb, :]
    r, k2, v, lw, a, kk = _rwkv_front(r_in, k_in, v_in, l_in, prev_r, prev_k, prev_v, prev_l, params)

    cum = _mm_exact_lhs(_block_tril(tb, L), lw)
    p_inc = jnp.exp(cum)
    p_exc = jnp.exp(cum - lw)
    p_inv = jnp.exp(-cum)
    r_t = r * p_inc
    k_h = k2 * p_inv

    n_pairs = R_HEADS // 2
    CP = [(c, p) for c in range(n_chunks) for p in range(n_pairs)]
    N = range(len(CP))
    kap_t, beta_h = [], []
    for p in range(n_pairs):
        sl = slice(p * LANES, (p + 1) * LANES)
        kk_p = kk[:, sl]
        kap = kk_p * (1.0 / jnp.maximum(jnp.sqrt(seg_sum(kk_p * kk_p)), 1e-12))
        beta_h.append(kap * a[:, sl] * p_inv[:, sl])
        kap_t.append(kap * p_exc[:, sl])
    cut = lambda x, c, p: x[c * L:(c + 1) * L, p * LANES:(p + 1) * LANES]
    kap_t = [kap_t[p][c * L:(c + 1) * L] for c, p in CP]
    beta_h = [beta_h[p][c * L:(c + 1) * L] for c, p in CP]
    rt_c = [cut(r_t, c, p) for c, p in CP]
    kh_c = [cut(k_h, c, p) for c, p in CP]
    v_c = [cut(v, c, p) for c, p in CP]
    p_last = [p_inc[(c + 1) * L - 1:(c + 1) * L, p * LANES:(p + 1) * LANES] for c, p in CP]
    bhl = [beta_h[i] * p_last[i] for i in N]
    khl = [kh_c[i] * p_last[i] for i in N]
    lhs2 = [jnp.concatenate([kap_t[i], rt_c[i]], axis=0) for i in N]
    bd_v = [bd(v_c[i]) for i in N]
    gb = [jnp.where(g_mask, _mm_nt(lhs2[i], bd(beta_h[i])), 0.0) for i in N]
    gk = [jnp.where(g_mask, _mm_nt(lhs2[i], bd(kh_c[i])), 0.0) for i in N]
    a_m = [gb[i][0:L] for i in N]
    d = [jnp.where(blk_mask, a_m[i], 0.0) for i in N]
    akv = [_mm(gk[i][0:L], bd_v[i]) for i in N]
    d2 = [_mm(d[i], bd(d[i])) for i in N]
    bd_d2 = [bd(d2[i]) for i in N]
    d4 = [_mm(d2[i], bd_d2[i]) for i in N]
    dd2 = [_mm(d[i], bd_d2[i]) for i in N]
    d8 = [_mm(d4[i], bd(d4[i])) for i in N]
    x = [eye - d[i] + d2[i] - dd2[i] for i in N]
    x = [x[i] + _mm(x[i], bd(d4[i])) for i in N]
    td = [x[i] + _mm(x[i], bd(d8[i])) for i in N]
    m = [_mm(td[i], bd(a_m[i] - d[i])) for i in N]
    m2 = [_mm(m[i], bd(m[i])) for i in N]
    q = [eye - m[i] + m2[i] - _mm(m[i], bd(m2[i])) for i in N]
    t_inv = [_mm(q[i], bd(td[i])) for i in N]
    kbar = [_mm(t_inv[i], bd(kap_t[i])) for i in N]
    u0 = [-_mm(t_inv[i], bd(akv[i])) for i in N]
    n_bd = [jnp.where(same_head, _mm_tn(kbar[i], bhl[i]), 0.0) for i in N]
    cz = [_mm_tn(jnp.concatenate([u0[i], v_c[i]], axis=0), jnp.concatenate([bhl[i], khl[i]], axis=0))
          for i in N]
    c_c = [cz[i][0:R_HD] * m_a + cz[i][R_HD:2 * R_HD] * m_b for i in N]
    qt = [rt_c[i] - _mm(gb[i][L:2 * L], bd(kbar[i])) for i in N]
    y0 = [_mm(gb[i][L:2 * L], bd(u0[i])) + _mm(gk[i][L:2 * L], bd_v[i]) for i in N]
    s = [sp_ref[p] for p in range(n_pairs)]
    y_ch = []
    for c in range(n_chunks):
        ids = [c * n_pairs + p for p in range(n_pairs)]
        y_ch.append([_mm_nt(qt[i], bd(s[p])) + y0[i] for p, i in enumerate(ids)])
        s = [s[p] * p_last[i] - _mm(s[p], n_bd[i]) + c_c[i] for p, i in enumerate(ids)]
    for p in range(n_pairs):
        sp_ref[p] = s[p]
    y_pairs = [jnp.concatenate([y_ch[c][p] for c in range(n_chunks)], axis=0) for p in range(n_pairs)]
    outs = []
    for p in range(n_pairs):
        sl = slice(p * LANES, (p + 1) * LANES)
        y = y_pairs[p]
        y = y - seg_sum(y) * (1.0 / R_HD)
        y = y * lax.rsqrt(seg_sum(y * y) * (1.0 / R_HD) + R_GN_EPS)
        bonus = seg_sum(r[:, sl] * k2[:, sl] * r_k[:, sl]) * v[:, sl]
        outs.append(y * gn[:, sl] + bonus)
    y_ref[...] = (jnp.concatenate(outs, axis=-1) * _silu(z_ref[...].astype(F32))).astype(y_ref.dtype)

    @pl.when(pl.program_id(1) == pl.num_programs(1) - 1)
    def _():
        for p in range(n_pairs):
            sp = sp_ref[p]
            s_ref[0, 2 * p] = sp[:, 0:R_HD]
            s_ref[0, 2 * p + 1] = sp[:, R_HD:2 * R_HD]


def _rwkv_params(mu, wb, w0, ab, a0, k_k, k_a, r_k, gn):
    row = lambda x: x.reshape(1, -1)
    zeros = jnp.zeros((R_LORA, R_W), F32)
    wb_p = jnp.concatenate([wb, zeros], axis=0).astype(BF16)
    ab_p = jnp.concatenate([zeros, ab], axis=0).astype(BF16)
    return (row(mu[0:R_W]), row(mu[R_W:2 * R_W]), row(mu[2 * R_W:3 * R_W]), row(mu[3 * R_W:]),
            wb_p, row(w0), ab_p, row(a0), row(k_k), row(k_a), row(r_k), row(gn))


def _rwkv_prompt(proj, side, params, batch, seq, tb):
    nt = seq // tb
    cb = lambda c, w: pl.BlockSpec((tb, w), lambda b, t, c=c: (b * nt + t, c // w))
    full = lambda a: pl.BlockSpec(a.shape, lambda b, t: (0,) * a.ndim)
    kern = functools.partial(_rwkv_prompt_kernel, n_chunks=tb // CHUNK)
    return pl.pallas_call(
        kern,
        grid=(batch, nt),
        in_specs=[cb(C_RZ, R_W), cb(C_RIN, R_W), cb(C_RIN + R_W, R_W), cb(C_RIN + 2 * R_W, R_W),
                  cb(S_RLORA, LANES)] + [full(a) for a in params],
        out_specs=[pl.BlockSpec((tb, R_W), lambda b, t: (b * nt + t, 0)),
                   pl.BlockSpec((1, R_HEADS, R_HD, R_HD), lambda b, t: (b, 0, 0, 0))],
        out_shape=[jax.ShapeDtypeStruct((batch * seq, R_W), BF16),
                   jax.ShapeDtypeStruct((batch, R_HEADS, R_HD, R_HD), F32)],
        scratch_shapes=[pltpu.VMEM((8, R_SHIFT_W), F32),
                        pltpu.VMEM((R_HEADS // 2, R_HD, LANES), F32)],
        compiler_params=_cparams(("parallel", "arbitrary")),
        name="rwkv_prompt",
    )(proj, proj, proj, proj, side, *params)


STEP_ROWS = 4
GLA_STEP_ROWS = 2


def _rwkv_step_kernel(z_ref, r_ref, k_ref, v_ref, l_ref, pr_ref, pk_ref, pv_ref, pl_ref, s_in, acc_ref,
                      mu_r, mu_k, mu_v, mu_l, wb, w0, ab, a0, k_k, k_a, r_k, gn,
                      y_ref, s_out, vec_ref, yrow_ref, *, bb):
    del acc_ref
    params = tuple(x[...] for x in (mu_r, mu_k, mu_v, mu_l, wb, w0, ab, a0, k_k, k_a))
    r, k2, v, lw, a, kk = _rwkv_front(r_ref[...].astype(F32), k_ref[...].astype(F32), v_ref[...].astype(F32),
                                      l_ref[...], pr_ref[...], pk_ref[...], pv_ref[...], pl_ref[...], params)
    w = jnp.exp(lw)
    kaps, betas = [], []
    for h in range(R_HEADS):
        sl = slice(h * R_HD, (h + 1) * R_HD)
        kk_h = kk[:, sl]
        kap = kk_h / jnp.maximum(jnp.sqrt(_lane_sum(kk_h * kk_h)), 1e-12)
        kaps.append(kap)
        betas.append(kap * a[:, sl])
    vec_ref[0] = jnp.concatenate(kaps, axis=-1)
    vec_ref[1] = jnp.concatenate(betas, axis=-1)
    vec_ref[2] = w
    vec_ref[3] = k2
    vec_ref[4] = v
    vec_ref[5] = r
    masks = _outer_masks()

    def per_rows(j, _):
        bs = [j * STEP_ROWS + i for i in range(STEP_ROWS)]
        vecs = [[vec_ref[i, pl.ds(b, 1), :] for i in range(6)] for b in bs]
        items = [(bi, h) for bi in range(STEP_ROWS) for h in range(R_HEADS)]
        hs = lambda x, h: x[:, h * R_HD:(h + 1) * R_HD]
        kap, beta, w_, k_, v_, r_ = ([hs(vecs[bi][i], h) for bi, h in items] for i in range(6))
        n = range(len(items))
        s = [s_in[bs[bi], h] for bi, h in items]
        sa = [-_mm_nt(_rows8(kap[i]), s[i])[0:1] for i in n]
        ops = [_outer_operands(masks, (sa[i], beta[i]), (v_[i], k_[i])) for i in n]
        s_new = [s[i] * w_[i] + _mm_tn(*ops[i]) for i in n]
        for i, (bi, h) in enumerate(items):
            s_out[bs[bi], h] = s_new[i]
        y_row = [_mm_nt(_rows8(r_[i]), s_new[i])[0:1] for i in n]
        for bi in range(STEP_ROWS):
            yrow_ref[pl.ds(bs[bi], 1), :] = jnp.concatenate(
                [y_row[bi * R_HEADS + h] for h in range(R_HEADS)], axis=-1)
        return 0

    lax.fori_loop(0, bb // STEP_ROWS, per_rows, 0)
    y = yrow_ref[...]
    y_heads = [y[:, h * R_HD:(h + 1) * R_HD] for h in range(R_HEADS)]
    y_ref[...] = _rwkv_out(y_heads, r, k2, v, z_ref[...].astype(F32), r_k[...], gn[...]).astype(y_ref.dtype)


def _layer_block(shape_tail, bb, layer):
    zeros = (0,) * len(shape_tail)
    return pl.BlockSpec((None, bb) + tuple(shape_tail), lambda i: (layer, i) + zeros)


def _rwkv_step(proj, side, shift_all, s_all, acc, params, layer, bb):
    nb = proj.shape[0]
    cb = lambda c, w: pl.BlockSpec((bb, w), lambda i, c=c: (i, c // w))
    sh = lambda c, w: pl.BlockSpec((None, bb, w), lambda i, c=c: (layer, i, c // w))
    full = lambda a: pl.BlockSpec(a.shape, lambda i: (0,) * a.ndim)
    st = _layer_block(s_all.shape[2:], bb, layer)
    kern = functools.partial(_rwkv_step_kernel, bb=bb)
    return pl.pallas_call(
        kern,
        grid=(nb // bb,),
        in_specs=[cb(C_RZ, R_W), cb(C_RIN, R_W), cb(C_RIN + R_W, R_W), cb(C_RIN + 2 * R_W, R_W),
                  cb(S_RLORA, LANES),
                  sh(0, R_W), sh(R_W, R_W), sh(2 * R_W, R_W), sh(3 * R_W, LANES), st,
                  pl.BlockSpec(memory_space=pl.ANY)]
                 + [full(a) for a in params],
        out_specs=[pl.BlockSpec((bb, R_W), lambda i: (i, 0)), st],
        out_shape=[jax.ShapeDtypeStruct((nb, R_W), BF16),
                   jax.ShapeDtypeStruct(acc.shape, F32)],
        input_output_aliases={10: 1},
        scratch_shapes=[pltpu.VMEM((6, bb, R_W), F32), pltpu.VMEM((bb, R_W), F32)],
        compiler_params=_cparams(("parallel",)),
        name="rwkv_step",
    )(proj, proj, proj, proj, side, shift_all, shift_all, shift_all, shift_all, s_all, acc, *params)


def _gla_log_gate(small, ab_p, a_bias):
    return _log_sigmoid(_mm(small, ab_p) + a_bias) / G_GATE_TEMP


def _gla_out(o_heads, z, gn):
    outs = []
    for h in range(G_HEADS):
        o = o_heads[h]
        outs.append(o * lax.rsqrt(jnp.mean(o * o, axis=-1, keepdims=True) + EPS))
    return jnp.concatenate(outs, axis=-1) * gn * _silu(z)


def _gla_prompt_kernel(qk_ref, v_ref, z_ref, sm_ref, ab, a_bias, gn, y_ref, s_ref, st_ref, *, n_chunks):
    L = CHUNK
    last = pl.num_programs(1) - 1

    @pl.when(pl.program_id(1) == 0)
    def _():
        st_ref[...] = jnp.zeros_like(st_ref)

    tril_incl = _iota2((L, L), 1) <= _iota2((L, L), 0)
    scale = G_DK ** -0.5
    C, H = range(n_chunks), range(G_HEADS)
    rws = [slice(c * L, (c + 1) * L) for c in C]
    sks = [slice(h * G_DK, (h + 1) * G_DK) for h in H]

    qk = qk_ref[...].astype(F32)
    q, k = qk[:, 0:G_QK] * scale, qk[:, G_QK:2 * G_QK]
    v = v_ref[...].astype(F32)
    la = _gla_log_gate(sm_ref[...], ab[...], a_bias[...])
    cum = _mm_exact_lhs(_block_tril(n_chunks * L, L), la)
    q0, qe, ke, kl, e_last = [], [], [], [], []
    for c in C:
        cum_c = cum[rws[c]]
        ref_row = cum_c[L // 2:L // 2 + 1, :]
        last_row = cum_c[L - 1:L, :]
        q0.append(q[rws[c]] * jnp.exp(cum_c))
        qe.append(q[rws[c]] * jnp.exp(cum_c - ref_row))
        ke.append(k[rws[c]] * jnp.exp(ref_row - cum_c))
        kl.append(k[rws[c]] * jnp.exp(last_row - cum_c))
        e_last.append(jnp.exp(last_row))
    v_h = [[v[rws[c], h * G_DV:(h + 1) * G_DV] for h in H] for c in C]
    att = [[jnp.where(tril_incl, _mm_nt(qe[c][:, sks[h]], ke[c][:, sks[h]]), 0.0) for h in H] for c in C]
    upd = [[_mm_tn(v_h[c][h], kl[c][:, sks[h]]) for h in H] for c in C]
    av = [[_mm(att[c][h], v_h[c][h]) for h in H] for c in C]
    st = [st_ref[h] for h in H]
    o_ch = []
    for c in C:
        o_ch.append([av[c][h] + _mm_nt(q0[c][:, sks[h]], st[h]) for h in H])
        st = [st[h] * e_last[c][:, sks[h]] + upd[c][h] for h in H]
    for h in H:
        st_ref[h] = st[h]
    o_heads = [jnp.concatenate([o_ch[c][h] for c in C], axis=0) for h in H]
    y_ref[...] = _gla_out(o_heads, z_ref[...].astype(F32), gn[...]).astype(y_ref.dtype)

    @pl.when(pl.program_id(1) == last)
    def _():
        for h in range(G_HEADS):
            s_ref[0, h] = st_ref[h].T


def _gla_params(ab, a_bias, gn):
    ab_p = jnp.zeros((LANES, G_QK), F32).at[SM_GA:SM_GA + G_LORA].set(ab).astype(BF16)
    return ab_p, a_bias.reshape(1, -1), gn.reshape(1, -1)


def _gla_prompt(proj, side, params, batch, seq, tb):
    nt = seq // tb
    cb = lambda c, w: pl.BlockSpec((tb, w), lambda b, t, c=c: (b * nt + t, c // w))
    full = lambda a: pl.BlockSpec(a.shape, lambda b, t: (0,) * a.ndim)
    kern = functools.partial(_gla_prompt_kernel, n_chunks=tb // CHUNK)
    return pl.pallas_call(
        kern,
        grid=(batch, nt),
        in_specs=[cb(C_GQK, 2 * G_QK), cb(C_GV, G_W), cb(C_GZ, G_W), cb(S_SMALL, LANES)]
                 + [full(a) for a in params],
        out_specs=[pl.BlockSpec((tb, G_W), lambda b, t: (b * nt + t, 0)),
                   pl.BlockSpec((1, G_HEADS, G_DK, G_DV), lambda b, t: (b, 0, 0, 0))],
        out_shape=[jax.ShapeDtypeStruct((batch * seq, G_W), BF16),
                   jax.ShapeDtypeStruct((batch, G_HEADS, G_DK, G_DV), F32)],
        scratch_shapes=[pltpu.VMEM((G_HEADS, G_DV, G_DK), F32)],
        compiler_params=_cparams(("parallel", "arbitrary")),
        name="gla_prompt",
    )(proj, proj, proj, side, *params)


def _gla_step_kernel(qk_ref, v_ref, z_ref, sm_ref, s_in, acc_ref, ab, a_bias, gn, y_ref, s_out,
                     vec_ref, vrow_ref, orow_ref, *, bb):
    del acc_ref
    qk = qk_ref[...].astype(F32)
    q = qk[:, 0:G_QK] * (G_DK ** -0.5)
    k = qk[:, G_QK:2 * G_QK]
    g = _gla_log_gate(sm_ref[...], ab[...], a_bias[...])
    vec_ref[0] = q
    vec_ref[1] = k
    vec_ref[2] = jnp.exp(g)
    vrow_ref[...] = v_ref[...].astype(F32)
    eye = (_iota2((G_DK, G_DK), 0) == _iota2((G_DK, G_DK), 1)).astype(F32)
    masks = _outer_masks()

    def per_rows(j, _):
        bs = [j * GLA_STEP_ROWS + i for i in range(GLA_STEP_ROWS)]
        vecs = [[vec_ref[i, pl.ds(b, 1), :] for i in range(3)] for b in bs]
        v_b = [vrow_ref[pl.ds(b, 1), :] for b in bs]
        items = [(bi, h) for bi in range(GLA_STEP_ROWS) for h in range(G_HEADS)]
        n = range(len(items))
        hk = lambda x, h: x[:, h * G_DK:(h + 1) * G_DK]
        q_, k_, e_ = ([hk(vecs[bi][i], h) for bi, h in items] for i in range(3))
        v_h = [v_b[bi][:, h * G_DV:(h + 1) * G_DV] for bi, h in items]
        s = [s_in[bs[bi], h] for bi, h in items]
        e_col = [_lane_sum(eye * e_[i]) for i in n]
        qk_dot = [_lane_sum(q_[i] * k_[i]) for i in n]
        o_s = [_mm(_rows8(q_[i] * e_[i]), s[i])[0:1] for i in n]
        kv = [_mm_tn(*_outer_operands(masks, (k_[i], v_h[i]))) for i in n]
        o = [qk_dot[i] * v_h[i] + o_s[i] for i in n]
        for i, (bi, h) in enumerate(items):
            s_out[bs[bi], h] = s[i] * e_col[i] + kv[i]
        for bi in range(GLA_STEP_ROWS):
            orow_ref[pl.ds(bs[bi], 1), :] = jnp.concatenate(
                [o[bi * G_HEADS + h] for h in range(G_HEADS)], axis=-1)
        return 0

    lax.fori_loop(0, bb // GLA_STEP_ROWS, per_rows, 0)
    o = orow_ref[...]
    o_heads = [o[:, h * G_DV:(h + 1) * G_DV] for h in range(G_HEADS)]
    y_ref[...] = _gla_out(o_heads, z_ref[...].astype(F32), gn[...]).astype(y_ref.dtype)


def _gla_step(proj, side, s_all, acc, params, layer, bb):
    nb = proj.shape[0]
    cb = lambda c, w: pl.BlockSpec((bb, w), lambda i, c=c: (i, c // w))
    full = lambda a: pl.BlockSpec(a.shape, lambda i: (0,) * a.ndim)
    st = _layer_block(s_all.shape[2:], bb, layer)
    kern = functools.partial(_gla_step_kernel, bb=bb)
    return pl.pallas_call(
        kern,
        grid=(nb // bb,),
        in_specs=[cb(C_GQK, 2 * G_QK), cb(C_GV, G_W), cb(C_GZ, G_W), cb(S_SMALL, LANES), st,
                  pl.BlockSpec(memory_space=pl.ANY)]
                 + [full(a) for a in params],
        out_specs=[pl.BlockSpec((bb, G_W), lambda i: (i, 0)), st],
        out_shape=[jax.ShapeDtypeStruct((nb, G_W), BF16), jax.ShapeDtypeStruct(acc.shape, F32)],
        input_output_aliases={5: 1},
        scratch_shapes=[pltpu.VMEM((3, bb, G_QK), F32), pltpu.VMEM((bb, G_W), F32),
                        pltpu.VMEM((bb, G_W), F32)],
        compiler_params=_cparams(("parallel",)),
        name="gla_step",
    )(proj, proj, proj, side, s_all, acc, *params)


def _mlstm_out(h_heads, o_pre, z, gn):
    outs = []
    for h in range(M_HEADS):
        y = h_heads[h]
        y = y - jnp.mean(y, axis=-1, keepdims=True)
        outs.append(y * lax.rsqrt(jnp.mean(y * y, axis=-1, keepdims=True) + EPS))
    return jnp.concatenate(outs, axis=-1) * gn * _sigmoid(o_pre) * _silu(z)


def _mlstm_prompt_kernel(q_ref, k_ref, v_ref, o_ref, z_ref, sm_ref, ifb, gn,
                         y_ref, c_ref, n_ref, m_ref, ct_ref, *, n_chunks):
    L = CHUNK

    @pl.when(pl.program_id(1) == 0)
    def _():
        ct_ref[...] = jnp.zeros_like(ct_ref)
        n_ref[...] = jnp.zeros_like(n_ref)
        m_ref[...] = jnp.zeros_like(m_ref)

    causal = _iota2((L, L), 0) <= _iota2((L, L), 1)
    eye = (_iota2((L, L), 0) == _iota2((L, L), 1)).astype(F32)
    scale = M_DK ** -0.5
    neg_inf = -jnp.inf
    C, H = range(n_chunks), range(M_HEADS)
    rws = [slice(c * L, (c + 1) * L) for c in C]
    sls = [slice(h * M_DK, (h + 1) * M_DK) for h in H]

    pre = sm_ref[...] + ifb[...]
    lf = _log_sigmoid(pre)
    b_all = _mm_exact_lhs(_block_tril(n_chunks * L, L), lf)
    lib_all = pre - pltpu.roll(b_all, LANES - (SM_MF - SM_MI), 1)
    sel = (_iota2((LANES, M_HEADS * LANES), 0)
           == SM_MI + _iota2((LANES, M_HEADS * LANES), 1) // LANES).astype(F32)
    lib_bc = _mm_exact_rhs(lib_all, sel)
    comb = jnp.where(_iota2((1, LANES), 1) < SM_MF, lib_all, b_all)
    q, k, v = q_ref[...].astype(F32) * scale, k_ref[...].astype(F32), v_ref[...].astype(F32)
    qs = [[q[rws[c], sls[h]] for h in H] for c in C]
    k_h = [[k[rws[c], sls[h]] for h in H] for c in C]
    v_h = [[v[rws[c], sls[h]] for h in H] for c in C]
    qk_raw = [[_mm_nt(k_h[c][h], qs[c][h]) for h in H] for c in C]
    d_log, d_max, b_row, lib_row = [], [], [], []
    for c in C:
        comb_t = comb[rws[c]].T
        b_row.append([comb_t[SM_MF + h:SM_MF + h + 1, :] for h in H])
        lib_row.append([comb_t[SM_MI + h:SM_MI + h + 1, :] for h in H])
        d_log.append([jnp.where(causal, lib_bc[rws[c], h * LANES:h * LANES + L] + b_row[c][h], neg_inf)
                      for h in H])
        d_max.append([jnp.max(d_log[c][h], axis=0, keepdims=True) for h in H])
    m_prev = [m_ref[0, :, h:h + 1] for h in H]
    m_t, w_inter, carry, w_state, qk = [], [], [], [], []
    for c in C:
        m_inter = [b_row[c][h] + m_prev[h] for h in H]
        m_t.append([jnp.maximum(m_inter[h], d_max[c][h]) for h in H])
        w_inter.append([jnp.exp(m_inter[h] - m_t[c][h]) for h in H])
        m_new = [m_t[c][h][:, L - 1:L] for h in H]
        b_last = [b_row[c][h][:, L - 1:L] for h in H]
        carry.append([jnp.exp(b_last[h] + m_prev[h] - m_new[h]) for h in H])
        w_state.append([jnp.exp(b_last[h] + lib_row[c][h] - m_new[h]) for h in H])
        qk.append([qk_raw[c][h] * jnp.exp(d_log[c][h] - m_t[c][h]) for h in H])
        m_prev = m_new
    for h in H:
        m_ref[0, :, h:h + 1] = m_prev[h]
    wk = [[_mm(eye * w_state[c][h], k_h[c][h]) for h in H] for c in C]
    n_upd = [[_mm(_rows8(w_state[c][h]), k_h[c][h])[0:1] for h in H] for c in C]
    upd = [[_mm_tn(v_h[c][h], wk[c][h]) for h in H] for c in C]
    qkv = [[_mm_tn(v_h[c][h], qk[c][h]) for h in H] for c in C]
    ct = [ct_ref[h] for h in H]
    ns = [n_ref[0, h:h + 1, :] for h in H]
    y_ch = []
    for c in C:
        qc = [_mm_nt(ct[h], qs[c][h]) for h in H]
        qn = [_mm_nt(_rows8(ns[h]), qs[c][h])[0:1] for h in H]
        yy = []
        for h in H:
            num = qkv[c][h] + w_inter[c][h] * qc[h]
            den = jnp.sum(qk[c][h], axis=0, keepdims=True) + w_inter[c][h] * qn[h]
            ht = num / jnp.maximum(jnp.abs(den), jnp.exp(-m_t[c][h]))
            ht = ht - jnp.mean(ht, axis=0, keepdims=True)
            yy.append((ht * lax.rsqrt(jnp.mean(ht * ht, axis=0, keepdims=True) + EPS)).T)
        y_ch.append(yy)
        ct = [carry[c][h] * ct[h] + upd[c][h] for h in H]
        ns = [carry[c][h] * ns[h] + n_upd[c][h] for h in H]
    for h in H:
        ct_ref[h] = ct[h]
        n_ref[0, h:h + 1, :] = ns[h]
    y = jnp.concatenate([jnp.concatenate([y_ch[c][h] for c in C], axis=0) for h in H], axis=-1)
    y_ref[...] = (y * gn[...] * _sigmoid(o_ref[...].astype(F32)) * _silu(z_ref[...].astype(F32))
                  ).astype(y_ref.dtype)

    @pl.when(pl.program_id(1) == pl.num_programs(1) - 1)
    def _():
        for h in H:
            c_ref[0, h] = ct_ref[h].T


def _mlstm_params(ifb, gn):
    ifb_p = jnp.zeros((1, LANES), F32).at[0, SM_MI:SM_MI + 2 * M_HEADS].set(ifb)
    return ifb_p, gn.reshape(1, -1)


def _mlstm_prompt(proj, side, params, batch, seq, tb):
    nt = seq // tb
    cb = lambda c, w: pl.BlockSpec((tb, w), lambda b, t, c=c: (b * nt + t, c // w))
    full = lambda a: pl.BlockSpec(a.shape, lambda b, t: (0,) * a.ndim)
    kern = functools.partial(_mlstm_prompt_kernel, n_chunks=tb // CHUNK)
    return pl.pallas_call(
        kern,
        grid=(batch, nt),
        in_specs=[cb(C_MQ, M_QK), cb(C_MK, M_QK), cb(C_MV, M_W), cb(C_MO, M_W), cb(C_MZ, M_W),
                  cb(S_SMALL, LANES)] + [full(a) for a in params],
        out_specs=[pl.BlockSpec((tb, M_W), lambda b, t: (b * nt + t, 0)),
                   pl.BlockSpec((1, M_HEADS, M_DK, M_DV), lambda b, t: (b, 0, 0, 0)),
                   pl.BlockSpec((1, M_HEADS, M_DK), lambda b, t: (b, 0, 0)),
                   pl.BlockSpec((1, 1, M_HEADS), lambda b, t: (b, 0, 0))],
        out_shape=[jax.ShapeDtypeStruct((batch * seq, M_W), BF16),
                   jax.ShapeDtypeStruct((batch, M_HEADS, M_DK, M_DV), F32),
                   jax.ShapeDtypeStruct((batch, M_HEADS, M_DK), F32),
                   jax.ShapeDtypeStruct((batch, 1, M_HEADS), F32)],
        scratch_shapes=[pltpu.VMEM((M_HEADS, M_DV, M_DK), F32)],
        compiler_params=_cparams(("parallel", "arbitrary")),
        name="mlstm_prompt",
    )(proj, proj, proj, proj, proj, side, *params)


def _mlstm_step_kernel(q_ref, k_ref, v_ref, o_ref, z_ref, sm_ref, c_in, n_in, m_in, acc_ref, ifb, gn,
                       y_ref, c_out, n_out, m_out, vec_ref, sc_ref, hrow_ref, *, bb):
    del acc_ref
    pre = sm_ref[...] + ifb[...]
    li = pre[:, SM_MI:SM_MI + M_HEADS]
    lf = _log_sigmoid(pre[:, SM_MF:SM_MF + M_HEADS])
    m0 = m_in[...]
    m_inter = lf + m0
    m_t = jnp.maximum(m_inter, li)
    m_out[...] = m_t
    pad = lambda x: jnp.concatenate([x, jnp.zeros((bb, LANES - M_HEADS), F32)], axis=-1)
    sc_ref[0] = pad(jnp.exp(m_inter - m_t))
    sc_ref[1] = pad(jnp.exp(li - m_t))
    sc_ref[2] = pad(jnp.exp(-m_t))
    vec_ref[0] = q_ref[...].astype(F32) * (M_DK ** -0.5)
    vec_ref[1] = k_ref[...].astype(F32)
    vec_ref[2] = v_ref[...].astype(F32)
    masks = _outer_masks()

    def per_rows(j, _):
        bs = [j * STEP_ROWS + i for i in range(STEP_ROWS)]
        vecs = [[vec_ref[i, pl.ds(b, 1), :] for i in range(3)] for b in bs]
        scs = [[sc_ref[i, pl.ds(b, 1), :] for i in range(3)] for b in bs]
        items = [(bi, h) for bi in range(STEP_ROWS) for h in range(M_HEADS)]
        n_it = range(len(items))
        hs = lambda x, h: x[:, h * M_DK:(h + 1) * M_DK]
        q_h, k_h, v_h = ([hs(vecs[bi][i], h) for bi, h in items] for i in range(3))
        wi, ei, en = ([scs[bi][i][:, h:h + 1] for bi, h in items] for i in range(3))
        c = [c_in[bs[bi], h] for bi, h in items]
        n = [n_in[bs[bi], h:h + 1, :] for bi, h in items]
        qk = [_lane_sum(q_h[i] * k_h[i]) * ei[i] for i in n_it]
        qn = [_lane_sum(q_h[i] * n[i]) for i in n_it]
        qc = [_mm(_rows8(q_h[i]), c[i])[0:1] for i in n_it]
        kv = [_mm_tn(*_outer_operands(masks, (ei[i] * k_h[i], v_h[i]))) for i in n_it]
        hh = []
        for i, (bi, h) in enumerate(items):
            num = qk[i] * v_h[i] + wi[i] * qc[i]
            den = qk[i] + wi[i] * qn[i]
            hh.append(num / jnp.maximum(jnp.abs(den), en[i]))
            c_out[bs[bi], h] = wi[i] * c[i] + kv[i]
            n_out[bs[bi], h:h + 1, :] = wi[i] * n[i] + ei[i] * k_h[i]
        for bi in range(STEP_ROWS):
            hrow_ref[pl.ds(bs[bi], 1), :] = jnp.concatenate(
                [hh[bi * M_HEADS + h] for h in range(M_HEADS)], axis=-1)
        return 0

    lax.fori_loop(0, bb // STEP_ROWS, per_rows, 0)
    hh = hrow_ref[...]
    h_heads = [hh[:, h * M_DV:(h + 1) * M_DV] for h in range(M_HEADS)]
    y_ref[...] = _mlstm_out(h_heads, o_ref[...].astype(F32), z_ref[...].astype(F32), gn[...]).astype(y_ref.dtype)


def _mlstm_step(proj, side, c_all, n_all, m_all, acc, params, layer, bb):
    nb = proj.shape[0]
    cb = lambda c, w: pl.BlockSpec((bb, w), lambda i, c=c: (i, c // w))
    full = lambda a: pl.BlockSpec(a.shape, lambda i: (0,) * a.ndim)
    cs = _layer_block(c_all.shape[2:], bb, layer)
    ns_in = _layer_block(n_all.shape[2:], bb, layer)
    ms_in = _layer_block(m_all.shape[2:], bb, layer)
    ns = pl.BlockSpec((bb, M_HEADS, M_DK), lambda i: (i, 0, 0))
    ms = pl.BlockSpec((bb, M_HEADS), lambda i: (i, 0))
    kern = functools.partial(_mlstm_step_kernel, bb=bb)
    return pl.pallas_call(
        kern,
        grid=(nb // bb,),
        in_specs=[cb(C_MQ, M_QK), cb(C_MK, M_QK), cb(C_MV, M_W), cb(C_MO, M_W), cb(C_MZ, M_W),
                  cb(S_SMALL, LANES), cs, ns_in, ms_in, pl.BlockSpec(memory_space=pl.ANY)]
                 + [full(a) for a in params],
        out_specs=[pl.BlockSpec((bb, M_W), lambda i: (i, 0)), cs, ns, ms],
        out_shape=[jax.ShapeDtypeStruct((nb, M_W), BF16), jax.ShapeDtypeStruct(acc.shape, F32),
                   jax.ShapeDtypeStruct(n_all.shape[1:], F32), jax.ShapeDtypeStruct(m_all.shape[1:], F32)],
        input_output_aliases={9: 1},
        scratch_shapes=[pltpu.VMEM((3, bb, M_QK), F32), pltpu.VMEM((3, bb, LANES), F32),
                        pltpu.VMEM((bb, M_W), F32)],
        compiler_params=_cparams(("parallel",)),
        name="mlstm_step",
    )(proj, proj, proj, proj, proj, side, c_all, n_all, m_all, acc, *params)


def _shift_rows(proj_rows, side_rows):
    return jnp.concatenate([proj_rows[:, C_RIN:C_RIN + 3 * R_W].astype(F32),
                            side_rows[:, S_RLORA:S_RLORA + 2 * R_LORA]], axis=-1)


def _pick(n, prefs):
    for p in prefs:
        if n % p == 0:
            return p
    return n


def kernel(x_prompt, x_sample, state_rwkv_shift, state_rwkv, state_gla, state_mlstm_C, state_mlstm_n,
           state_mlstm_m, norm_pre, norm_post, w_in, r_mu_shift, r_w_lora_b, r_w0, r_a_lora_b, r_a0,
           r_k_k, r_k_a, r_r_k, r_gn, g_a_lora_b, g_a_bias, g_gn, m_if_bias, m_gn, w_br_rwkv, w_br_gla,
           w_br_mlstm, w_out):
    depth = w_in.shape[0]
    bp, tp, _ = x_prompt.shape
    bs, ts, _ = x_sample.shape
    assert ts == 1 and tp % CHUNK == 0
    assert w_in.shape[1:] == (D_MODEL, N_IN)
    w_t = jnp.transpose(w_in, (0, 2, 1))
    wr, wg, wm, wo = (w.astype(BF16) for w in (w_br_rwkv, w_br_gla, w_br_mlstm, w_out))

    mp = bp * tp
    tm_in = _pick(mp, (2048, 1024, 512, 256, 128, 64))
    tm_mg = _pick(mp, (512, 256, 128, 64))
    tb = _pick(tp, (256, 128, 64))
    bb = _pick(bs, (16, 8))

    xp = x_prompt.reshape(mp, D_MODEL)
    xs = x_sample.reshape(bs, D_MODEL)
    new_p = [[] for _ in range(6)]
    new_s = [[] for _ in range(3)]
    acc_r, acc_g, acc_c = (jnp.zeros(s.shape, F32) for s in (state_rwkv, state_gla, state_mlstm_C))
    for l in range(depth):
        g_pre = norm_pre[l].reshape(1, -1)
        g_post = norm_post[l].reshape(1, -1)
        rp = _rwkv_params(r_mu_shift[l], r_w_lora_b[l], r_w0[l], r_a_lora_b[l], r_a0[l], r_k_k[l],
                          r_k_a[l], r_r_k[l], r_gn[l])
        gp = _gla_params(g_a_lora_b[l], g_a_bias[l], g_gn[l])
        mparams = _mlstm_params(m_if_bias[l], m_gn[l])

        proj, side = _inproj(xp, g_pre, w_t, l, tm_in)
        y_r, s_r = _rwkv_prompt(proj, side, rp, bp, tp, tb)
        y_g, s_g = _gla_prompt(proj, side, gp, bp, tp, tb)
        y_m, c, n, m = _mlstm_prompt(proj, side, mparams, bp, tp, tb)
        xp = _merge(proj, y_r, y_g, y_m, xp, wr, wg, wm, wo, g_post, l, tm_mg)
        shift = _shift_rows(proj.reshape(bp, tp, C_RLORA)[:, tp - 1], side.reshape(bp, tp, SIDE_W)[:, tp - 1])
        for lst, val in zip(new_p, (shift, s_r, s_g, c, n, m.reshape(bp, M_HEADS))):
            lst.append(val)

        proj, side = _inproj(xs, g_pre, w_t, l, bs)
        y_r, acc_r = _rwkv_step(proj, side, state_rwkv_shift, state_rwkv, acc_r, rp, l, bb)
        y_g, acc_g = _gla_step(proj, side, state_gla, acc_g, gp, l, bb)
        y_m, acc_c, n, m = _mlstm_step(proj, side, state_mlstm_C, state_mlstm_n, state_mlstm_m, acc_c,
                                       mparams, l, bb)
        xs = _merge(proj, y_r, y_g, y_m, xs, wr, wg, wm, wo, g_post, l, bs)
        shift = _shift_rows(proj, side)
        for lst, val in zip(new_s, (shift, n, m)):
            lst.append(val)

    stk = lambda lst: jnp.stack(lst)
    return (xp.reshape(bp, tp, D_MODEL), xs.reshape(bs, ts, D_MODEL),
            *(stk(v) for v in new_p),
            stk(new_s[0]), acc_r, acc_g, acc_c, stk(new_s[1]), stk(new_s[2]))
```

```python
import functools
import math

import jax
import jax.numpy as jnp
from jax import lax
from jax.experimental import pallas as pl
from jax.experimental.pallas import tpu as pltpu

F32 = jnp.float32
BF16 = jnp.bfloat16

D_MODEL = 1024
R_HEADS, R_HD = 8, 64
R_W = R_HEADS * R_HD
R_LORA = 64
R_SHIFT_W = 3 * R_W + 2 * R_LORA
R_GN_EPS = 64e-5
G_HEADS, G_DK, G_DV = 4, 64, 128
G_QK = G_HEADS * G_DK
G_W = G_HEADS * G_DV
G_LORA = 16
G_GATE_TEMP = 16.0
M_HEADS, M_DK, M_DV = 4, 128, 128
M_QK = M_HEADS * M_DK
M_W = M_HEADS * M_DV
EPS = 1e-6

LANES = 128
SUBLANES = 8
VMEM_LIMIT = 48 * 1024 * 1024

C_GATE = 0
C_RZ = 3072
C_GV = 3584
C_GZ = 4096
C_MQ = 4608
C_MK = 5120
C_MV = 5632
C_MO = 6144
C_MZ = 6656
C_GQK = 7168
C_RIN = 7680
C_RLORA = C_RIN + 3 * R_W
C_SMALL = C_RLORA + LANES
SM_GA, SM_MI, SM_MF = 0, 16, 20
N_USED = C_SMALL + LANES
PROJ_TN = 512
N_PAD = -(-N_USED // PROJ_TN) * PROJ_TN
N_MAIN_BLOCKS = C_RLORA // PROJ_TN
_W_SEGMENTS = (("r_in", R_SHIFT_W), ("r_z", R_W), ("g_q", G_QK), ("g_k", G_QK), ("g_v", G_W), ("g_a", G_LORA),
               ("g_z", G_W), ("m_q", M_QK), ("m_k", M_QK), ("m_v", M_W), ("m_i", M_HEADS), ("m_f", M_HEADS),
               ("m_o", M_W), ("m_z", M_W), ("gate", 3 * D_MODEL))
W_OFF = {}
_off = 0
for _name, _width in _W_SEGMENTS:
    W_OFF[_name] = _off
    _off += _width
N_IN = _off
W_MAIN_ROWS = tuple(W_OFF[name] + PROJ_TN * b
                    for name, nblk in (("gate", 6), ("r_z", 1), ("g_v", 1), ("g_z", 1), ("m_q", 1), ("m_k", 1),
                                       ("m_v", 1), ("m_o", 1), ("m_z", 1), ("g_q", 1), ("r_in", 3))
                    for b in range(nblk))
assert len(W_MAIN_ROWS) == N_MAIN_BLOCKS and all(r % SUBLANES == 0 for r in W_MAIN_ROWS)
SIDE_W = 2 * LANES
S_RLORA, S_SMALL = 0, LANES

CHUNK = 64


def _cparams(sem):
    return pltpu.CompilerParams(dimension_semantics=sem, vmem_limit_bytes=VMEM_LIMIT)


def _mm(a, b):
    return jnp.dot(a.astype(BF16), b.astype(BF16), preferred_element_type=F32)


def _mm_nt(a, b):
    return lax.dot_general(a.astype(BF16), b.astype(BF16), (((1,), (1,)), ((), ())),
                           preferred_element_type=F32)


def _mm_tn(a, b):
    return lax.dot_general(a.astype(BF16), b.astype(BF16), (((0,), (0,)), ((), ())),
                           preferred_element_type=F32)


def _mm_exact_lhs(m01, x):
    hi = x.astype(BF16)
    r1 = x - hi.astype(F32)
    mid = r1.astype(BF16)
    lo = (r1 - mid.astype(F32)).astype(BF16)
    m = m01.astype(BF16)
    d = lambda p: jnp.dot(m, p, preferred_element_type=F32)
    return d(hi) + d(mid) + d(lo)


def _mm_exact_rhs(x, m01):
    hi = x.astype(BF16)
    r1 = x - hi.astype(F32)
    mid = r1.astype(BF16)
    lo = (r1 - mid.astype(F32)).astype(BF16)
    m = m01.astype(BF16)
    d = lambda p: jnp.dot(p, m, preferred_element_type=F32)
    return d(hi) + d(mid) + d(lo)


def _sigmoid(x):
    return 1.0 / (1.0 + jnp.exp(-x))


def _silu(x):
    return x * _sigmoid(x)


def _softplus(x):
    return jnp.maximum(x, 0.0) + jnp.log(1.0 + jnp.exp(-jnp.abs(x)))


def _log_sigmoid(x):
    return -_softplus(-x)


def _iota2(shape, dim):
    return lax.broadcasted_iota(jnp.int32, shape, dim)


def _lane_sum(x):
    return jnp.sum(x, axis=-1, keepdims=True)


def _rows8(x):
    return jnp.broadcast_to(x, (8, x.shape[1]))


def _outer_masks():
    rid = _iota2((8, 1), 0)
    on = lambda *rows: sum((rid == r).astype(F32) for r in rows)
    return (on(0, 2), on(1), on(0, 1), on(2)), (on(3, 5), on(4), on(3, 4), on(5))


def _outer_operands(masks, *pairs):
    a_op, b_op = 0.0, 0.0
    for (a, b), (m_ahi, m_alo, m_bhi, m_blo) in zip(pairs, masks):
        a_hi = a.astype(BF16).astype(F32)
        b_hi = b.astype(BF16).astype(F32)
        a_op = a_op + m_ahi * a_hi + m_alo * (a - a_hi)
        b_op = b_op + m_bhi * b_hi + m_blo * (b - b_hi)
    return a_op, b_op


def _block_tril(n, blk):
    r, c = _iota2((n, n), 0), _iota2((n, n), 1)
    return ((r - c).astype(jnp.uint32) <= (r % blk).astype(jnp.uint32)).astype(F32)


def _inproj_kernel(rows_ref, x_ref, g_ref, w_ref, wl_ref, wa_ref, wif_ref, o_ref, side_ref, h_ref, ws_ref):
    del rows_ref

    @pl.when(pl.program_id(1) == 0)
    def _():
        x = x_ref[...]
        y = x * lax.rsqrt(jnp.mean(x * x, axis=-1, keepdims=True) + EPS)
        h_ref[...] = (y * g_ref[...]).astype(BF16)
        ws_ref[...] = jnp.zeros_like(ws_ref)
        ws_ref[S_RLORA:S_RLORA + 2 * R_LORA, :] = wl_ref[0]
        ws_ref[S_SMALL + SM_GA:S_SMALL + SM_GA + G_LORA, :] = wa_ref[0]
        ws_ref[S_SMALL + SM_MI:S_SMALL + SM_MI + 2 * M_HEADS, :] = wif_ref[0]
        side_ref[...] = _mm_nt(h_ref[...], ws_ref[...])

    o_ref[...] = _mm_nt(h_ref[...], w_ref[0]).astype(BF16)


def _inproj(x2d, g_pre, w_t, layer, tm):
    m = x2d.shape[0]
    w_blk = lambda n: (pl.Element(1), pl.Element(n), pl.Element(D_MODEL))
    w_rows = lambda n, start: pl.BlockSpec(w_blk(n), lambda i, j, rows, start=start: (layer, start, 0))
    grid_spec = pltpu.PrefetchScalarGridSpec(
        num_scalar_prefetch=1,
        grid=(m // tm, N_MAIN_BLOCKS),
        in_specs=[pl.BlockSpec((tm, D_MODEL), lambda i, j, rows: (i, 0)),
                  pl.BlockSpec((1, D_MODEL), lambda i, j, rows: (0, 0)),
                  pl.BlockSpec(w_blk(PROJ_TN), lambda i, j, rows: (layer, rows[j] * SUBLANES, 0)),
                  w_rows(2 * R_LORA, W_OFF["r_in"] + 3 * R_W),
                  w_rows(G_LORA, W_OFF["g_a"]),
                  w_rows(2 * M_HEADS, W_OFF["m_i"])],
        out_specs=[pl.BlockSpec((tm, PROJ_TN), lambda i, j, rows: (i, j)),
                   pl.BlockSpec((tm, SIDE_W), lambda i, j, rows: (i, 0))],
        scratch_shapes=[pltpu.VMEM((tm, D_MODEL), BF16), pltpu.VMEM((SIDE_W, D_MODEL), F32)])
    return pl.pallas_call(
        _inproj_kernel,
        grid_spec=grid_spec,
        out_shape=[jax.ShapeDtypeStruct((m, C_RLORA), BF16),
                   jax.ShapeDtypeStruct((m, SIDE_W), F32)],
        compiler_params=_cparams(("parallel", "arbitrary")),
        name="inproj",
    )(jnp.asarray([r // SUBLANES for r in W_MAIN_ROWS], jnp.int32), x2d, g_pre, w_t, w_t, w_t, w_t)


def _merge_kernel(gate_ref, yr_ref, yg_ref, ym_ref, x_ref, wr_ref, wg_ref, wm_ref, wo_ref, gp_ref, o_ref):
    d = D_MODEL
    gate = lambda b: _sigmoid(gate_ref[:, b * d:(b + 1) * d].astype(F32))
    merged = (gate(0) * _mm(yr_ref[...], wr_ref[...]) + gate(1) * _mm(yg_ref[...], wg_ref[...])
              + gate(2) * _mm(ym_ref[...], wm_ref[...]))
    o = _mm(merged, wo_ref[...])
    y = o * lax.rsqrt(jnp.mean(o * o, axis=-1, keepdims=True) + EPS)
    o_ref[...] = x_ref[...] + y * gp_ref[...]


def _merge(proj, y_r, y_g, y_m, x2d, w_r, w_g, w_m, w_o, g_post, layer, tm):
    m = x2d.shape[0]
    row = lambda w: pl.BlockSpec((tm, w), lambda i: (i, 0))
    full = lambda a: pl.BlockSpec(a.shape, lambda i: (0, 0))
    wl = lambda a: pl.BlockSpec((None,) + a.shape[1:], lambda i: (layer, 0, 0))
    return pl.pallas_call(
        _merge_kernel,
        grid=(m // tm,),
        in_specs=[row(3 * D_MODEL), row(R_W), row(G_W), row(M_W), row(D_MODEL),
                  wl(w_r), wl(w_g), wl(w_m), wl(w_o), full(g_post)],
        out_specs=row(D_MODEL),
        out_shape=jax.ShapeDtypeStruct((m, D_MODEL), F32),
        compiler_params=_cparams(("parallel",)),
        name="merge",
    )(proj, y_r, y_g, y_m, x2d, w_r, w_g, w_m, w_o, g_post)


def _rwkv_front(r_in, k_in, v_in, l_in, prev_r, prev_k, prev_v, prev_l, p):
    (mu_r, mu_k, mu_v, mu_l, wb, w0, ab, a0, k_k, k_a) = p
    r = r_in + mu_r * (prev_r - r_in)
    k = k_in + mu_k * (prev_k - k_in)
    v = v_in + mu_v * (prev_v - v_in)
    lo = l_in + mu_l * (prev_l - l_in)
    log_w = -_softplus(-(w0 + _mm(jnp.tanh(lo), wb))) - 0.5
    lw = -jnp.exp(log_w)
    a = _sigmoid(a0 + _mm(lo, ab))
    kk = k * k_k
    k2 = k * (1.0 + (a - 1.0) * k_a)
    return r, k2, v, lw, a, kk


def _rwkv_prompt_kernel(z_ref, r_ref, k_ref, v_ref, l_ref,
                        mu_r, mu_k, mu_v, mu_l, wb, w0, ab, a0, k_k, k_a, r_k, gn,
                        y_ref, s_ref, carry_ref, sp_ref, *, n_chunks):
    L = CHUNK

    @pl.when(pl.program_id(1) == 0)
    def _():
        sp_ref[...] = jnp.zeros_like(sp_ref)
        carry_ref[...] = jnp.zeros_like(carry_ref)

    tb = n_chunks * L
    params = tuple(x[...] for x in (mu_r, mu_k, mu_v, mu_l, wb, w0, ab, a0, k_k, k_a))
    assert L == R_HD
    m_a = (_iota2((1, LANES), 1) < R_HD).astype(F32)
    m_b = 1.0 - m_a
    m_a16, m_b16 = m_a.astype(BF16), m_b.astype(BF16)
    trow = _iota2((L, LANES), 0)
    tcol = _iota2((L, LANES), 1) % L
    eye = (tcol == trow).astype(F32)
    blk_mask = (trow // 16) == (tcol // 16)
    r2 = _iota2((2 * L, LANES), 0)
    g_mask = (_iota2((2 * L, LANES), 1) % L) < (r2 % L) + (r2 >= L).astype(jnp.int32)
    same_head = (_iota2((LANES, LANES), 0) // R_HD) == (_iota2((LANES, LANES), 1) // R_HD)
    first_row = _iota2((tb, 1), 0) == 0

    def seg_sum(x):
        return _lane_sum(x * m_a) * m_a + _lane_sum(x * m_b) * m_b

    def bd(y):
        yb = y.astype(BF16)
        return jnp.concatenate([yb * m_a16, yb * m_b16], axis=0)

    def shifted(x, carry):
        return jnp.where(first_row, carry, pltpu.roll(x, 1, 0))

    r_in, k_in, v_in = (x[...].astype(F32) for x in (r_ref, k_ref, v_ref))
    l_in = l_ref[...]
    prev_r = shifted(r_in, carry_ref[0:1, 0:R_W])
    prev_k = shifted(k_in, carry_ref[0:1, R_W:2 * R_W])
    prev_v = shifted(v_in, carry_ref[0:1, 2 * R_W:3 * R_W])
    prev_l = shifted(l_in, carry_ref[0:1, 3 * R_W:R_SHIFT_W])
    carry_ref[0:1, 0:R_W] = r_in[tb - 1:tb, :]
    carry_ref[0:1, R_W:2 * R_W] = k_in[tb - 1:tb, :]
    carry_ref[0:1, 2 * R_W:3 * R_W] = v_in[tb - 1:tb, :]
    carry_ref[0:1, 3 * R_W:R_SHIFT_W] = l_in[tb - 1:tb, :]
    r, k2, v, lw, a, kk = _rwkv_front(r_in, k_in, v_in, l_in, prev_r, prev_k, prev_v, prev_l, params)

    cum = _mm_exact_lhs(_block_tril(tb, L), lw)
    p_inc = jnp.exp(cum)
    p_exc = jnp.exp(cum - lw)
    p_inv = jnp.exp(-cum)
    r_t = r * p_inc
    k_h = k2 * p_inv

    n_pairs = R_HEADS // 2
    CP = [(c, p) for c in range(n_chunks) for p in range(n_pairs)]
    N = range(len(CP))
    kap_t, beta_h = [], []
    for p in range(n_pairs):
        sl = slice(p * LANES, (p + 1) * LANES)
        kk_p = kk[:, sl]
        kap = kk_p * (1.0 / jnp.maximum(jnp.sqrt(seg_sum(kk_p * kk_p)), 1e-12))
        beta_h.append(kap * a[:, sl] * p_inv[:, sl])
        kap_t.append(kap * p_exc[:, sl])
    cut = lambda x, c, p: x[c * L:(c + 1) * L, p * LANES:(p + 1) * LANES]
    kap_t = [kap_t[p][c * L:(c + 1) * L] for c, p in CP]
    beta_h = [beta_h[p][c * L:(c + 1) * L] for c, p in CP]
    rt_c = [cut(r_t, c, p) for c, p in CP]
    kh_c = [cut(k_h, c, p) for c, p in CP]
    v_c = [cut(v, c, p) for c, p in CP]
    p_last = [p_inc[(c + 1) * L - 1:(c + 1) * L, p * LANES:(p + 1) * LANES] for c, p in CP]
    bhl = [beta_h[i] * p_last[i] for i in N]
    khl = [kh_c[i] * p_last[i] for i in N]
    lhs2 = [jnp.concatenate([kap_t[i], rt_c[i]], axis=0) for i in N]
    bd_v = [bd(v_c[i]) for i in N]
    gb = [jnp.where(g_mask, _mm_nt(lhs2[i], bd(beta_h[i])), 0.0) for i in N]
    gk = [jnp.where(g_mask, _mm_nt(lhs2[i], bd(kh_c[i])), 0.0) for i in N]
    a_m = [gb[i][0:L] for i in N]
    d = [jnp.where(blk_mask, a_m[i], 0.0) for i in N]
    akv = [_mm(gk[i][0:L], bd_v[i]) for i in N]
    d2 = [_mm(d[i], bd(d[i])) for i in N]
    bd_d2 = [bd(d2[i]) for i in N]
    d4 = [_mm(d2[i], bd_d2[i]) for i in N]
    dd2 = [_mm(d[i], bd_d2[i]) for i in N]
    d8 = [_mm(d4[i], bd(d4[i])) for i in N]
    x = [eye - d[i] + d2[i] - dd2[i] for i in N]
    x = [x[i] + _mm(x[i], bd(d4[i])) for i in N]
    td = [x[i] + _mm(x[i], bd(d8[i])) for i in N]
    m = [_mm(td[i], bd(a_m[i] - d[i])) for i in N]
    m2 = [_mm(m[i], bd(m[i])) for i in N]
    q = [eye - m[i] + m2[i] - _mm(m[i], bd(m2[i])) for i in N]
    t_inv = [_mm(q[i], bd(td[i])) for i in N]
    kbar = [_mm(t_inv[i], bd(kap_t[i])) for i in N]
    u0 = [-_mm(t_inv[i], bd(akv[i])) for i in N]
    n_bd = [jnp.where(same_head, _mm_tn(kbar[i], bhl[i]), 0.0) for i in N]
    cz = [_mm_tn(jnp.concatenate([u0[i], v_c[i]], axis=0), jnp.concatenate([bhl[i], khl[i]], axis=0))
          for i in N]
    c_c = [cz[i][0:R_HD] * m_a + cz[i][R_HD:2 * R_HD] * m_b for i in N]
    qt = [rt_c[i] - _mm(gb[i][L:2 * L], bd(kbar[i])) for i in N]
    y0 = [_mm(gb[i][L:2 * L], bd(u0[i])) + _mm(gk[i][L:2 * L], bd_v[i]) for i in N]
    s = [sp_ref[p] for p in range(n_pairs)]
    y_ch = []
    for c in range(n_chunks):
        ids = [c * n_pairs + p for p in range(n_pairs)]
        y_ch.append([_mm_nt(qt[i], bd(s[p])) + y0[i] for p, i in enumerate(ids)])
        s = [s[p] * p_last[i] - _mm(s[p], n_bd[i]) + c_c[i] for p, i in enumerate(ids)]
    for p in range(n_pairs):
        sp_ref[p] = s[p]
    y_pairs = [jnp.concatenate([y_ch[c][p] for c in range(n_chunks)], axis=0) for p in range(n_pairs)]
    outs = []
    for p in range(n_pairs):
        sl = slice(p * LANES, (p + 1) * LANES)
        y = y_pairs[p]
        y = y - seg_sum(y) * (1.0 / R_HD)
        y = y * lax.rsqrt(seg_sum(y * y) * (1.0 / R_HD) + R_GN_EPS)
        bonus = seg_sum(r[:, sl] * k2[:, sl] * r_k[:, sl]) * v[:, sl]
        outs.append(y * gn[:, sl] + bonus)
    y_ref[...] = (jnp.concatenate(outs, axis=-1) * _silu(z_ref[...].astype(F32))).astype(y_ref.dtype)

    @pl.when(pl.program_id(1) == pl.num_programs(1) - 1)
    def _():
        for p in range(n_pairs):
            sp = sp_ref[p]
            s_ref[0, 2 * p] = sp[:, 0:R_HD]
            s_ref[0, 2 * p + 1] = sp[:, R_HD:2 * R_HD]


def _rwkv_params(mu, wb, w0, ab, a0, k_k, k_a, r_k, gn):
    row = lambda x: x.reshape(1, -1)
    zeros = jnp.zeros((R_LORA, R_W), F32)
    wb_p = jnp.concatenate([wb, zeros], axis=0).astype(BF16)
    ab_p = jnp.concatenate([zeros, ab], axis=0).astype(BF16)
    return (row(mu[0:R_W]), row(mu[R_W:2 * R_W]), row(mu[2 * R_W:3 * R_W]), row(mu[3 * R_W:]),
            wb_p, row(w0), ab_p, row(a0), row(k_k), row(k_a), row(r_k), row(gn))


def _rwkv_prompt(proj, side, params, batch, seq, tb):
    nt = seq // tb
    cb = lambda c, w: pl.BlockSpec((tb, w), lambda b, t, c=c: (b * nt + t, c // w))
    full = lambda a: pl.BlockSpec(a.shape, lambda b, t: (0,) * a.ndim)
    kern = functools.partial(_rwkv_prompt_kernel, n_chunks=tb // CHUNK)
    return pl.pallas_call(
        kern,
        grid=(batch, nt),
        in_specs=[cb(C_RZ, R_W), cb(C_RIN, R_W), cb(C_RIN + R_W, R_W), cb(C_RIN + 2 * R_W, R_W),
                  cb(S_RLORA, LANES)] + [full(a) for a in params],
        out_specs=[pl.BlockSpec((tb, R_W), lambda b, t: (b * nt + t, 0)),
                   pl.BlockSpec((1, R_HEADS, R_HD, R_HD), lambda b, t: (b, 0, 0, 0))],
        out_shape=[jax.ShapeDtypeStruct((batch * seq, R_W), BF16),
                   jax.ShapeDtypeStruct((batch, R_HEADS, R_HD, R_HD), F32)],
        scratch_shapes=[pltpu.VMEM((8, R_SHIFT_W), F32),
                        pltpu.VMEM((R_HEADS // 2, R_HD, LANES), F32)],
        compiler_params=_cparams(("parallel", "arbitrary")),
        name="rwkv_prompt",
    )(proj, proj, proj, proj, side, *params)


STEP_ROWS = 4
GLA_STEP_ROWS = 2


def _rwkv_lanes_kernel(z_ref, r_ref, k_ref, v_ref, l_ref, pr_ref, pk_ref, pv_ref, pl_ref, s_in, acc_ref,
                       mu_r, mu_k, mu_v, mu_l, wb, w0, ab, a0, k_k, k_a, rk_t, gn_t,
                       y_ref, s_out, vec_ref, ycol_ref):
    del acc_ref
    h = pl.program_id(0)

    @pl.when(h == 0)
    def _():
        params = tuple(x[...] for x in (mu_r, mu_k, mu_v, mu_l, wb, w0, ab, a0, k_k, k_a))
        r, k2, v, lw, a, kk = _rwkv_front(r_ref[...].astype(F32), k_ref[...].astype(F32),
                                          v_ref[...].astype(F32), l_ref[...], pr_ref[...], pk_ref[...],
                                          pv_ref[...], pl_ref[...], params)
        kaps, betas = [], []
        for hh in range(R_HEADS):
            sl = slice(hh * R_HD, (hh + 1) * R_HD)
            kk_h = kk[:, sl]
            kap = kk_h / jnp.maximum(jnp.sqrt(_lane_sum(kk_h * kk_h)), 1e-12)
            kaps.append(kap)
            betas.append(kap * a[:, sl])
        rows = (jnp.concatenate(kaps, axis=-1), jnp.concatenate(betas, axis=-1), jnp.exp(lw), k2, v, r,
                _silu(z_ref[...].astype(F32)))
        for i, x in enumerate(rows):
            vec_ref[i] = x.T

    base = pl.multiple_of(h * R_HD, R_HD)
    chan = pl.ds(base, R_HD)
    kap_t, beta_t, w_t, k_t, v_t, r_t, gz_t = (vec_ref[i, chan, :] for i in range(7))

    def per_v(vi, _):
        s = s_in[vi]
        sa = -jnp.sum(s * kap_t, axis=0, keepdims=True)
        v_row = vec_ref[4, pl.ds(base + vi, 1), :]
        s_new = s * w_t + sa * beta_t + v_row * k_t
        s_out[vi] = s_new
        ycol_ref[pl.ds(vi, 1), :] = jnp.sum(s_new * r_t, axis=0, keepdims=True)
        return 0

    lax.fori_loop(0, R_HD, per_v, 0, unroll=8)
    y = ycol_ref[...]
    y = y - jnp.mean(y, axis=0, keepdims=True)
    y = y * lax.rsqrt(jnp.mean(y * y, axis=0, keepdims=True) + R_GN_EPS)
    bonus = jnp.sum(r_t * k_t * rk_t[chan, :], axis=0, keepdims=True) * v_t
    y_ref[...] = ((y * gn_t[chan, :] + bonus) * gz_t).astype(y_ref.dtype)


def _rwkv_step_lanes(proj, side, shift_all, s_lanes, acc, params, layer):
    nb = proj.shape[0]
    assert nb == LANES
    cb = lambda c, w: pl.BlockSpec((nb, w), lambda h, c=c: (0, c // w))
    sh = lambda c, w: pl.BlockSpec((None, nb, w), lambda h, c=c: (layer, 0, c // w))
    full = lambda a: pl.BlockSpec(a.shape, lambda h: (0,) * a.ndim)
    st = pl.BlockSpec((None, None, R_HD, R_HD, nb), lambda h: (layer, h, 0, 0, 0))
    return pl.pallas_call(
        _rwkv_lanes_kernel,
        grid=(R_HEADS,),
        in_specs=[cb(C_RZ, R_W), cb(C_RIN, R_W), cb(C_RIN + R_W, R_W), cb(C_RIN + 2 * R_W, R_W),
                  cb(S_RLORA, LANES),
                  sh(0, R_W), sh(R_W, R_W), sh(2 * R_W, R_W), sh(3 * R_W, LANES), st,
                  pl.BlockSpec(memory_space=pl.ANY)]
                 + [full(a) for a in params],
        out_specs=[pl.BlockSpec((None, R_HD, nb), lambda h: (h, 0, 0)), st],
        out_shape=[jax.ShapeDtypeStruct((R_HEADS, R_HD, nb), BF16),
                   jax.ShapeDtypeStruct(acc.shape, F32)],
        input_output_aliases={10: 1},
        scratch_shapes=[pltpu.VMEM((7, R_W, nb), F32), pltpu.VMEM((R_HD, nb), F32)],
        compiler_params=_cparams(("arbitrary",)),
        name="rwkv_step",
    )(proj, proj, proj, proj, side, shift_all, shift_all, shift_all, shift_all, s_lanes, acc, *params)


def _layer_block(shape_tail, bb, layer):
    zeros = (0,) * len(shape_tail)
    return pl.BlockSpec((None, bb) + tuple(shape_tail), lambda i: (layer, i) + zeros)


def _gla_log_gate(small, ab_p, a_bias):
    return _log_sigmoid(_mm(small, ab_p) + a_bias) / G_GATE_TEMP


def _gla_out(o_heads, z, gn):
    outs = []
    for h in range(G_HEADS):
        o = o_heads[h]
        outs.append(o * lax.rsqrt(jnp.mean(o * o, axis=-1, keepdims=True) + EPS))
    return jnp.concatenate(outs, axis=-1) * gn * _silu(z)


def _gla_prompt_kernel(qk_ref, v_ref, z_ref, sm_ref, ab, a_bias, gn, y_ref, s_ref, st_ref, *, n_chunks):
    L = CHUNK
    last = pl.num_programs(1) - 1

    @pl.when(pl.program_id(1) == 0)
    def _():
        st_ref[...] = jnp.zeros_like(st_ref)

    tril_incl = _iota2((L, L), 1) <= _iota2((L, L), 0)
    scale = G_DK ** -0.5
    C, H = range(n_chunks), range(G_HEADS)
    rws = [slice(c * L, (c + 1) * L) for c in C]
    sks = [slice(h * G_DK, (h + 1) * G_DK) for h in H]

    qk = qk_ref[...].astype(F32)
    q, k = qk[:, 0:G_QK] * scale, qk[:, G_QK:2 * G_QK]
    v = v_ref[...].astype(F32)
    la = _gla_log_gate(sm_ref[...], ab[...], a_bias[...])
    cum = _mm_exact_lhs(_block_tril(n_chunks * L, L), la)
    q0, qe, ke, kl, e_last = [], [], [], [], []
    for c in C:
        cum_c = cum[rws[c]]
        ref_row = cum_c[L // 2:L // 2 + 1, :]
        last_row = cum_c[L - 1:L, :]
        q0.append(q[rws[c]] * jnp.exp(cum_c))
        qe.append(q[rws[c]] * jnp.exp(cum_c - ref_row))
        ke.append(k[rws[c]] * jnp.exp(ref_row - cum_c))
        kl.append(k[rws[c]] * jnp.exp(last_row - cum_c))
        e_last.append(jnp.exp(last_row))
    v_h = [[v[rws[c], h * G_DV:(h + 1) * G_DV] for h in H] for c in C]
    att = [[jnp.where(tril_incl, _mm_nt(qe[c][:, sks[h]], ke[c][:, sks[h]]), 0.0) for h in H] for c in C]
    upd = [[_mm_tn(v_h[c][h], kl[c][:, sks[h]]) for h in H] for c in C]
    av = [[_mm(att[c][h], v_h[c][h]) for h in H] for c in C]
    st = [st_ref[h] for h in H]
    o_ch = []
    for c in C:
        o_ch.append([av[c][h] + _mm_nt(q0[c][:, sks[h]], st[h]) for h in H])
        st = [st[h] * e_last[c][:, sks[h]] + upd[c][h] for h in H]
    for h in H:
        st_ref[h] = st[h]
    o_heads = [jnp.concatenate([o_ch[c][h] for c in C], axis=0) for h in H]
    y_ref[...] = _gla_out(o_heads, z_ref[...].astype(F32), gn[...]).astype(y_ref.dtype)

    @pl.when(pl.program_id(1) == last)
    def _():
        for h in range(G_HEADS):
            s_ref[0, h] = st_ref[h].T


def _gla_params(ab, a_bias, gn):
    ab_p = jnp.zeros((LANES, G_QK), F32).at[SM_GA:SM_GA + G_LORA].set(ab).astype(BF16)
    return ab_p, a_bias.reshape(1, -1), gn.reshape(1, -1)


def _gla_prompt(proj, side, params, batch, seq, tb):
    nt = seq // tb
    cb = lambda c, w: pl.BlockSpec((tb, w), lambda b, t, c=c: (b * nt + t, c // w))
    full = lambda a: pl.BlockSpec(a.shape, lambda b, t: (0,) * a.ndim)
    kern = functools.partial(_gla_prompt_kernel, n_chunks=tb // CHUNK)
    return pl.pallas_call(
        kern,
        grid=(batch, nt),
        in_specs=[cb(C_GQK, 2 * G_QK), cb(C_GV, G_W), cb(C_GZ, G_W), cb(S_SMALL, LANES)]
                 + [full(a) for a in params],
        out_specs=[pl.BlockSpec((tb, G_W), lambda b, t: (b * nt + t, 0)),
                   pl.BlockSpec((1, G_HEADS, G_DK, G_DV), lambda b, t: (b, 0, 0, 0))],
        out_shape=[jax.ShapeDtypeStruct((batch * seq, G_W), BF16),
                   jax.ShapeDtypeStruct((batch, G_HEADS, G_DK, G_DV), F32)],
        scratch_shapes=[pltpu.VMEM((G_HEADS, G_DV, G_DK), F32)],
        compiler_params=_cparams(("parallel", "arbitrary")),
        name="gla_prompt",
    )(proj, proj, proj, side, *params)


def _gla_step_kernel(qk_ref, v_ref, z_ref, sm_ref, s_in, acc_ref, ab, a_bias, gn, y_ref, s_out,
                     vec_ref, vrow_ref, orow_ref, *, bb):
    del acc_ref
    qk = qk_ref[...].astype(F32)
    q = qk[:, 0:G_QK] * (G_DK ** -0.5)
    k = qk[:, G_QK:2 * G_QK]
    g = _gla_log_gate(sm_ref[...], ab[...], a_bias[...])
    vec_ref[0] = q
    vec_ref[1] = k
    vec_ref[2] = jnp.exp(g)
    vrow_ref[...] = v_ref[...].astype(F32)
    eye = (_iota2((G_DK, G_DK), 0) == _iota2((G_DK, G_DK), 1)).astype(F32)
    masks = _outer_masks()

    def per_rows(j, _):
        bs = [j * GLA_STEP_ROWS + i for i in range(GLA_STEP_ROWS)]
        vecs = [[vec_ref[i, pl.ds(b, 1), :] for i in range(3)] for b in bs]
        v_b = [vrow_ref[pl.ds(b, 1), :] for b in bs]
        items = [(bi, h) for bi in range(GLA_STEP_ROWS) for h in range(G_HEADS)]
        n = range(len(items))
        hk = lambda x, h: x[:, h * G_DK:(h + 1) * G_DK]
        q_, k_, e_ = ([hk(vecs[bi][i], h) for bi, h in items] for i in range(3))
        v_h = [v_b[bi][:, h * G_DV:(h + 1) * G_DV] for bi, h in items]
        s = [s_in[bs[bi], h] for bi, h in items]
        e_col = [_lane_sum(eye * e_[i]) for i in n]
        qk_dot = [_lane_sum(q_[i] * k_[i]) for i in n]
        o_s = [_mm(_rows8(q_[i] * e_[i]), s[i])[0:1] for i in n]
        kv = [_mm_tn(*_outer_operands(masks, (k_[i], v_h[i]))) for i in n]
        o = [qk_dot[i] * v_h[i] + o_s[i] for i in n]
        for i, (bi, h) in enumerate(items):
            s_out[bs[bi], h] = s[i] * e_col[i] + kv[i]
        for bi in range(GLA_STEP_ROWS):
            orow_ref[pl.ds(bs[bi], 1), :] = jnp.concatenate(
                [o[bi * G_HEADS + h] for h in range(G_HEADS)], axis=-1)
        return 0

    lax.fori_loop(0, bb // GLA_STEP_ROWS, per_rows, 0)
    o = orow_ref[...]
    o_heads = [o[:, h * G_DV:(h + 1) * G_DV] for h in range(G_HEADS)]
    y_ref[...] = _gla_out(o_heads, z_ref[...].astype(F32), gn[...]).astype(y_ref.dtype)


def _gla_step(proj, side, s_all, acc, params, layer, bb):
    nb = proj.shape[0]
    cb = lambda c, w: pl.BlockSpec((bb, w), lambda i, c=c: (i, c // w))
    full = lambda a: pl.BlockSpec(a.shape, lambda i: (0,) * a.ndim)
    st = _layer_block(s_all.shape[2:], bb, layer)
    kern = functools.partial(_gla_step_kernel, bb=bb)
    return pl.pallas_call(
        kern,
        grid=(nb // bb,),
        in_specs=[cb(C_GQK, 2 * G_QK), cb(C_GV, G_W), cb(C_GZ, G_W), cb(S_SMALL, LANES), st,
                  pl.BlockSpec(memory_space=pl.ANY)]
                 + [full(a) for a in params],
        out_specs=[pl.BlockSpec((bb, G_W), lambda i: (i, 0)), st],
        out_shape=[jax.ShapeDtypeStruct((nb, G_W), BF16), jax.ShapeDtypeStruct(acc.shape, F32)],
        input_output_aliases={5: 1},
        scratch_shapes=[pltpu.VMEM((3, bb, G_QK), F32), pltpu.VMEM((bb, G_W), F32),
                        pltpu.VMEM((bb, G_W), F32)],
        compiler_params=_cparams(("parallel",)),
        name="gla_step",
    )(proj, proj, proj, side, s_all, acc, *params)


def _mlstm_out(h_heads, o_pre, z, gn):
    outs = []
    for h in range(M_HEADS):
        y = h_heads[h]
        y = y - jnp.mean(y, axis=-1, keepdims=True)
        outs.append(y * lax.rsqrt(jnp.mean(y * y, axis=-1, keepdims=True) + EPS))
    return jnp.concatenate(outs, axis=-1) * gn * _sigmoid(o_pre) * _silu(z)


def _mlstm_prompt_kernel(q_ref, k_ref, v_ref, o_ref, z_ref, sm_ref, ifb, gn,
                         y_ref, c_ref, n_ref, m_ref, ct_ref, *, n_chunks):
    L = CHUNK

    @pl.when(pl.program_id(1) == 0)
    def _():
        ct_ref[...] = jnp.zeros_like(ct_ref)
        n_ref[...] = jnp.zeros_like(n_ref)
        m_ref[...] = jnp.zeros_like(m_ref)

    causal = _iota2((L, L), 0) <= _iota2((L, L), 1)
    eye = (_iota2((L, L), 0) == _iota2((L, L), 1)).astype(F32)
    scale = M_DK ** -0.5
    neg_inf = -jnp.inf
    C, H = range(n_chunks), range(M_HEADS)
    rws = [slice(c * L, (c + 1) * L) for c in C]
    sls = [slice(h * M_DK, (h + 1) * M_DK) for h in H]

    pre = sm_ref[...] + ifb[...]
    lf = _log_sigmoid(pre)
    b_all = _mm_exact_lhs(_block_tril(n_chunks * L, L), lf)
    lib_all = pre - pltpu.roll(b_all, LANES - (SM_MF - SM_MI), 1)
    sel = (_iota2((LANES, M_HEADS * LANES), 0)
           == SM_MI + _iota2((LANES, M_HEADS * LANES), 1) // LANES).astype(F32)
    lib_bc = _mm_exact_rhs(lib_all, sel)
    comb = jnp.where(_iota2((1, LANES), 1) < SM_MF, lib_all, b_all)
    q, k, v = q_ref[...].astype(F32) * scale, k_ref[...].astype(F32), v_ref[...].astype(F32)
    qs = [[q[rws[c], sls[h]] for h in H] for c in C]
    k_h = [[k[rws[c], sls[h]] for h in H] for c in C]
    v_h = [[v[rws[c], sls[h]] for h in H] for c in C]
    qk_raw = [[_mm_nt(k_h[c][h], qs[c][h]) for h in H] for c in C]
    d_log, d_max, b_row, lib_row = [], [], [], []
    for c in C:
        comb_t = comb[rws[c]].T
        b_row.append([comb_t[SM_MF + h:SM_MF + h + 1, :] for h in H])
        lib_row.append([comb_t[SM_MI + h:SM_MI + h + 1, :] for h in H])
        d_log.append([jnp.where(causal, lib_bc[rws[c], h * LANES:h * LANES + L] + b_row[c][h], neg_inf)
                      for h in H])
        d_max.append([jnp.max(d_log[c][h], axis=0, keepdims=True) for h in H])
    m_prev = [m_ref[0, :, h:h + 1] for h in H]
    m_t, w_inter, carry, w_state, qk = [], [], [], [], []
    for c in C:
        m_inter = [b_row[c][h] + m_prev[h] for h in H]
        m_t.append([jnp.maximum(m_inter[h], d_max[c][h]) for h in H])
        w_inter.append([jnp.exp(m_inter[h] - m_t[c][h]) for h in H])
        m_new = [m_t[c][h][:, L - 1:L] for h in H]
        b_last = [b_row[c][h][:, L - 1:L] for h in H]
        carry.append([jnp.exp(b_last[h] + m_prev[h] - m_new[h]) for h in H])
        w_state.append([jnp.exp(b_last[h] + lib_row[c][h] - m_new[h]) for h in H])
        qk.append([qk_raw[c][h] * jnp.exp(d_log[c][h] - m_t[c][h]) for h in H])
        m_prev = m_new
    for h in H:
        m_ref[0, :, h:h + 1] = m_prev[h]
    wk = [[_mm(eye * w_state[c][h], k_h[c][h]) for h in H] for c in C]
    n_upd = [[_mm(_rows8(w_state[c][h]), k_h[c][h])[0:1] for h in H] for c in C]
    upd = [[_mm_tn(v_h[c][h], wk[c][h]) for h in H] for c in C]
    qkv = [[_mm_tn(v_h[c][h], qk[c][h]) for h in H] for c in C]
    ct = [ct_ref[h] for h in H]
    ns = [n_ref[0, h:h + 1, :] for h in H]
    y_ch = []
    for c in C:
        qc = [_mm_nt(ct[h], qs[c][h]) for h in H]
        qn = [_mm_nt(_rows8(ns[h]), qs[c][h])[0:1] for h in H]
        yy = []
        for h in H:
            num = qkv[c][h] + w_inter[c][h] * qc[h]
            den = jnp.sum(qk[c][h], axis=0, keepdims=True) + w_inter[c][h] * qn[h]
            ht = num / jnp.maximum(jnp.abs(den), jnp.exp(-m_t[c][h]))
            ht = ht - jnp.mean(ht, axis=0, keepdims=True)
            yy.append((ht * lax.rsqrt(jnp.mean(ht * ht, axis=0, keepdims=True) + EPS)).T)
        y_ch.append(yy)
        ct = [carry[c][h] * ct[h] + upd[c][h] for h in H]
        ns = [carry[c][h] * ns[h] + n_upd[c][h] for h in H]
    for h in H:
        ct_ref[h] = ct[h]
        n_ref[0, h:h + 1, :] = ns[h]
    y = jnp.concatenate([jnp.concatenate([y_ch[c][h] for c in C], axis=0) for h in H], axis=-1)
    y_ref[...] = (y * gn[...] * _sigmoid(o_ref[...].astype(F32)) * _silu(z_ref[...].astype(F32))
                  ).astype(y_ref.dtype)

    @pl.when(pl.program_id(1) == pl.num_programs(1) - 1)
    def _():
        for h in H:
            c_ref[0, h] = ct_ref[h].T


def _mlstm_params(ifb, gn):
    ifb_p = jnp.zeros((1, LANES), F32).at[0, SM_MI:SM_MI + 2 * M_HEADS].set(ifb)
    return ifb_p, gn.reshape(1, -1)


def _mlstm_prompt(proj, side, params, batch, seq, tb):
    nt = seq // tb
    cb = lambda c, w: pl.BlockSpec((tb, w), lambda b, t, c=c: (b * nt + t, c // w))
    full = lambda a: pl.BlockSpec(a.shape, lambda b, t: (0,) * a.ndim)
    kern = functools.partial(_mlstm_prompt_kernel, n_chunks=tb // CHUNK)
    return pl.pallas_call(
        kern,
        grid=(batch, nt),
        in_specs=[cb(C_MQ, M_QK), cb(C_MK, M_QK), cb(C_MV, M_W), cb(C_MO, M_W), cb(C_MZ, M_W),
                  cb(S_SMALL, LANES)] + [full(a) for a in params],
        out_specs=[pl.BlockSpec((tb, M_W), lambda b, t: (b * nt + t, 0)),
                   pl.BlockSpec((1, M_HEADS, M_DK, M_DV), lambda b, t: (b, 0, 0, 0)),
                   pl.BlockSpec((1, M_HEADS, M_DK), lambda b, t: (b, 0, 0)),
                   pl.BlockSpec((1, 1, M_HEADS), lambda b, t: (b, 0, 0))],
        out_shape=[jax.ShapeDtypeStruct((batch * seq, M_W), BF16),
                   jax.ShapeDtypeStruct((batch, M_HEADS, M_DK, M_DV), F32),
                   jax.ShapeDtypeStruct((batch, M_HEADS, M_DK), F32),
                   jax.ShapeDtypeStruct((batch, 1, M_HEADS), F32)],
        scratch_shapes=[pltpu.VMEM((M_HEADS, M_DV, M_DK), F32)],
        compiler_params=_cparams(("parallel", "arbitrary")),
        name="mlstm_prompt",
    )(proj, proj, proj, proj, proj, side, *params)


def _mlstm_step_kernel(q_ref, k_ref, v_ref, o_ref, z_ref, sm_ref, c_in, n_in, m_in, acc_ref, ifb, gn,
                       y_ref, c_out, n_out, m_out, vec_ref, sc_ref, hrow_ref, *, bb):
    del acc_ref
    pre = sm_ref[...] + ifb[...]
    li = pre[:, SM_MI:SM_MI + M_HEADS]
    lf = _log_sigmoid(pre[:, SM_MF:SM_MF + M_HEADS])
    m0 = m_in[...]
    m_inter = lf + m0
    m_t = jnp.maximum(m_inter, li)
    m_out[...] = m_t
    pad = lambda x: jnp.concatenate([x, jnp.zeros((bb, LANES - M_HEADS), F32)], axis=-1)
    sc_ref[0] = pad(jnp.exp(m_inter - m_t))
    sc_ref[1] = pad(jnp.exp(li - m_t))
    sc_ref[2] = pad(jnp.exp(-m_t))
    vec_ref[0] = q_ref[...].astype(F32) * (M_DK ** -0.5)
    vec_ref[1] = k_ref[...].astype(F32)
    vec_ref[2] = v_ref[...].astype(F32)
    masks = _outer_masks()

    def per_rows(j, _):
        bs = [j * STEP_ROWS + i for i in range(STEP_ROWS)]
        vecs = [[vec_ref[i, pl.ds(b, 1), :] for i in range(3)] for b in bs]
        scs = [[sc_ref[i, pl.ds(b, 1), :] for i in range(3)] for b in bs]
        items = [(bi, h) for bi in range(STEP_ROWS) for h in range(M_HEADS)]
        n_it = range(len(items))
        hs = lambda x, h: x[:, h * M_DK:(h + 1) * M_DK]
        q_h, k_h, v_h = ([hs(vecs[bi][i], h) for bi, h in items] for i in range(3))
        wi, ei, en = ([scs[bi][i][:, h:h + 1] for bi, h in items] for i in range(3))
        c = [c_in[bs[bi], h] for bi, h in items]
        n = [n_in[bs[bi], h:h + 1, :] for bi, h in items]
        qk = [_lane_sum(q_h[i] * k_h[i]) * ei[i] for i in n_it]
        qn = [_lane_sum(q_h[i] * n[i]) for i in n_it]
        qc = [_mm(_rows8(q_h[i]), c[i])[0:1] for i in n_it]
        kv = [_mm_tn(*_outer_operands(masks, (ei[i] * k_h[i], v_h[i]))) for i in n_it]
        hh = []
        for i, (bi, h) in enumerate(items):
            num = qk[i] * v_h[i] + wi[i] * qc[i]
            den = qk[i] + wi[i] * qn[i]
            hh.append(num / jnp.maximum(jnp.abs(den), en[i]))
            c_out[bs[bi], h] = wi[i] * c[i] + kv[i]
            n_out[bs[bi], h:h + 1, :] = wi[i] * n[i] + ei[i] * k_h[i]
        for bi in range(STEP_ROWS):
            hrow_ref[pl.ds(bs[bi], 1), :] = jnp.concatenate(
                [hh[bi * M_HEADS + h] for h in range(M_HEADS)], axis=-1)
        return 0

    lax.fori_loop(0, bb // STEP_ROWS, per_rows, 0)
    hh = hrow_ref[...]
    h_heads = [hh[:, h * M_DV:(h + 1) * M_DV] for h in range(M_HEADS)]
    y_ref[...] = _mlstm_out(h_heads, o_ref[...].astype(F32), z_ref[...].astype(F32), gn[...]).astype(y_ref.dtype)


def _mlstm_step(proj, side, c_all, n_all, m_all, acc, params, layer, bb):
    nb = proj.shape[0]
    cb = lambda c, w: pl.BlockSpec((bb, w), lambda i, c=c: (i, c // w))
    full = lambda a: pl.BlockSpec(a.shape, lambda i: (0,) * a.ndim)
    cs = _layer_block(c_all.shape[2:], bb, layer)
    ns_in = _layer_block(n_all.shape[2:], bb, layer)
    ms_in = _layer_block(m_all.shape[2:], bb, layer)
    ns = pl.BlockSpec((bb, M_HEADS, M_DK), lambda i: (i, 0, 0))
    ms = pl.BlockSpec((bb, M_HEADS), lambda i: (i, 0))
    kern = functools.partial(_mlstm_step_kernel, bb=bb)
    return pl.pallas_call(
        kern,
        grid=(nb // bb,),
        in_specs=[cb(C_MQ, M_QK), cb(C_MK, M_QK), cb(C_MV, M_W), cb(C_MO, M_W), cb(C_MZ, M_W),
                  cb(S_SMALL, LANES), cs, ns_in, ms_in, pl.BlockSpec(memory_space=pl.ANY)]
                 + [full(a) for a in params],
        out_specs=[pl.BlockSpec((bb, M_W), lambda i: (i, 0)), cs, ns, ms],
        out_shape=[jax.ShapeDtypeStruct((nb, M_W), BF16), jax.ShapeDtypeStruct(acc.shape, F32),
                   jax.ShapeDtypeStruct(n_all.shape[1:], F32), jax.ShapeDtypeStruct(m_all.shape[1:], F32)],
        input_output_aliases={9: 1},
        scratch_shapes=[pltpu.VMEM((3, bb, M_QK), F32), pltpu.VMEM((3, bb, LANES), F32),
                        pltpu.VMEM((bb, M_W), F32)],
        compiler_params=_cparams(("parallel",)),
        name="mlstm_step",
    )(proj, proj, proj, proj, proj, side, c_all, n_all, m_all, acc, *params)


def _shift_rows(proj_rows, side_rows):
    return jnp.concatenate([proj_rows[:, C_RIN:C_RIN + 3 * R_W].astype(F32),
                            side_rows[:, S_RLORA:S_RLORA + 2 * R_LORA]], axis=-1)


def _pick(n, prefs):
    for p in prefs:
        if n % p == 0:
            return p
    return n


def kernel(x_prompt, x_sample, state_rwkv_shift, state_rwkv, state_gla, state_mlstm_C, state_mlstm_n,
           state_mlstm_m, norm_pre, norm_post, w_in, r_mu_shift, r_w_lora_b, r_w0, r_a_lora_b, r_a0,
           r_k_k, r_k_a, r_r_k, r_gn, g_a_lora_b, g_a_bias, g_gn, m_if_bias, m_gn, w_br_rwkv, w_br_gla,
           w_br_mlstm, w_out):
    depth = w_in.shape[0]
    bp, tp, _ = x_prompt.shape
    bs, ts, _ = x_sample.shape
    assert ts == 1 and tp % CHUNK == 0
    assert w_in.shape[1:] == (D_MODEL, N_IN)
    w_t = jnp.transpose(w_in, (0, 2, 1))
    wr, wg, wm, wo = (w.astype(BF16) for w in (w_br_rwkv, w_br_gla, w_br_mlstm, w_out))

    mp = bp * tp
    tm_in = _pick(mp, (2048, 1024, 512, 256, 128, 64))
    tm_mg = _pick(mp, (512, 256, 128, 64))
    tb = _pick(tp, (256, 128, 64))
    bb = _pick(bs, (16, 8))

    xp = x_prompt.reshape(mp, D_MODEL)
    xs = x_sample.reshape(bs, D_MODEL)
    new_p = [[] for _ in range(6)]
    new_s = [[] for _ in range(3)]
    s_lanes = jnp.transpose(state_rwkv, (0, 2, 3, 4, 1))
    acc_r, acc_g, acc_c = (jnp.zeros(s.shape, F32) for s in (s_lanes, state_gla, state_mlstm_C))
    for l in range(depth):
        g_pre = norm_pre[l].reshape(1, -1)
        g_post = norm_post[l].reshape(1, -1)
        rp = _rwkv_params(r_mu_shift[l], r_w_lora_b[l], r_w0[l], r_a_lora_b[l], r_a0[l], r_k_k[l],
                          r_k_a[l], r_r_k[l], r_gn[l])
        gp = _gla_params(g_a_lora_b[l], g_a_bias[l], g_gn[l])
        mparams = _mlstm_params(m_if_bias[l], m_gn[l])

        proj, side = _inproj(xp, g_pre, w_t, l, tm_in)
        y_r, s_r = _rwkv_prompt(proj, side, rp, bp, tp, tb)
        y_g, s_g = _gla_prompt(proj, side, gp, bp, tp, tb)
        y_m, c, n, m = _mlstm_prompt(proj, side, mparams, bp, tp, tb)
        xp = _merge(proj, y_r, y_g, y_m, xp, wr, wg, wm, wo, g_post, l, tm_mg)
        shift = _shift_rows(proj.reshape(bp, tp, C_RLORA)[:, tp - 1], side.reshape(bp, tp, SIDE_W)[:, tp - 1])
        for lst, val in zip(new_p, (shift, s_r, s_g, c, n, m.reshape(bp, M_HEADS))):
            lst.append(val)

        proj, side = _inproj(xs, g_pre, w_t, l, bs)
        col = lambda x: jnp.broadcast_to(x[:, None], (R_W, LANES))
        y_t, acc_r = _rwkv_step_lanes(proj, side, state_rwkv_shift, s_lanes, acc_r,
                                      rp[:10] + (col(r_r_k[l]), col(r_gn[l])), l)
        y_r = jnp.transpose(y_t, (2, 0, 1)).reshape(bs, R_W)
        y_g, acc_g = _gla_step(proj, side, state_gla, acc_g, gp, l, bb)
        y_m, acc_c, n, m = _mlstm_step(proj, side, state_mlstm_C, state_mlstm_n, state_mlstm_m, acc_c,
                                       mparams, l, bb)
        xs = _merge(proj, y_r, y_g, y_m, xs, wr, wg, wm, wo, g_post, l, bs)
        shift = _shift_rows(proj, side)
        for lst, val in zip(new_s, (shift, n, m)):
            lst.append(val)

    stk = lambda lst: jnp.stack(lst)
    return (xp.reshape(bp, tp, D_MODEL), xs.reshape(bs, ts, D_MODEL),
            *(stk(v) for v in new_p),
            stk(new_s[0]), jnp.transpose(acc_r, (0, 4, 1, 2, 3)), acc_g, acc_c, stk(new_s[1]), stk(new_s[2]))
```

```python
import functools
import math

import jax
import jax.numpy as jnp
from jax import lax
from jax.experimental import pallas as pl
from jax.experimental.pallas import tpu as pltpu

F32 = jnp.float32
BF16 = jnp.bfloat16

D_MODEL = 1024
R_HEADS, R_HD = 8, 64
R_W = R_HEADS * R_HD
R_LORA = 64
R_SHIFT_W = 3 * R_W + 2 * R_LORA
R_GN_EPS = 64e-5
G_HEADS, G_DK, G_DV = 4, 64, 128
G_QK = G_HEADS * G_DK
G_W = G_HEADS * G_DV
G_LORA = 16
G_GATE_TEMP = 16.0
M_HEADS, M_DK, M_DV = 4, 128, 128
M_QK = M_HEADS * M_DK
M_W = M_HEADS * M_DV
EPS = 1e-6

LANES = 128
SUBLANES = 8
VMEM_LIMIT = 48 * 1024 * 1024

C_GATE = 0
C_RZ = 3072
C_GV = 3584
C_GZ = 4096
C_MQ = 4608
C_MK = 5120
C_MV = 5632
C_MO = 6144
C_MZ = 6656
C_GQK = 7168
C_RIN = 7680
C_RLORA = C_RIN + 3 * R_W
C_SMALL = C_RLORA + LANES
SM_GA, SM_MI, SM_MF = 0, 16, 20
N_USED = C_SMALL + LANES
PROJ_TN = 512
N_PAD = -(-N_USED // PROJ_TN) * PROJ_TN
N_MAIN_BLOCKS = C_RLORA // PROJ_TN
_W_SEGMENTS = (("r_in", R_SHIFT_W), ("r_z", R_W), ("g_q", G_QK), ("g_k", G_QK), ("g_v", G_W), ("g_a", G_LORA),
               ("g_z", G_W), ("m_q", M_QK), ("m_k", M_QK), ("m_v", M_W), ("m_i", M_HEADS), ("m_f", M_HEADS),
               ("m_o", M_W), ("m_z", M_W), ("gate", 3 * D_MODEL))
W_OFF = {}
_off = 0
for _name, _width in _W_SEGMENTS:
    W_OFF[_name] = _off
    _off += _width
N_IN = _off
W_MAIN_ROWS = tuple(W_OFF[name] + PROJ_TN * b
                    for name, nblk in (("gate", 6), ("r_z", 1), ("g_v", 1), ("g_z", 1), ("m_q", 1), ("m_k", 1),
                                       ("m_v", 1), ("m_o", 1), ("m_z", 1), ("g_q", 1), ("r_in", 3))
                    for b in range(nblk))
assert len(W_MAIN_ROWS) == N_MAIN_BLOCKS and all(r % SUBLANES == 0 for r in W_MAIN_ROWS)
SIDE_W = 2 * LANES
S_RLORA, S_SMALL = 0, LANES

CHUNK = 64


def _cparams(sem):
    return pltpu.CompilerParams(dimension_semantics=sem, vmem_limit_bytes=VMEM_LIMIT)


def _mm(a, b):
    return jnp.dot(a.astype(BF16), b.astype(BF16), preferred_element_type=F32)


def _mm_nt(a, b):
    return lax.dot_general(a.astype(BF16), b.astype(BF16), (((1,), (1,)), ((), ())),
                           preferred_element_type=F32)


def _mm_tn(a, b):
    return lax.dot_general(a.astype(BF16), b.astype(BF16), (((0,), (0,)), ((), ())),
                           preferred_element_type=F32)


def _mm_exact_lhs(m01, x):
    hi = x.astype(BF16)
    r1 = x - hi.astype(F32)
    mid = r1.astype(BF16)
    lo = (r1 - mid.astype(F32)).astype(BF16)
    m = m01.astype(BF16)
    d = lambda p: jnp.dot(m, p, preferred_element_type=F32)
    return d(hi) + d(mid) + d(lo)


def _mm_exact_rhs(x, m01):
    hi = x.astype(BF16)
    r1 = x - hi.astype(F32)
    mid = r1.astype(BF16)
    lo = (r1 - mid.astype(F32)).astype(BF16)
    m = m01.astype(BF16)
    d = lambda p: jnp.dot(p, m, preferred_element_type=F32)
    return d(hi) + d(mid) + d(lo)


def _sigmoid(x):
    return 1.0 / (1.0 + jnp.exp(-x))


def _silu(x):
    return x * _sigmoid(x)


def _softplus(x):
    return jnp.maximum(x, 0.0) + jnp.log(1.0 + jnp.exp(-jnp.abs(x)))


def _log_sigmoid(x):
    return -_softplus(-x)


def _iota2(shape, dim):
    return lax.broadcasted_iota(jnp.int32, shape, dim)


def _lane_sum(x):
    return jnp.sum(x, axis=-1, keepdims=True)


def _rows8(x):
    return jnp.broadcast_to(x, (8, x.shape[1]))


def _outer_masks():
    rid = _iota2((8, 1), 0)
    on = lambda *rows: sum((rid == r).astype(F32) for r in rows)
    return (on(0, 2), on(1), on(0, 1), on(2)), (on(3, 5), on(4), on(3, 4), on(5))


def _outer_operands(masks, *pairs):
    a_op, b_op = 0.0, 0.0
    for (a, b), (m_ahi, m_alo, m_bhi, m_blo) in zip(pairs, masks):
        a_hi = a.astype(BF16).astype(F32)
        b_hi = b.astype(BF16).astype(F32)
        a_op = a_op + m_ahi * a_hi + m_alo * (a - a_hi)
        b_op = b_op + m_bhi * b_hi + m_blo * (b - b_hi)
    return a_op, b_op


def _block_tril(n, blk):
    r, c = _iota2((n, n), 0), _iota2((n, n), 1)
    return ((r - c).astype(jnp.uint32) <= (r % blk).astype(jnp.uint32)).astype(F32)


def _inproj_kernel(rows_ref, x_ref, xs_ref, g_ref, w_ref, wl_ref, wa_ref, wif_ref,
                   o_ref, side_ref, os_ref, sides_ref, h_ref, hs_ref, ws_ref):
    del rows_ref
    i, j = pl.program_id(0), pl.program_id(1)

    def normed(x):
        return (x * lax.rsqrt(jnp.mean(x * x, axis=-1, keepdims=True) + EPS) * g_ref[...]).astype(BF16)

    @pl.when(j == 0)
    def _():
        h_ref[...] = normed(x_ref[...])
        ws_ref[...] = jnp.zeros_like(ws_ref)
        ws_ref[S_RLORA:S_RLORA + 2 * R_LORA, :] = wl_ref[0]
        ws_ref[S_SMALL + SM_GA:S_SMALL + SM_GA + G_LORA, :] = wa_ref[0]
        ws_ref[S_SMALL + SM_MI:S_SMALL + SM_MI + 2 * M_HEADS, :] = wif_ref[0]
        side_ref[...] = _mm_nt(h_ref[...], ws_ref[...])

    @pl.when(jnp.logical_and(i == 0, j == 0))
    def _():
        hs_ref[...] = normed(xs_ref[...])
        sides_ref[...] = _mm_nt(hs_ref[...], ws_ref[...])

    w = w_ref[0].astype(BF16)
    o_ref[...] = _mm_nt(h_ref[...], w).astype(BF16)

    @pl.when(i == 0)
    def _():
        os_ref[...] = _mm_nt(hs_ref[...], w).astype(BF16)


def _inproj(x2d, xs2d, g_pre, w_t, layer, tm):
    m, ms = x2d.shape[0], xs2d.shape[0]
    w_blk = lambda n: (pl.Element(1), pl.Element(n), pl.Element(D_MODEL))
    w_rows = lambda n, start: pl.BlockSpec(w_blk(n), lambda i, j, rows, start=start: (layer, start, 0))
    last = N_MAIN_BLOCKS - 1
    grid_spec = pltpu.PrefetchScalarGridSpec(
        num_scalar_prefetch=1,
        grid=(m // tm, N_MAIN_BLOCKS),
        in_specs=[pl.BlockSpec((tm, D_MODEL), lambda i, j, rows: (i, 0)),
                  pl.BlockSpec((ms, D_MODEL), lambda i, j, rows: (0, 0)),
                  pl.BlockSpec((1, D_MODEL), lambda i, j, rows: (0, 0)),
                  pl.BlockSpec(w_blk(PROJ_TN), lambda i, j, rows: (layer, rows[j] * SUBLANES, 0)),
                  w_rows(2 * R_LORA, W_OFF["r_in"] + 3 * R_W),
                  w_rows(G_LORA, W_OFF["g_a"]),
                  w_rows(2 * M_HEADS, W_OFF["m_i"])],
        out_specs=[pl.BlockSpec((tm, PROJ_TN), lambda i, j, rows: (i, j)),
                   pl.BlockSpec((tm, SIDE_W), lambda i, j, rows: (i, 0)),
                   pl.BlockSpec((ms, PROJ_TN), lambda i, j, rows: (0, jnp.where(i == 0, j, last))),
                   pl.BlockSpec((ms, SIDE_W), lambda i, j, rows: (0, 0))],
        scratch_shapes=[pltpu.VMEM((tm, D_MODEL), BF16), pltpu.VMEM((ms, D_MODEL), BF16),
                        pltpu.VMEM((SIDE_W, D_MODEL), F32)])
    return pl.pallas_call(
        _inproj_kernel,
        grid_spec=grid_spec,
        out_shape=[jax.ShapeDtypeStruct((m, C_RLORA), BF16), jax.ShapeDtypeStruct((m, SIDE_W), F32),
                   jax.ShapeDtypeStruct((ms, C_RLORA), BF16), jax.ShapeDtypeStruct((ms, SIDE_W), F32)],
        compiler_params=_cparams(("arbitrary", "arbitrary")),
        name="inproj",
    )(jnp.asarray([r // SUBLANES for r in W_MAIN_ROWS], jnp.int32), x2d, xs2d, g_pre, w_t, w_t, w_t, w_t)


def _merge_kernel(gate_ref, yr_ref, yg_ref, ym_ref, x_ref, wr_ref, wg_ref, wm_ref, wo_ref, gp_ref, o_ref):
    d = D_MODEL
    gate = lambda b: _sigmoid(gate_ref[:, b * d:(b + 1) * d].astype(F32))
    merged = (gate(0) * _mm(yr_ref[...], wr_ref[...]) + gate(1) * _mm(yg_ref[...], wg_ref[...])
              + gate(2) * _mm(ym_ref[...], wm_ref[...]))
    o = _mm(merged, wo_ref[...])
    y = o * lax.rsqrt(jnp.mean(o * o, axis=-1, keepdims=True) + EPS)
    o_ref[...] = x_ref[...] + y * gp_ref[...]


def _merge(proj, y_r, y_g, y_m, x2d, w_r, w_g, w_m, w_o, g_post, layer, tm):
    m = x2d.shape[0]
    row = lambda w: pl.BlockSpec((tm, w), lambda i: (i, 0))
    full = lambda a: pl.BlockSpec(a.shape, lambda i: (0, 0))
    wl = lambda a: pl.BlockSpec((None,) + a.shape[1:], lambda i: (layer, 0, 0))
    return pl.pallas_call(
        _merge_kernel,
        grid=(m // tm,),
        in_specs=[row(3 * D_MODEL), row(R_W), row(G_W), row(M_W), row(D_MODEL),
                  wl(w_r), wl(w_g), wl(w_m), wl(w_o), full(g_post)],
        out_specs=row(D_MODEL),
        out_shape=jax.ShapeDtypeStruct((m, D_MODEL), F32),
        compiler_params=_cparams(("parallel",)),
        name="merge",
    )(proj, y_r, y_g, y_m, x2d, w_r, w_g, w_m, w_o, g_post)


def _rwkv_front(r_in, k_in, v_in, l_in, prev_r, prev_k, prev_v, prev_l, p):
    (mu_r, mu_k, mu_v, mu_l, wb, w0, ab, a0, k_k, k_a) = p
    r = r_in + mu_r * (prev_r - r_in)
    k = k_in + mu_k * (prev_k - k_in)
    v = v_in + mu_v * (prev_v - v_in)
    lo = l_in + mu_l * (prev_l - l_in)
    log_w = -_softplus(-(w0 + _mm(jnp.tanh(lo), wb))) - 0.5
    lw = -jnp.exp(log_w)
    a = _sigmoid(a0 + _mm(lo, ab))
    kk = k * k_k
    k2 = k * (1.0 + (a - 1.0) * k_a)
    return r, k2, v, lw, a, kk


def _rwkv_prompt_kernel(z_ref, r_ref, k_ref, v_ref, l_ref,
                        mu_r, mu_k, mu_v, mu_l, wb, w0, ab, a0, k_k, k_a, r_k, gn,
                        y_ref, s_ref, carry_ref, sp_ref, *, n_chunks):
    L = CHUNK

    @pl.when(pl.program_id(1) == 0)
    def _():
        sp_ref[...] = jnp.zeros_like(sp_ref)
        carry_ref[...] = jnp.zeros_like(carry_ref)

    tb = n_chunks * L
    params = tuple(x[...] for x in (mu_r, mu_k, mu_v, mu_l, wb, w0, ab, a0, k_k, k_a))
    assert L == R_HD
    m_a = (_iota2((1, LANES), 1) < R_HD).astype(F32)
    m_b = 1.0 - m_a
    m_a16, m_b16 = m_a.astype(BF16), m_b.astype(BF16)
    trow = _iota2((L, LANES), 0)
    tcol = _iota2((L, LANES), 1) % L
    eye = (tcol == trow).astype(F32)
    blk_mask = (trow // 16) == (tcol // 16)
    r2 = _iota2((2 * L, LANES), 0)
    g_mask = (_iota2((2 * L, LANES), 1) % L) < (r2 % L) + (r2 >= L).astype(jnp.int32)
    same_head = (_iota2((LANES, LANES), 0) // R_HD) == (_iota2((LANES, LANES), 1) // R_HD)
    first_row = _iota2((tb, 1), 0) == 0

    def seg_sum(x):
        return _lane_sum(x * m_a) * m_a + _lane_sum(x * m_b) * m_b

    def bd(y):
        yb = y.astype(BF16)
        return jnp.concatenate([yb * m_a16, yb * m_b16], axis=0)

    def shifted(x, carry):
        return jnp.where(first_row, carry, pltpu.roll(x, 1, 0))

    r_in, k_in, v_in = (x[...].astype(F32) for x in (r_ref, k_ref, v_ref))
    l_in = l_ref[...]
    prev_r = shifted(r_in, carry_ref[0:1, 0:R_W])
    prev_k = shifted(k_in, carry_ref[0:1, R_W:2 * R_W])
    prev_v = shifted(v_in, carry_ref[0:1, 2 * R_W:3 * R_W])
    prev_l = shifted(l_in, carry_ref[0:1, 3 * R_W:R_SHIFT_W])
    carry_ref[0:1, 0:R_W] = r_in[tb - 1:tb, :]
    carry_ref[0:1, R_W:2 * R_W] = k_in[tb - 1:tb, :]
    carry_ref[0:1, 2 * R_W:3 * R_W] = v_in[tb - 1:tb, :]
    carry_ref[0:1, 3 * R_W:R_SHIFT_W] = l_in[tb - 1:tb, :]
    r, k2, v, lw, a, kk = _rwkv_front(r_in, k_in, v_in, l_in, prev_r, prev_k, prev_v, prev_l, params)

    cum = _mm_exact_lhs(_block_tril(tb, L), lw)
    p_inc = jnp.exp(cum)
    p_exc = jnp.exp(cum - lw)
    p_inv = jnp.exp(-cum)
    r_t = r * p_inc
    k_h = k2 * p_inv

    n_pairs = R_HEADS // 2
    CP = [(c, p) for c in range(n_chunks) for p in range(n_pairs)]
    N = range(len(CP))
    kap_t, beta_h = [], []
    for p in range(n_pairs):
        sl = slice(p * LANES, (p + 1) * LANES)
        kk_p = kk[:, sl]
        kap = kk_p * (1.0 / jnp.maximum(jnp.sqrt(seg_sum(kk_p * kk_p)), 1e-12))
        beta_h.append(kap * a[:, sl] * p_inv[:, sl])
        kap_t.append(kap * p_exc[:, sl])
    cut = lambda x, c, p: x[c * L:(c + 1) * L, p * LANES:(p + 1) * LANES]
    kap_t = [kap_t[p][c * L:(c + 1) * L] for c, p in CP]
    beta_h = [beta_h[p][c * L:(c + 1) * L] for c, p in CP]
    rt_c = [cut(r_t, c, p) for c, p in CP]
    kh_c = [cut(k_h, c, p) for c, p in CP]
    v_c = [cut(v, c, p) for c, p in CP]
    p_last = [p_inc[(c + 1) * L - 1:(c + 1) * L, p * LANES:(p + 1) * LANES] for c, p in CP]
    bhl = [beta_h[i] * p_last[i] for i in N]
    khl = [kh_c[i] * p_last[i] for i in N]
    lhs2 = [jnp.concatenate([kap_t[i], rt_c[i]], axis=0) for i in N]
    bd_v = [bd(v_c[i]) for i in N]
    gb = [jnp.where(g_mask, _mm_nt(lhs2[i], bd(beta_h[i])), 0.0) for i in N]
    gk = [jnp.where(g_mask, _mm_nt(lhs2[i], bd(kh_c[i])), 0.0) for i in N]
    a_m = [gb[i][0:L] for i in N]
    d = [jnp.where(blk_mask, a_m[i], 0.0) for i in N]
    akv = [_mm(gk[i][0:L], bd_v[i]) for i in N]
    d2 = [_mm(d[i], bd(d[i])) for i in N]
    bd_d2 = [bd(d2[i]) for i in N]
    d4 = [_mm(d2[i], bd_d2[i]) for i in N]
    dd2 = [_mm(d[i], bd_d2[i]) for i in N]
    d8 = [_mm(d4[i], bd(d4[i])) for i in N]
    x = [eye - d[i] + d2[i] - dd2[i] for i in N]
    x = [x[i] + _mm(x[i], bd(d4[i])) for i in N]
    td = [x[i] + _mm(x[i], bd(d8[i])) for i in N]
    m = [_mm(td[i], bd(a_m[i] - d[i])) for i in N]
    m2 = [_mm(m[i], bd(m[i])) for i in N]
    q = [eye - m[i] + m2[i] - _mm(m[i], bd(m2[i])) for i in N]
    t_inv = [_mm(q[i], bd(td[i])) for i in N]
    kbar = [_mm(t_inv[i], bd(kap_t[i])) for i in N]
    u0 = [-_mm(t_inv[i], bd(akv[i])) for i in N]
    n_bd = [jnp.where(same_head, _mm_tn(kbar[i], bhl[i]), 0.0) for i in N]
    cz = [_mm_tn(jnp.concatenate([u0[i], v_c[i]], axis=0), jnp.concatenate([bhl[i], khl[i]], axis=0))
          for i in N]
    c_c = [cz[i][0:R_HD] * m_a + cz[i][R_HD:2 * R_HD] * m_b for i in N]
    qt = [rt_c[i] - _mm(gb[i][L:2 * L], bd(kbar[i])) for i in N]
    y0 = [_mm(gb[i][L:2 * L], bd(u0[i])) + _mm(gk[i][L:2 * L], bd_v[i]) for i in N]
    s = [sp_ref[p] for p in range(n_pairs)]
    y_ch = []
    for c in range(n_chunks):
        ids = [c * n_pairs + p for p in range(n_pairs)]
        y_ch.append([_mm_nt(qt[i], bd(s[p])) + y0[i] for p, i in enumerate(ids)])
        s = [s[p] * p_last[i] - _mm(s[p], n_bd[i]) + c_c[i] for p, i in enumerate(ids)]
    for p in range(n_pairs):
        sp_ref[p] = s[p]
    y_pairs = [jnp.concatenate([y_ch[c][p] for c in range(n_chunks)], axis=0) for p in range(n_pairs)]
    outs = []
    for p in range(n_pairs):
        sl = slice(p * LANES, (p + 1) * LANES)
        y = y_pairs[p]
        y = y - seg_sum(y) * (1.0 / R_HD)
        y = y * lax.rsqrt(seg_sum(y * y) * (1.0 / R_HD) + R_GN_EPS)
        bonus = seg_sum(r[:, sl] * k2[:, sl] * r_k[:, sl]) * v[:, sl]
        outs.append(y * gn[:, sl] + bonus)
    y_ref[...] = (jnp.concatenate(outs, axis=-1) * _silu(z_ref[...].astype(F32))).astype(y_ref.dtype)

    @pl.when(pl.program_id(1) == pl.num_programs(1) - 1)
    def _():
        for p in range(n_pairs):
            sp = sp_ref[p]
            s_ref[0, 2 * p] = sp[:, 0:R_HD]
            s_ref[0, 2 * p + 1] = sp[:, R_HD:2 * R_HD]


def _rwkv_params(mu, wb, w0, ab, a0, k_k, k_a, r_k, gn):
    row = lambda x: x.reshape(1, -1)
    zeros = jnp.zeros((R_LORA, R_W), F32)
    wb_p = jnp.concatenate([wb, zeros], axis=0).astype(BF16)
    ab_p = jnp.concatenate([zeros, ab], axis=0).astype(BF16)
    return (row(mu[0:R_W]), row(mu[R_W:2 * R_W]), row(mu[2 * R_W:3 * R_W]), row(mu[3 * R_W:]),
            wb_p, row(w0), ab_p, row(a0), row(k_k), row(k_a), row(r_k), row(gn))


def _rwkv_prompt(proj, side, params, batch, seq, tb):
    nt = seq // tb
    cb = lambda c, w: pl.BlockSpec((tb, w), lambda b, t, c=c: (b * nt + t, c // w))
    full = lambda a: pl.BlockSpec(a.shape, lambda b, t: (0,) * a.ndim)
    kern = functools.partial(_rwkv_prompt_kernel, n_chunks=tb // CHUNK)
    return pl.pallas_call(
        kern,
        grid=(batch, nt),
        in_specs=[cb(C_RZ, R_W), cb(C_RIN, R_W), cb(C_RIN + R_W, R_W), cb(C_RIN + 2 * R_W, R_W),
                  cb(S_RLORA, LANES)] + [full(a) for a in params],
        out_specs=[pl.BlockSpec((tb, R_W), lambda b, t: (b * nt + t, 0)),
                   pl.BlockSpec((1, R_HEADS, R_HD, R_HD), lambda b, t: (b, 0, 0, 0))],
        out_shape=[jax.ShapeDtypeStruct((batch * seq, R_W), BF16),
                   jax.ShapeDtypeStruct((batch, R_HEADS, R_HD, R_HD), F32)],
        scratch_shapes=[pltpu.VMEM((8, R_SHIFT_W), F32),
                        pltpu.VMEM((R_HEADS // 2, R_HD, LANES), F32)],
        compiler_params=_cparams(("parallel", "arbitrary")),
        name="rwkv_prompt",
    )(proj, proj, proj, proj, side, *params)


STEP_ROWS = 4
GLA_STEP_ROWS = 2


def _rwkv_lanes_kernel(z_ref, r_ref, k_ref, v_ref, l_ref, pr_ref, pk_ref, pv_ref, pl_ref, s_in, acc_ref,
                       mu_r, mu_k, mu_v, mu_l, wb, w0, ab, a0, k_k, k_a, rk_t, gn_t,
                       y_ref, s_out, vec_ref, ycol_ref):
    del acc_ref
    h = pl.program_id(0)

    @pl.when(h == 0)
    def _():
        params = tuple(x[...] for x in (mu_r, mu_k, mu_v, mu_l, wb, w0, ab, a0, k_k, k_a))
        r, k2, v, lw, a, kk = _rwkv_front(r_ref[...].astype(F32), k_ref[...].astype(F32),
                                          v_ref[...].astype(F32), l_ref[...], pr_ref[...], pk_ref[...],
                                          pv_ref[...], pl_ref[...], params)
        kaps, betas = [], []
        for hh in range(R_HEADS):
            sl = slice(hh * R_HD, (hh + 1) * R_HD)
            kk_h = kk[:, sl]
            kap = kk_h / jnp.maximum(jnp.sqrt(_lane_sum(kk_h * kk_h)), 1e-12)
            kaps.append(kap)
            betas.append(kap * a[:, sl])
        rows = (jnp.concatenate(kaps, axis=-1), jnp.concatenate(betas, axis=-1), jnp.exp(lw), k2, v, r,
                _silu(z_ref[...].astype(F32)))
        for i, x in enumerate(rows):
            vec_ref[i] = x.T

    base = pl.multiple_of(h * R_HD, R_HD)
    chan = pl.ds(base, R_HD)
    kap_t, beta_t, w_t, k_t, v_t, r_t, gz_t = (vec_ref[i, chan, :] for i in range(7))

    def per_v(vi, _):
        s = s_in[vi]
        sa = -jnp.sum(s * kap_t, axis=0, keepdims=True)
        v_row = vec_ref[4, pl.ds(base + vi, 1), :]
        s_new = s * w_t + sa * beta_t + v_row * k_t
        s_out[vi] = s_new
        ycol_ref[pl.ds(vi, 1), :] = jnp.sum(s_new * r_t, axis=0, keepdims=True)
        return 0

    lax.fori_loop(0, R_HD, per_v, 0, unroll=8)
    y = ycol_ref[...]
    y = y - jnp.mean(y, axis=0, keepdims=True)
    y = y * lax.rsqrt(jnp.mean(y * y, axis=0, keepdims=True) + R_GN_EPS)
    bonus = jnp.sum(r_t * k_t * rk_t[chan, :], axis=0, keepdims=True) * v_t
    y_ref[...] = ((y * gn_t[chan, :] + bonus) * gz_t).astype(y_ref.dtype)


def _rwkv_step_lanes(proj, side, shift_all, s_lanes, acc, params, layer):
    nb = proj.shape[0]
    assert nb == LANES
    cb = lambda c, w: pl.BlockSpec((nb, w), lambda h, c=c: (0, c // w))
    sh = lambda c, w: pl.BlockSpec((None, nb, w), lambda h, c=c: (layer, 0, c // w))
    full = lambda a: pl.BlockSpec(a.shape, lambda h: (0,) * a.ndim)
    st = pl.BlockSpec((None, None, R_HD, R_HD, nb), lambda h: (layer, h, 0, 0, 0))
    return pl.pallas_call(
        _rwkv_lanes_kernel,
        grid=(R_HEADS,),
        in_specs=[cb(C_RZ, R_W), cb(C_RIN, R_W), cb(C_RIN + R_W, R_W), cb(C_RIN + 2 * R_W, R_W),
                  cb(S_RLORA, LANES),
                  sh(0, R_W), sh(R_W, R_W), sh(2 * R_W, R_W), sh(3 * R_W, LANES), st,
                  pl.BlockSpec(memory_space=pl.ANY)]
                 + [full(a) for a in params],
        out_specs=[pl.BlockSpec((None, R_HD, nb), lambda h: (h, 0, 0)), st],
        out_shape=[jax.ShapeDtypeStruct((R_HEADS, R_HD, nb), BF16),
                   jax.ShapeDtypeStruct(acc.shape, F32)],
        input_output_aliases={10: 1},
        scratch_shapes=[pltpu.VMEM((7, R_W, nb), F32), pltpu.VMEM((R_HD, nb), F32)],
        compiler_params=_cparams(("arbitrary",)),
        name="rwkv_step",
    )(proj, proj, proj, proj, side, shift_all, shift_all, shift_all, shift_all, s_lanes, acc, *params)


def _layer_block(shape_tail, bb, layer):
    zeros = (0,) * len(shape_tail)
    return pl.BlockSpec((None, bb) + tuple(shape_tail), lambda i: (layer, i) + zeros)


def _gla_log_gate(small, ab_p, a_bias):
    return _log_sigmoid(_mm(small, ab_p) + a_bias) / G_GATE_TEMP


def _gla_out(o_heads, z, gn):
    outs = []
    for h in range(G_HEADS):
        o = o_heads[h]
        outs.append(o * lax.rsqrt(jnp.mean(o * o, axis=-1, keepdims=True) + EPS))
    return jnp.concatenate(outs, axis=-1) * gn * _silu(z)


def _gla_prompt_kernel(qk_ref, v_ref, z_ref, sm_ref, ab, a_bias, gn, y_ref, s_ref, st_ref, *, n_chunks):
    L = CHUNK
    last = pl.num_programs(1) - 1

    @pl.when(pl.program_id(1) == 0)
    def _():
        st_ref[...] = jnp.zeros_like(st_ref)

    tril_incl = _iota2((L, L), 1) <= _iota2((L, L), 0)
    scale = G_DK ** -0.5
    C, H = range(n_chunks), range(G_HEADS)
    rws = [slice(c * L, (c + 1) * L) for c in C]
    sks = [slice(h * G_DK, (h + 1) * G_DK) for h in H]

    qk = qk_ref[...].astype(F32)
    q, k = qk[:, 0:G_QK] * scale, qk[:, G_QK:2 * G_QK]
    v = v_ref[...].astype(F32)
    la = _gla_log_gate(sm_ref[...], ab[...], a_bias[...])
    cum = _mm_exact_lhs(_block_tril(n_chunks * L, L), la)
    q0, qe, ke, kl, e_last = [], [], [], [], []
    for c in C:
        cum_c = cum[rws[c]]
        ref_row = cum_c[L // 2:L // 2 + 1, :]
        last_row = cum_c[L - 1:L, :]
        q0.append(q[rws[c]] * jnp.exp(cum_c))
        qe.append(q[rws[c]] * jnp.exp(cum_c - ref_row))
        ke.append(k[rws[c]] * jnp.exp(ref_row - cum_c))
        kl.append(k[rws[c]] * jnp.exp(last_row - cum_c))
        e_last.append(jnp.exp(last_row))
    v_h = [[v[rws[c], h * G_DV:(h + 1) * G_DV] for h in H] for c in C]
    att = [[jnp.where(tril_incl, _mm_nt(qe[c][:, sks[h]], ke[c][:, sks[h]]), 0.0) for h in H] for c in C]
    upd = [[_mm_tn(v_h[c][h], kl[c][:, sks[h]]) for h in H] for c in C]
    av = [[_mm(att[c][h], v_h[c][h]) for h in H] for c in C]
    st = [st_ref[h] for h in H]
    o_ch = []
    for c in C:
        o_ch.append([av[c][h] + _mm_nt(q0[c][:, sks[h]], st[h]) for h in H])
        st = [st[h] * e_last[c][:, sks[h]] + upd[c][h] for h in H]
    for h in H:
        st_ref[h] = st[h]
    o_heads = [jnp.concatenate([o_ch[c][h] for c in C], axis=0) for h in H]
    y_ref[...] = _gla_out(o_heads, z_ref[...].astype(F32), gn[...]).astype(y_ref.dtype)

    @pl.when(pl.program_id(1) == last)
    def _():
        for h in range(G_HEADS):
            s_ref[0, h] = st_ref[h].T


def _gla_params(ab, a_bias, gn):
    ab_p = jnp.zeros((LANES, G_QK), F32).at[SM_GA:SM_GA + G_LORA].set(ab).astype(BF16)
    return ab_p, a_bias.reshape(1, -1), gn.reshape(1, -1)


def _gla_prompt(proj, side, params, batch, seq, tb):
    nt = seq // tb
    cb = lambda c, w: pl.BlockSpec((tb, w), lambda b, t, c=c: (b * nt + t, c // w))
    full = lambda a: pl.BlockSpec(a.shape, lambda b, t: (0,) * a.ndim)
    kern = functools.partial(_gla_prompt_kernel, n_chunks=tb // CHUNK)
    return pl.pallas_call(
        kern,
        grid=(batch, nt),
        in_specs=[cb(C_GQK, 2 * G_QK), cb(C_GV, G_W), cb(C_GZ, G_W), cb(S_SMALL, LANES)]
                 + [full(a) for a in params],
        out_specs=[pl.BlockSpec((tb, G_W), lambda b, t: (b * nt + t, 0)),
                   pl.BlockSpec((1, G_HEADS, G_DK, G_DV), lambda b, t: (b, 0, 0, 0))],
        out_shape=[jax.ShapeDtypeStruct((batch * seq, G_W), BF16),
                   jax.ShapeDtypeStruct((batch, G_HEADS, G_DK, G_DV), F32)],
        scratch_shapes=[pltpu.VMEM((G_HEADS, G_DV, G_DK), F32)],
        compiler_params=_cparams(("parallel", "arbitrary")),
        name="gla_prompt",
    )(proj, proj, proj, side, *params)


def _gla_step_kernel(qk_ref, v_ref, z_ref, sm_ref, s_in, acc_ref, ab, a_bias, gn, y_ref, s_out,
                     vec_ref, vrow_ref, orow_ref, *, bb):
    del acc_ref
    qk = qk_ref[...].astype(F32)
    q = qk[:, 0:G_QK] * (G_DK ** -0.5)
    k = qk[:, G_QK:2 * G_QK]
    g = _gla_log_gate(sm_ref[...], ab[...], a_bias[...])
    vec_ref[0] = q
    vec_ref[1] = k
    vec_ref[2] = jnp.exp(g)
    vrow_ref[...] = v_ref[...].astype(F32)
    eye = (_iota2((G_DK, G_DK), 0) == _iota2((G_DK, G_DK), 1)).astype(F32)
    masks = _outer_masks()

    def per_rows(j, _):
        bs = [j * GLA_STEP_ROWS + i for i in range(GLA_STEP_ROWS)]
        vecs = [[vec_ref[i, pl.ds(b, 1), :] for i in range(3)] for b in bs]
        v_b = [vrow_ref[pl.ds(b, 1), :] for b in bs]
        items = [(bi, h) for bi in range(GLA_STEP_ROWS) for h in range(G_HEADS)]
        n = range(len(items))
        hk = lambda x, h: x[:, h * G_DK:(h + 1) * G_DK]
        q_, k_, e_ = ([hk(vecs[bi][i], h) for bi, h in items] for i in range(3))
        v_h = [v_b[bi][:, h * G_DV:(h + 1) * G_DV] for bi, h in items]
        s = [s_in[bs[bi], h] for bi, h in items]
        e_col = [_lane_sum(eye * e_[i]) for i in n]
        qk_dot = [_lane_sum(q_[i] * k_[i]) for i in n]
        o_s = [_mm(_rows8(q_[i] * e_[i]), s[i])[0:1] for i in n]
        kv = [_mm_tn(*_outer_operands(masks, (k_[i], v_h[i]))) for i in n]
        o = [qk_dot[i] * v_h[i] + o_s[i] for i in n]
        for i, (bi, h) in enumerate(items):
            s_out[bs[bi], h] = s[i] * e_col[i] + kv[i]
        for bi in range(GLA_STEP_ROWS):
            orow_ref[pl.ds(bs[bi], 1), :] = jnp.concatenate(
                [o[bi * G_HEADS + h] for h in range(G_HEADS)], axis=-1)
        return 0

    lax.fori_loop(0, bb // GLA_STEP_ROWS, per_rows, 0)
    o = orow_ref[...]
    o_heads = [o[:, h * G_DV:(h + 1) * G_DV] for h in range(G_HEADS)]
    y_ref[...] = _gla_out(o_heads, z_ref[...].astype(F32), gn[...]).astype(y_ref.dtype)


def _gla_step(proj, side, s_all, acc, params, layer, bb):
    nb = proj.shape[0]
    cb = lambda c, w: pl.BlockSpec((bb, w), lambda i, c=c: (i, c // w))
    full = lambda a: pl.BlockSpec(a.shape, lambda i: (0,) * a.ndim)
    st = _layer_block(s_all.shape[2:], bb, layer)
    kern = functools.partial(_gla_step_kernel, bb=bb)
    return pl.pallas_call(
        kern,
        grid=(nb // bb,),
        in_specs=[cb(C_GQK, 2 * G_QK), cb(C_GV, G_W), cb(C_GZ, G_W), cb(S_SMALL, LANES), st,
                  pl.BlockSpec(memory_space=pl.ANY)]
                 + [full(a) for a in params],
        out_specs=[pl.BlockSpec((bb, G_W), lambda i: (i, 0)), st],
        out_shape=[jax.ShapeDtypeStruct((nb, G_W), BF16), jax.ShapeDtypeStruct(acc.shape, F32)],
        input_output_aliases={5: 1},
        scratch_shapes=[pltpu.VMEM((3, bb, G_QK), F32), pltpu.VMEM((bb, G_W), F32),
                        pltpu.VMEM((bb, G_W), F32)],
        compiler_params=_cparams(("parallel",)),
        name="gla_step",
    )(proj, proj, proj, side, s_all, acc, *params)


def _mlstm_out(h_heads, o_pre, z, gn):
    outs = []
    for h in range(M_HEADS):
        y = h_heads[h]
        y = y - jnp.mean(y, axis=-1, keepdims=True)
        outs.append(y * lax.rsqrt(jnp.mean(y * y, axis=-1, keepdims=True) + EPS))
    return jnp.concatenate(outs, axis=-1) * gn * _sigmoid(o_pre) * _silu(z)


def _mlstm_prompt_kernel(q_ref, k_ref, v_ref, o_ref, z_ref, sm_ref, ifb, gn,
                         y_ref, c_ref, n_ref, m_ref, ct_ref, *, n_chunks):
    L = CHUNK

    @pl.when(pl.program_id(1) == 0)
    def _():
        ct_ref[...] = jnp.zeros_like(ct_ref)
        n_ref[...] = jnp.zeros_like(n_ref)
        m_ref[...] = jnp.zeros_like(m_ref)

    causal = _iota2((L, L), 0) <= _iota2((L, L), 1)
    eye = (_iota2((L, L), 0) == _iota2((L, L), 1)).astype(F32)
    scale = M_DK ** -0.5
    neg_inf = -jnp.inf
    C, H = range(n_chunks), range(M_HEADS)
    rws = [slice(c * L, (c + 1) * L) for c in C]
    sls = [slice(h * M_DK, (h + 1) * M_DK) for h in H]

    pre = sm_ref[...] + ifb[...]
    lf = _log_sigmoid(pre)
    b_all = _mm_exact_lhs(_block_tril(n_chunks * L, L), lf)
    lib_all = pre - pltpu.roll(b_all, LANES - (SM_MF - SM_MI), 1)
    sel = (_iota2((LANES, M_HEADS * LANES), 0)
           == SM_MI + _iota2((LANES, M_HEADS * LANES), 1) // LANES).astype(F32)
    lib_bc = _mm_exact_rhs(lib_all, sel)
    comb = jnp.where(_iota2((1, LANES), 1) < SM_MF, lib_all, b_all)
    q, k, v = q_ref[...].astype(F32) * scale, k_ref[...].astype(F32), v_ref[...].astype(F32)
    qs = [[q[rws[c], sls[h]] for h in H] for c in C]
    k_h = [[k[rws[c], sls[h]] for h in H] for c in C]
    v_h = [[v[rws[c], sls[h]] for h in H] for c in C]
    qk_raw = [[_mm_nt(k_h[c][h], qs[c][h]) for h in H] for c in C]
    d_log, d_max, b_row, lib_row = [], [], [], []
    for c in C:
        comb_t = comb[rws[c]].T
        b_row.append([comb_t[SM_MF + h:SM_MF + h + 1, :] for h in H])
        lib_row.append([comb_t[SM_MI + h:SM_MI + h + 1, :] for h in H])
        d_log.append([jnp.where(causal, lib_bc[rws[c], h * LANES:h * LANES + L] + b_row[c][h], neg_inf)
                      for h in H])
        d_max.append([jnp.max(d_log[c][h], axis=0, keepdims=True) for h in H])
    m_prev = [m_ref[0, :, h:h + 1] for h in H]
    m_t, w_inter, carry, w_state, qk = [], [], [], [], []
    for c in C:
        m_inter = [b_row[c][h] + m_prev[h] for h in H]
        m_t.append([jnp.maximum(m_inter[h], d_max[c][h]) for h in H])
        w_inter.append([jnp.exp(m_inter[h] - m_t[c][h]) for h in H])
        m_new = [m_t[c][h][:, L - 1:L] for h in H]
        b_last = [b_row[c][h][:, L - 1:L] for h in H]
        carry.append([jnp.exp(b_last[h] + m_prev[h] - m_new[h]) for h in H])
        w_state.append([jnp.exp(b_last[h] + lib_row[c][h] - m_new[h]) for h in H])
        qk.append([qk_raw[c][h] * jnp.exp(d_log[c][h] - m_t[c][h]) for h in H])
        m_prev = m_new
    for h in H:
        m_ref[0, :, h:h + 1] = m_prev[h]
    wk = [[_mm(eye * w_state[c][h], k_h[c][h]) for h in H] for c in C]
    n_upd = [[_mm(_rows8(w_state[c][h]), k_h[c][h])[0:1] for h in H] for c in C]
    upd = [[_mm_tn(v_h[c][h], wk[c][h]) for h in H] for c in C]
    qkv = [[_mm_tn(v_h[c][h], qk[c][h]) for h in H] for c in C]
    ct = [ct_ref[h] for h in H]
    ns = [n_ref[0, h:h + 1, :] for h in H]
    y_ch = []
    for c in C:
        qc = [_mm_nt(ct[h], qs[c][h]) for h in H]
        qn = [_mm_nt(_rows8(ns[h]), qs[c][h])[0:1] for h in H]
        yy = []
        for h in H:
            num = qkv[c][h] + w_inter[c][h] * qc[h]
            den = jnp.sum(qk[c][h], axis=0, keepdims=True) + w_inter[c][h] * qn[h]
            ht = num / jnp.maximum(jnp.abs(den), jnp.exp(-m_t[c][h]))
            ht = ht - jnp.mean(ht, axis=0, keepdims=True)
            yy.append((ht * lax.rsqrt(jnp.mean(ht * ht, axis=0, keepdims=True) + EPS)).T)
        y_ch.append(yy)
        ct = [carry[c][h] * ct[h] + upd[c][h] for h in H]
        ns = [carry[c][h] * ns[h] + n_upd[c][h] for h in H]
    for h in H:
        ct_ref[h] = ct[h]
        n_ref[0, h:h + 1, :] = ns[h]
    y = jnp.concatenate([jnp.concatenate([y_ch[c][h] for c in C], axis=0) for h in H], axis=-1)
    y_ref[...] = (y * gn[...] * _sigmoid(o_ref[...].astype(F32)) * _silu(z_ref[...].astype(F32))
                  ).astype(y_ref.dtype)

    @pl.when(pl.program_id(1) == pl.num_programs(1) - 1)
    def _():
        for h in H:
            c_ref[0, h] = ct_ref[h].T


def _mlstm_params(ifb, gn):
    ifb_p = jnp.zeros((1, LANES), F32).at[0, SM_MI:SM_MI + 2 * M_HEADS].set(ifb)
    return ifb_p, gn.reshape(1, -1)


def _mlstm_prompt(proj, side, params, batch, seq, tb):
    nt = seq // tb
    cb = lambda c, w: pl.BlockSpec((tb, w), lambda b, t, c=c: (b * nt + t, c // w))
    full = lambda a: pl.BlockSpec(a.shape, lambda b, t: (0,) * a.ndim)
    kern = functools.partial(_mlstm_prompt_kernel, n_chunks=tb // CHUNK)
    return pl.pallas_call(
        kern,
        grid=(batch, nt),
        in_specs=[cb(C_MQ, M_QK), cb(C_MK, M_QK), cb(C_MV, M_W), cb(C_MO, M_W), cb(C_MZ, M_W),
                  cb(S_SMALL, LANES)] + [full(a) for a in params],
        out_specs=[pl.BlockSpec((tb, M_W), lambda b, t: (b * nt + t, 0)),
                   pl.BlockSpec((1, M_HEADS, M_DK, M_DV), lambda b, t: (b, 0, 0, 0)),
                   pl.BlockSpec((1, M_HEADS, M_DK), lambda b, t: (b, 0, 0)),
                   pl.BlockSpec((1, 1, M_HEADS), lambda b, t: (b, 0, 0))],
        out_shape=[jax.ShapeDtypeStruct((batch * seq, M_W), BF16),
                   jax.ShapeDtypeStruct((batch, M_HEADS, M_DK, M_DV), F32),
                   jax.ShapeDtypeStruct((batch, M_HEADS, M_DK), F32),
                   jax.ShapeDtypeStruct((batch, 1, M_HEADS), F32)],
        scratch_shapes=[pltpu.VMEM((M_HEADS, M_DV, M_DK), F32)],
        compiler_params=_cparams(("parallel", "arbitrary")),
        name="mlstm_prompt",
    )(proj, proj, proj, proj, proj, side, *params)


def _mlstm_step_kernel(q_ref, k_ref, v_ref, o_ref, z_ref, sm_ref, c_in, n_in, m_in, acc_ref, ifb, gn,
                       y_ref, c_out, n_out, m_out, vec_ref, sc_ref, hrow_ref, *, bb):
    del acc_ref
    pre = sm_ref[...] + ifb[...]
    li = pre[:, SM_MI:SM_MI + M_HEADS]
    lf = _log_sigmoid(pre[:, SM_MF:SM_MF + M_HEADS])
    m0 = m_in[...]
    m_inter = lf + m0
    m_t = jnp.maximum(m_inter, li)
    m_out[...] = m_t
    pad = lambda x: jnp.concatenate([x, jnp.zeros((bb, LANES - M_HEADS), F32)], axis=-1)
    sc_ref[0] = pad(jnp.exp(m_inter - m_t))
    sc_ref[1] = pad(jnp.exp(li - m_t))
    sc_ref[2] = pad(jnp.exp(-m_t))
    vec_ref[0] = q_ref[...].astype(F32) * (M_DK ** -0.5)
    vec_ref[1] = k_ref[...].astype(F32)
    vec_ref[2] = v_ref[...].astype(F32)
    masks = _outer_masks()

    def per_rows(j, _):
        bs = [j * STEP_ROWS + i for i in range(STEP_ROWS)]
        vecs = [[vec_ref[i, pl.ds(b, 1), :] for i in range(3)] for b in bs]
        scs = [[sc_ref[i, pl.ds(b, 1), :] for i in range(3)] for b in bs]
        items = [(bi, h) for bi in range(STEP_ROWS) for h in range(M_HEADS)]
        n_it = range(len(items))
        hs = lambda x, h: x[:, h * M_DK:(h + 1) * M_DK]
        q_h, k_h, v_h = ([hs(vecs[bi][i], h) for bi, h in items] for i in range(3))
        wi, ei, en = ([scs[bi][i][:, h:h + 1] for bi, h in items] for i in range(3))
        c = [c_in[bs[bi], h] for bi, h in items]
        n = [n_in[bs[bi], h:h + 1, :] for bi, h in items]
        qk = [_lane_sum(q_h[i] * k_h[i]) * ei[i] for i in n_it]
        qn = [_lane_sum(q_h[i] * n[i]) for i in n_it]
        qc = [_mm(_rows8(q_h[i]), c[i])[0:1] for i in n_it]
        kv = [_mm_tn(*_outer_operands(masks, (ei[i] * k_h[i], v_h[i]))) for i in n_it]
        hh = []
        for i, (bi, h) in enumerate(items):
            num = qk[i] * v_h[i] + wi[i] * qc[i]
            den = qk[i] + wi[i] * qn[i]
            hh.append(num / jnp.maximum(jnp.abs(den), en[i]))
            c_out[bs[bi], h] = wi[i] * c[i] + kv[i]
            n_out[bs[bi], h:h + 1, :] = wi[i] * n[i] + ei[i] * k_h[i]
        for bi in range(STEP_ROWS):
            hrow_ref[pl.ds(bs[bi], 1), :] = jnp.concatenate(
                [hh[bi * M_HEADS + h] for h in range(M_HEADS)], axis=-1)
        return 0

    lax.fori_loop(0, bb // STEP_ROWS, per_rows, 0)
    hh = hrow_ref[...]
    h_heads = [hh[:, h * M_DV:(h + 1) * M_DV] for h in range(M_HEADS)]
    y_ref[...] = _mlstm_out(h_heads, o_ref[...].astype(F32), z_ref[...].astype(F32), gn[...]).astype(y_ref.dtype)


def _mlstm_step(proj, side, c_all, n_all, m_all, acc, params, layer, bb):
    nb = proj.shape[0]
    cb = lambda c, w: pl.BlockSpec((bb, w), lambda i, c=c: (i, c // w))
    full = lambda a: pl.BlockSpec(a.shape, lambda i: (0,) * a.ndim)
    cs = _layer_block(c_all.shape[2:], bb, layer)
    ns_in = _layer_block(n_all.shape[2:], bb, layer)
    ms_in = _layer_block(m_all.shape[2:], bb, layer)
    ns = pl.BlockSpec((bb, M_HEADS, M_DK), lambda i: (i, 0, 0))
    ms = pl.BlockSpec((bb, M_HEADS), lambda i: (i, 0))
    kern = functools.partial(_mlstm_step_kernel, bb=bb)
    return pl.pallas_call(
        kern,
        grid=(nb // bb,),
        in_specs=[cb(C_MQ, M_QK), cb(C_MK, M_QK), cb(C_MV, M_W), cb(C_MO, M_W), cb(C_MZ, M_W),
                  cb(S_SMALL, LANES), cs, ns_in, ms_in, pl.BlockSpec(memory_space=pl.ANY)]
                 + [full(a) for a in params],
        out_specs=[pl.BlockSpec((bb, M_W), lambda i: (i, 0)), cs, ns, ms],
        out_shape=[jax.ShapeDtypeStruct((nb, M_W), BF16), jax.ShapeDtypeStruct(acc.shape, F32),
                   jax.ShapeDtypeStruct(n_all.shape[1:], F32), jax.ShapeDtypeStruct(m_all.shape[1:], F32)],
        input_output_aliases={9: 1},
        scratch_shapes=[pltpu.VMEM((3, bb, M_QK), F32), pltpu.VMEM((3, bb, LANES), F32),
                        pltpu.VMEM((bb, M_W), F32)],
        compiler_params=_cparams(("parallel",)),
        name="mlstm_step",
    )(proj, proj, proj, proj, proj, side, c_all, n_all, m_all, acc, *params)


def _shift_rows(proj_rows, side_rows):
    return jnp.concatenate([proj_rows[:, C_RIN:C_RIN + 3 * R_W].astype(F32),
                            side_rows[:, S_RLORA:S_RLORA + 2 * R_LORA]], axis=-1)


def _pick(n, prefs):
    for p in prefs:
        if n % p == 0:
            return p
    return n


def kernel(x_prompt, x_sample, state_rwkv_shift, state_rwkv, state_gla, state_mlstm_C, state_mlstm_n,
           state_mlstm_m, norm_pre, norm_post, w_in, r_mu_shift, r_w_lora_b, r_w0, r_a_lora_b, r_a0,
           r_k_k, r_k_a, r_r_k, r_gn, g_a_lora_b, g_a_bias, g_gn, m_if_bias, m_gn, w_br_rwkv, w_br_gla,
           w_br_mlstm, w_out):
    depth = w_in.shape[0]
    bp, tp, _ = x_prompt.shape
    bs, ts, _ = x_sample.shape
    assert ts == 1 and tp % CHUNK == 0
    assert w_in.shape[1:] == (D_MODEL, N_IN)
    w_t = jnp.transpose(w_in, (0, 2, 1))
    wr, wg, wm, wo = (w.astype(BF16) for w in (w_br_rwkv, w_br_gla, w_br_mlstm, w_out))

    mp = bp * tp
    tm_in = _pick(mp, (2048, 1024, 512, 256, 128, 64))
    tm_mg = _pick(mp, (512, 256, 128, 64))
    tb = _pick(tp, (256, 128, 64))
    bb = _pick(bs, (16, 8))

    xp = x_prompt.reshape(mp, D_MODEL)
    xs = x_sample.reshape(bs, D_MODEL)
    new_p = [[] for _ in range(6)]
    new_s = [[] for _ in range(3)]
    s_lanes = jnp.transpose(state_rwkv, (0, 2, 3, 4, 1))
    acc_r, acc_g, acc_c = (jnp.zeros(s.shape, F32) for s in (s_lanes, state_gla, state_mlstm_C))
    for l in range(depth):
        g_pre = norm_pre[l].reshape(1, -1)
        g_post = norm_post[l].reshape(1, -1)
        rp = _rwkv_params(r_mu_shift[l], r_w_lora_b[l], r_w0[l], r_a_lora_b[l], r_a0[l], r_k_k[l],
                          r_k_a[l], r_r_k[l], r_gn[l])
        gp = _gla_params(g_a_lora_b[l], g_a_bias[l], g_gn[l])
        mparams = _mlstm_params(m_if_bias[l], m_gn[l])

        proj, side, proj_s, side_s = _inproj(xp, xs, g_pre, w_t, l, tm_in)
        y_r, s_r = _rwkv_prompt(proj, side, rp, bp, tp, tb)
        y_g, s_g = _gla_prompt(proj, side, gp, bp, tp, tb)
        y_m, c, n, m = _mlstm_prompt(proj, side, mparams, bp, tp, tb)
        xp = _merge(proj, y_r, y_g, y_m, xp, wr, wg, wm, wo, g_post, l, tm_mg)
        shift = _shift_rows(proj.reshape(bp, tp, C_RLORA)[:, tp - 1], side.reshape(bp, tp, SIDE_W)[:, tp - 1])
        for lst, val in zip(new_p, (shift, s_r, s_g, c, n, m.reshape(bp, M_HEADS))):
            lst.append(val)

        proj, side = proj_s, side_s
        col = lambda x: jnp.broadcast_to(x[:, None], (R_W, LANES))
        y_t, acc_r = _rwkv_step_lanes(proj, side, state_rwkv_shift, s_lanes, acc_r,
                                      rp[:10] + (col(r_r_k[l]), col(r_gn[l])), l)
        y_r = jnp.transpose(y_t, (2, 0, 1)).reshape(bs, R_W)
        y_g, acc_g = _gla_step(proj, side, state_gla, acc_g, gp, l, bb)
        y_m, acc_c, n, m = _mlstm_step(proj, side, state_mlstm_C, state_mlstm_n, state_mlstm_m, acc_c,
                                       mparams, l, bb)
        xs = _merge(proj, y_r, y_g, y_m, xs, wr, wg, wm, wo, g_post, l, bs)
        shift = _shift_rows(proj, side)
        for lst, val in zip(new_s, (shift, n, m)):
            lst.append(val)

    stk = lambda lst: jnp.stack(lst)
    return (xp.reshape(bp, tp, D_MODEL), xs.reshape(bs, ts, D_MODEL),
            *(stk(v) for v in new_p),
            stk(new_s[0]), jnp.transpose(acc_r, (0, 4, 1, 2, 3)), acc_g, acc_c, stk(new_s[1]), stk(new_s[2]))
```

```python
import functools
import math

import jax
import jax.numpy as jnp
from jax import lax
from jax.experimental import pallas as pl
from jax.experimental.pallas import tpu as pltpu

F32 = jnp.float32
BF16 = jnp.bfloat16

D_MODEL = 1024
R_HEADS, R_HD = 8, 64
R_W = R_HEADS * R_HD
R_LORA = 64
R_SHIFT_W = 3 * R_W + 2 * R_LORA
R_GN_EPS = 64e-5
G_HEADS, G_DK, G_DV = 4, 64, 128
G_QK = G_HEADS * G_DK
G_W = G_HEADS * G_DV
G_LORA = 16
G_GATE_TEMP = 16.0
M_HEADS, M_DK, M_DV = 4, 128, 128
M_QK = M_HEADS * M_DK
M_W = M_HEADS * M_DV
EPS = 1e-6

LANES = 128
SUBLANES = 8
VMEM_LIMIT = 48 * 1024 * 1024

C_GATE = 0
C_RZ = 3072
C_GV = 3584
C_GZ = 4096
C_MQ = 4608
C_MK = 5120
C_MV = 5632
C_MO = 6144
C_MZ = 6656
C_GQK = 7168
C_RIN = 7680
C_RLORA = C_RIN + 3 * R_W
C_SMALL = C_RLORA + LANES
SM_GA, SM_MI, SM_MF = 0, 16, 20
N_USED = C_SMALL + LANES
PROJ_TN = 512
N_PAD = -(-N_USED // PROJ_TN) * PROJ_TN
N_MAIN_BLOCKS = C_RLORA // PROJ_TN
_W_SEGMENTS = (("r_in", R_SHIFT_W), ("r_z", R_W), ("g_q", G_QK), ("g_k", G_QK), ("g_v", G_W), ("g_a", G_LORA),
               ("g_z", G_W), ("m_q", M_QK), ("m_k", M_QK), ("m_v", M_W), ("m_i", M_HEADS), ("m_f", M_HEADS),
               ("m_o", M_W), ("m_z", M_W), ("gate", 3 * D_MODEL))
W_OFF = {}
_off = 0
for _name, _width in _W_SEGMENTS:
    W_OFF[_name] = _off
    _off += _width
N_IN = _off
W_MAIN_ROWS = tuple(W_OFF[name] + PROJ_TN * b
                    for name, nblk in (("gate", 6), ("r_z", 1), ("g_v", 1), ("g_z", 1), ("m_q", 1), ("m_k", 1),
                                       ("m_v", 1), ("m_o", 1), ("m_z", 1), ("g_q", 1), ("r_in", 3))
                    for b in range(nblk))
assert len(W_MAIN_ROWS) == N_MAIN_BLOCKS and all(r % SUBLANES == 0 for r in W_MAIN_ROWS)
SIDE_W = 2 * LANES
S_RLORA, S_SMALL = 0, LANES

CHUNK = 64
RWKV_GROUP = 16


def _cparams(sem):
    return pltpu.CompilerParams(dimension_semantics=sem, vmem_limit_bytes=VMEM_LIMIT)


def _mm(a, b):
    return jnp.dot(a.astype(BF16), b.astype(BF16), preferred_element_type=F32)


def _mm_nt(a, b):
    return lax.dot_general(a.astype(BF16), b.astype(BF16), (((1,), (1,)), ((), ())),
                           preferred_element_type=F32)


def _mm_tn(a, b):
    return lax.dot_general(a.astype(BF16), b.astype(BF16), (((0,), (0,)), ((), ())),
                           preferred_element_type=F32)


def _mm_exact_lhs(m01, x):
    hi = x.astype(BF16)
    r1 = x - hi.astype(F32)
    mid = r1.astype(BF16)
    lo = (r1 - mid.astype(F32)).astype(BF16)
    m = m01.astype(BF16)
    d = lambda p: jnp.dot(m, p, preferred_element_type=F32)
    return d(hi) + d(mid) + d(lo)


def _mm_exact_rhs(x, m01):
    hi = x.astype(BF16)
    r1 = x - hi.astype(F32)
    mid = r1.astype(BF16)
    lo = (r1 - mid.astype(F32)).astype(BF16)
    m = m01.astype(BF16)
    d = lambda p: jnp.dot(p, m, preferred_element_type=F32)
    return d(hi) + d(mid) + d(lo)


def _sigmoid(x):
    return 0.5 * jnp.tanh(0.5 * x) + 0.5


def _silu(x):
    return x * _sigmoid(x)


def _softplus(x):
    return jnp.maximum(x, 0.0) + jnp.log(1.0 + jnp.exp(-jnp.abs(x)))


def _log_sigmoid(x):
    return -_softplus(-x)


def _iota2(shape, dim):
    return lax.broadcasted_iota(jnp.int32, shape, dim)


def _lane_sum(x):
    return jnp.sum(x, axis=-1, keepdims=True)


def _rows8(x):
    return jnp.broadcast_to(x, (8, x.shape[1]))


def _outer_masks():
    rid = _iota2((8, 1), 0)
    on = lambda *rows: sum((rid == r).astype(F32) for r in rows)
    return (on(0, 2), on(1), on(0, 1), on(2)), (on(3, 5), on(4), on(3, 4), on(5))


def _outer_operands(masks, *pairs):
    a_op, b_op = 0.0, 0.0
    for (a, b), (m_ahi, m_alo, m_bhi, m_blo) in zip(pairs, masks):
        a_hi = a.astype(BF16).astype(F32)
        b_hi = b.astype(BF16).astype(F32)
        a_op = a_op + m_ahi * a_hi + m_alo * (a - a_hi)
        b_op = b_op + m_bhi * b_hi + m_blo * (b - b_hi)
    return a_op, b_op


def _block_tril(n, blk):
    r, c = _iota2((n, n), 0), _iota2((n, n), 1)
    return ((r - c).astype(jnp.uint32) <= (r % blk).astype(jnp.uint32)).astype(F32)


def _inproj_kernel(rows_ref, x_ref, xs_ref, g_ref, w_ref, wl_ref, wa_ref, wif_ref,
                   o_ref, side_ref, os_ref, sides_ref, h_ref, hs_ref, ws_ref):
    del rows_ref
    i, j = pl.program_id(0), pl.program_id(1)

    def normed(x):
        return (x * lax.rsqrt(jnp.mean(x * x, axis=-1, keepdims=True) + EPS) * g_ref[...]).astype(BF16)

    @pl.when(j == 0)
    def _():
        h_ref[...] = normed(x_ref[...])
        ws_ref[...] = jnp.zeros_like(ws_ref)
        ws_ref[S_RLORA:S_RLORA + 2 * R_LORA, :] = wl_ref[0]
        ws_ref[S_SMALL + SM_GA:S_SMALL + SM_GA + G_LORA, :] = wa_ref[0]
        ws_ref[S_SMALL + SM_MI:S_SMALL + SM_MI + 2 * M_HEADS, :] = wif_ref[0]
        side_ref[...] = _mm_nt(h_ref[...], ws_ref[...])

    @pl.when(jnp.logical_and(i == 0, j == 0))
    def _():
        hs_ref[...] = normed(xs_ref[...])
        sides_ref[...] = _mm_nt(hs_ref[...], ws_ref[...])

    w = w_ref[0].astype(BF16)
    o_ref[...] = _mm_nt(h_ref[...], w).astype(BF16)

    @pl.when(i == 0)
    def _():
        os_ref[...] = _mm_nt(hs_ref[...], w).astype(BF16)


def _inproj(x2d, xs2d, g_pre, w_t, layer, tm):
    m, ms = x2d.shape[0], xs2d.shape[0]
    w_blk = lambda n: (pl.Element(1), pl.Element(n), pl.Element(D_MODEL))
    w_rows = lambda n, start: pl.BlockSpec(w_blk(n), lambda i, j, rows, start=start: (layer, start, 0))
    last = N_MAIN_BLOCKS - 1
    grid_spec = pltpu.PrefetchScalarGridSpec(
        num_scalar_prefetch=1,
        grid=(m // tm, N_MAIN_BLOCKS),
        in_specs=[pl.BlockSpec((tm, D_MODEL), lambda i, j, rows: (i, 0)),
                  pl.BlockSpec((ms, D_MODEL), lambda i, j, rows: (0, 0)),
                  pl.BlockSpec((1, D_MODEL), lambda i, j, rows: (0, 0)),
                  pl.BlockSpec(w_blk(PROJ_TN), lambda i, j, rows: (layer, rows[j] * SUBLANES, 0)),
                  w_rows(2 * R_LORA, W_OFF["r_in"] + 3 * R_W),
                  w_rows(G_LORA, W_OFF["g_a"]),
                  w_rows(2 * M_HEADS, W_OFF["m_i"])],
        out_specs=[pl.BlockSpec((tm, PROJ_TN), lambda i, j, rows: (i, j)),
                   pl.BlockSpec((tm, SIDE_W), lambda i, j, rows: (i, 0)),
                   pl.BlockSpec((ms, PROJ_TN), lambda i, j, rows: (0, jnp.where(i == 0, j, last))),
                   pl.BlockSpec((ms, SIDE_W), lambda i, j, rows: (0, 0))],
        scratch_shapes=[pltpu.VMEM((tm, D_MODEL), BF16), pltpu.VMEM((ms, D_MODEL), BF16),
                        pltpu.VMEM((SIDE_W, D_MODEL), F32)])
    return pl.pallas_call(
        _inproj_kernel,
        grid_spec=grid_spec,
        out_shape=[jax.ShapeDtypeStruct((m, C_RLORA), BF16), jax.ShapeDtypeStruct((m, SIDE_W), F32),
                   jax.ShapeDtypeStruct((ms, C_RLORA), BF16), jax.ShapeDtypeStruct((ms, SIDE_W), F32)],
        compiler_params=_cparams(("arbitrary", "arbitrary")),
        name="inproj",
    )(jnp.asarray([r // SUBLANES for r in W_MAIN_ROWS], jnp.int32), x2d, xs2d, g_pre, w_t, w_t, w_t, w_t)


def _merge_kernel(gate_ref, yr_ref, yg_ref, ym_ref, x_ref, wr_ref, wg_ref, wm_ref, wo_ref, gp_ref, o_ref):
    d = D_MODEL
    gate = lambda b: _sigmoid(gate_ref[:, b * d:(b + 1) * d].astype(F32))
    merged = (gate(0) * _mm(yr_ref[...], wr_ref[...]) + gate(1) * _mm(yg_ref[...], wg_ref[...])
              + gate(2) * _mm(ym_ref[...], wm_ref[...]))
    o = _mm(merged, wo_ref[...])
    y = o * lax.rsqrt(jnp.mean(o * o, axis=-1, keepdims=True) + EPS)
    o_ref[...] = x_ref[...] + y * gp_ref[...]


def _merge(proj, y_r, y_g, y_m, x2d, w_r, w_g, w_m, w_o, g_post, layer, tm):
    m = x2d.shape[0]
    row = lambda w: pl.BlockSpec((tm, w), lambda i: (i, 0))
    full = lambda a: pl.BlockSpec(a.shape, lambda i: (0, 0))
    wl = lambda a: pl.BlockSpec((None,) + a.shape[1:], lambda i: (layer, 0, 0))
    return pl.pallas_call(
        _merge_kernel,
        grid=(m // tm,),
        in_specs=[row(3 * D_MODEL), row(R_W), row(G_W), row(M_W), row(D_MODEL),
                  wl(w_r), wl(w_g), wl(w_m), wl(w_o), full(g_post)],
        out_specs=row(D_MODEL),
        out_shape=jax.ShapeDtypeStruct((m, D_MODEL), F32),
        compiler_params=_cparams(("parallel",)),
        name="merge",
    )(proj, y_r, y_g, y_m, x2d, w_r, w_g, w_m, w_o, g_post)


def _rwkv_front(r_in, k_in, v_in, l_in, prev_r, prev_k, prev_v, prev_l, p):
    (mu_r, mu_k, mu_v, mu_l, wb, w0, ab, a0, k_k, k_a) = p
    r = r_in + mu_r * (prev_r - r_in)
    k = k_in + mu_k * (prev_k - k_in)
    v = v_in + mu_v * (prev_v - v_in)
    lo = l_in + mu_l * (prev_l - l_in)
    log_w = -_softplus(-(w0 + _mm(jnp.tanh(lo), wb))) - 0.5
    lw = -jnp.exp(log_w)
    a = _sigmoid(a0 + _mm(lo, ab))
    kk = k * k_k
    k2 = k * (1.0 + (a - 1.0) * k_a)
    return r, k2, v, lw, a, kk


def _rwkv_prompt_kernel(z_ref, r_ref, k_ref, v_ref, l_ref,
                        mu_r, mu_k, mu_v, mu_l, wb, w0, ab, a0, k_k, k_a, r_k, gn,
                        y_ref, s_ref, carry_ref, sp_ref, *, n_chunks):
    L = CHUNK

    @pl.when(pl.program_id(1) == 0)
    def _():
        sp_ref[...] = jnp.zeros_like(sp_ref)
        carry_ref[...] = jnp.zeros_like(carry_ref)

    tb = n_chunks * L
    params = tuple(x[...] for x in (mu_r, mu_k, mu_v, mu_l, wb, w0, ab, a0, k_k, k_a))
    assert L == R_HD
    m_a = (_iota2((1, LANES), 1) < R_HD).astype(F32)
    m_b = 1.0 - m_a
    m_a16, m_b16 = m_a.astype(BF16), m_b.astype(BF16)
    trow = _iota2((L, LANES), 0)
    tcol = _iota2((L, LANES), 1) % L
    eye = (tcol == trow).astype(F32)
    blk_mask = (trow // 16) == (tcol // 16)
    r2 = _iota2((2 * L, LANES), 0)
    g_mask = (_iota2((2 * L, LANES), 1) % L) < (r2 % L) + (r2 >= L).astype(jnp.int32)
    same_head = (_iota2((LANES, LANES), 0) // R_HD) == (_iota2((LANES, LANES), 1) // R_HD)
    first_row = _iota2((tb, 1), 0) == 0

    def seg_sum(x):
        return _lane_sum(x * m_a) * m_a + _lane_sum(x * m_b) * m_b

    def bd(y):
        yb = y.astype(BF16)
        return jnp.concatenate([yb * m_a16, yb * m_b16], axis=0)

    def shifted(x, carry):
        return jnp.where(first_row, carry, pltpu.roll(x, 1, 0))

    r_in, k_in, v_in = (x[...].astype(F32) for x in (r_ref, k_ref, v_ref))
    l_in = l_ref[...]
    prev_r = shifted(r_in, carry_ref[0:1, 0:R_W])
    prev_k = shifted(k_in, carry_ref[0:1, R_W:2 * R_W])
    prev_v = shifted(v_in, carry_ref[0:1, 2 * R_W:3 * R_W])
    prev_l = shifted(l_in, carry_ref[0:1, 3 * R_W:R_SHIFT_W])
    carry_ref[0:1, 0:R_W] = r_in[tb - 1:tb, :]
    carry_ref[0:1, R_W:2 * R_W] = k_in[tb - 1:tb, :]
    carry_ref[0:1, 2 * R_W:3 * R_W] = v_in[tb - 1:tb, :]
    carry_ref[0:1, 3 * R_W:R_SHIFT_W] = l_in[tb - 1:tb, :]
    r, k2, v, lw, a, kk = _rwkv_front(r_in, k_in, v_in, l_in, prev_r, prev_k, prev_v, prev_l, params)

    cum = _mm_exact_lhs(_block_tril(tb, L), lw)
    p_inc = jnp.exp(cum)
    p_exc = jnp.exp(cum - lw)
    p_inv = jnp.exp(-cum)
    r_t = r * p_inc
    k_h = k2 * p_inv

    n_pairs = R_HEADS // 2
    CP = [(c, p) for c in range(n_chunks) for p in range(n_pairs)]
    N = range(len(CP))
    kap_t, beta_h = [], []
    for p in range(n_pairs):
        sl = slice(p * LANES, (p + 1) * LANES)
        kk_p = kk[:, sl]
        kap = kk_p * jnp.minimum(lax.rsqrt(seg_sum(kk_p * kk_p)), 1e12)
        beta_h.append(kap * a[:, sl] * p_inv[:, sl])
        kap_t.append(kap * p_exc[:, sl])
    cut = lambda x, c, p: x[c * L:(c + 1) * L, p * LANES:(p + 1) * LANES]
    kap_t = [kap_t[p][c * L:(c + 1) * L] for c, p in CP]
    beta_h = [beta_h[p][c * L:(c + 1) * L] for c, p in CP]
    rt_c = [cut(r_t, c, p) for c, p in CP]
    kh_c = [cut(k_h, c, p) for c, p in CP]
    v_c = [cut(v, c, p) for c, p in CP]
    p_last = [p_inc[(c + 1) * L - 1:(c + 1) * L, p * LANES:(p + 1) * LANES] for c, p in CP]
    bhl = [beta_h[i] * p_last[i] for i in N]
    khl = [kh_c[i] * p_last[i] for i in N]
    n_bd, c_c, qt, y0 = {}, {}, {}, {}
    for g0 in range(0, len(CP), RWKV_GROUP):
        G = range(g0, min(g0 + RWKV_GROUP, len(CP)))
        lhs2 = {i: jnp.concatenate([kap_t[i], rt_c[i]], axis=0) for i in G}
        bd_v = {i: bd(v_c[i]) for i in G}
        gb = {i: jnp.where(g_mask, _mm_nt(lhs2[i], bd(beta_h[i])), 0.0) for i in G}
        gk = {i: jnp.where(g_mask, _mm_nt(lhs2[i], bd(kh_c[i])), 0.0) for i in G}
        a_m = {i: gb[i][0:L] for i in G}
        d = {i: jnp.where(blk_mask, a_m[i], 0.0) for i in G}
        akv = {i: _mm(gk[i][0:L], bd_v[i]) for i in G}
        d2 = {i: _mm(d[i], bd(d[i])) for i in G}
        bd_d2 = {i: bd(d2[i]) for i in G}
        d4 = {i: _mm(d2[i], bd_d2[i]) for i in G}
        dd2 = {i: _mm(d[i], bd_d2[i]) for i in G}
        d8 = {i: _mm(d4[i], bd(d4[i])) for i in G}
        x = {i: eye - d[i] + d2[i] - dd2[i] for i in G}
        x = {i: x[i] + _mm(x[i], bd(d4[i])) for i in G}
        td = {i: x[i] + _mm(x[i], bd(d8[i])) for i in G}
        m = {i: _mm(td[i], bd(a_m[i] - d[i])) for i in G}
        m2 = {i: _mm(m[i], bd(m[i])) for i in G}
        q = {i: eye - m[i] + m2[i] - _mm(m[i], bd(m2[i])) for i in G}
        t_inv = {i: _mm(q[i], bd(td[i])) for i in G}
        kbar = {i: _mm(t_inv[i], bd(kap_t[i])) for i in G}
        u0 = {i: -_mm(t_inv[i], bd(akv[i])) for i in G}
        n_bd.update({i: jnp.where(same_head, _mm_tn(kbar[i], bhl[i]), 0.0) for i in G})
        cz = {i: _mm_tn(jnp.concatenate([u0[i], v_c[i]], axis=0), jnp.concatenate([bhl[i], khl[i]], axis=0))
              for i in G}
        c_c.update({i: cz[i][0:R_HD] * m_a + cz[i][R_HD:2 * R_HD] * m_b for i in G})
        qt.update({i: rt_c[i] - _mm(gb[i][L:2 * L], bd(kbar[i])) for i in G})
        y0.update({i: _mm(gb[i][L:2 * L], bd(u0[i])) + _mm(gk[i][L:2 * L], bd_v[i]) for i in G})
    s = [sp_ref[p] for p in range(n_pairs)]
    y_ch = []
    for c in range(n_chunks):
        ids = [c * n_pairs + p for p in range(n_pairs)]
        y_ch.append([_mm_nt(qt[i], bd(s[p])) + y0[i] for p, i in enumerate(ids)])
        s = [s[p] * p_last[i] - _mm(s[p], n_bd[i]) + c_c[i] for p, i in enumerate(ids)]
    for p in range(n_pairs):
        sp_ref[p] = s[p]
    y_pairs = [jnp.concatenate([y_ch[c][p] for c in range(n_chunks)], axis=0) for p in range(n_pairs)]
    outs = []
    for p in range(n_pairs):
        sl = slice(p * LANES, (p + 1) * LANES)
        y = y_pairs[p]
        y = y - seg_sum(y) * (1.0 / R_HD)
        y = y * lax.rsqrt(seg_sum(y * y) * (1.0 / R_HD) + R_GN_EPS)
        bonus = seg_sum(r[:, sl] * k2[:, sl] * r_k[:, sl]) * v[:, sl]
        outs.append(y * gn[:, sl] + bonus)
    y_ref[...] = (jnp.concatenate(outs, axis=-1) * _silu(z_ref[...].astype(F32))).astype(y_ref.dtype)

    @pl.when(pl.program_id(1) == pl.num_programs(1) - 1)
    def _():
        for p in range(n_pairs):
            sp = sp_ref[p]
            s_ref[0, 2 * p] = sp[:, 0:R_HD]
            s_ref[0, 2 * p + 1] = sp[:, R_HD:2 * R_HD]


def _rwkv_params(mu, wb, w0, ab, a0, k_k, k_a, r_k, gn):
    row = lambda x: x.reshape(1, -1)
    zeros = jnp.zeros((R_LORA, R_W), F32)
    wb_p = jnp.concatenate([wb, zeros], axis=0).astype(BF16)
    ab_p = jnp.concatenate([zeros, ab], axis=0).astype(BF16)
    return (row(mu[0:R_W]), row(mu[R_W:2 * R_W]), row(mu[2 * R_W:3 * R_W]), row(mu[3 * R_W:]),
            wb_p, row(w0), ab_p, row(a0), row(k_k), row(k_a), row(r_k), row(gn))


def _rwkv_prompt(proj, side, params, batch, seq, tb):
    nt = seq // tb
    cb = lambda c, w: pl.BlockSpec((tb, w), lambda b, t, c=c: (b * nt + t, c // w))
    full = lambda a: pl.BlockSpec(a.shape, lambda b, t: (0,) * a.ndim)
    kern = functools.partial(_rwkv_prompt_kernel, n_chunks=tb // CHUNK)
    return pl.pallas_call(
        kern,
        grid=(batch, nt),
        in_specs=[cb(C_RZ, R_W), cb(C_RIN, R_W), cb(C_RIN + R_W, R_W), cb(C_RIN + 2 * R_W, R_W),
                  cb(S_RLORA, LANES)] + [full(a) for a in params],
        out_specs=[pl.BlockSpec((tb, R_W), lambda b, t: (b * nt + t, 0)),
                   pl.BlockSpec((1, R_HEADS, R_HD, R_HD), lambda b, t: (b, 0, 0, 0))],
        out_shape=[jax.ShapeDtypeStruct((batch * seq, R_W), BF16),
                   jax.ShapeDtypeStruct((batch, R_HEADS, R_HD, R_HD), F32)],
        scratch_shapes=[pltpu.VMEM((8, R_SHIFT_W), F32),
                        pltpu.VMEM((R_HEADS // 2, R_HD, LANES), F32)],
        compiler_params=_cparams(("parallel", "arbitrary")),
        name="rwkv_prompt",
    )(proj, proj, proj, proj, side, *params)


STEP_ROWS = 4
GLA_STEP_ROWS = 2


def _rwkv_lanes_kernel(z_ref, r_ref, k_ref, v_ref, l_ref, pr_ref, pk_ref, pv_ref, pl_ref, s_in, acc_ref,
                       mu_r, mu_k, mu_v, mu_l, wb, w0, ab, a0, k_k, k_a, rk_t, gn_t,
                       y_ref, s_out, vec_ref, ycol_ref):
    del acc_ref
    h = pl.program_id(0)

    @pl.when(h == 0)
    def _():
        params = tuple(x[...] for x in (mu_r, mu_k, mu_v, mu_l, wb, w0, ab, a0, k_k, k_a))
        r, k2, v, lw, a, kk = _rwkv_front(r_ref[...].astype(F32), k_ref[...].astype(F32),
                                          v_ref[...].astype(F32), l_ref[...], pr_ref[...], pk_ref[...],
                                          pv_ref[...], pl_ref[...], params)
        kaps, betas = [], []
        for hh in range(R_HEADS):
            sl = slice(hh * R_HD, (hh + 1) * R_HD)
            kk_h = kk[:, sl]
            kap = kk_h / jnp.maximum(jnp.sqrt(_lane_sum(kk_h * kk_h)), 1e-12)
            kaps.append(kap)
            betas.append(kap * a[:, sl])
        rows = (jnp.concatenate(kaps, axis=-1), jnp.concatenate(betas, axis=-1), jnp.exp(lw), k2, v, r,
                _silu(z_ref[...].astype(F32)))
        for i, x in enumerate(rows):
            vec_ref[i] = x.T

    base = pl.multiple_of(h * R_HD, R_HD)
    chan = pl.ds(base, R_HD)
    kap_t, beta_t, w_t, k_t, v_t, r_t, gz_t = (vec_ref[i, chan, :] for i in range(7))

    def per_v(vi, _):
        s = s_in[vi]
        sa = -jnp.sum(s * kap_t, axis=0, keepdims=True)
        v_row = vec_ref[4, pl.ds(base + vi, 1), :]
        s_new = s * w_t + sa * beta_t + v_row * k_t
        s_out[vi] = s_new
        ycol_ref[pl.ds(vi, 1), :] = jnp.sum(s_new * r_t, axis=0, keepdims=True)
        return 0

    lax.fori_loop(0, R_HD, per_v, 0, unroll=8)
    y = ycol_ref[...]
    y = y - jnp.mean(y, axis=0, keepdims=True)
    y = y * lax.rsqrt(jnp.mean(y * y, axis=0, keepdims=True) + R_GN_EPS)
    bonus = jnp.sum(r_t * k_t * rk_t[chan, :], axis=0, keepdims=True) * v_t
    y_ref[...] = ((y * gn_t[chan, :] + bonus) * gz_t).astype(y_ref.dtype)


def _rwkv_step_lanes(proj, side, shift_all, s_lanes, acc, params, layer):
    nb = proj.shape[0]
    assert nb == LANES
    cb = lambda c, w: pl.BlockSpec((nb, w), lambda h, c=c: (0, c // w))
    sh = lambda c, w: pl.BlockSpec((None, nb, w), lambda h, c=c: (layer, 0, c // w))
    full = lambda a: pl.BlockSpec(a.shape, lambda h: (0,) * a.ndim)
    st = pl.BlockSpec((None, None, R_HD, R_HD, nb), lambda h: (layer, h, 0, 0, 0))
    return pl.pallas_call(
        _rwkv_lanes_kernel,
        grid=(R_HEADS,),
        in_specs=[cb(C_RZ, R_W), cb(C_RIN, R_W), cb(C_RIN + R_W, R_W), cb(C_RIN + 2 * R_W, R_W),
                  cb(S_RLORA, LANES),
                  sh(0, R_W), sh(R_W, R_W), sh(2 * R_W, R_W), sh(3 * R_W, LANES), st,
                  pl.BlockSpec(memory_space=pl.ANY)]
                 + [full(a) for a in params],
        out_specs=[pl.BlockSpec((None, R_HD, nb), lambda h: (h, 0, 0)), st],
        out_shape=[jax.ShapeDtypeStruct((R_HEADS, R_HD, nb), BF16),
                   jax.ShapeDtypeStruct(acc.shape, F32)],
        input_output_aliases={10: 1},
        scratch_shapes=[pltpu.VMEM((7, R_W, nb), F32), pltpu.VMEM((R_HD, nb), F32)],
        compiler_params=_cparams(("arbitrary",)),
        name="rwkv_step",
    )(proj, proj, proj, proj, side, shift_all, shift_all, shift_all, shift_all, s_lanes, acc, *params)


def _layer_block(shape_tail, bb, layer):
    zeros = (0,) * len(shape_tail)
    return pl.BlockSpec((None, bb) + tuple(shape_tail), lambda i: (layer, i) + zeros)


def _gla_log_gate(small, ab_p, a_bias):
    return _log_sigmoid(_mm(small, ab_p) + a_bias) / G_GATE_TEMP


def _gla_out(o_heads, z, gn):
    outs = []
    for h in range(G_HEADS):
        o = o_heads[h]
        outs.append(o * lax.rsqrt(jnp.mean(o * o, axis=-1, keepdims=True) + EPS))
    return jnp.concatenate(outs, axis=-1) * gn * _silu(z)


def _gla_prompt_kernel(qk_ref, v_ref, z_ref, sm_ref, ab, a_bias, gn, y_ref, s_ref, st_ref, *, n_chunks):
    L = CHUNK
    last = pl.num_programs(1) - 1

    @pl.when(pl.program_id(1) == 0)
    def _():
        st_ref[...] = jnp.zeros_like(st_ref)

    tril_incl = _iota2((L, L), 1) <= _iota2((L, L), 0)
    scale = G_DK ** -0.5
    C, H = range(n_chunks), range(G_HEADS)
    rws = [slice(c * L, (c + 1) * L) for c in C]
    sks = [slice(h * G_DK, (h + 1) * G_DK) for h in H]

    qk = qk_ref[...].astype(F32)
    q, k = qk[:, 0:G_QK] * scale, qk[:, G_QK:2 * G_QK]
    v = v_ref[...].astype(F32)
    la = _gla_log_gate(sm_ref[...], ab[...], a_bias[...])
    cum = _mm_exact_lhs(_block_tril(n_chunks * L, L), la)
    q0, qe, ke, kl, e_last = [], [], [], [], []
    for c in C:
        cum_c = cum[rws[c]]
        ref_row = cum_c[L // 2:L // 2 + 1, :]
        last_row = cum_c[L - 1:L, :]
        q0.append(q[rws[c]] * jnp.exp(cum_c))
        qe.append(q[rws[c]] * jnp.exp(cum_c - ref_row))
        ke.append(k[rws[c]] * jnp.exp(ref_row - cum_c))
        kl.append(k[rws[c]] * jnp.exp(last_row - cum_c))
        e_last.append(jnp.exp(last_row))
    v_h = [[v[rws[c], h * G_DV:(h + 1) * G_DV] for h in H] for c in C]
    att = [[jnp.where(tril_incl, _mm_nt(qe[c][:, sks[h]], ke[c][:, sks[h]]), 0.0) for h in H] for c in C]
    upd = [[_mm_tn(v_h[c][h], kl[c][:, sks[h]]) for h in H] for c in C]
    av = [[_mm(att[c][h], v_h[c][h]) for h in H] for c in C]
    st = [st_ref[h] for h in H]
    o_ch = []
    for c in C:
        o_ch.append([av[c][h] + _mm_nt(q0[c][:, sks[h]], st[h]) for h in H])
        st = [st[h] * e_last[c][:, sks[h]] + upd[c][h] for h in H]
    for h in H:
        st_ref[h] = st[h]
    o_heads = [jnp.concatenate([o_ch[c][h] for c in C], axis=0) for h in H]
    y_ref[...] = _gla_out(o_heads, z_ref[...].astype(F32), gn[...]).astype(y_ref.dtype)

    @pl.when(pl.program_id(1) == last)
    def _():
        for h in range(G_HEADS):
            s_ref[0, h] = st_ref[h].T


def _gla_params(ab, a_bias, gn):
    ab_p = jnp.zeros((LANES, G_QK), F32).at[SM_GA:SM_GA + G_LORA].set(ab).astype(BF16)
    return ab_p, a_bias.reshape(1, -1), gn.reshape(1, -1)


def _gla_prompt(proj, side, params, batch, seq, tb):
    nt = seq // tb
    cb = lambda c, w: pl.BlockSpec((tb, w), lambda b, t, c=c: (b * nt + t, c // w))
    full = lambda a: pl.BlockSpec(a.shape, lambda b, t: (0,) * a.ndim)
    kern = functools.partial(_gla_prompt_kernel, n_chunks=tb // CHUNK)
    return pl.pallas_call(
        kern,
        grid=(batch, nt),
        in_specs=[cb(C_GQK, 2 * G_QK), cb(C_GV, G_W), cb(C_GZ, G_W), cb(S_SMALL, LANES)]
                 + [full(a) for a in params],
        out_specs=[pl.BlockSpec((tb, G_W), lambda b, t: (b * nt + t, 0)),
                   pl.BlockSpec((1, G_HEADS, G_DK, G_DV), lambda b, t: (b, 0, 0, 0))],
        out_shape=[jax.ShapeDtypeStruct((batch * seq, G_W), BF16),
                   jax.ShapeDtypeStruct((batch, G_HEADS, G_DK, G_DV), F32)],
        scratch_shapes=[pltpu.VMEM((G_HEADS, G_DV, G_DK), F32)],
        compiler_params=_cparams(("parallel", "arbitrary")),
        name="gla_prompt",
    )(proj, proj, proj, side, *params)


def _gla_step_kernel(qk_ref, v_ref, z_ref, sm_ref, s_in, acc_ref, ab, a_bias, gn, y_ref, s_out,
                     vec_ref, vrow_ref, orow_ref, *, bb):
    del acc_ref
    qk = qk_ref[...].astype(F32)
    q = qk[:, 0:G_QK] * (G_DK ** -0.5)
    k = qk[:, G_QK:2 * G_QK]
    g = _gla_log_gate(sm_ref[...], ab[...], a_bias[...])
    vec_ref[0] = q
    vec_ref[1] = k
    vec_ref[2] = jnp.exp(g)
    vrow_ref[...] = v_ref[...].astype(F32)
    eye = (_iota2((G_DK, G_DK), 0) == _iota2((G_DK, G_DK), 1)).astype(F32)
    masks = _outer_masks()

    def per_rows(j, _):
        bs = [j * GLA_STEP_ROWS + i for i in range(GLA_STEP_ROWS)]
        vecs = [[vec_ref[i, pl.ds(b, 1), :] for i in range(3)] for b in bs]
        v_b = [vrow_ref[pl.ds(b, 1), :] for b in bs]
        items = [(bi, h) for bi in range(GLA_STEP_ROWS) for h in range(G_HEADS)]
        n = range(len(items))
        hk = lambda x, h: x[:, h * G_DK:(h + 1) * G_DK]
        q_, k_, e_ = ([hk(vecs[bi][i], h) for bi, h in items] for i in range(3))
        v_h = [v_b[bi][:, h * G_DV:(h + 1) * G_DV] for bi, h in items]
        s = [s_in[bs[bi], h] for bi, h in items]
        e_col = [_lane_sum(eye * e_[i]) for i in n]
        qk_dot = [_lane_sum(q_[i] * k_[i]) for i in n]
        o_s = [_mm(_rows8(q_[i] * e_[i]), s[i])[0:1] for i in n]
        kv = [_mm_tn(*_outer_operands(masks, (k_[i], v_h[i]))) for i in n]
        o = [qk_dot[i] * v_h[i] + o_s[i] for i in n]
        for i, (bi, h) in enumerate(items):
            s_out[bs[bi], h] = s[i] * e_col[i] + kv[i]
        for bi in range(GLA_STEP_ROWS):
            orow_ref[pl.ds(bs[bi], 1), :] = jnp.concatenate(
                [o[bi * G_HEADS + h] for h in range(G_HEADS)], axis=-1)
        return 0

    lax.fori_loop(0, bb // GLA_STEP_ROWS, per_rows, 0)
    o = orow_ref[...]
    o_heads = [o[:, h * G_DV:(h + 1) * G_DV] for h in range(G_HEADS)]
    y_ref[...] = _gla_out(o_heads, z_ref[...].astype(F32), gn[...]).astype(y_ref.dtype)


def _gla_step(proj, side, s_all, acc, params, layer, bb):
    nb = proj.shape[0]
    cb = lambda c, w: pl.BlockSpec((bb, w), lambda i, c=c: (i, c // w))
    full = lambda a: pl.BlockSpec(a.shape, lambda i: (0,) * a.ndim)
    st = _layer_block(s_all.shape[2:], bb, layer)
    kern = functools.partial(_gla_step_kernel, bb=bb)
    return pl.pallas_call(
        kern,
        grid=(nb // bb,),
        in_specs=[cb(C_GQK, 2 * G_QK), cb(C_GV, G_W), cb(C_GZ, G_W), cb(S_SMALL, LANES), st,
                  pl.BlockSpec(memory_space=pl.ANY)]
                 + [full(a) for a in params],
        out_specs=[pl.BlockSpec((bb, G_W), lambda i: (i, 0)), st],
        out_shape=[jax.ShapeDtypeStruct((nb, G_W), BF16), jax.ShapeDtypeStruct(acc.shape, F32)],
        input_output_aliases={5: 1},
        scratch_shapes=[pltpu.VMEM((3, bb, G_QK), F32), pltpu.VMEM((bb, G_W), F32),
                        pltpu.VMEM((bb, G_W), F32)],
        compiler_params=_cparams(("parallel",)),
        name="gla_step",
    )(proj, proj, proj, side, s_all, acc, *params)


def _mlstm_out(h_heads, o_pre, z, gn):
    outs = []
    for h in range(M_HEADS):
        y = h_heads[h]
        y = y - jnp.mean(y, axis=-1, keepdims=True)
        outs.append(y * lax.rsqrt(jnp.mean(y * y, axis=-1, keepdims=True) + EPS))
    return jnp.concatenate(outs, axis=-1) * gn * _sigmoid(o_pre) * _silu(z)


def _mlstm_prompt_kernel(q_ref, k_ref, v_ref, o_ref, z_ref, sm_ref, ifb, gn,
                         y_ref, c_ref, n_ref, m_ref, ct_ref, *, n_chunks):
    L = CHUNK

    @pl.when(pl.program_id(1) == 0)
    def _():
        ct_ref[...] = jnp.zeros_like(ct_ref)
        n_ref[...] = jnp.zeros_like(n_ref)
        m_ref[...] = jnp.zeros_like(m_ref)

    causal = _iota2((L, L), 0) <= _iota2((L, L), 1)
    eye = (_iota2((L, L), 0) == _iota2((L, L), 1)).astype(F32)
    scale = M_DK ** -0.5
    neg_inf = -jnp.inf
    C, H = range(n_chunks), range(M_HEADS)
    rws = [slice(c * L, (c + 1) * L) for c in C]
    sls = [slice(h * M_DK, (h + 1) * M_DK) for h in H]

    pre = sm_ref[...] + ifb[...]
    lf = _log_sigmoid(pre)
    b_all = _mm_exact_lhs(_block_tril(n_chunks * L, L), lf)
    lib_all = pre - pltpu.roll(b_all, LANES - (SM_MF - SM_MI), 1)
    sel = (_iota2((LANES, M_HEADS * LANES), 0)
           == SM_MI + _iota2((LANES, M_HEADS * LANES), 1) // LANES).astype(F32)
    lib_bc = _mm_exact_rhs(lib_all, sel)
    comb = jnp.where(_iota2((1, LANES), 1) < SM_MF, lib_all, b_all)
    q, k, v = q_ref[...].astype(F32) * scale, k_ref[...].astype(F32), v_ref[...].astype(F32)
    qs = [[q[rws[c], sls[h]] for h in H] for c in C]
    k_h = [[k[rws[c], sls[h]] for h in H] for c in C]
    v_h = [[v[rws[c], sls[h]] for h in H] for c in C]
    qk_raw = [[_mm_nt(k_h[c][h], qs[c][h]) for h in H] for c in C]
    d_log, d_max, b_row, lib_row = [], [], [], []
    for c in C:
        comb_t = comb[rws[c]].T
        b_row.append([comb_t[SM_MF + h:SM_MF + h + 1, :] for h in H])
        lib_row.append([comb_t[SM_MI + h:SM_MI + h + 1, :] for h in H])
        d_log.append([jnp.where(causal, lib_bc[rws[c], h * LANES:h * LANES + L] + b_row[c][h], neg_inf)
                      for h in H])
        d_max.append([jnp.max(d_log[c][h], axis=0, keepdims=True) for h in H])
    m_prev = [m_ref[0, :, h:h + 1] for h in H]
    m_t, w_inter, carry, w_state, qk = [], [], [], [], []
    for c in C:
        m_inter = [b_row[c][h] + m_prev[h] for h in H]
        m_t.append([jnp.maximum(m_inter[h], d_max[c][h]) for h in H])
        w_inter.append([jnp.exp(m_inter[h] - m_t[c][h]) for h in H])
        m_new = [m_t[c][h][:, L - 1:L] for h in H]
        b_last = [b_row[c][h][:, L - 1:L] for h in H]
        carry.append([jnp.exp(b_last[h] + m_prev[h] - m_new[h]) for h in H])
        w_state.append([jnp.exp(b_last[h] + lib_row[c][h] - m_new[h]) for h in H])
        qk.append([qk_raw[c][h] * jnp.exp(d_log[c][h] - m_t[c][h]) for h in H])
        m_prev = m_new
    for h in H:
        m_ref[0, :, h:h + 1] = m_prev[h]
    wk = [[_mm(eye * w_state[c][h], k_h[c][h]) for h in H] for c in C]
    n_upd = [[_mm(_rows8(w_state[c][h]), k_h[c][h])[0:1] for h in H] for c in C]
    upd = [[_mm_tn(v_h[c][h], wk[c][h]) for h in H] for c in C]
    qkv = [[_mm_tn(v_h[c][h], qk[c][h]) for h in H] for c in C]
    ct = [ct_ref[h] for h in H]
    ns = [n_ref[0, h:h + 1, :] for h in H]
    y_ch = []
    for c in C:
        qc = [_mm_nt(ct[h], qs[c][h]) for h in H]
        qn = [_mm_nt(_rows8(ns[h]), qs[c][h])[0:1] for h in H]
        yy = []
        for h in H:
            num = qkv[c][h] + w_inter[c][h] * qc[h]
            den = jnp.sum(qk[c][h], axis=0, keepdims=True) + w_inter[c][h] * qn[h]
            ht = num / jnp.maximum(jnp.abs(den), jnp.exp(-m_t[c][h]))
            ht = ht - jnp.mean(ht, axis=0, keepdims=True)
            yy.append((ht * lax.rsqrt(jnp.mean(ht * ht, axis=0, keepdims=True) + EPS)).T)
        y_ch.append(yy)
        ct = [carry[c][h] * ct[h] + upd[c][h] for h in H]
        ns = [carry[c][h] * ns[h] + n_upd[c][h] for h in H]
    for h in H:
        ct_ref[h] = ct[h]
        n_ref[0, h:h + 1, :] = ns[h]
    y = jnp.concatenate([jnp.concatenate([y_ch[c][h] for c in C], axis=0) for h in H], axis=-1)
    y_ref[...] = (y * gn[...] * _sigmoid(o_ref[...].astype(F32)) * _silu(z_ref[...].astype(F32))
                  ).astype(y_ref.dtype)

    @pl.when(pl.program_id(1) == pl.num_programs(1) - 1)
    def _():
        for h in H:
            c_ref[0, h] = ct_ref[h].T


def _mlstm_params(ifb, gn):
    ifb_p = jnp.zeros((1, LANES), F32).at[0, SM_MI:SM_MI + 2 * M_HEADS].set(ifb)
    return ifb_p, gn.reshape(1, -1)


def _mlstm_prompt(proj, side, params, batch, seq, tb):
    nt = seq // tb
    cb = lambda c, w: pl.BlockSpec((tb, w), lambda b, t, c=c: (b * nt + t, c // w))
    full = lambda a: pl.BlockSpec(a.shape, lambda b, t: (0,) * a.ndim)
    kern = functools.partial(_mlstm_prompt_kernel, n_chunks=tb // CHUNK)
    return pl.pallas_call(
        kern,
        grid=(batch, nt),
        in_specs=[cb(C_MQ, M_QK), cb(C_MK, M_QK), cb(C_MV, M_W), cb(C_MO, M_W), cb(C_MZ, M_W),
                  cb(S_SMALL, LANES)] + [full(a) for a in params],
        out_specs=[pl.BlockSpec((tb, M_W), lambda b, t: (b * nt + t, 0)),
                   pl.BlockSpec((1, M_HEADS, M_DK, M_DV), lambda b, t: (b, 0, 0, 0)),
                   pl.BlockSpec((1, M_HEADS, M_DK), lambda b, t: (b, 0, 0)),
                   pl.BlockSpec((1, 1, M_HEADS), lambda b, t: (b, 0, 0))],
        out_shape=[jax.ShapeDtypeStruct((batch * seq, M_W), BF16),
                   jax.ShapeDtypeStruct((batch, M_HEADS, M_DK, M_DV), F32),
                   jax.ShapeDtypeStruct((batch, M_HEADS, M_DK), F32),
                   jax.ShapeDtypeStruct((batch, 1, M_HEADS), F32)],
        scratch_shapes=[pltpu.VMEM((M_HEADS, M_DV, M_DK), F32)],
        compiler_params=_cparams(("parallel", "arbitrary")),
        name="mlstm_prompt",
    )(proj, proj, proj, proj, proj, side, *params)


def _mlstm_step_kernel(q_ref, k_ref, v_ref, o_ref, z_ref, sm_ref, c_in, n_in, m_in, acc_ref, ifb, gn,
                       y_ref, c_out, n_out, m_out, vec_ref, sc_ref, hrow_ref, *, bb):
    del acc_ref
    pre = sm_ref[...] + ifb[...]
    li = pre[:, SM_MI:SM_MI + M_HEADS]
    lf = _log_sigmoid(pre[:, SM_MF:SM_MF + M_HEADS])
    m0 = m_in[...]
    m_inter = lf + m0
    m_t = jnp.maximum(m_inter, li)
    m_out[...] = m_t
    pad = lambda x: jnp.concatenate([x, jnp.zeros((bb, LANES - M_HEADS), F32)], axis=-1)
    sc_ref[0] = pad(jnp.exp(m_inter - m_t))
    sc_ref[1] = pad(jnp.exp(li - m_t))
    sc_ref[2] = pad(jnp.exp(-m_t))
    vec_ref[0] = q_ref[...].astype(F32) * (M_DK ** -0.5)
    vec_ref[1] = k_ref[...].astype(F32)
    vec_ref[2] = v_ref[...].astype(F32)
    masks = _outer_masks()

    def per_rows(j, _):
        bs = [j * STEP_ROWS + i for i in range(STEP_ROWS)]
        vecs = [[vec_ref[i, pl.ds(b, 1), :] for i in range(3)] for b in bs]
        scs = [[sc_ref[i, pl.ds(b, 1), :] for i in range(3)] for b in bs]
        items = [(bi, h) for bi in range(STEP_ROWS) for h in range(M_HEADS)]
        n_it = range(len(items))
        hs = lambda x, h: x[:, h * M_DK:(h + 1) * M_DK]
        q_h, k_h, v_h = ([hs(vecs[bi][i], h) for bi, h in items] for i in range(3))
        wi, ei, en = ([scs[bi][i][:, h:h + 1] for bi, h in items] for i in range(3))
        c = [c_in[bs[bi], h] for bi, h in items]
        n = [n_in[bs[bi], h:h + 1, :] for bi, h in items]
        qk = [_lane_sum(q_h[i] * k_h[i]) * ei[i] for i in n_it]
        qn = [_lane_sum(q_h[i] * n[i]) for i in n_it]
        qc = [_mm(_rows8(q_h[i]), c[i])[0:1] for i in n_it]
        kv = [_mm_tn(*_outer_operands(masks, (ei[i] * k_h[i], v_h[i]))) for i in n_it]
        hh = []
        for i, (bi, h) in enumerate(items):
            num = qk[i] * v_h[i] + wi[i] * qc[i]
            den = qk[i] + wi[i] * qn[i]
            hh.append(num / jnp.maximum(jnp.abs(den), en[i]))
            c_out[bs[bi], h] = wi[i] * c[i] + kv[i]
            n_out[bs[bi], h:h + 1, :] = wi[i] * n[i] + ei[i] * k_h[i]
        for bi in range(STEP_ROWS):
            hrow_ref[pl.ds(bs[bi], 1), :] = jnp.concatenate(
                [hh[bi * M_HEADS + h] for h in range(M_HEADS)], axis=-1)
        return 0

    lax.fori_loop(0, bb // STEP_ROWS, per_rows, 0)
    hh = hrow_ref[...]
    h_heads = [hh[:, h * M_DV:(h + 1) * M_DV] for h in range(M_HEADS)]
    y_ref[...] = _mlstm_out(h_heads, o_ref[...].astype(F32), z_ref[...].astype(F32), gn[...]).astype(y_ref.dtype)


def _mlstm_step(proj, side, c_all, n_all, m_all, acc, params, layer, bb):
    nb = proj.shape[0]
    cb = lambda c, w: pl.BlockSpec((bb, w), lambda i, c=c: (i, c // w))
    full = lambda a: pl.BlockSpec(a.shape, lambda i: (0,) * a.ndim)
    cs = _layer_block(c_all.shape[2:], bb, layer)
    ns_in = _layer_block(n_all.shape[2:], bb, layer)
    ms_in = _layer_block(m_all.shape[2:], bb, layer)
    ns = pl.BlockSpec((bb, M_HEADS, M_DK), lambda i: (i, 0, 0))
    ms = pl.BlockSpec((bb, M_HEADS), lambda i: (i, 0))
    kern = functools.partial(_mlstm_step_kernel, bb=bb)
    return pl.pallas_call(
        kern,
        grid=(nb // bb,),
        in_specs=[cb(C_MQ, M_QK), cb(C_MK, M_QK), cb(C_MV, M_W), cb(C_MO, M_W), cb(C_MZ, M_W),
                  cb(S_SMALL, LANES), cs, ns_in, ms_in, pl.BlockSpec(memory_space=pl.ANY)]
                 + [full(a) for a in params],
        out_specs=[pl.BlockSpec((bb, M_W), lambda i: (i, 0)), cs, ns, ms],
        out_shape=[jax.ShapeDtypeStruct((nb, M_W), BF16), jax.ShapeDtypeStruct(acc.shape, F32),
                   jax.ShapeDtypeStruct(n_all.shape[1:], F32), jax.ShapeDtypeStruct(m_all.shape[1:], F32)],
        input_output_aliases={9: 1},
        scratch_shapes=[pltpu.VMEM((3, bb, M_QK), F32), pltpu.VMEM((3, bb, LANES), F32),
                        pltpu.VMEM((bb, M_W), F32)],
        compiler_params=_cparams(("parallel",)),
        name="mlstm_step",
    )(proj, proj, proj, proj, proj, side, c_all, n_all, m_all, acc, *params)


def _shift_rows(proj_rows, side_rows):
    return jnp.concatenate([proj_rows[:, C_RIN:C_RIN + 3 * R_W].astype(F32),
                            side_rows[:, S_RLORA:S_RLORA + 2 * R_LORA]], axis=-1)


def _pick(n, prefs):
    for p in prefs:
        if n % p == 0:
            return p
    return n


def kernel(x_prompt, x_sample, state_rwkv_shift, state_rwkv, state_gla, state_mlstm_C, state_mlstm_n,
           state_mlstm_m, norm_pre, norm_post, w_in, r_mu_shift, r_w_lora_b, r_w0, r_a_lora_b, r_a0,
           r_k_k, r_k_a, r_r_k, r_gn, g_a_lora_b, g_a_bias, g_gn, m_if_bias, m_gn, w_br_rwkv, w_br_gla,
           w_br_mlstm, w_out):
    depth = w_in.shape[0]
    bp, tp, _ = x_prompt.shape
    bs, ts, _ = x_sample.shape
    assert ts == 1 and tp % CHUNK == 0
    assert w_in.shape[1:] == (D_MODEL, N_IN)
    w_t = jnp.transpose(w_in, (0, 2, 1))
    wr, wg, wm, wo = (w.astype(BF16) for w in (w_br_rwkv, w_br_gla, w_br_mlstm, w_out))

    mp = bp * tp
    tm_in = _pick(mp, (2048, 1024, 512, 256, 128, 64))
    tm_mg = _pick(mp, (512, 256, 128, 64))
    tb = _pick(tp, (256, 128, 64))
    bb = _pick(bs, (16, 8))

    xp = x_prompt.reshape(mp, D_MODEL)
    xs = x_sample.reshape(bs, D_MODEL)
    new_p = [[] for _ in range(6)]
    new_s = [[] for _ in range(3)]
    s_lanes = jnp.transpose(state_rwkv, (0, 2, 3, 4, 1))
    acc_r, acc_g, acc_c = (jnp.zeros(s.shape, F32) for s in (s_lanes, state_gla, state_mlstm_C))
    for l in range(depth):
        g_pre = norm_pre[l].reshape(1, -1)
        g_post = norm_post[l].reshape(1, -1)
        rp = _rwkv_params(r_mu_shift[l], r_w_lora_b[l], r_w0[l], r_a_lora_b[l], r_a0[l], r_k_k[l],
                          r_k_a[l], r_r_k[l], r_gn[l])
        gp = _gla_params(g_a_lora_b[l], g_a_bias[l], g_gn[l])
        mparams = _mlstm_params(m_if_bias[l], m_gn[l])

        proj, side, proj_s, side_s = _inproj(xp, xs, g_pre, w_t, l, tm_in)
        y_r, s_r = _rwkv_prompt(proj, side, rp, bp, tp, tb)
        y_g, s_g = _gla_prompt(proj, side, gp, bp, tp, tb)
        y_m, c, n, m = _mlstm_prompt(proj, side, mparams, bp, tp, tb)
        xp = _merge(proj, y_r, y_g, y_m, xp, wr, wg, wm, wo, g_post, l, tm_mg)
        shift = _shift_rows(proj.reshape(bp, tp, C_RLORA)[:, tp - 1], side.reshape(bp, tp, SIDE_W)[:, tp - 1])
        for lst, val in zip(new_p, (shift, s_r, s_g, c, n, m.reshape(bp, M_HEADS))):
            lst.append(val)

        proj, side = proj_s, side_s
        col = lambda x: jnp.broadcast_to(x[:, None], (R_W, LANES))
        y_t, acc_r = _rwkv_step_lanes(proj, side, state_rwkv_shift, s_lanes, acc_r,
                                      rp[:10] + (col(r_r_k[l]), col(r_gn[l])), l)
        y_r = jnp.transpose(y_t, (2, 0, 1)).reshape(bs, R_W)
        y_g, acc_g = _gla_step(proj, side, state_gla, acc_g, gp, l, bb)
        y_m, acc_c, n, m = _mlstm_step(proj, side, state_mlstm_C, state_mlstm_n, state_mlstm_m, acc_c,
                                       mparams, l, bb)
        xs = _merge(proj, y_r, y_g, y_m, xs, wr, wg, wm, wo, g_post, l, bs)
        shift = _shift_rows(proj, side)
        for lst, val in zip(new_s, (shift, n, m)):
            lst.append(val)

    stk = lambda lst: jnp.stack(lst)
    return (xp.reshape(bp, tp, D_MODEL), xs.reshape(bs, ts, D_MODEL),
            *(stk(v) for v in new_p),
            stk(new_s[0]), jnp.transpose(acc_r, (0, 4, 1, 2, 3)), acc_g, acc_c, stk(new_s[1]), stk(new_s[2]))
```

```python
import functools
import math

import jax
import jax.numpy as jnp
from jax import lax
from jax.experimental import pallas as pl
from jax.experimental.pallas import tpu as pltpu

F32 = jnp.float32
BF16 = jnp.bfloat16

D_MODEL = 1024
R_HEADS, R_HD = 8, 64
R_W = R_HEADS * R_HD
R_LORA = 64
R_SHIFT_W = 3 * R_W + 2 * R_LORA
R_GN_EPS = 64e-5
G_HEADS, G_DK, G_DV = 4, 64, 128
G_QK = G_HEADS * G_DK
G_W = G_HEADS * G_DV
G_LORA = 16
G_GATE_TEMP = 16.0
M_HEADS, M_DK, M_DV = 4, 128, 128
M_QK = M_HEADS * M_DK
M_W = M_HEADS * M_DV
EPS = 1e-6

LANES = 128
SUBLANES = 8
VMEM_LIMIT = 48 * 1024 * 1024

C_GATE = 0
C_RZ = 3072
C_GV = 3584
C_GZ = 4096
C_MQ = 4608
C_MK = 5120
C_MV = 5632
C_MO = 6144
C_MZ = 6656
C_GQK = 7168
C_RIN = 7680
C_RLORA = C_RIN + 3 * R_W
C_SMALL = C_RLORA + LANES
SM_GA, SM_MI, SM_MF = 0, 16, 20
N_USED = C_SMALL + LANES
PROJ_TN = 512
N_PAD = -(-N_USED // PROJ_TN) * PROJ_TN
N_MAIN_BLOCKS = C_RLORA // PROJ_TN
_W_SEGMENTS = (("r_in", R_SHIFT_W), ("r_z", R_W), ("g_q", G_QK), ("g_k", G_QK), ("g_v", G_W), ("g_a", G_LORA),
               ("g_z", G_W), ("m_q", M_QK), ("m_k", M_QK), ("m_v", M_W), ("m_i", M_HEADS), ("m_f", M_HEADS),
               ("m_o", M_W), ("m_z", M_W), ("gate", 3 * D_MODEL))
W_OFF = {}
_off = 0
for _name, _width in _W_SEGMENTS:
    W_OFF[_name] = _off
    _off += _width
N_IN = _off
W_MAIN_ROWS = tuple(W_OFF[name] + PROJ_TN * b
                    for name, nblk in (("gate", 6), ("r_z", 1), ("g_v", 1), ("g_z", 1), ("m_q", 1), ("m_k", 1),
                                       ("m_v", 1), ("m_o", 1), ("m_z", 1), ("g_q", 1), ("r_in", 3))
                    for b in range(nblk))
assert len(W_MAIN_ROWS) == N_MAIN_BLOCKS and all(r % SUBLANES == 0 for r in W_MAIN_ROWS)
SIDE_W = 2 * LANES
S_RLORA, S_SMALL = 0, LANES

CHUNK = 64
RWKV_GROUP = 16


def _cparams(sem):
    return pltpu.CompilerParams(dimension_semantics=sem, vmem_limit_bytes=VMEM_LIMIT)


def _mm(a, b):
    return jnp.dot(a.astype(BF16), b.astype(BF16), preferred_element_type=F32)


def _mm_nt(a, b):
    return lax.dot_general(a.astype(BF16), b.astype(BF16), (((1,), (1,)), ((), ())),
                           preferred_element_type=F32)


def _mm_tn(a, b):
    return lax.dot_general(a.astype(BF16), b.astype(BF16), (((0,), (0,)), ((), ())),
                           preferred_element_type=F32)


def _mm_exact_lhs(m01, x):
    hi = x.astype(BF16)
    r1 = x - hi.astype(F32)
    mid = r1.astype(BF16)
    lo = (r1 - mid.astype(F32)).astype(BF16)
    m = m01.astype(BF16)
    d = lambda p: jnp.dot(m, p, preferred_element_type=F32)
    return d(hi) + d(mid) + d(lo)


def _mm_exact_rhs(x, m01):
    hi = x.astype(BF16)
    r1 = x - hi.astype(F32)
    mid = r1.astype(BF16)
    lo = (r1 - mid.astype(F32)).astype(BF16)
    m = m01.astype(BF16)
    d = lambda p: jnp.dot(p, m, preferred_element_type=F32)
    return d(hi) + d(mid) + d(lo)


def _sigmoid(x):
    return 0.5 * jnp.tanh(0.5 * x) + 0.5


def _silu(x):
    return x * _sigmoid(x)


def _softplus(x):
    return jnp.maximum(x, 0.0) + jnp.log(1.0 + jnp.exp(-jnp.abs(x)))


def _log_sigmoid(x):
    return -_softplus(-x)


def _iota2(shape, dim):
    return lax.broadcasted_iota(jnp.int32, shape, dim)


def _lane_sum(x):
    return jnp.sum(x, axis=-1, keepdims=True)


def _rows8(x):
    return jnp.broadcast_to(x, (8, x.shape[1]))


def _outer_masks():
    rid = _iota2((8, 1), 0)
    on = lambda *rows: sum((rid == r).astype(F32) for r in rows)
    return (on(0, 2), on(1), on(0, 1), on(2)), (on(3, 5), on(4), on(3, 4), on(5))


def _outer_operands(masks, *pairs):
    a_op, b_op = 0.0, 0.0
    for (a, b), (m_ahi, m_alo, m_bhi, m_blo) in zip(pairs, masks):
        a_hi = a.astype(BF16).astype(F32)
        b_hi = b.astype(BF16).astype(F32)
        a_op = a_op + m_ahi * a_hi + m_alo * (a - a_hi)
        b_op = b_op + m_bhi * b_hi + m_blo * (b - b_hi)
    return a_op, b_op


def _block_tril(n, blk):
    r, c = _iota2((n, n), 0), _iota2((n, n), 1)
    return ((r - c).astype(jnp.uint32) <= (r % blk).astype(jnp.uint32)).astype(F32)


def _inproj_kernel(rows_ref, x_ref, xs_ref, g_ref, w_ref, wl_ref, wa_ref, wif_ref,
                   o_ref, side_ref, os_ref, sides_ref, h_ref, hs_ref, ws_ref):
    del rows_ref
    i, j = pl.program_id(0), pl.program_id(1)

    def normed(x):
        return (x * lax.rsqrt(jnp.mean(x * x, axis=-1, keepdims=True) + EPS) * g_ref[...]).astype(BF16)

    @pl.when(j == 0)
    def _():
        h_ref[...] = normed(x_ref[...])
        ws_ref[...] = jnp.zeros_like(ws_ref)
        ws_ref[S_RLORA:S_RLORA + 2 * R_LORA, :] = wl_ref[0]
        ws_ref[S_SMALL + SM_GA:S_SMALL + SM_GA + G_LORA, :] = wa_ref[0]
        ws_ref[S_SMALL + SM_MI:S_SMALL + SM_MI + 2 * M_HEADS, :] = wif_ref[0]
        side_ref[...] = _mm_nt(h_ref[...], ws_ref[...])

    @pl.when(jnp.logical_and(i == 0, j == 0))
    def _():
        hs_ref[...] = normed(xs_ref[...])
        sides_ref[...] = _mm_nt(hs_ref[...], ws_ref[...])

    w = w_ref[0].astype(BF16)
    o_ref[...] = _mm_nt(h_ref[...], w).astype(BF16)

    @pl.when(i == 0)
    def _():
        os_ref[...] = _mm_nt(hs_ref[...], w).astype(BF16)


def _inproj(x2d, xs2d, g_pre, w_t, layer, tm):
    m, ms = x2d.shape[0], xs2d.shape[0]
    w_blk = lambda n: (pl.Element(1), pl.Element(n), pl.Element(D_MODEL))
    w_rows = lambda n, start: pl.BlockSpec(w_blk(n), lambda i, j, rows, start=start: (layer, start, 0))
    last = N_MAIN_BLOCKS - 1
    grid_spec = pltpu.PrefetchScalarGridSpec(
        num_scalar_prefetch=1,
        grid=(m // tm, N_MAIN_BLOCKS),
        in_specs=[pl.BlockSpec((tm, D_MODEL), lambda i, j, rows: (i, 0)),
                  pl.BlockSpec((ms, D_MODEL), lambda i, j, rows: (0, 0)),
                  pl.BlockSpec((1, D_MODEL), lambda i, j, rows: (0, 0)),
                  pl.BlockSpec(w_blk(PROJ_TN), lambda i, j, rows: (layer, rows[j] * SUBLANES, 0)),
                  w_rows(2 * R_LORA, W_OFF["r_in"] + 3 * R_W),
                  w_rows(G_LORA, W_OFF["g_a"]),
                  w_rows(2 * M_HEADS, W_OFF["m_i"])],
        out_specs=[pl.BlockSpec((tm, PROJ_TN), lambda i, j, rows: (i, j)),
                   pl.BlockSpec((tm, SIDE_W), lambda i, j, rows: (i, 0)),
                   pl.BlockSpec((ms, PROJ_TN), lambda i, j, rows: (0, jnp.where(i == 0, j, last))),
                   pl.BlockSpec((ms, SIDE_W), lambda i, j, rows: (0, 0))],
        scratch_shapes=[pltpu.VMEM((tm, D_MODEL), BF16), pltpu.VMEM((ms, D_MODEL), BF16),
                        pltpu.VMEM((SIDE_W, D_MODEL), F32)])
    return pl.pallas_call(
        _inproj_kernel,
        grid_spec=grid_spec,
        out_shape=[jax.ShapeDtypeStruct((m, C_RLORA), BF16), jax.ShapeDtypeStruct((m, SIDE_W), F32),
                   jax.ShapeDtypeStruct((ms, C_RLORA), BF16), jax.ShapeDtypeStruct((ms, SIDE_W), F32)],
        compiler_params=_cparams(("arbitrary", "arbitrary")),
        name="inproj",
    )(jnp.asarray([r // SUBLANES for r in W_MAIN_ROWS], jnp.int32), x2d, xs2d, g_pre, w_t, w_t, w_t, w_t)


def _merge_kernel(gate_ref, yr_ref, yg_ref, ym_ref, x_ref, wr_ref, wg_ref, wm_ref, wo_ref, gp_ref, o_ref):
    d = D_MODEL
    gate = lambda b: _sigmoid(gate_ref[:, b * d:(b + 1) * d].astype(F32))
    merged = (gate(0) * _mm(yr_ref[...], wr_ref[...]) + gate(1) * _mm(yg_ref[...], wg_ref[...])
              + gate(2) * _mm(ym_ref[...], wm_ref[...]))
    o = _mm(merged, wo_ref[...])
    y = o * lax.rsqrt(jnp.mean(o * o, axis=-1, keepdims=True) + EPS)
    o_ref[...] = x_ref[...] + y * gp_ref[...]


def _merge(proj, y_r, y_g, y_m, x2d, w_r, w_g, w_m, w_o, g_post, layer, tm):
    m = x2d.shape[0]
    row = lambda w: pl.BlockSpec((tm, w), lambda i: (i, 0))
    full = lambda a: pl.BlockSpec(a.shape, lambda i: (0, 0))
    wl = lambda a: pl.BlockSpec((None,) + a.shape[1:], lambda i: (layer, 0, 0))
    return pl.pallas_call(
        _merge_kernel,
        grid=(m // tm,),
        in_specs=[row(3 * D_MODEL), row(R_W), row(G_W), row(M_W), row(D_MODEL),
                  wl(w_r), wl(w_g), wl(w_m), wl(w_o), full(g_post)],
        out_specs=row(D_MODEL),
        out_shape=jax.ShapeDtypeStruct((m, D_MODEL), F32),
        compiler_params=_cparams(("parallel",)),
        name="merge",
    )(proj, y_r, y_g, y_m, x2d, w_r, w_g, w_m, w_o, g_post)


def _rwkv_front(r_in, k_in, v_in, l_in, prev_r, prev_k, prev_v, prev_l, p):
    (mu_r, mu_k, mu_v, mu_l, wb, w0, ab, a0, k_k, k_a) = p
    r = r_in + mu_r * (prev_r - r_in)
    k = k_in + mu_k * (prev_k - k_in)
    v = v_in + mu_v * (prev_v - v_in)
    lo = l_in + mu_l * (prev_l - l_in)
    log_w = -_softplus(-(w0 + _mm(jnp.tanh(lo), wb))) - 0.5
    lw = -jnp.exp(log_w)
    a = _sigmoid(a0 + _mm(lo, ab))
    kk = k * k_k
    k2 = k * (1.0 + (a - 1.0) * k_a)
    return r, k2, v, lw, a, kk


def _rwkv_prompt_kernel(z_ref, r_ref, k_ref, v_ref, l_ref,
                        mu_r, mu_k, mu_v, mu_l, wb, w0, ab, a0, k_k, k_a, r_k, gn,
                        y_ref, s_ref, carry_ref, sp_ref, *, n_chunks):
    L = CHUNK

    @pl.when(pl.program_id(1) == 0)
    def _():
        sp_ref[...] = jnp.zeros_like(sp_ref)
        carry_ref[...] = jnp.zeros_like(carry_ref)

    tb = n_chunks * L
    params = tuple(x[...] for x in (mu_r, mu_k, mu_v, mu_l, wb, w0, ab, a0, k_k, k_a))
    assert L == R_HD
    m_a = (_iota2((1, LANES), 1) < R_HD).astype(F32)
    m_b = 1.0 - m_a
    m_a16, m_b16 = m_a.astype(BF16), m_b.astype(BF16)
    trow = _iota2((L, LANES), 0)
    tcol = _iota2((L, LANES), 1) % L
    eye = (tcol == trow).astype(F32)
    blk_mask = (trow // 16) == (tcol // 16)
    r2 = _iota2((2 * L, LANES), 0)
    g_mask = (_iota2((2 * L, LANES), 1) % L) < (r2 % L) + (r2 >= L).astype(jnp.int32)
    same_head = (_iota2((LANES, LANES), 0) // R_HD) == (_iota2((LANES, LANES), 1) // R_HD)
    first_row = _iota2((tb, 1), 0) == 0

    def seg_sum(x):
        return _lane_sum(x * m_a) * m_a + _lane_sum(x * m_b) * m_b

    def bd(y):
        yb = y.astype(BF16)
        return jnp.concatenate([yb * m_a16, yb * m_b16], axis=0)

    def shifted(x, carry):
        return jnp.where(first_row, carry, pltpu.roll(x, 1, 0))

    r_in, k_in, v_in = (x[...].astype(F32) for x in (r_ref, k_ref, v_ref))
    l_in = l_ref[...]
    prev_r = shifted(r_in, carry_ref[0:1, 0:R_W])
    prev_k = shifted(k_in, carry_ref[0:1, R_W:2 * R_W])
    prev_v = shifted(v_in, carry_ref[0:1, 2 * R_W:3 * R_W])
    prev_l = shifted(l_in, carry_ref[0:1, 3 * R_W:R_SHIFT_W])
    carry_ref[0:1, 0:R_W] = r_in[tb - 1:tb, :]
    carry_ref[0:1, R_W:2 * R_W] = k_in[tb - 1:tb, :]
    carry_ref[0:1, 2 * R_W:3 * R_W] = v_in[tb - 1:tb, :]
    carry_ref[0:1, 3 * R_W:R_SHIFT_W] = l_in[tb - 1:tb, :]
    r, k2, v, lw, a, kk = _rwkv_front(r_in, k_in, v_in, l_in, prev_r, prev_k, prev_v, prev_l, params)

    cum = _mm_exact_lhs(_block_tril(tb, L), lw)
    p_inc = jnp.exp(cum)
    p_exc = jnp.exp(cum - lw)
    p_inv = jnp.exp(-cum)
    r_t = r * p_inc
    k_h = k2 * p_inv

    n_pairs = R_HEADS // 2
    CP = [(c, p) for c in range(n_chunks) for p in range(n_pairs)]
    N = range(len(CP))
    kap_t, beta_h = [], []
    for p in range(n_pairs):
        sl = slice(p * LANES, (p + 1) * LANES)
        kk_p = kk[:, sl]
        kap = kk_p * jnp.minimum(lax.rsqrt(seg_sum(kk_p * kk_p)), 1e12)
        beta_h.append(kap * a[:, sl] * p_inv[:, sl])
        kap_t.append(kap * p_exc[:, sl])
    cut = lambda x, c, p: x[c * L:(c + 1) * L, p * LANES:(p + 1) * LANES]
    kap_t = [kap_t[p][c * L:(c + 1) * L] for c, p in CP]
    beta_h = [beta_h[p][c * L:(c + 1) * L] for c, p in CP]
    rt_c = [cut(r_t, c, p) for c, p in CP]
    kh_c = [cut(k_h, c, p) for c, p in CP]
    v_c = [cut(v, c, p) for c, p in CP]
    p_last = [p_inc[(c + 1) * L - 1:(c + 1) * L, p * LANES:(p + 1) * LANES] for c, p in CP]
    bhl = [beta_h[i] * p_last[i] for i in N]
    khl = [kh_c[i] * p_last[i] for i in N]
    n_bd, c_c, qt, y0 = {}, {}, {}, {}
    for g0 in range(0, len(CP), RWKV_GROUP):
        G = range(g0, min(g0 + RWKV_GROUP, len(CP)))
        lhs2 = {i: jnp.concatenate([kap_t[i], rt_c[i]], axis=0) for i in G}
        bd_v = {i: bd(v_c[i]) for i in G}
        gb = {i: jnp.where(g_mask, _mm_nt(lhs2[i], bd(beta_h[i])), 0.0) for i in G}
        gk = {i: jnp.where(g_mask, _mm_nt(lhs2[i], bd(kh_c[i])), 0.0) for i in G}
        a_m = {i: gb[i][0:L] for i in G}
        d = {i: jnp.where(blk_mask, a_m[i], 0.0) for i in G}
        akv = {i: _mm(gk[i][0:L], bd_v[i]) for i in G}
        d2 = {i: _mm(d[i], bd(d[i])) for i in G}
        bd_d2 = {i: bd(d2[i]) for i in G}
        d4 = {i: _mm(d2[i], bd_d2[i]) for i in G}
        dd2 = {i: _mm(d[i], bd_d2[i]) for i in G}
        d8 = {i: _mm(d4[i], bd(d4[i])) for i in G}
        x = {i: eye - d[i] + d2[i] - dd2[i] for i in G}
        x = {i: x[i] + _mm(x[i], bd(d4[i])) for i in G}
        td = {i: x[i] + _mm(x[i], bd(d8[i])) for i in G}
        m = {i: _mm(td[i], bd(a_m[i] - d[i])) for i in G}
        m2 = {i: _mm(m[i], bd(m[i])) for i in G}
        q = {i: eye - m[i] + m2[i] - _mm(m[i], bd(m2[i])) for i in G}
        t_inv = {i: _mm(q[i], bd(td[i])) for i in G}
        kbar = {i: _mm(t_inv[i], bd(kap_t[i])) for i in G}
        u0 = {i: -_mm(t_inv[i], bd(akv[i])) for i in G}
        n_bd.update({i: jnp.where(same_head, _mm_tn(kbar[i], bhl[i]), 0.0) for i in G})
        cz = {i: _mm_tn(jnp.concatenate([u0[i], v_c[i]], axis=0), jnp.concatenate([bhl[i], khl[i]], axis=0))
              for i in G}
        c_c.update({i: cz[i][0:R_HD] * m_a + cz[i][R_HD:2 * R_HD] * m_b for i in G})
        qt.update({i: rt_c[i] - _mm(gb[i][L:2 * L], bd(kbar[i])) for i in G})
        y0.update({i: _mm(gb[i][L:2 * L], bd(u0[i])) + _mm(gk[i][L:2 * L], bd_v[i]) for i in G})
    s = [sp_ref[p] for p in range(n_pairs)]
    y_ch = []
    for c in range(n_chunks):
        ids = [c * n_pairs + p for p in range(n_pairs)]
        y_ch.append([_mm_nt(qt[i], bd(s[p])) + y0[i] for p, i in enumerate(ids)])
        s = [s[p] * p_last[i] - _mm(s[p], n_bd[i]) + c_c[i] for p, i in enumerate(ids)]
    for p in range(n_pairs):
        sp_ref[p] = s[p]
    y_pairs = [jnp.concatenate([y_ch[c][p] for c in range(n_chunks)], axis=0) for p in range(n_pairs)]
    outs = []
    for p in range(n_pairs):
        sl = slice(p * LANES, (p + 1) * LANES)
        y = y_pairs[p]
        y = y - seg_sum(y) * (1.0 / R_HD)
        y = y * lax.rsqrt(seg_sum(y * y) * (1.0 / R_HD) + R_GN_EPS)
        bonus = seg_sum(r[:, sl] * k2[:, sl] * r_k[:, sl]) * v[:, sl]
        outs.append(y * gn[:, sl] + bonus)
    y_ref[...] = (jnp.concatenate(outs, axis=-1) * _silu(z_ref[...].astype(F32))).astype(y_ref.dtype)

    @pl.when(pl.program_id(1) == pl.num_programs(1) - 1)
    def _():
        for p in range(n_pairs):
            sp = sp_ref[p]
            s_ref[0, 2 * p] = sp[:, 0:R_HD]
            s_ref[0, 2 * p + 1] = sp[:, R_HD:2 * R_HD]


def _rwkv_params(mu, wb, w0, ab, a0, k_k, k_a, r_k, gn):
    row = lambda x: x.reshape(1, -1)
    zeros = jnp.zeros((R_LORA, R_W), F32)
    wb_p = jnp.concatenate([wb, zeros], axis=0).astype(BF16)
    ab_p = jnp.concatenate([zeros, ab], axis=0).astype(BF16)
    return (row(mu[0:R_W]), row(mu[R_W:2 * R_W]), row(mu[2 * R_W:3 * R_W]), row(mu[3 * R_W:]),
            wb_p, row(w0), ab_p, row(a0), row(k_k), row(k_a), row(r_k), row(gn))


def _rwkv_prompt(proj, side, params, batch, seq, tb):
    nt = seq // tb
    cb = lambda c, w: pl.BlockSpec((tb, w), lambda b, t, c=c: (b * nt + t, c // w))
    full = lambda a: pl.BlockSpec(a.shape, lambda b, t: (0,) * a.ndim)
    kern = functools.partial(_rwkv_prompt_kernel, n_chunks=tb // CHUNK)
    return pl.pallas_call(
        kern,
        grid=(batch, nt),
        in_specs=[cb(C_RZ, R_W), cb(C_RIN, R_W), cb(C_RIN + R_W, R_W), cb(C_RIN + 2 * R_W, R_W),
                  cb(S_RLORA, LANES)] + [full(a) for a in params],
        out_specs=[pl.BlockSpec((tb, R_W), lambda b, t: (b * nt + t, 0)),
                   pl.BlockSpec((1, R_HEADS, R_HD, R_HD), lambda b, t: (b, 0, 0, 0))],
        out_shape=[jax.ShapeDtypeStruct((batch * seq, R_W), BF16),
                   jax.ShapeDtypeStruct((batch, R_HEADS, R_HD, R_HD), F32)],
        scratch_shapes=[pltpu.VMEM((8, R_SHIFT_W), F32),
                        pltpu.VMEM((R_HEADS // 2, R_HD, LANES), F32)],
        compiler_params=_cparams(("parallel", "arbitrary")),
        name="rwkv_prompt",
    )(proj, proj, proj, proj, side, *params)


STEP_ROWS = 4
GLA_STEP_ROWS = 2


def _rwkv_lanes_kernel(z_ref, r_ref, k_ref, v_ref, l_ref, pr_ref, pk_ref, pv_ref, pl_ref, s_in, acc_ref,
                       mu_r, mu_k, mu_v, mu_l, wb, w0, ab, a0, k_k, k_a, rk_t, gn_t,
                       y_ref, s_out, vec_ref, ycol_ref):
    del acc_ref
    h = pl.program_id(0)

    @pl.when(h == 0)
    def _():
        params = tuple(x[...] for x in (mu_r, mu_k, mu_v, mu_l, wb, w0, ab, a0, k_k, k_a))
        r, k2, v, lw, a, kk = _rwkv_front(r_ref[...].astype(F32), k_ref[...].astype(F32),
                                          v_ref[...].astype(F32), l_ref[...], pr_ref[...], pk_ref[...],
                                          pv_ref[...], pl_ref[...], params)
        kaps, betas = [], []
        for hh in range(R_HEADS):
            sl = slice(hh * R_HD, (hh + 1) * R_HD)
            kk_h = kk[:, sl]
            kap = kk_h / jnp.maximum(jnp.sqrt(_lane_sum(kk_h * kk_h)), 1e-12)
            kaps.append(kap)
            betas.append(kap * a[:, sl])
        rows = (jnp.concatenate(kaps, axis=-1), jnp.concatenate(betas, axis=-1), jnp.exp(lw), k2, v, r,
                _silu(z_ref[...].astype(F32)))
        for i, x in enumerate(rows):
            vec_ref[i] = x.T

    base = pl.multiple_of(h * R_HD, R_HD)
    chan = pl.ds(base, R_HD)
    kap_t, beta_t, w_t, k_t, v_t, r_t, gz_t = (vec_ref[i, chan, :] for i in range(7))

    def per_v(vi, _):
        s = s_in[vi]
        sa = -jnp.sum(s * kap_t, axis=0, keepdims=True)
        v_row = vec_ref[4, pl.ds(base + vi, 1), :]
        s_new = s * w_t + sa * beta_t + v_row * k_t
        s_out[vi] = s_new
        ycol_ref[pl.ds(vi, 1), :] = jnp.sum(s_new * r_t, axis=0, keepdims=True)
        return 0

    lax.fori_loop(0, R_HD, per_v, 0, unroll=8)
    y = ycol_ref[...]
    y = y - jnp.mean(y, axis=0, keepdims=True)
    y = y * lax.rsqrt(jnp.mean(y * y, axis=0, keepdims=True) + R_GN_EPS)
    bonus = jnp.sum(r_t * k_t * rk_t[chan, :], axis=0, keepdims=True) * v_t
    y_ref[...] = ((y * gn_t[chan, :] + bonus) * gz_t).astype(y_ref.dtype)


def _rwkv_step_lanes(proj, side, shift_all, s_lanes, acc, params, layer):
    nb = proj.shape[0]
    assert nb == LANES
    cb = lambda c, w: pl.BlockSpec((nb, w), lambda h, c=c: (0, c // w))
    sh = lambda c, w: pl.BlockSpec((None, nb, w), lambda h, c=c: (layer, 0, c // w))
    full = lambda a: pl.BlockSpec(a.shape, lambda h: (0,) * a.ndim)
    st = pl.BlockSpec((None, None, R_HD, R_HD, nb), lambda h: (layer, h, 0, 0, 0))
    return pl.pallas_call(
        _rwkv_lanes_kernel,
        grid=(R_HEADS,),
        in_specs=[cb(C_RZ, R_W), cb(C_RIN, R_W), cb(C_RIN + R_W, R_W), cb(C_RIN + 2 * R_W, R_W),
                  cb(S_RLORA, LANES),
                  sh(0, R_W), sh(R_W, R_W), sh(2 * R_W, R_W), sh(3 * R_W, LANES), st,
                  pl.BlockSpec(memory_space=pl.ANY)]
                 + [full(a) for a in params],
        out_specs=[pl.BlockSpec((None, R_HD, nb), lambda h: (h, 0, 0)), st],
        out_shape=[jax.ShapeDtypeStruct((R_HEADS, R_HD, nb), BF16),
                   jax.ShapeDtypeStruct(acc.shape, F32)],
        input_output_aliases={10: 1},
        scratch_shapes=[pltpu.VMEM((7, R_W, nb), F32), pltpu.VMEM((R_HD, nb), F32)],
        compiler_params=_cparams(("arbitrary",)),
        name="rwkv_step",
    )(proj, proj, proj, proj, side, shift_all, shift_all, shift_all, shift_all, s_lanes, acc, *params)


def _layer_block(shape_tail, bb, layer):
    zeros = (0,) * len(shape_tail)
    return pl.BlockSpec((None, bb) + tuple(shape_tail), lambda i: (layer, i) + zeros)


def _gla_log_gate(small, ab_p, a_bias):
    return _log_sigmoid(_mm(small, ab_p) + a_bias) / G_GATE_TEMP


def _gla_out(o_heads, z, gn):
    outs = []
    for h in range(G_HEADS):
        o = o_heads[h]
        outs.append(o * lax.rsqrt(jnp.mean(o * o, axis=-1, keepdims=True) + EPS))
    return jnp.concatenate(outs, axis=-1) * gn * _silu(z)


def _gla_prompt_kernel(qk_ref, v_ref, z_ref, sm_ref, ab, a_bias, gn, y_ref, s_ref, st_ref, *, n_chunks):
    L = CHUNK
    last = pl.num_programs(1) - 1

    @pl.when(pl.program_id(1) == 0)
    def _():
        st_ref[...] = jnp.zeros_like(st_ref)

    tril_incl = _iota2((L, L), 1) <= _iota2((L, L), 0)
    scale = G_DK ** -0.5
    C, H = range(n_chunks), range(G_HEADS)
    rws = [slice(c * L, (c + 1) * L) for c in C]
    sks = [slice(h * G_DK, (h + 1) * G_DK) for h in H]

    qk = qk_ref[...].astype(F32)
    q, k = qk[:, 0:G_QK] * scale, qk[:, G_QK:2 * G_QK]
    v = v_ref[...].astype(F32)
    la = _gla_log_gate(sm_ref[...], ab[...], a_bias[...])
    cum = _mm_exact_lhs(_block_tril(n_chunks * L, L), la)
    q0, qe, ke, kl, e_last = [], [], [], [], []
    for c in C:
        cum_c = cum[rws[c]]
        ref_row = cum_c[L // 2:L // 2 + 1, :]
        last_row = cum_c[L - 1:L, :]
        q0.append(q[rws[c]] * jnp.exp(cum_c))
        qe.append(q[rws[c]] * jnp.exp(cum_c - ref_row))
        ke.append(k[rws[c]] * jnp.exp(ref_row - cum_c))
        kl.append(k[rws[c]] * jnp.exp(last_row - cum_c))
        e_last.append(jnp.exp(last_row))
    v_h = [[v[rws[c], h * G_DV:(h + 1) * G_DV] for h in H] for c in C]
    att = [[jnp.where(tril_incl, _mm_nt(qe[c][:, sks[h]], ke[c][:, sks[h]]), 0.0) for h in H] for c in C]
    upd = [[_mm_tn(v_h[c][h], kl[c][:, sks[h]]) for h in H] for c in C]
    av = [[_mm(att[c][h], v_h[c][h]) for h in H] for c in C]
    st = [st_ref[h] for h in H]
    o_ch = []
    for c in C:
        o_ch.append([av[c][h] + _mm_nt(q0[c][:, sks[h]], st[h]) for h in H])
        st = [st[h] * e_last[c][:, sks[h]] + upd[c][h] for h in H]
    for h in H:
        st_ref[h] = st[h]
    o_heads = [jnp.concatenate([o_ch[c][h] for c in C], axis=0) for h in H]
    y_ref[...] = _gla_out(o_heads, z_ref[...].astype(F32), gn[...]).astype(y_ref.dtype)

    @pl.when(pl.program_id(1) == last)
    def _():
        for h in range(G_HEADS):
            s_ref[0, h] = st_ref[h].T


def _gla_params(ab, a_bias, gn):
    ab_p = jnp.zeros((LANES, G_QK), F32).at[SM_GA:SM_GA + G_LORA].set(ab).astype(BF16)
    return ab_p, a_bias.reshape(1, -1), gn.reshape(1, -1)


def _gla_prompt(proj, side, params, batch, seq, tb):
    nt = seq // tb
    cb = lambda c, w: pl.BlockSpec((tb, w), lambda b, t, c=c: (b * nt + t, c // w))
    full = lambda a: pl.BlockSpec(a.shape, lambda b, t: (0,) * a.ndim)
    kern = functools.partial(_gla_prompt_kernel, n_chunks=tb // CHUNK)
    return pl.pallas_call(
        kern,
        grid=(batch, nt),
        in_specs=[cb(C_GQK, 2 * G_QK), cb(C_GV, G_W), cb(C_GZ, G_W), cb(S_SMALL, LANES)]
                 + [full(a) for a in params],
        out_specs=[pl.BlockSpec((tb, G_W), lambda b, t: (b * nt + t, 0)),
                   pl.BlockSpec((1, G_HEADS, G_DK, G_DV), lambda b, t: (b, 0, 0, 0))],
        out_shape=[jax.ShapeDtypeStruct((batch * seq, G_W), BF16),
                   jax.ShapeDtypeStruct((batch, G_HEADS, G_DK, G_DV), F32)],
        scratch_shapes=[pltpu.VMEM((G_HEADS, G_DV, G_DK), F32)],
        compiler_params=_cparams(("parallel", "arbitrary")),
        name="gla_prompt",
    )(proj, proj, proj, side, *params)


def _gla_step_kernel(qk_ref, v_ref, z_ref, sm_ref, s_in, acc_ref, ab, a_bias, gn, y_ref, s_out,
                     vec_ref, vrow_ref, orow_ref, *, bb):
    del acc_ref
    qk = qk_ref[...].astype(F32)
    q = qk[:, 0:G_QK] * (G_DK ** -0.5)
    k = qk[:, G_QK:2 * G_QK]
    g = _gla_log_gate(sm_ref[...], ab[...], a_bias[...])
    vec_ref[0] = q
    vec_ref[1] = k
    vec_ref[2] = jnp.exp(g)
    vrow_ref[...] = v_ref[...].astype(F32)
    eye = (_iota2((G_DK, G_DK), 0) == _iota2((G_DK, G_DK), 1)).astype(F32)
    masks = _outer_masks()

    def per_rows(j, _):
        bs = [j * GLA_STEP_ROWS + i for i in range(GLA_STEP_ROWS)]
        vecs = [[vec_ref[i, pl.ds(b, 1), :] for i in range(3)] for b in bs]
        v_b = [vrow_ref[pl.ds(b, 1), :] for b in bs]
        items = [(bi, h) for bi in range(GLA_STEP_ROWS) for h in range(G_HEADS)]
        n = range(len(items))
        hk = lambda x, h: x[:, h * G_DK:(h + 1) * G_DK]
        q_, k_, e_ = ([hk(vecs[bi][i], h) for bi, h in items] for i in range(3))
        v_h = [v_b[bi][:, h * G_DV:(h + 1) * G_DV] for bi, h in items]
        s = [s_in[bs[bi], h] for bi, h in items]
        e_col = [_lane_sum(eye * e_[i]) for i in n]
        qk_dot = [_lane_sum(q_[i] * k_[i]) for i in n]
        o_s = [_mm(_rows8(q_[i] * e_[i]), s[i])[0:1] for i in n]
        kv = [_mm_tn(*_outer_operands(masks, (k_[i], v_h[i]))) for i in n]
        o = [qk_dot[i] * v_h[i] + o_s[i] for i in n]
        for i, (bi, h) in enumerate(items):
            s_out[bs[bi], h] = s[i] * e_col[i] + kv[i]
        for bi in range(GLA_STEP_ROWS):
            orow_ref[pl.ds(bs[bi], 1), :] = jnp.concatenate(
                [o[bi * G_HEADS + h] for h in range(G_HEADS)], axis=-1)
        return 0

    lax.fori_loop(0, bb // GLA_STEP_ROWS, per_rows, 0)
    o = orow_ref[...]
    o_heads = [o[:, h * G_DV:(h + 1) * G_DV] for h in range(G_HEADS)]
    y_ref[...] = _gla_out(o_heads, z_ref[...].astype(F32), gn[...]).astype(y_ref.dtype)


def _gla_step(proj, side, s_all, acc, params, layer, bb):
    nb = proj.shape[0]
    cb = lambda c, w: pl.BlockSpec((bb, w), lambda i, c=c: (i, c // w))
    full = lambda a: pl.BlockSpec(a.shape, lambda i: (0,) * a.ndim)
    st = _layer_block(s_all.shape[2:], bb, layer)
    kern = functools.partial(_gla_step_kernel, bb=bb)
    return pl.pallas_call(
        kern,
        grid=(nb // bb,),
        in_specs=[cb(C_GQK, 2 * G_QK), cb(C_GV, G_W), cb(C_GZ, G_W), cb(S_SMALL, LANES), st,
                  pl.BlockSpec(memory_space=pl.ANY)]
                 + [full(a) for a in params],
        out_specs=[pl.BlockSpec((bb, G_W), lambda i: (i, 0)), st],
        out_shape=[jax.ShapeDtypeStruct((nb, G_W), BF16), jax.ShapeDtypeStruct(acc.shape, F32)],
        input_output_aliases={5: 1},
        scratch_shapes=[pltpu.VMEM((3, bb, G_QK), F32), pltpu.VMEM((bb, G_W), F32),
                        pltpu.VMEM((bb, G_W), F32)],
        compiler_params=_cparams(("parallel",)),
        name="gla_step",
    )(proj, proj, proj, side, s_all, acc, *params)


def _mlstm_out(h_heads, o_pre, z, gn):
    outs = []
    for h in range(M_HEADS):
        y = h_heads[h]
        y = y - jnp.mean(y, axis=-1, keepdims=True)
        outs.append(y * lax.rsqrt(jnp.mean(y * y, axis=-1, keepdims=True) + EPS))
    return jnp.concatenate(outs, axis=-1) * gn * _sigmoid(o_pre) * _silu(z)


def _mlstm_prompt_kernel(q_ref, k_ref, v_ref, o_ref, z_ref, sm_ref, ifb, gn,
                         y_ref, c_ref, n_ref, m_ref, ct_ref, *, n_chunks):
    L = CHUNK

    @pl.when(pl.program_id(1) == 0)
    def _():
        ct_ref[...] = jnp.zeros_like(ct_ref)
        n_ref[...] = jnp.zeros_like(n_ref)
        m_ref[...] = jnp.zeros_like(m_ref)

    causal = _iota2((L, L), 0) <= _iota2((L, L), 1)
    eye = (_iota2((L, L), 0) == _iota2((L, L), 1)).astype(F32)
    scale = M_DK ** -0.5
    neg_inf = -jnp.inf
    C, H = range(n_chunks), range(M_HEADS)
    rws = [slice(c * L, (c + 1) * L) for c in C]
    sls = [slice(h * M_DK, (h + 1) * M_DK) for h in H]

    pre = sm_ref[...] + ifb[...]
    lf = _log_sigmoid(pre)
    b_all = _mm_exact_lhs(_block_tril(n_chunks * L, L), lf)
    lib_all = pre - pltpu.roll(b_all, LANES - (SM_MF - SM_MI), 1)
    sel = (_iota2((LANES, M_HEADS * LANES), 0)
           == SM_MI + _iota2((LANES, M_HEADS * LANES), 1) // LANES).astype(F32)
    lib_bc = _mm_exact_rhs(lib_all, sel)
    comb = jnp.where(_iota2((1, LANES), 1) < SM_MF, lib_all, b_all)
    q, k, v = q_ref[...].astype(F32) * scale, k_ref[...].astype(F32), v_ref[...].astype(F32)
    qs = [[q[rws[c], sls[h]] for h in H] for c in C]
    k_h = [[k[rws[c], sls[h]] for h in H] for c in C]
    v_h = [[v[rws[c], sls[h]] for h in H] for c in C]
    qk_raw = [[_mm_nt(k_h[c][h], qs[c][h]) for h in H] for c in C]
    d_log, d_max, b_row, lib_row = [], [], [], []
    for c in C:
        comb_t = comb[rws[c]].T
        b_row.append([comb_t[SM_MF + h:SM_MF + h + 1, :] for h in H])
        lib_row.append([comb_t[SM_MI + h:SM_MI + h + 1, :] for h in H])
        d_log.append([jnp.where(causal, lib_bc[rws[c], h * LANES:h * LANES + L] + b_row[c][h], neg_inf)
                      for h in H])
        d_max.append([jnp.max(d_log[c][h], axis=0, keepdims=True) for h in H])
    m_prev = [m_ref[0, :, h:h + 1] for h in H]
    m_t, w_inter, carry, w_state, qk = [], [], [], [], []
    for c in C:
        m_inter = [b_row[c][h] + m_prev[h] for h in H]
        m_t.append([jnp.maximum(m_inter[h], d_max[c][h]) for h in H])
        w_inter.append([jnp.exp(m_inter[h] - m_t[c][h]) for h in H])
        m_new = [m_t[c][h][:, L - 1:L] for h in H]
        b_last = [b_row[c][h][:, L - 1:L] for h in H]
        carry.append([jnp.exp(b_last[h] + m_prev[h] - m_new[h]) for h in H])
        w_state.append([jnp.exp(b_last[h] + lib_row[c][h] - m_new[h]) for h in H])
        qk.append([qk_raw[c][h] * jnp.exp(d_log[c][h] - m_t[c][h]) for h in H])
        m_prev = m_new
    for h in H:
        m_ref[0, :, h:h + 1] = m_prev[h]
    wk = [[_mm(eye * w_state[c][h], k_h[c][h]) for h in H] for c in C]
    n_upd = [[_mm(_rows8(w_state[c][h]), k_h[c][h])[0:1] for h in H] for c in C]
    upd = [[_mm_tn(v_h[c][h], wk[c][h]) for h in H] for c in C]
    qkv = [[_mm_tn(v_h[c][h], qk[c][h]) for h in H] for c in C]
    ct = [ct_ref[h] for h in H]
    ns = [n_ref[0, h:h + 1, :] for h in H]
    y_ch = []
    for c in C:
        qc = [_mm_nt(ct[h], qs[c][h]) for h in H]
        qn = [_mm_nt(_rows8(ns[h]), qs[c][h])[0:1] for h in H]
        yy = []
        for h in H:
            num = qkv[c][h] + w_inter[c][h] * qc[h]
            den = jnp.sum(qk[c][h], axis=0, keepdims=True) + w_inter[c][h] * qn[h]
            ht = num / jnp.maximum(jnp.abs(den), jnp.exp(-m_t[c][h]))
            ht = ht - jnp.mean(ht, axis=0, keepdims=True)
            yy.append((ht * lax.rsqrt(jnp.mean(ht * ht, axis=0, keepdims=True) + EPS)).T)
        y_ch.append(yy)
        ct = [carry[c][h] * ct[h] + upd[c][h] for h in H]
        ns = [carry[c][h] * ns[h] + n_upd[c][h] for h in H]
    for h in H:
        ct_ref[h] = ct[h]
        n_ref[0, h:h + 1, :] = ns[h]
    y = jnp.concatenate([jnp.concatenate([y_ch[c][h] for c in C], axis=0) for h in H], axis=-1)
    y_ref[...] = (y * gn[...] * _sigmoid(o_ref[...].astype(F32)) * _silu(z_ref[...].astype(F32))
                  ).astype(y_ref.dtype)

    @pl.when(pl.program_id(1) == pl.num_programs(1) - 1)
    def _():
        for h in H:
            c_ref[0, h] = ct_ref[h].T


def _mlstm_params(ifb, gn):
    ifb_p = jnp.zeros((1, LANES), F32).at[0, SM_MI:SM_MI + 2 * M_HEADS].set(ifb)
    return ifb_p, gn.reshape(1, -1)


def _mlstm_prompt(proj, side, params, batch, seq, tb):
    nt = seq // tb
    cb = lambda c, w: pl.BlockSpec((tb, w), lambda b, t, c=c: (b * nt + t, c // w))
    full = lambda a: pl.BlockSpec(a.shape, lambda b, t: (0,) * a.ndim)
    kern = functools.partial(_mlstm_prompt_kernel, n_chunks=tb // CHUNK)
    return pl.pallas_call(
        kern,
        grid=(batch, nt),
        in_specs=[cb(C_MQ, M_QK), cb(C_MK, M_QK), cb(C_MV, M_W), cb(C_MO, M_W), cb(C_MZ, M_W),
                  cb(S_SMALL, LANES)] + [full(a) for a in params],
        out_specs=[pl.BlockSpec((tb, M_W), lambda b, t: (b * nt + t, 0)),
                   pl.BlockSpec((1, M_HEADS, M_DK, M_DV), lambda b, t: (b, 0, 0, 0)),
                   pl.BlockSpec((1, M_HEADS, M_DK), lambda b, t: (b, 0, 0)),
                   pl.BlockSpec((1, 1, M_HEADS), lambda b, t: (b, 0, 0))],
        out_shape=[jax.ShapeDtypeStruct((batch * seq, M_W), BF16),
                   jax.ShapeDtypeStruct((batch, M_HEADS, M_DK, M_DV), F32),
                   jax.ShapeDtypeStruct((batch, M_HEADS, M_DK), F32),
                   jax.ShapeDtypeStruct((batch, 1, M_HEADS), F32)],
        scratch_shapes=[pltpu.VMEM((M_HEADS, M_DV, M_DK), F32)],
        compiler_params=_cparams(("parallel", "arbitrary")),
        name="mlstm_prompt",
    )(proj, proj, proj, proj, proj, side, *params)


def _mlstm_step_kernel(q_ref, k_ref, v_ref, o_ref, z_ref, sm_ref, c_in, n_in, m_in, acc_ref, ifb, gn,
                       y_ref, c_out, n_out, m_out, vec_ref, sc_ref, hrow_ref, *, bb):
    del acc_ref
    pre = sm_ref[...] + ifb[...]
    li = pre[:, SM_MI:SM_MI + M_HEADS]
    lf = _log_sigmoid(pre[:, SM_MF:SM_MF + M_HEADS])
    m0 = m_in[...]
    m_inter = lf + m0
    m_t = jnp.maximum(m_inter, li)
    m_out[...] = m_t
    pad = lambda x: jnp.concatenate([x, jnp.zeros((bb, LANES - M_HEADS), F32)], axis=-1)
    sc_ref[0] = pad(jnp.exp(m_inter - m_t))
    sc_ref[1] = pad(jnp.exp(li - m_t))
    sc_ref[2] = pad(jnp.exp(-m_t))
    vec_ref[0] = q_ref[...].astype(F32) * (M_DK ** -0.5)
    vec_ref[1] = k_ref[...].astype(F32)
    vec_ref[2] = v_ref[...].astype(F32)
    masks = _outer_masks()

    def per_rows(j, _):
        bs = [j * STEP_ROWS + i for i in range(STEP_ROWS)]
        vecs = [[vec_ref[i, pl.ds(b, 1), :] for i in range(3)] for b in bs]
        scs = [[sc_ref[i, pl.ds(b, 1), :] for i in range(3)] for b in bs]
        items = [(bi, h) for bi in range(STEP_ROWS) for h in range(M_HEADS)]
        n_it = range(len(items))
        hs = lambda x, h: x[:, h * M_DK:(h + 1) * M_DK]
        q_h, k_h, v_h = ([hs(vecs[bi][i], h) for bi, h in items] for i in range(3))
        wi, ei, en = ([scs[bi][i][:, h:h + 1] for bi, h in items] for i in range(3))
        c = [c_in[bs[bi], h] for bi, h in items]
        n = [n_in[bs[bi], h:h + 1, :] for bi, h in items]
        qk = [_lane_sum(q_h[i] * k_h[i]) * ei[i] for i in n_it]
        qn = [_lane_sum(q_h[i] * n[i]) for i in n_it]
        qc = [_mm(_rows8(q_h[i]), c[i])[0:1] for i in n_it]
        kv = [_mm_tn(*_outer_operands(masks, (ei[i] * k_h[i], v_h[i]))) for i in n_it]
        hh = []
        for i, (bi, h) in enumerate(items):
            num = qk[i] * v_h[i] + wi[i] * qc[i]
            den = qk[i] + wi[i] * qn[i]
            hh.append(num / jnp.maximum(jnp.abs(den), en[i]))
            c_out[bs[bi], h] = wi[i] * c[i] + kv[i]
            n_out[bs[bi], h:h + 1, :] = wi[i] * n[i] + ei[i] * k_h[i]
        for bi in range(STEP_ROWS):
            hrow_ref[pl.ds(bs[bi], 1), :] = jnp.concatenate(
                [hh[bi * M_HEADS + h] for h in range(M_HEADS)], axis=-1)
        return 0

    lax.fori_loop(0, bb // STEP_ROWS, per_rows, 0)
    hh = hrow_ref[...]
    h_heads = [hh[:, h * M_DV:(h + 1) * M_DV] for h in range(M_HEADS)]
    y_ref[...] = _mlstm_out(h_heads, o_ref[...].astype(F32), z_ref[...].astype(F32), gn[...]).astype(y_ref.dtype)


def _mlstm_step(proj, side, c_all, n_all, m_all, acc, params, layer, bb):
    nb = proj.shape[0]
    cb = lambda c, w: pl.BlockSpec((bb, w), lambda i, c=c: (i, c // w))
    full = lambda a: pl.BlockSpec(a.shape, lambda i: (0,) * a.ndim)
    cs = _layer_block(c_all.shape[2:], bb, layer)
    ns_in = _layer_block(n_all.shape[2:], bb, layer)
    ms_in = _layer_block(m_all.shape[2:], bb, layer)
    ns = pl.BlockSpec((bb, M_HEADS, M_DK), lambda i: (i, 0, 0))
    ms = pl.BlockSpec((bb, M_HEADS), lambda i: (i, 0))
    kern = functools.partial(_mlstm_step_kernel, bb=bb)
    return pl.pallas_call(
        kern,
        grid=(nb // bb,),
        in_specs=[cb(C_MQ, M_QK), cb(C_MK, M_QK), cb(C_MV, M_W), cb(C_MO, M_W), cb(C_MZ, M_W),
                  cb(S_SMALL, LANES), cs, ns_in, ms_in, pl.BlockSpec(memory_space=pl.ANY)]
                 + [full(a) for a in params],
        out_specs=[pl.BlockSpec((bb, M_W), lambda i: (i, 0)), cs, ns, ms],
        out_shape=[jax.ShapeDtypeStruct((nb, M_W), BF16), jax.ShapeDtypeStruct(acc.shape, F32),
                   jax.ShapeDtypeStruct(n_all.shape[1:], F32), jax.ShapeDtypeStruct(m_all.shape[1:], F32)],
        input_output_aliases={9: 1},
        scratch_shapes=[pltpu.VMEM((3, bb, M_QK), F32), pltpu.VMEM((3, bb, LANES), F32),
                        pltpu.VMEM((bb, M_W), F32)],
        compiler_params=_cparams(("parallel",)),
        name="mlstm_step",
    )(proj, proj, proj, proj, proj, side, c_all, n_all, m_all, acc, *params)


def _shift_rows(proj_rows, side_rows):
    return jnp.concatenate([proj_rows[:, C_RIN:C_RIN + 3 * R_W].astype(F32),
                            side_rows[:, S_RLORA:S_RLORA + 2 * R_LORA]], axis=-1)


def _pick(n, prefs):
    for p in prefs:
        if n % p == 0:
            return p
    return n


def kernel(x_prompt, x_sample, state_rwkv_shift, state_rwkv, state_gla, state_mlstm_C, state_mlstm_n,
           state_mlstm_m, norm_pre, norm_post, w_in, r_mu_shift, r_w_lora_b, r_w0, r_a_lora_b, r_a0,
           r_k_k, r_k_a, r_r_k, r_gn, g_a_lora_b, g_a_bias, g_gn, m_if_bias, m_gn, w_br_rwkv, w_br_gla,
           w_br_mlstm, w_out):
    depth = w_in.shape[0]
    bp, tp, _ = x_prompt.shape
    bs, ts, _ = x_sample.shape
    assert ts == 1 and tp % CHUNK == 0
    assert w_in.shape[1:] == (D_MODEL, N_IN)
    w_t = jnp.transpose(w_in, (0, 2, 1))
    wr, wg, wm, wo = (w.astype(BF16) for w in (w_br_rwkv, w_br_gla, w_br_mlstm, w_out))

    mp = bp * tp
    tm_in = _pick(mp, (2048, 1024, 512, 256, 128, 64))
    tm_mg = _pick(mp, (1024, 512, 256, 128, 64))
    tb = _pick(tp, (256, 128, 64))
    bb = _pick(bs, (16, 8))

    xp = x_prompt.reshape(mp, D_MODEL)
    xs = x_sample.reshape(bs, D_MODEL)
    new_p = [[] for _ in range(6)]
    new_s = [[] for _ in range(3)]
    s_lanes = jnp.transpose(state_rwkv, (0, 2, 3, 4, 1))
    acc_r, acc_g, acc_c = (jnp.zeros(s.shape, F32) for s in (s_lanes, state_gla, state_mlstm_C))
    for l in range(depth):
        g_pre = norm_pre[l].reshape(1, -1)
        g_post = norm_post[l].reshape(1, -1)
        rp = _rwkv_params(r_mu_shift[l], r_w_lora_b[l], r_w0[l], r_a_lora_b[l], r_a0[l], r_k_k[l],
                          r_k_a[l], r_r_k[l], r_gn[l])
        gp = _gla_params(g_a_lora_b[l], g_a_bias[l], g_gn[l])
        mparams = _mlstm_params(m_if_bias[l], m_gn[l])

        proj, side, proj_s, side_s = _inproj(xp, xs, g_pre, w_t, l, tm_in)
        y_r, s_r = _rwkv_prompt(proj, side, rp, bp, tp, tb)
        y_g, s_g = _gla_prompt(proj, side, gp, bp, tp, tb)
        y_m, c, n, m = _mlstm_prompt(proj, side, mparams, bp, tp, tb)
        xp = _merge(proj, y_r, y_g, y_m, xp, wr, wg, wm, wo, g_post, l, tm_mg)
        shift = _shift_rows(proj.reshape(bp, tp, C_RLORA)[:, tp - 1], side.reshape(bp, tp, SIDE_W)[:, tp - 1])
        for lst, val in zip(new_p, (shift, s_r, s_g, c, n, m.reshape(bp, M_HEADS))):
            lst.append(val)

        proj, side = proj_s, side_s
        col = lambda x: jnp.broadcast_to(x[:, None], (R_W, LANES))
        y_t, acc_r = _rwkv_step_lanes(proj, side, state_rwkv_shift, s_lanes, acc_r,
                                      rp[:10] + (col(r_r_k[l]), col(r_gn[l])), l)
        y_r = jnp.transpose(y_t, (2, 0, 1)).reshape(bs, R_W)
        y_g, acc_g = _gla_step(proj, side, state_gla, acc_g, gp, l, bb)
        y_m, acc_c, n, m = _mlstm_step(proj, side, state_mlstm_C, state_mlstm_n, state_mlstm_m, acc_c,
                                       mparams, l, bb)
        xs = _merge(proj, y_r, y_g, y_m, xs, wr, wg, wm, wo, g_post, l, bs)
        shift = _shift_rows(proj, side)
        for lst, val in zip(new_s, (shift, n, m)):
            lst.append(val)

    stk = lambda lst: jnp.stack(lst)
    return (xp.reshape(bp, tp, D_MODEL), xs.reshape(bs, ts, D_MODEL),
            *(stk(v) for v in new_p),
            stk(new_s[0]), jnp.transpose(acc_r, (0, 4, 1, 2, 3)), acc_g, acc_c, stk(new_s[1]), stk(new_s[2]))
```

```python
import functools
import math

import jax
import jax.numpy as jnp
from jax import lax
from jax.experimental import pallas as pl
from jax.experimental.pallas import tpu as pltpu

F32 = jnp.float32
BF16 = jnp.bfloat16

D_MODEL = 1024
R_HEADS, R_HD = 8, 64
R_W = R_HEADS * R_HD
R_LORA = 64
R_SHIFT_W = 3 * R_W + 2 * R_LORA
R_GN_EPS = 64e-5
G_HEADS, G_DK, G_DV = 4, 64, 128
G_QK = G_HEADS * G_DK
G_W = G_HEADS * G_DV
G_LORA = 16
G_GATE_TEMP = 16.0
M_HEADS, M_DK, M_DV = 4, 128, 128
M_QK = M_HEADS * M_DK
M_W = M_HEADS * M_DV
EPS = 1e-6

LANES = 128
SUBLANES = 8
VMEM_LIMIT = 48 * 1024 * 1024

C_GATE = 0
C_RZ = 3072
C_GV = 3584
C_GZ = 4096
C_MQ = 4608
C_MK = 5120
C_MV = 5632
C_MO = 6144
C_MZ = 6656
C_GQK = 7168
C_RIN = 7680
C_RLORA = C_RIN + 3 * R_W
C_SMALL = C_RLORA + LANES
SM_GA, SM_MI, SM_MF = 0, 16, 20
N_USED = C_SMALL + LANES
PROJ_TN = 512
N_PAD = -(-N_USED // PROJ_TN) * PROJ_TN
N_MAIN_BLOCKS = C_RLORA // PROJ_TN
_W_SEGMENTS = (("r_in", R_SHIFT_W), ("r_z", R_W), ("g_q", G_QK), ("g_k", G_QK), ("g_v", G_W), ("g_a", G_LORA),
               ("g_z", G_W), ("m_q", M_QK), ("m_k", M_QK), ("m_v", M_W), ("m_i", M_HEADS), ("m_f", M_HEADS),
               ("m_o", M_W), ("m_z", M_W), ("gate", 3 * D_MODEL))
W_OFF = {}
_off = 0
for _name, _width in _W_SEGMENTS:
    W_OFF[_name] = _off
    _off += _width
N_IN = _off
W_MAIN_ROWS = tuple(W_OFF[name] + PROJ_TN * b
                    for name, nblk in (("gate", 6), ("r_z", 1), ("g_v", 1), ("g_z", 1), ("m_q", 1), ("m_k", 1),
                                       ("m_v", 1), ("m_o", 1), ("m_z", 1), ("g_q", 1), ("r_in", 3))
                    for b in range(nblk))
assert len(W_MAIN_ROWS) == N_MAIN_BLOCKS and all(r % SUBLANES == 0 for r in W_MAIN_ROWS)
SIDE_W = 2 * LANES
S_RLORA, S_SMALL = 0, LANES

CHUNK = 64
RWKV_GROUP = 16


def _cparams(sem):
    return pltpu.CompilerParams(dimension_semantics=sem, vmem_limit_bytes=VMEM_LIMIT)


def _mm(a, b):
    return jnp.dot(a.astype(BF16), b.astype(BF16), preferred_element_type=F32)


def _mm_nt(a, b):
    return lax.dot_general(a.astype(BF16), b.astype(BF16), (((1,), (1,)), ((), ())),
                           preferred_element_type=F32)


def _mm_tn(a, b):
    return lax.dot_general(a.astype(BF16), b.astype(BF16), (((0,), (0,)), ((), ())),
                           preferred_element_type=F32)


def _mm_exact_lhs(m01, x):
    hi = x.astype(BF16)
    r1 = x - hi.astype(F32)
    mid = r1.astype(BF16)
    lo = (r1 - mid.astype(F32)).astype(BF16)
    m = m01.astype(BF16)
    d = lambda p: jnp.dot(m, p, preferred_element_type=F32)
    return d(hi) + d(mid) + d(lo)


def _mm_exact_rhs(x, m01):
    hi = x.astype(BF16)
    r1 = x - hi.astype(F32)
    mid = r1.astype(BF16)
    lo = (r1 - mid.astype(F32)).astype(BF16)
    m = m01.astype(BF16)
    d = lambda p: jnp.dot(p, m, preferred_element_type=F32)
    return d(hi) + d(mid) + d(lo)


def _sigmoid(x):
    return 0.5 * jnp.tanh(0.5 * x) + 0.5


def _silu(x):
    return x * _sigmoid(x)


def _softplus(x):
    return jnp.maximum(x, 0.0) + jnp.log(1.0 + jnp.exp(-jnp.abs(x)))


def _log_sigmoid(x):
    return -_softplus(-x)


def _iota2(shape, dim):
    return lax.broadcasted_iota(jnp.int32, shape, dim)


def _lane_sum(x):
    return jnp.sum(x, axis=-1, keepdims=True)


def _rows8(x):
    return jnp.broadcast_to(x, (8, x.shape[1]))


def _outer_masks():
    rid = _iota2((8, 1), 0)
    on = lambda *rows: sum((rid == r).astype(F32) for r in rows)
    return (on(0, 2), on(1), on(0, 1), on(2)), (on(3, 5), on(4), on(3, 4), on(5))


def _outer_operands(masks, *pairs):
    a_op, b_op = 0.0, 0.0
    for (a, b), (m_ahi, m_alo, m_bhi, m_blo) in zip(pairs, masks):
        a_hi = a.astype(BF16).astype(F32)
        b_hi = b.astype(BF16).astype(F32)
        a_op = a_op + m_ahi * a_hi + m_alo * (a - a_hi)
        b_op = b_op + m_bhi * b_hi + m_blo * (b - b_hi)
    return a_op, b_op


def _block_tril(n, blk):
    r, c = _iota2((n, n), 0), _iota2((n, n), 1)
    return ((r - c).astype(jnp.uint32) <= (r % blk).astype(jnp.uint32)).astype(F32)


def _inproj_kernel(rows_ref, x_ref, xs_ref, g_ref, w_ref, wl_ref, wa_ref, wif_ref,
                   o_ref, side_ref, os_ref, sides_ref, h_ref, hs_ref, ws_ref):
    del rows_ref
    i, j = pl.program_id(0), pl.program_id(1)

    def normed(x):
        return (x * lax.rsqrt(jnp.mean(x * x, axis=-1, keepdims=True) + EPS) * g_ref[...]).astype(BF16)

    @pl.when(j == 0)
    def _():
        h_ref[...] = normed(x_ref[...])
        ws_ref[...] = jnp.zeros_like(ws_ref)
        ws_ref[S_RLORA:S_RLORA + 2 * R_LORA, :] = wl_ref[0]
        ws_ref[S_SMALL + SM_GA:S_SMALL + SM_GA + G_LORA, :] = wa_ref[0]
        ws_ref[S_SMALL + SM_MI:S_SMALL + SM_MI + 2 * M_HEADS, :] = wif_ref[0]
        side_ref[...] = _mm_nt(h_ref[...], ws_ref[...])

    @pl.when(jnp.logical_and(i == 0, j == 0))
    def _():
        hs_ref[...] = normed(xs_ref[...])
        sides_ref[...] = _mm_nt(hs_ref[...], ws_ref[...])

    w = w_ref[0].astype(BF16)
    o_ref[...] = _mm_nt(h_ref[...], w).astype(BF16)

    @pl.when(i == 0)
    def _():
        os_ref[...] = _mm_nt(hs_ref[...], w).astype(BF16)


def _inproj(x2d, xs2d, g_pre, w_t, layer, tm):
    m, ms = x2d.shape[0], xs2d.shape[0]
    w_blk = lambda n: (pl.Element(1), pl.Element(n), pl.Element(D_MODEL))
    w_rows = lambda n, start: pl.BlockSpec(w_blk(n), lambda i, j, rows, start=start: (layer, start, 0))
    last = N_MAIN_BLOCKS - 1
    grid_spec = pltpu.PrefetchScalarGridSpec(
        num_scalar_prefetch=1,
        grid=(m // tm, N_MAIN_BLOCKS),
        in_specs=[pl.BlockSpec((tm, D_MODEL), lambda i, j, rows: (i, 0)),
                  pl.BlockSpec((ms, D_MODEL), lambda i, j, rows: (0, 0)),
                  pl.BlockSpec((1, D_MODEL), lambda i, j, rows: (0, 0)),
                  pl.BlockSpec(w_blk(PROJ_TN), lambda i, j, rows: (layer, rows[j] * SUBLANES, 0)),
                  w_rows(2 * R_LORA, W_OFF["r_in"] + 3 * R_W),
                  w_rows(G_LORA, W_OFF["g_a"]),
                  w_rows(2 * M_HEADS, W_OFF["m_i"])],
        out_specs=[pl.BlockSpec((tm, PROJ_TN), lambda i, j, rows: (i, j)),
                   pl.BlockSpec((tm, SIDE_W), lambda i, j, rows: (i, 0)),
                   pl.BlockSpec((ms, PROJ_TN), lambda i, j, rows: (0, jnp.where(i == 0, j, last))),
                   pl.BlockSpec((ms, SIDE_W), lambda i, j, rows: (0, 0))],
        scratch_shapes=[pltpu.VMEM((tm, D_MODEL), BF16), pltpu.VMEM((ms, D_MODEL), BF16),
                        pltpu.VMEM((SIDE_W, D_MODEL), F32)])
    return pl.pallas_call(
        _inproj_kernel,
        grid_spec=grid_spec,
        out_shape=[jax.ShapeDtypeStruct((m, C_RLORA), BF16), jax.ShapeDtypeStruct((m, SIDE_W), F32),
                   jax.ShapeDtypeStruct((ms, C_RLORA), BF16), jax.ShapeDtypeStruct((ms, SIDE_W), F32)],
        compiler_params=_cparams(("arbitrary", "arbitrary")),
        name="inproj",
    )(jnp.asarray([r // SUBLANES for r in W_MAIN_ROWS], jnp.int32), x2d, xs2d, g_pre, w_t, w_t, w_t, w_t)


def _merge_kernel(gate_ref, yr_ref, yg_ref, ym_ref, x_ref, wr_ref, wg_ref, wm_ref, wo_ref, gp_ref, o_ref):
    d = D_MODEL
    gate = lambda b: _sigmoid(gate_ref[:, b * d:(b + 1) * d].astype(F32))
    merged = (gate(0) * _mm(yr_ref[...], wr_ref[...]) + gate(1) * _mm(yg_ref[...], wg_ref[...])
              + gate(2) * _mm(ym_ref[...], wm_ref[...]))
    o = _mm(merged, wo_ref[...])
    y = o * lax.rsqrt(jnp.mean(o * o, axis=-1, keepdims=True) + EPS)
    o_ref[...] = x_ref[...] + y * gp_ref[...]


def _merge(proj, y_r, y_g, y_m, x2d, w_r, w_g, w_m, w_o, g_post, layer, tm):
    m = x2d.shape[0]
    row = lambda w: pl.BlockSpec((tm, w), lambda i: (i, 0))
    full = lambda a: pl.BlockSpec(a.shape, lambda i: (0, 0))
    wl = lambda a: pl.BlockSpec((None,) + a.shape[1:], lambda i: (layer, 0, 0))
    return pl.pallas_call(
        _merge_kernel,
        grid=(m // tm,),
        in_specs=[row(3 * D_MODEL), row(R_W), row(G_W), row(M_W), row(D_MODEL),
                  wl(w_r), wl(w_g), wl(w_m), wl(w_o), full(g_post)],
        out_specs=row(D_MODEL),
        out_shape=jax.ShapeDtypeStruct((m, D_MODEL), F32),
        compiler_params=_cparams(("parallel",)),
        name="merge",
    )(proj, y_r, y_g, y_m, x2d, w_r, w_g, w_m, w_o, g_post)


def _rwkv_front(r_in, k_in, v_in, l_in, prev_r, prev_k, prev_v, prev_l, p):
    (mu_r, mu_k, mu_v, mu_l, wb, w0, ab, a0, k_k, k_a) = p
    r = r_in + mu_r * (prev_r - r_in)
    k = k_in + mu_k * (prev_k - k_in)
    v = v_in + mu_v * (prev_v - v_in)
    lo = l_in + mu_l * (prev_l - l_in)
    log_w = -_softplus(-(w0 + _mm(jnp.tanh(lo), wb))) - 0.5
    lw = -jnp.exp(log_w)
    a = _sigmoid(a0 + _mm(lo, ab))
    kk = k * k_k
    k2 = k * (1.0 + (a - 1.0) * k_a)
    return r, k2, v, lw, a, kk


def _rwkv_prompt_kernel(z_ref, r_ref, k_ref, v_ref, l_ref,
                        mu_r, mu_k, mu_v, mu_l, wb, w0, ab, a0, k_k, k_a, r_k, gn,
                        y_ref, s_ref, carry_ref, sp_ref, *, n_chunks):
    L = CHUNK

    @pl.when(pl.program_id(1) == 0)
    def _():
        sp_ref[...] = jnp.zeros_like(sp_ref)
        carry_ref[...] = jnp.zeros_like(carry_ref)

    tb = n_chunks * L
    params = tuple(x[...] for x in (mu_r, mu_k, mu_v, mu_l, wb, w0, ab, a0, k_k, k_a))
    assert L == R_HD
    m_a = (_iota2((1, LANES), 1) < R_HD).astype(F32)
    m_b = 1.0 - m_a
    m_a16, m_b16 = m_a.astype(BF16), m_b.astype(BF16)
    trow = _iota2((L, LANES), 0)
    tcol = _iota2((L, LANES), 1) % L
    eye = (tcol == trow).astype(F32)
    blk_mask = (trow // 16) == (tcol // 16)
    r2 = _iota2((2 * L, LANES), 0)
    g_mask = (_iota2((2 * L, LANES), 1) % L) < (r2 % L) + (r2 >= L).astype(jnp.int32)
    same_head = (_iota2((LANES, LANES), 0) // R_HD) == (_iota2((LANES, LANES), 1) // R_HD)
    first_row = _iota2((tb, 1), 0) == 0

    def seg_sum(x):
        return _lane_sum(x * m_a) * m_a + _lane_sum(x * m_b) * m_b

    def bd(y):
        yb = y.astype(BF16)
        return jnp.concatenate([yb * m_a16, yb * m_b16], axis=0)

    def shifted(x, carry):
        return jnp.where(first_row, carry, pltpu.roll(x, 1, 0))

    r_in, k_in, v_in = (x[...].astype(F32) for x in (r_ref, k_ref, v_ref))
    l_in = l_ref[...]
    prev_r = shifted(r_in, carry_ref[0:1, 0:R_W])
    prev_k = shifted(k_in, carry_ref[0:1, R_W:2 * R_W])
    prev_v = shifted(v_in, carry_ref[0:1, 2 * R_W:3 * R_W])
    prev_l = shifted(l_in, carry_ref[0:1, 3 * R_W:R_SHIFT_W])
    carry_ref[0:1, 0:R_W] = r_in[tb - 1:tb, :]
    carry_ref[0:1, R_W:2 * R_W] = k_in[tb - 1:tb, :]
    carry_ref[0:1, 2 * R_W:3 * R_W] = v_in[tb - 1:tb, :]
    carry_ref[0:1, 3 * R_W:R_SHIFT_W] = l_in[tb - 1:tb, :]
    r, k2, v, lw, a, kk = _rwkv_front(r_in, k_in, v_in, l_in, prev_r, prev_k, prev_v, prev_l, params)

    cum = _mm_exact_lhs(_block_tril(tb, L), lw)
    p_inc = jnp.exp(cum)
    p_exc = jnp.exp(cum - lw)
    p_inv = jnp.exp(-cum)
    r_t = r * p_inc
    k_h = k2 * p_inv

    n_pairs = R_HEADS // 2
    CP = [(c, p) for c in range(n_chunks) for p in range(n_pairs)]
    N = range(len(CP))
    kap_t, beta_h = [], []
    for p in range(n_pairs):
        sl = slice(p * LANES, (p + 1) * LANES)
        kk_p = kk[:, sl]
        kap = kk_p * jnp.minimum(lax.rsqrt(seg_sum(kk_p * kk_p)), 1e12)
        beta_h.append(kap * a[:, sl] * p_inv[:, sl])
        kap_t.append(kap * p_exc[:, sl])
    cut = lambda x, c, p: x[c * L:(c + 1) * L, p * LANES:(p + 1) * LANES]
    kap_t = [kap_t[p][c * L:(c + 1) * L] for c, p in CP]
    beta_h = [beta_h[p][c * L:(c + 1) * L] for c, p in CP]
    rt_c = [cut(r_t, c, p) for c, p in CP]
    kh_c = [cut(k_h, c, p) for c, p in CP]
    v_c = [cut(v, c, p) for c, p in CP]
    p_last = [p_inc[(c + 1) * L - 1:(c + 1) * L, p * LANES:(p + 1) * LANES] for c, p in CP]
    bhl = [beta_h[i] * p_last[i] for i in N]
    khl = [kh_c[i] * p_last[i] for i in N]
    n_bd, c_c, qt, y0 = {}, {}, {}, {}
    for g0 in range(0, len(CP), RWKV_GROUP):
        G = range(g0, min(g0 + RWKV_GROUP, len(CP)))
        lhs2 = {i: jnp.concatenate([kap_t[i], rt_c[i]], axis=0) for i in G}
        bd_v = {i: bd(v_c[i]) for i in G}
        gb = {i: jnp.where(g_mask, _mm_nt(lhs2[i], bd(beta_h[i])), 0.0) for i in G}
        gk = {i: jnp.where(g_mask, _mm_nt(lhs2[i], bd(kh_c[i])), 0.0) for i in G}
        a_m = {i: gb[i][0:L] for i in G}
        d = {i: jnp.where(blk_mask, a_m[i], 0.0) for i in G}
        akv = {i: _mm(gk[i][0:L], bd_v[i]) for i in G}
        d2 = {i: _mm(d[i], bd(d[i])) for i in G}
        bd_d2 = {i: bd(d2[i]) for i in G}
        d4 = {i: _mm(d2[i], bd_d2[i]) for i in G}
        dd2 = {i: _mm(d[i], bd_d2[i]) for i in G}
        d8 = {i: _mm(d4[i], bd(d4[i])) for i in G}
        x = {i: eye - d[i] + d2[i] - dd2[i] for i in G}
        x = {i: x[i] + _mm(x[i], bd(d4[i])) for i in G}
        td = {i: x[i] + _mm(x[i], bd(d8[i])) for i in G}
        m = {i: _mm(td[i], bd(a_m[i] - d[i])) for i in G}
        m2 = {i: _mm(m[i], bd(m[i])) for i in G}
        q = {i: eye - m[i] + m2[i] - _mm(m[i], bd(m2[i])) for i in G}
        t_inv = {i: _mm(q[i], bd(td[i])) for i in G}
        kbar = {i: _mm(t_inv[i], bd(kap_t[i])) for i in G}
        u0 = {i: -_mm(t_inv[i], bd(akv[i])) for i in G}
        n_bd.update({i: jnp.where(same_head, _mm_tn(kbar[i], bhl[i]), 0.0) for i in G})
        cz = {i: _mm_tn(jnp.concatenate([u0[i], v_c[i]], axis=0), jnp.concatenate([bhl[i], khl[i]], axis=0))
              for i in G}
        c_c.update({i: cz[i][0:R_HD] * m_a + cz[i][R_HD:2 * R_HD] * m_b for i in G})
        qt.update({i: rt_c[i] - _mm(gb[i][L:2 * L], bd(kbar[i])) for i in G})
        y0.update({i: _mm(gb[i][L:2 * L], bd(u0[i])) + _mm(gk[i][L:2 * L], bd_v[i]) for i in G})
    s = [sp_ref[p] for p in range(n_pairs)]
    y_ch = []
    for c in range(n_chunks):
        ids = [c * n_pairs + p for p in range(n_pairs)]
        y_ch.append([_mm_nt(qt[i], bd(s[p])) + y0[i] for p, i in enumerate(ids)])
        s = [s[p] * p_last[i] - _mm(s[p], n_bd[i]) + c_c[i] for p, i in enumerate(ids)]
    for p in range(n_pairs):
        sp_ref[p] = s[p]
    y_pairs = [jnp.concatenate([y_ch[c][p] for c in range(n_chunks)], axis=0) for p in range(n_pairs)]
    outs = []
    for p in range(n_pairs):
        sl = slice(p * LANES, (p + 1) * LANES)
        y = y_pairs[p]
        y = y - seg_sum(y) * (1.0 / R_HD)
        y = y * lax.rsqrt(seg_sum(y * y) * (1.0 / R_HD) + R_GN_EPS)
        bonus = seg_sum(r[:, sl] * k2[:, sl] * r_k[:, sl]) * v[:, sl]
        outs.append(y * gn[:, sl] + bonus)
    y_ref[...] = (jnp.concatenate(outs, axis=-1) * _silu(z_ref[...].astype(F32))).astype(y_ref.dtype)

    @pl.when(pl.program_id(1) == pl.num_programs(1) - 1)
    def _():
        for p in range(n_pairs):
            sp = sp_ref[p]
            s_ref[0, 2 * p] = sp[:, 0:R_HD]
            s_ref[0, 2 * p + 1] = sp[:, R_HD:2 * R_HD]


def _rwkv_params(mu, wb, w0, ab, a0, k_k, k_a, r_k, gn):
    row = lambda x: x.reshape(1, -1)
    zeros = jnp.zeros((R_LORA, R_W), F32)
    wb_p = jnp.concatenate([wb, zeros], axis=0).astype(BF16)
    ab_p = jnp.concatenate([zeros, ab], axis=0).astype(BF16)
    return (row(mu[0:R_W]), row(mu[R_W:2 * R_W]), row(mu[2 * R_W:3 * R_W]), row(mu[3 * R_W:]),
            wb_p, row(w0), ab_p, row(a0), row(k_k), row(k_a), row(r_k), row(gn))


def _rwkv_prompt(proj, side, params, batch, seq, tb):
    nt = seq // tb
    cb = lambda c, w: pl.BlockSpec((tb, w), lambda b, t, c=c: (b * nt + t, c // w))
    full = lambda a: pl.BlockSpec(a.shape, lambda b, t: (0,) * a.ndim)
    kern = functools.partial(_rwkv_prompt_kernel, n_chunks=tb // CHUNK)
    return pl.pallas_call(
        kern,
        grid=(batch, nt),
        in_specs=[cb(C_RZ, R_W), cb(C_RIN, R_W), cb(C_RIN + R_W, R_W), cb(C_RIN + 2 * R_W, R_W),
                  cb(S_RLORA, LANES)] + [full(a) for a in params],
        out_specs=[pl.BlockSpec((tb, R_W), lambda b, t: (b * nt + t, 0)),
                   pl.BlockSpec((1, R_HEADS, R_HD, R_HD), lambda b, t: (b, 0, 0, 0))],
        out_shape=[jax.ShapeDtypeStruct((batch * seq, R_W), BF16),
                   jax.ShapeDtypeStruct((batch, R_HEADS, R_HD, R_HD), F32)],
        scratch_shapes=[pltpu.VMEM((8, R_SHIFT_W), F32),
                        pltpu.VMEM((R_HEADS // 2, R_HD, LANES), F32)],
        compiler_params=_cparams(("parallel", "arbitrary")),
        name="rwkv_prompt",
    )(proj, proj, proj, proj, side, *params)


STEP_ROWS = 4
GLA_STEP_ROWS = 2


def _rwkv_lanes_kernel(z_ref, r_ref, k_ref, v_ref, l_ref, pr_ref, pk_ref, pv_ref, pl_ref, s_in, acc_ref,
                       mu_r, mu_k, mu_v, mu_l, wb, w0, ab, a0, k_k, k_a, rk_t, gn_t,
                       y_ref, s_out, vec_ref, ycol_ref):
    del acc_ref
    h = pl.program_id(0)

    @pl.when(h == 0)
    def _():
        params = tuple(x[...] for x in (mu_r, mu_k, mu_v, mu_l, wb, w0, ab, a0, k_k, k_a))
        r, k2, v, lw, a, kk = _rwkv_front(r_ref[...].astype(F32), k_ref[...].astype(F32),
                                          v_ref[...].astype(F32), l_ref[...], pr_ref[...], pk_ref[...],
                                          pv_ref[...], pl_ref[...], params)
        kaps, betas = [], []
        for hh in range(R_HEADS):
            sl = slice(hh * R_HD, (hh + 1) * R_HD)
            kk_h = kk[:, sl]
            kap = kk_h / jnp.maximum(jnp.sqrt(_lane_sum(kk_h * kk_h)), 1e-12)
            kaps.append(kap)
            betas.append(kap * a[:, sl])
        rows = (jnp.concatenate(kaps, axis=-1), jnp.concatenate(betas, axis=-1), jnp.exp(lw), k2, v, r,
                _silu(z_ref[...].astype(F32)))
        for i, x in enumerate(rows):
            vec_ref[i] = x.T

    base = pl.multiple_of(h * R_HD, R_HD)
    chan = pl.ds(base, R_HD)
    kap_t, beta_t, w_t, k_t, v_t, r_t, gz_t = (vec_ref[i, chan, :] for i in range(7))

    def per_v(vi, _):
        s = s_in[vi]
        sa = -jnp.sum(s * kap_t, axis=0, keepdims=True)
        v_row = vec_ref[4, pl.ds(base + vi, 1), :]
        s_new = s * w_t + sa * beta_t + v_row * k_t
        s_out[vi] = s_new
        ycol_ref[pl.ds(vi, 1), :] = jnp.sum(s_new * r_t, axis=0, keepdims=True)
        return 0

    lax.fori_loop(0, R_HD, per_v, 0, unroll=8)
    y = ycol_ref[...]
    y = y - jnp.mean(y, axis=0, keepdims=True)
    y = y * lax.rsqrt(jnp.mean(y * y, axis=0, keepdims=True) + R_GN_EPS)
    bonus = jnp.sum(r_t * k_t * rk_t[chan, :], axis=0, keepdims=True) * v_t
    y_ref[...] = ((y * gn_t[chan, :] + bonus) * gz_t).astype(y_ref.dtype)


def _rwkv_step_lanes(proj, side, shift_all, s_lanes, acc, params, layer):
    nb = proj.shape[0]
    assert nb == LANES
    cb = lambda c, w: pl.BlockSpec((nb, w), lambda h, c=c: (0, c // w))
    sh = lambda c, w: pl.BlockSpec((None, nb, w), lambda h, c=c: (layer, 0, c // w))
    full = lambda a: pl.BlockSpec(a.shape, lambda h: (0,) * a.ndim)
    st = pl.BlockSpec((None, None, R_HD, R_HD, nb), lambda h: (layer, h, 0, 0, 0))
    return pl.pallas_call(
        _rwkv_lanes_kernel,
        grid=(R_HEADS,),
        in_specs=[cb(C_RZ, R_W), cb(C_RIN, R_W), cb(C_RIN + R_W, R_W), cb(C_RIN + 2 * R_W, R_W),
                  cb(S_RLORA, LANES),
                  sh(0, R_W), sh(R_W, R_W), sh(2 * R_W, R_W), sh(3 * R_W, LANES), st,
                  pl.BlockSpec(memory_space=pl.ANY)]
                 + [full(a) for a in params],
        out_specs=[pl.BlockSpec((None, R_HD, nb), lambda h: (h, 0, 0)), st],
        out_shape=[jax.ShapeDtypeStruct((R_HEADS, R_HD, nb), BF16),
                   jax.ShapeDtypeStruct(acc.shape, F32)],
        input_output_aliases={10: 1},
        scratch_shapes=[pltpu.VMEM((7, R_W, nb), F32), pltpu.VMEM((R_HD, nb), F32)],
        compiler_params=_cparams(("arbitrary",)),
        name="rwkv_step",
    )(proj, proj, proj, proj, side, shift_all, shift_all, shift_all, shift_all, s_lanes, acc, *params)


def _layer_block(shape_tail, bb, layer):
    zeros = (0,) * len(shape_tail)
    return pl.BlockSpec((None, bb) + tuple(shape_tail), lambda i: (layer, i) + zeros)


def _gla_log_gate(small, ab_p, a_bias):
    return _log_sigmoid(_mm(small, ab_p) + a_bias) / G_GATE_TEMP


def _gla_out(o_heads, z, gn):
    outs = []
    for h in range(G_HEADS):
        o = o_heads[h]
        outs.append(o * lax.rsqrt(jnp.mean(o * o, axis=-1, keepdims=True) + EPS))
    return jnp.concatenate(outs, axis=-1) * gn * _silu(z)


def _gla_prompt_kernel(qk_ref, v_ref, z_ref, sm_ref, ab, a_bias, gn, y_ref, s_ref, st_ref, *, n_chunks):
    L = CHUNK
    last = pl.num_programs(1) - 1

    @pl.when(pl.program_id(1) == 0)
    def _():
        st_ref[...] = jnp.zeros_like(st_ref)

    tril_incl = _iota2((L, L), 1) <= _iota2((L, L), 0)
    scale = G_DK ** -0.5
    C, H = range(n_chunks), range(G_HEADS)
    rws = [slice(c * L, (c + 1) * L) for c in C]
    sks = [slice(h * G_DK, (h + 1) * G_DK) for h in H]

    qk = qk_ref[...].astype(F32)
    q, k = qk[:, 0:G_QK] * scale, qk[:, G_QK:2 * G_QK]
    v = v_ref[...].astype(F32)
    la = _gla_log_gate(sm_ref[...], ab[...], a_bias[...])
    cum = _mm_exact_lhs(_block_tril(n_chunks * L, L), la)
    q0, qe, ke, kl, e_last = [], [], [], [], []
    for c in C:
        cum_c = cum[rws[c]]
        ref_row = cum_c[L // 2:L // 2 + 1, :]
        last_row = cum_c[L - 1:L, :]
        q0.append(q[rws[c]] * jnp.exp(cum_c))
        qe.append(q[rws[c]] * jnp.exp(cum_c - ref_row))
        ke.append(k[rws[c]] * jnp.exp(ref_row - cum_c))
        kl.append(k[rws[c]] * jnp.exp(last_row - cum_c))
        e_last.append(jnp.exp(last_row))
    v_h = [[v[rws[c], h * G_DV:(h + 1) * G_DV] for h in H] for c in C]
    att = [[jnp.where(tril_incl, _mm_nt(qe[c][:, sks[h]], ke[c][:, sks[h]]), 0.0) for h in H] for c in C]
    upd = [[_mm_tn(v_h[c][h], kl[c][:, sks[h]]) for h in H] for c in C]
    av = [[_mm(att[c][h], v_h[c][h]) for h in H] for c in C]
    st = [st_ref[h] for h in H]
    o_ch = []
    for c in C:
        o_ch.append([av[c][h] + _mm_nt(q0[c][:, sks[h]], st[h]) for h in H])
        st = [st[h] * e_last[c][:, sks[h]] + upd[c][h] for h in H]
    for h in H:
        st_ref[h] = st[h]
    o_heads = [jnp.concatenate([o_ch[c][h] for c in C], axis=0) for h in H]
    y_ref[...] = _gla_out(o_heads, z_ref[...].astype(F32), gn[...]).astype(y_ref.dtype)

    @pl.when(pl.program_id(1) == last)
    def _():
        for h in range(G_HEADS):
            s_ref[0, h] = st_ref[h].T


def _gla_params(ab, a_bias, gn):
    ab_p = jnp.zeros((LANES, G_QK), F32).at[SM_GA:SM_GA + G_LORA].set(ab).astype(BF16)
    return ab_p, a_bias.reshape(1, -1), gn.reshape(1, -1)


def _gla_prompt(proj, side, params, batch, seq, tb):
    nt = seq // tb
    cb = lambda c, w: pl.BlockSpec((tb, w), lambda b, t, c=c: (b * nt + t, c // w))
    full = lambda a: pl.BlockSpec(a.shape, lambda b, t: (0,) * a.ndim)
    kern = functools.partial(_gla_prompt_kernel, n_chunks=tb // CHUNK)
    return pl.pallas_call(
        kern,
        grid=(batch, nt),
        in_specs=[cb(C_GQK, 2 * G_QK), cb(C_GV, G_W), cb(C_GZ, G_W), cb(S_SMALL, LANES)]
                 + [full(a) for a in params],
        out_specs=[pl.BlockSpec((tb, G_W), lambda b, t: (b * nt + t, 0)),
                   pl.BlockSpec((1, G_HEADS, G_DK, G_DV), lambda b, t: (b, 0, 0, 0))],
        out_shape=[jax.ShapeDtypeStruct((batch * seq, G_W), BF16),
                   jax.ShapeDtypeStruct((batch, G_HEADS, G_DK, G_DV), F32)],
        scratch_shapes=[pltpu.VMEM((G_HEADS, G_DV, G_DK), F32)],
        compiler_params=_cparams(("parallel", "arbitrary")),
        name="gla_prompt",
    )(proj, proj, proj, side, *params)


def _gla_step_kernel(qk_ref, v_ref, z_ref, sm_ref, s_in, acc_ref, ab, a_bias, gn, y_ref, s_out,
                     vec_ref, vrow_ref, orow_ref, *, bb):
    del acc_ref
    qk = qk_ref[...].astype(F32)
    q = qk[:, 0:G_QK] * (G_DK ** -0.5)
    k = qk[:, G_QK:2 * G_QK]
    g = _gla_log_gate(sm_ref[...], ab[...], a_bias[...])
    vec_ref[0] = q
    vec_ref[1] = k
    vec_ref[2] = jnp.exp(g)
    vrow_ref[...] = v_ref[...].astype(F32)
    eye = (_iota2((G_DK, G_DK), 0) == _iota2((G_DK, G_DK), 1)).astype(F32)
    masks = _outer_masks()

    def per_rows(j, _):
        bs = [j * GLA_STEP_ROWS + i for i in range(GLA_STEP_ROWS)]
        vecs = [[vec_ref[i, pl.ds(b, 1), :] for i in range(3)] for b in bs]
        v_b = [vrow_ref[pl.ds(b, 1), :] for b in bs]
        items = [(bi, h) for bi in range(GLA_STEP_ROWS) for h in range(G_HEADS)]
        n = range(len(items))
        hk = lambda x, h: x[:, h * G_DK:(h + 1) * G_DK]
        q_, k_, e_ = ([hk(vecs[bi][i], h) for bi, h in items] for i in range(3))
        v_h = [v_b[bi][:, h * G_DV:(h + 1) * G_DV] for bi, h in items]
        s = [s_in[bs[bi], h] for bi, h in items]
        e_col = [_lane_sum(eye * e_[i]) for i in n]
        qk_dot = [_lane_sum(q_[i] * k_[i]) for i in n]
        o_s = [_mm(_rows8(q_[i] * e_[i]), s[i])[0:1] for i in n]
        kv = [_mm_tn(*_outer_operands(masks, (k_[i], v_h[i]))) for i in n]
        o = [qk_dot[i] * v_h[i] + o_s[i] for i in n]
        for i, (bi, h) in enumerate(items):
            s_out[bs[bi], h] = s[i] * e_col[i] + kv[i]
        for bi in range(GLA_STEP_ROWS):
            orow_ref[pl.ds(bs[bi], 1), :] = jnp.concatenate(
                [o[bi * G_HEADS + h] for h in range(G_HEADS)], axis=-1)
        return 0

    lax.fori_loop(0, bb // GLA_STEP_ROWS, per_rows, 0)
    o = orow_ref[...]
    o_heads = [o[:, h * G_DV:(h + 1) * G_DV] for h in range(G_HEADS)]
    y_ref[...] = _gla_out(o_heads, z_ref[...].astype(F32), gn[...]).astype(y_ref.dtype)


def _gla_step(proj, side, s_all, acc, params, layer, bb):
    nb = proj.shape[0]
    cb = lambda c, w: pl.BlockSpec((bb, w), lambda i, c=c: (i, c // w))
    full = lambda a: pl.BlockSpec(a.shape, lambda i: (0,) * a.ndim)
    st = _layer_block(s_all.shape[2:], bb, layer)
    kern = functools.partial(_gla_step_kernel, bb=bb)
    return pl.pallas_call(
        kern,
        grid=(nb // bb,),
        in_specs=[cb(C_GQK, 2 * G_QK), cb(C_GV, G_W), cb(C_GZ, G_W), cb(S_SMALL, LANES), st,
                  pl.BlockSpec(memory_space=pl.ANY)]
                 + [full(a) for a in params],
        out_specs=[pl.BlockSpec((bb, G_W), lambda i: (i, 0)), st],
        out_shape=[jax.ShapeDtypeStruct((nb, G_W), BF16), jax.ShapeDtypeStruct(acc.shape, F32)],
        input_output_aliases={5: 1},
        scratch_shapes=[pltpu.VMEM((3, bb, G_QK), F32), pltpu.VMEM((bb, G_W), F32),
                        pltpu.VMEM((bb, G_W), F32)],
        compiler_params=_cparams(("parallel",)),
        name="gla_step",
    )(proj, proj, proj, side, s_all, acc, *params)


def _mlstm_out(h_heads, o_pre, z, gn):
    outs = []
    for h in range(M_HEADS):
        y = h_heads[h]
        y = y - jnp.mean(y, axis=-1, keepdims=True)
        outs.append(y * lax.rsqrt(jnp.mean(y * y, axis=-1, keepdims=True) + EPS))
    return jnp.concatenate(outs, axis=-1) * gn * _sigmoid(o_pre) * _silu(z)


def _mlstm_prompt_kernel(q_ref, k_ref, v_ref, o_ref, z_ref, sm_ref, ifb, gn,
                         y_ref, c_ref, n_ref, m_ref, ct_ref, *, n_chunks):
    L = CHUNK

    @pl.when(pl.program_id(1) == 0)
    def _():
        ct_ref[...] = jnp.zeros_like(ct_ref)
        n_ref[...] = jnp.zeros_like(n_ref)
        m_ref[...] = jnp.zeros_like(m_ref)

    causal = _iota2((L, L), 0) <= _iota2((L, L), 1)
    eye = (_iota2((L, L), 0) == _iota2((L, L), 1)).astype(F32)
    scale = M_DK ** -0.5
    neg_inf = -jnp.inf
    C, H = range(n_chunks), range(M_HEADS)
    rws = [slice(c * L, (c + 1) * L) for c in C]
    sls = [slice(h * M_DK, (h + 1) * M_DK) for h in H]

    pre = sm_ref[...] + ifb[...]
    lf = _log_sigmoid(pre)
    b_all = _mm_exact_lhs(_block_tril(n_chunks * L, L), lf)
    lib_all = pre - pltpu.roll(b_all, LANES - (SM_MF - SM_MI), 1)
    sel = (_iota2((LANES, M_HEADS * LANES), 0)
           == SM_MI + _iota2((LANES, M_HEADS * LANES), 1) // LANES).astype(F32)
    lib_bc = _mm_exact_rhs(lib_all, sel)
    comb = jnp.where(_iota2((1, LANES), 1) < SM_MF, lib_all, b_all)
    q, k, v = q_ref[...].astype(F32) * scale, k_ref[...].astype(F32), v_ref[...].astype(F32)
    qs = [[q[rws[c], sls[h]] for h in H] for c in C]
    k_h = [[k[rws[c], sls[h]] for h in H] for c in C]
    v_h = [[v[rws[c], sls[h]] for h in H] for c in C]
    qk_raw = [[_mm_nt(k_h[c][h], qs[c][h]) for h in H] for c in C]
    d_log, d_max, b_row, lib_row = [], [], [], []
    for c in C:
        comb_t = comb[rws[c]].T
        b_row.append([comb_t[SM_MF + h:SM_MF + h + 1, :] for h in H])
        lib_row.append([comb_t[SM_MI + h:SM_MI + h + 1, :] for h in H])
        d_log.append([jnp.where(causal, lib_bc[rws[c], h * LANES:h * LANES + L] + b_row[c][h], neg_inf)
                      for h in H])
        d_max.append([jnp.max(d_log[c][h], axis=0, keepdims=True) for h in H])
    m_prev = [m_ref[0, :, h:h + 1] for h in H]
    m_t, w_inter, carry, w_state, qk = [], [], [], [], []
    for c in C:
        m_inter = [b_row[c][h] + m_prev[h] for h in H]
        m_t.append([jnp.maximum(m_inter[h], d_max[c][h]) for h in H])
        w_inter.append([jnp.exp(m_inter[h] - m_t[c][h]) for h in H])
        m_new = [m_t[c][h][:, L - 1:L] for h in H]
        b_last = [b_row[c][h][:, L - 1:L] for h in H]
        carry.append([jnp.exp(b_last[h] + m_prev[h] - m_new[h]) for h in H])
        w_state.append([jnp.exp(b_last[h] + lib_row[c][h] - m_new[h]) for h in H])
        qk.append([qk_raw[c][h] * jnp.exp(d_log[c][h] - m_t[c][h]) for h in H])
        m_prev = m_new
    for h in H:
        m_ref[0, :, h:h + 1] = m_prev[h]
    wk = [[_mm(eye * w_state[c][h], k_h[c][h]) for h in H] for c in C]
    n_upd = [[_mm(_rows8(w_state[c][h]), k_h[c][h])[0:1] for h in H] for c in C]
    upd = [[_mm_tn(v_h[c][h], wk[c][h]) for h in H] for c in C]
    qkv = [[_mm_tn(v_h[c][h], qk[c][h]) for h in H] for c in C]
    ct = [ct_ref[h] for h in H]
    ns = [n_ref[0, h:h + 1, :] for h in H]
    y_ch = []
    for c in C:
        qc = [_mm_nt(ct[h], qs[c][h]) for h in H]
        qn = [_mm_nt(_rows8(ns[h]), qs[c][h])[0:1] for h in H]
        yy = []
        for h in H:
            num = qkv[c][h] + w_inter[c][h] * qc[h]
            den = jnp.sum(qk[c][h], axis=0, keepdims=True) + w_inter[c][h] * qn[h]
            ht = num / jnp.maximum(jnp.abs(den), jnp.exp(-m_t[c][h]))
            ht = ht - jnp.mean(ht, axis=0, keepdims=True)
            yy.append((ht * lax.rsqrt(jnp.mean(ht * ht, axis=0, keepdims=True) + EPS)).T)
        y_ch.append(yy)
        ct = [carry[c][h] * ct[h] + upd[c][h] for h in H]
        ns = [carry[c][h] * ns[h] + n_upd[c][h] for h in H]
    for h in H:
        ct_ref[h] = ct[h]
        n_ref[0, h:h + 1, :] = ns[h]
    y = jnp.concatenate([jnp.concatenate([y_ch[c][h] for c in C], axis=0) for h in H], axis=-1)
    y_ref[...] = (y * gn[...] * _sigmoid(o_ref[...].astype(F32)) * _silu(z_ref[...].astype(F32))
                  ).astype(y_ref.dtype)

    @pl.when(pl.program_id(1) == pl.num_programs(1) - 1)
    def _():
        for h in H:
            c_ref[0, h] = ct_ref[h].T


def _mlstm_params(ifb, gn):
    ifb_p = jnp.zeros((1, LANES), F32).at[0, SM_MI:SM_MI + 2 * M_HEADS].set(ifb)
    return ifb_p, gn.reshape(1, -1)


def _mlstm_prompt(proj, side, params, batch, seq, tb):
    nt = seq // tb
    cb = lambda c, w: pl.BlockSpec((tb, w), lambda b, t, c=c: (b * nt + t, c // w))
    full = lambda a: pl.BlockSpec(a.shape, lambda b, t: (0,) * a.ndim)
    kern = functools.partial(_mlstm_prompt_kernel, n_chunks=tb // CHUNK)
    return pl.pallas_call(
        kern,
        grid=(batch, nt),
        in_specs=[cb(C_MQ, M_QK), cb(C_MK, M_QK), cb(C_MV, M_W), cb(C_MO, M_W), cb(C_MZ, M_W),
                  cb(S_SMALL, LANES)] + [full(a) for a in params],
        out_specs=[pl.BlockSpec((tb, M_W), lambda b, t: (b * nt + t, 0)),
                   pl.BlockSpec((1, M_HEADS, M_DK, M_DV), lambda b, t: (b, 0, 0, 0)),
                   pl.BlockSpec((1, M_HEADS, M_DK), lambda b, t: (b, 0, 0)),
                   pl.BlockSpec((1, 1, M_HEADS), lambda b, t: (b, 0, 0))],
        out_shape=[jax.ShapeDtypeStruct((batch * seq, M_W), BF16),
                   jax.ShapeDtypeStruct((batch, M_HEADS, M_DK, M_DV), F32),
                   jax.ShapeDtypeStruct((batch, M_HEADS, M_DK), F32),
                   jax.ShapeDtypeStruct((batch, 1, M_HEADS), F32)],
        scratch_shapes=[pltpu.VMEM((M_HEADS, M_DV, M_DK), F32)],
        compiler_params=_cparams(("parallel", "arbitrary")),
        name="mlstm_prompt",
    )(proj, proj, proj, proj, proj, side, *params)


def _mlstm_step_kernel(q_ref, k_ref, v_ref, o_ref, z_ref, sm_ref, c_in, n_in, m_in, acc_ref, ifb, gn,
                       y_ref, c_out, n_out, m_out, vec_ref, sc_ref, hrow_ref, *, bb):
    del acc_ref
    pre = sm_ref[...] + ifb[...]
    li = pre[:, SM_MI:SM_MI + M_HEADS]
    lf = _log_sigmoid(pre[:, SM_MF:SM_MF + M_HEADS])
    m0 = m_in[...]
    m_inter = lf + m0
    m_t = jnp.maximum(m_inter, li)
    m_out[...] = m_t
    pad = lambda x: jnp.concatenate([x, jnp.zeros((bb, LANES - M_HEADS), F32)], axis=-1)
    sc_ref[0] = pad(jnp.exp(m_inter - m_t))
    sc_ref[1] = pad(jnp.exp(li - m_t))
    sc_ref[2] = pad(jnp.exp(-m_t))
    vec_ref[0] = q_ref[...].astype(F32) * (M_DK ** -0.5)
    vec_ref[1] = k_ref[...].astype(F32)
    vec_ref[2] = v_ref[...].astype(F32)
    masks = _outer_masks()

    def per_rows(j, _):
        bs = [j * STEP_ROWS + i for i in range(STEP_ROWS)]
        vecs = [[vec_ref[i, pl.ds(b, 1), :] for i in range(3)] for b in bs]
        scs = [[sc_ref[i, pl.ds(b, 1), :] for i in range(3)] for b in bs]
        items = [(bi, h) for bi in range(STEP_ROWS) for h in range(M_HEADS)]
        n_it = range(len(items))
        hs = lambda x, h: x[:, h * M_DK:(h + 1) * M_DK]
        q_h, k_h, v_h = ([hs(vecs[bi][i], h) for bi, h in items] for i in range(3))
        wi, ei, en = ([scs[bi][i][:, h:h + 1] for bi, h in items] for i in range(3))
        c = [c_in[bs[bi], h] for bi, h in items]
        n = [n_in[bs[bi], h:h + 1, :] for bi, h in items]
        qk = [_lane_sum(q_h[i] * k_h[i]) * ei[i] for i in n_it]
        qn = [_lane_sum(q_h[i] * n[i]) for i in n_it]
        qc = [_mm(_rows8(q_h[i]), c[i])[0:1] for i in n_it]
        kv = [_mm_tn(*_outer_operands(masks, (ei[i] * k_h[i], v_h[i]))) for i in n_it]
        hh = []
        for i, (bi, h) in enumerate(items):
            num = qk[i] * v_h[i] + wi[i] * qc[i]
            den = qk[i] + wi[i] * qn[i]
            hh.append(num / jnp.maximum(jnp.abs(den), en[i]))
            c_out[bs[bi], h] = wi[i] * c[i] + kv[i]
            n_out[bs[bi], h:h + 1, :] = wi[i] * n[i] + ei[i] * k_h[i]
        for bi in range(STEP_ROWS):
            hrow_ref[pl.ds(bs[bi], 1), :] = jnp.concatenate(
                [hh[bi * M_HEADS + h] for h in range(M_HEADS)], axis=-1)
        return 0

    lax.fori_loop(0, bb // STEP_ROWS, per_rows, 0)
    hh = hrow_ref[...]
    h_heads = [hh[:, h * M_DV:(h + 1) * M_DV] for h in range(M_HEADS)]
    y_ref[...] = _mlstm_out(h_heads, o_ref[...].astype(F32), z_ref[...].astype(F32), gn[...]).astype(y_ref.dtype)


def _mlstm_step(proj, side, c_all, n_all, m_all, acc, params, layer, bb):
    nb = proj.shape[0]
    cb = lambda c, w: pl.BlockSpec((bb, w), lambda i, c=c: (i, c // w))
    full = lambda a: pl.BlockSpec(a.shape, lambda i: (0,) * a.ndim)
    cs = _layer_block(c_all.shape[2:], bb, layer)
    ns_in = _layer_block(n_all.shape[2:], bb, layer)
    ms_in = _layer_block(m_all.shape[2:], bb, layer)
    ns = pl.BlockSpec((bb, M_HEADS, M_DK), lambda i: (i, 0, 0))
    ms = pl.BlockSpec((bb, M_HEADS), lambda i: (i, 0))
    kern = functools.partial(_mlstm_step_kernel, bb=bb)
    return pl.pallas_call(
        kern,
        grid=(nb // bb,),
        in_specs=[cb(C_MQ, M_QK), cb(C_MK, M_QK), cb(C_MV, M_W), cb(C_MO, M_W), cb(C_MZ, M_W),
                  cb(S_SMALL, LANES), cs, ns_in, ms_in, pl.BlockSpec(memory_space=pl.ANY)]
                 + [full(a) for a in params],
        out_specs=[pl.BlockSpec((bb, M_W), lambda i: (i, 0)), cs, ns, ms],
        out_shape=[jax.ShapeDtypeStruct((nb, M_W), BF16), jax.ShapeDtypeStruct(acc.shape, F32),
                   jax.ShapeDtypeStruct(n_all.shape[1:], F32), jax.ShapeDtypeStruct(m_all.shape[1:], F32)],
        input_output_aliases={9: 1},
        scratch_shapes=[pltpu.VMEM((3, bb, M_QK), F32), pltpu.VMEM((3, bb, LANES), F32),
                        pltpu.VMEM((bb, M_W), F32)],
        compiler_params=_cparams(("parallel",)),
        name="mlstm_step",
    )(proj, proj, proj, proj, proj, side, c_all, n_all, m_all, acc, *params)


def _shift_rows(proj_rows, side_rows):
    return jnp.concatenate([proj_rows[:, C_RIN:C_RIN + 3 * R_W].astype(F32),
                            side_rows[:, S_RLORA:S_RLORA + 2 * R_LORA]], axis=-1)


def _pick(n, prefs):
    for p in prefs:
        if n % p == 0:
            return p
    return n


def kernel(x_prompt, x_sample, state_rwkv_shift, state_rwkv, state_gla, state_mlstm_C, state_mlstm_n,
           state_mlstm_m, norm_pre, norm_post, w_in, r_mu_shift, r_w_lora_b, r_w0, r_a_lora_b, r_a0,
           r_k_k, r_k_a, r_r_k, r_gn, g_a_lora_b, g_a_bias, g_gn, m_if_bias, m_gn, w_br_rwkv, w_br_gla,
           w_br_mlstm, w_out):
    depth = w_in.shape[0]
    bp, tp, _ = x_prompt.shape
    bs, ts, _ = x_sample.shape
    assert ts == 1 and tp % CHUNK == 0
    assert w_in.shape[1:] == (D_MODEL, N_IN)
    w_t = jnp.transpose(w_in, (0, 2, 1))
    wr, wg, wm, wo = (w.astype(BF16) for w in (w_br_rwkv, w_br_gla, w_br_mlstm, w_out))

    mp = bp * tp
    tm_in = _pick(mp, (2048, 1024, 512, 256, 128, 64))
    tm_mg = _pick(mp, (1024, 512, 256, 128, 64))
    tb = _pick(tp, (256, 128, 64))
    bb = _pick(bs, (16, 8))

    xp = x_prompt.reshape(mp, D_MODEL)
    xs = x_sample.reshape(bs, D_MODEL)
    new_p = [[] for _ in range(6)]
    new_s = [[] for _ in range(3)]
    s_lanes = jnp.transpose(state_rwkv, (0, 2, 3, 4, 1))
    acc_r, acc_g, acc_c = (jnp.zeros(s.shape, F32) for s in (s_lanes, state_gla, state_mlstm_C))
    for l in range(depth):
        g_pre = norm_pre[l].reshape(1, -1)
        g_post = norm_post[l].reshape(1, -1)
        rp = _rwkv_params(r_mu_shift[l], r_w_lora_b[l], r_w0[l], r_a_lora_b[l], r_a0[l], r_k_k[l],
                          r_k_a[l], r_r_k[l], r_gn[l])
        gp = _gla_params(g_a_lora_b[l], g_a_bias[l], g_gn[l])
        mparams = _mlstm_params(m_if_bias[l], m_gn[l])

        proj, side, proj_s, side_s = _inproj(xp, xs, g_pre, w_t, l, tm_in)
        y_r, s_r = _rwkv_prompt(proj, side, rp, bp, tp, 2 * tb)
        y_g, s_g = _gla_prompt(proj, side, gp, bp, tp, tb)
        y_m, c, n, m = _mlstm_prompt(proj, side, mparams, bp, tp, tb)
        xp = _merge(proj, y_r, y_g, y_m, xp, wr, wg, wm, wo, g_post, l, tm_mg)
        shift = _shift_rows(proj.reshape(bp, tp, C_RLORA)[:, tp - 1], side.reshape(bp, tp, SIDE_W)[:, tp - 1])
        for lst, val in zip(new_p, (shift, s_r, s_g, c, n, m.reshape(bp, M_HEADS))):
            lst.append(val)

        proj, side = proj_s, side_s
        col = lambda x: jnp.broadcast_to(x[:, None], (R_W, LANES))
        y_t, acc_r = _rwkv_step_lanes(proj, side, state_rwkv_shift, s_lanes, acc_r,
                                      rp[:10] + (col(r_r_k[l]), col(r_gn[l])), l)
        y_r = jnp.transpose(y_t, (2, 0, 1)).reshape(bs, R_W)
        y_g, acc_g = _gla_step(proj, side, state_gla, acc_g, gp, l, bb)
        y_m, acc_c, n, m = _mlstm_step(proj, side, state_mlstm_C, state_mlstm_n, state_mlstm_m, acc_c,
                                       mparams, l, bb)
        xs = _merge(proj, y_r, y_g, y_m, xs, wr, wg, wm, wo, g_post, l, bs)
        shift = _shift_rows(proj, side)
        for lst, val in zip(new_s, (shift, n, m)):
            lst.append(val)

    stk = lambda lst: jnp.stack(lst)
    return (xp.reshape(bp, tp, D_MODEL), xs.reshape(bs, ts, D_MODEL),
            *(stk(v) for v in new_p),
            stk(new_s[0]), jnp.transpose(acc_r, (0, 4, 1, 2, 3)), acc_g, acc_c, stk(new_s[1]), stk(new_s[2]))
```
